```python
import jax, jax.numpy as jnp
from jax import lax
import numpy as np

D_MODEL = 1024
BATCH = 32
SEQ = 256
DEPTH = 4
DEC_BATCH = 2
DEC_SEQ = 1024
PAST_LEN = 512

GRID_W = 64
N_EVEN = (DEPTH + 1) // 2
N_ODD = DEPTH // 2
EPS = 1e-6
GLA_HEADS = 4
GLA_DK = 64
GLA_DV = 128
GLA_LOWRANK = 16
GLA_GATE_TEMP = 16.0
GLA_CHUNK = 64
ATT_HEADS = 8
ATT_KV_HEADS = 2
ATT_GROUP = ATT_HEADS // ATT_KV_HEADS
ATT_HD = 64
WINDOW = 128
ATT_BLOCK = 128
ROPE_BASE = 10000.0
ROPE_AXIS_DIMS = ATT_HD // 2
NEG = -1e30
IN_WIDTHS = (GLA_HEADS * GLA_DK, GLA_HEADS * GLA_DK, GLA_HEADS * GLA_DV, GLA_HEADS * GLA_DV,
             ATT_HEADS * ATT_HD, ATT_KV_HEADS * ATT_HD, ATT_KV_HEADS * ATT_HD, 2 * GLA_LOWRANK)
IN_WIDTH = sum(IN_WIDTHS)
IN_SPLITS = tuple(int(v) for v in np.cumsum(IN_WIDTHS)[:-1])
MIX_WIDTH = GLA_HEADS * GLA_DV + ATT_HEADS * ATT_HD
RWKV_HEAD = 64
RWKV_HEADS = D_MODEL // RWKV_HEAD
W_LORA = 64
A_LORA = 64
G_LORA = 128
RWKV_LN_EPS = 64e-5
N_GROUPS = 4
EXP_PER_GROUP = 4
N_EXPERTS = N_GROUPS * EXP_PER_GROUP
TOP_K = 2
D_EXPERT = 512

kernel_name = "hybrid_diffusion_gla_swa_rwkv7_hmoe_step"

f32 = jnp.float32


def rmsnorm(x, w):
    xf = x.astype(f32)
    y = xf * lax.rsqrt(jnp.mean(xf * xf, axis=-1, keepdims=True) + EPS)
    return (y * w.astype(f32)).astype(x.dtype)


def ada_mod(cond, w, b):
    mod = jax.nn.silu(cond) @ w + b
    return [m[:, None, :] for m in jnp.split(mod, 6, axis=-1)]


def modulate(x, shift, scale):
    return x * (1.0 + scale) + shift


def rope_tables(row_pos, col_pos):
    freqs = ROPE_BASE ** (-jnp.arange(0, ROPE_AXIS_DIMS, 2, dtype=f32) / ROPE_AXIS_DIMS)
    ang_r = row_pos.astype(f32)[:, None] * freqs[None, :]
    ang_c = col_pos.astype(f32)[:, None] * freqs[None, :]
    return jnp.cos(ang_r), jnp.sin(ang_r), jnp.cos(ang_c), jnp.sin(ang_c)


def rotate(x, cos, sin):
    x1, x2 = jnp.split(x, 2, axis=-1)
    cos = cos[None, :, None, :].astype(x.dtype)
    sin = sin[None, :, None, :].astype(x.dtype)
    return jnp.concatenate([x1 * cos - x2 * sin, x1 * sin + x2 * cos], axis=-1)


def apply_axial_rope(x, tables):
    cr, sr, cc, sc = tables
    xr, xc = jnp.split(x, 2, axis=-1)
    return jnp.concatenate([rotate(xr, cr, sr), rotate(xc, cc, sc)], axis=-1)


def gla_scan(q, k, v, logdecay, s0):
    B, L, H, _ = q.shape
    n = L // GLA_CHUNK

    def to_chunks(a):
        return a.reshape(B, n, GLA_CHUNK, H, a.shape[-1]).transpose(1, 0, 3, 2, 4).astype(f32)

    qc, kc, vc, gc = to_chunks(q), to_chunks(k), to_chunks(v), to_chunks(logdecay)
    causal = jnp.tril(jnp.ones((GLA_CHUNK, GLA_CHUNK), bool))

    def step(s, inp):
        qi, ki, vi, gi = inp
        b = jnp.cumsum(gi, axis=-2)
        b_last = b[..., -1:, :]
        q_e = qi * jnp.exp(b)
        att = jnp.einsum('bhik,bhjk->bhij', q_e, ki * jnp.exp(-b))
        att = jnp.where(causal, att, 0.0)
        o = jnp.einsum('bhik,bhkv->bhiv', q_e, s) + jnp.einsum('bhij,bhjv->bhiv', att, vi)
        s_new = jnp.exp(b_last[..., 0, :])[..., None] * s + jnp.einsum('bhjk,bhjv->bhkv', ki * jnp.exp(b_last - b), vi)
        return s_new, o

    s_fin, o = lax.scan(step, s0.astype(f32), (qc, kc, vc, gc))
    o = o.transpose(1, 0, 3, 2, 4).reshape(B, L, H, v.shape[-1])
    return o.astype(q.dtype), s_fin


def gla_bidir(q, k, v, ld_f, ld_b, s_f0, s_b0):
    o_f, s_f = gla_scan(q, k, v, ld_f, s_f0)
    o_b, s_b = gla_scan(q[:, ::-1], k[:, ::-1], v[:, ::-1], ld_b[:, ::-1], s_b0)
    return o_f + o_b[:, ::-1], s_f, s_b


def ctx_attn(q, k, v, sink):
    B, L = q.shape[:2]
    nb = L // ATT_BLOCK
    scale = ATT_HD ** -0.5
    qb = q.reshape(B, nb, ATT_BLOCK, ATT_KV_HEADS, ATT_GROUP, ATT_HD).transpose(1, 0, 2, 3, 4, 5)
    sink_b = sink.reshape(ATT_KV_HEADS, ATT_GROUP)[None, :, :, None, None].astype(f32)

    def block(q_blk):
        s = jnp.einsum('bqhgd,bkhd->bhgqk', q_blk, k).astype(f32) * scale
        s_sink = jnp.broadcast_to(sink_b, s.shape[:-1] + (1,))
        p = jax.nn.softmax(jnp.concatenate([s_sink, s], axis=-1), axis=-1)[..., 1:].astype(v.dtype)
        return jnp.einsum('bhgqk,bkhd->bqhgd', p, v).reshape(B, ATT_BLOCK, ATT_HEADS * ATT_HD)

    o = lax.map(block, qb)
    return o.transpose(1, 0, 2, 3).reshape(B, L, ATT_HEADS * ATT_HD)


def window_attn_latent(q, k, v, k_ctx, v_ctx, sink):
    B, S = q.shape[:2]
    nb = S // ATT_BLOCK
    Lc = k_ctx.shape[1]
    scale = ATT_HD ** -0.5
    qb = q.reshape(B, nb, ATT_BLOCK, ATT_KV_HEADS, ATT_GROUP, ATT_HD).transpose(1, 0, 2, 3, 4, 5)
    pad = ((0, 0), (ATT_BLOCK, ATT_BLOCK), (0, 0), (0, 0))
    kp = jnp.pad(k, pad)
    vp = jnp.pad(v, pad)
    sink_b = sink.reshape(ATT_KV_HEADS, ATT_GROUP)[None, :, :, None, None].astype(f32)
    offs_q = jnp.arange(ATT_BLOCK)
    offs_k = jnp.arange(3 * ATT_BLOCK) - ATT_BLOCK

    def block(args):
        bi, q_blk = args
        k_band = lax.dynamic_slice_in_dim(kp, bi * ATT_BLOCK, 3 * ATT_BLOCK, axis=1)
        v_band = lax.dynamic_slice_in_dim(vp, bi * ATT_BLOCK, 3 * ATT_BLOCK, axis=1)
        qpos = bi * ATT_BLOCK + offs_q
        kpos = bi * ATT_BLOCK + offs_k
        valid = (jnp.abs(qpos[:, None] - kpos[None, :]) <= WINDOW) & (kpos >= 0)[None, :] & (kpos < S)[None, :]
        s_lat = jnp.einsum('bqhgd,bkhd->bhgqk', q_blk, k_band).astype(f32) * scale
        s_lat = jnp.where(valid, s_lat, NEG)
        s_ctx = jnp.einsum('bqhgd,bchd->bhgqc', q_blk, k_ctx).astype(f32) * scale
        s_sink = jnp.broadcast_to(sink_b, s_ctx.shape[:-1] + (1,))
        p = jax.nn.softmax(jnp.concatenate([s_sink, s_ctx, s_lat], axis=-1), axis=-1).astype(v.dtype)
        o = (jnp.einsum('bhgqc,bchd->bqhgd', p[..., 1:1 + Lc], v_ctx)
             + jnp.einsum('bhgqk,bkhd->bqhgd', p[..., 1 + Lc:], v_band))
        return o.reshape(B, ATT_BLOCK, ATT_HEADS * ATT_HD)

    o = lax.map(block, (jnp.arange(nb), qb))
    return o.transpose(1, 0, 2, 3).reshape(B, S, ATT_HEADS * ATT_HD)


def even_projections(h, w_in, dec_w, dec_b):
    B, L, _ = h.shape
    gq, gk, gv, gg, aq, ak, av, lr = jnp.split(h @ w_in, IN_SPLITS, axis=-1)
    gq = gq.reshape(B, L, GLA_HEADS, GLA_DK) * (GLA_DK ** -0.5)
    gk = gk.reshape(B, L, GLA_HEADS, GLA_DK)
    gv = gv.reshape(B, L, GLA_HEADS, GLA_DV)
    lr_f, lr_b = jnp.split(lr, 2, axis=-1)

    def logdecay(lr_d, w, b):
        return (jax.nn.log_sigmoid((lr_d @ w + b).astype(f32)) / GLA_GATE_TEMP).reshape(B, L, GLA_HEADS, GLA_DK)

    ld_f = logdecay(lr_f, dec_w[0], dec_b[0])
    ld_b = logdecay(lr_b, dec_w[1], dec_b[1])
    aq = aq.reshape(B, L, ATT_HEADS, ATT_HD)
    ak = ak.reshape(B, L, ATT_KV_HEADS, ATT_HD)
    av = av.reshape(B, L, ATT_KV_HEADS, ATT_HD)
    return gq, gk, gv, gg, ld_f, ld_b, aq, ak, av


def even_output(o_gla, gg, o_att, gla_norm, w_out):
    B, L = o_gla.shape[:2]
    o_gla = rmsnorm(o_gla, gla_norm).reshape(B, L, GLA_HEADS * GLA_DV) * jax.nn.silu(gg)
    return jnp.concatenate([o_gla, o_att.astype(o_gla.dtype)], axis=-1) @ w_out


def even_mixer_ctx(h, w_in, w_out, dec_w, dec_b, gla_norm, sink):
    gq, gk, gv, gg, ld_f, ld_b, aq, ak, av = even_projections(h, w_in, dec_w, dec_b)
    zero = jnp.zeros((h.shape[0], GLA_HEADS, GLA_DK, GLA_DV), f32)
    o_gla, s_f, s_b = gla_bidir(gq, gk, gv, ld_f, ld_b, zero, zero)
    o_att = ctx_attn(aq, ak, av, sink)
    return even_output(o_gla, gg, o_att, gla_norm, w_out), ak, av, s_f, s_b


def even_mixer_lat(h, w_in, w_out, dec_w, dec_b, gla_norm, sink, rope, k_ctx, v_ctx, s_f0, s_b0):
    gq, gk, gv, gg, ld_f, ld_b, aq, ak, av = even_projections(h, w_in, dec_w, dec_b)
    o_gla, _, _ = gla_bidir(gq, gk, gv, ld_f, ld_b, s_f0, s_b0)
    o_att = window_attn_latent(apply_axial_rope(aq, rope), apply_axial_rope(ak, rope), av,
                               k_ctx.astype(aq.dtype), v_ctx.astype(av.dtype), sink)
    return even_output(o_gla, gg, o_att, gla_norm, w_out)


def centred_shift(x):
    xp = jnp.pad(x, ((0, 0), (1, 1), (0, 0)))
    return 0.5 * (xp[:, :-2] + xp[:, 2:])


def rwkv_scan(r, w, k, v, kk, a, s0):
    def step(s, inp):
        r_t, w_t, k_t, v_t, kk_t, a_t = inp
        sa = jnp.einsum('bhvk,bhk->bhv', s, -kk_t)
        s = s * w_t[:, :, None, :] + sa[..., None] * (kk_t * a_t)[:, :, None, :] + v_t[..., None] * k_t[:, :, None, :]
        return s, jnp.einsum('bhvk,bhk->bhv', s, r_t)

    xs = tuple(t.transpose(1, 0, 2, 3) for t in (r, w, k, v, kk, a))
    s_fin, out = lax.scan(step, s0.astype(f32), xs)
    return out.transpose(1, 0, 2, 3), s_fin


def rwkv_mix(h, mu, wr, wk, wv, wo, w0, w1, w2, a0, a1, a2, g1, g2, k_k, k_a, r_k, lnx_w, lnx_b, s_f0, s_b0):
    B, L, D = h.shape
    H, N = RWKV_HEADS, RWKV_HEAD
    xx = centred_shift(h) - h
    xr, xw, xk, xv, xa, xg = [h + xx * mu[j] for j in range(6)]
    r = (xr @ wr).reshape(B, L, H, N).astype(f32)
    k = (xk @ wk).reshape(B, L, H, N).astype(f32)
    v = (xv @ wv).reshape(B, L, H, N).astype(f32)
    g = jax.nn.sigmoid(xg @ g1) @ g2
    kk = k * k_k.reshape(H, N).astype(f32)
    kk = kk * lax.rsqrt(jnp.sum(kk * kk, axis=-1, keepdims=True) + 1e-12)

    def direction(d, flip, s0):
        w_raw = -jax.nn.softplus(-(w0[d] + jnp.tanh(xw @ w1[d]) @ w2[d]).astype(f32)) - 0.5
        decay = jnp.exp(-jnp.exp(w_raw)).reshape(B, L, H, N)
        a = jax.nn.sigmoid((a0[d] + (xa @ a1[d]) @ a2[d]).astype(f32)).reshape(B, L, H, N)
        k_d = k * (1.0 + (a - 1.0) * k_a.reshape(H, N).astype(f32))
        seqs = (r, decay, k_d, v, kk, a)
        if flip:
            seqs = tuple(t[:, ::-1] for t in seqs)
        o, s = rwkv_scan(*seqs, s0)
        if flip:
            o = o[:, ::-1]
        return o, s, k_d

    o_f, s_f, k_f = direction(0, False, s_f0)
    o_b, s_b, k_b = direction(1, True, s_b0)
    o = o_f + o_b
    m = jnp.mean(o, axis=-1, keepdims=True)
    var = jnp.mean(jnp.square(o - m), axis=-1, keepdims=True)
    o = (o - m) * lax.rsqrt(var + RWKV_LN_EPS) * lnx_w.reshape(H, N).astype(f32) + lnx_b.reshape(H, N).astype(f32)
    bonus = jnp.sum(r * 0.5 * (k_f + k_b) * r_k.astype(f32), axis=-1, keepdims=True) * v
    o = (o + bonus).reshape(B, L, D).astype(h.dtype)
    return (o * g) @ wo, s_f, s_b


def hier_moe(h, w_grp, b_grp, w_exp, b_exp, w_gate, w_up, w_down):
    B, L, D = h.shape
    t = h.reshape(-1, D)
    grp_logits = (t @ w_grp + b_grp).astype(f32)
    grp_prob = jax.nn.softmax(grp_logits, axis=-1)
    g_top = jnp.argmax(grp_logits, axis=-1)
    p_g = jnp.take_along_axis(grp_prob, g_top[:, None], axis=1)
    exp_logits = (t @ w_exp + b_exp).astype(f32).reshape(-1, N_GROUPS, EXP_PER_GROUP)
    sel = jnp.take_along_axis(exp_logits, g_top[:, None, None], axis=1)[:, 0]
    top_v, top_i = lax.top_k(sel, TOP_K)
    wts = p_g * jax.nn.softmax(top_v, axis=-1)
    idx = g_top[:, None] * EXP_PER_GROUP + top_i
    gates = jnp.sum(jax.nn.one_hot(idx, N_EXPERTS, dtype=f32) * wts[..., None], axis=1)
    hid = jax.nn.silu(jnp.einsum('td,edf->tef', t, w_gate)) * jnp.einsum('td,edf->tef', t, w_up)
    hid = hid * gates[..., None].astype(hid.dtype)
    return jnp.einsum('tef,efd->td', hid, w_down).reshape(B, L, D)


def setup_inputs(seed: int = 0) -> dict:
    key = jax.random.key(seed)
    ks = iter(jax.random.split(key, 64))
    D = D_MODEL

    def nrm(shape, scale=1.0):
        return jax.random.normal(next(ks), shape, f32) * scale

    def gain(shape):
        return 1.0 + nrm(shape, 0.05)

    return {
        "x_prompt": nrm((BATCH, SEQ, D)),
        "x_sample": nrm((DEC_BATCH, DEC_SEQ, D)),
        "c": nrm((DEC_BATCH, D)),
        "cache_attn_k": nrm((DEC_BATCH, N_EVEN, PAST_LEN, ATT_KV_HEADS, ATT_HD)),
        "cache_attn_v": nrm((DEC_BATCH, N_EVEN, PAST_LEN, ATT_KV_HEADS, ATT_HD)),
        "state_gla": nrm((DEC_BATCH, N_EVEN, 2, GLA_HEADS, GLA_DK, GLA_DV), 0.5),
        "state_rwkv": nrm((DEC_BATCH, N_ODD, 2, RWKV_HEADS, RWKV_HEAD, RWKV_HEAD), 0.3),
        "c_ctx": nrm((D,)),
        "ada_w": nrm((DEPTH, D, 6 * D), 0.5 * D ** -0.5),
        "ada_b": nrm((DEPTH, 6 * D), 0.02),
        "norm_mix": gain((DEPTH, D)),
        "norm_ffn": gain((DEPTH, D)),
        "norm_out": gain((D,)),
        "ev_w_in": nrm((N_EVEN, D, IN_WIDTH), D ** -0.5),
        "ev_w_out": nrm((N_EVEN, MIX_WIDTH, D), MIX_WIDTH ** -0.5),
        "gla_dec_w": nrm((N_EVEN, 2, GLA_LOWRANK, GLA_HEADS * GLA_DK), GLA_LOWRANK ** -0.5),
        "gla_dec_b": nrm((N_EVEN, 2, GLA_HEADS * GLA_DK), 0.5),
        "gla_norm": gain((N_EVEN, GLA_DV)),
        "att_sink": nrm((N_EVEN, ATT_HEADS), 0.5),
        "rw_mu": jax.random.uniform(next(ks), (N_ODD, 6, D), f32),
        "rw_wr": nrm((N_ODD, D, D), D ** -0.5),
        "rw_wk": nrm((N_ODD, D, D), D ** -0.5),
        "rw_wv": nrm((N_ODD, D, D), D ** -0.5),
        "rw_wo": nrm((N_ODD, D, D), D ** -0.5),
        "rw_w0": -1.0 + nrm((N_ODD, 2, D), 0.5),
        "rw_w1": nrm((N_ODD, 2, D, W_LORA), D ** -0.5),
        "rw_w2": nrm((N_ODD, 2, W_LORA, D), 0.5 * W_LORA ** -0.5),
        "rw_a0": nrm((N_ODD, 2, D), 0.5),
        "rw_a1": nrm((N_ODD, 2, D, A_LORA), D ** -0.5),
        "rw_a2": nrm((N_ODD, 2, A_LORA, D), 0.5 * A_LORA ** -0.5),
        "rw_g1": nrm((N_ODD, D, G_LORA), D ** -0.5),
        "rw_g2": nrm((N_ODD, G_LORA, D), G_LORA ** -0.5),
        "rw_kk": 0.85 + nrm((N_ODD, D), 0.05),
        "rw_ka": 1.0 + nrm((N_ODD, D), 0.05),
        "rw_rk": nrm((N_ODD, RWKV_HEADS, RWKV_HEAD), 0.1),
        "rw_lnx_w": gain((N_ODD, D)),
        "rw_lnx_b": nrm((N_ODD, D), 0.02),
        "moe_w_grp": nrm((DEPTH, D, N_GROUPS), D ** -0.5),
        "moe_b_grp": nrm((DEPTH, N_GROUPS), 0.01),
        "moe_w_exp": nrm((DEPTH, D, N_EXPERTS), D ** -0.5),
        "moe_b_exp": nrm((DEPTH, N_EXPERTS), 0.01),
        "moe_w_gate": nrm((DEPTH, N_EXPERTS, D, D_EXPERT), D ** -0.5),
        "moe_w_up": nrm((DEPTH, N_EXPERTS, D, D_EXPERT), D ** -0.5),
        "moe_w_down": nrm((DEPTH, N_EXPERTS, D_EXPERT, D), D_EXPERT ** -0.5),
    }


def reference(x_prompt, x_sample, c, cache_attn_k, cache_attn_v, state_gla, state_rwkv, c_ctx,
              ada_w, ada_b, norm_mix, norm_ffn, norm_out, ev_w_in, ev_w_out, gla_dec_w, gla_dec_b, gla_norm,
              att_sink, rw_mu, rw_wr, rw_wk, rw_wv, rw_wo, rw_w0, rw_w1, rw_w2, rw_a0, rw_a1, rw_a2, rw_g1, rw_g2,
              rw_kk, rw_ka, rw_rk, rw_lnx_w, rw_lnx_b, moe_w_grp, moe_b_grp, moe_w_exp, moe_b_exp,
              moe_w_gate, moe_w_up, moe_w_down):
    n_lat = x_sample.shape[1]
    rows = n_lat // GRID_W
    row_pos = jnp.repeat(jnp.arange(rows), GRID_W)
    col_pos = jnp.tile(jnp.arange(GRID_W), rows)
    rope = rope_tables(row_pos, col_pos)

    xc, xl = x_prompt, x_sample
    cond_ctx = c_ctx[None, :]
    ks_out, vs_out, gla_out, rwkv_out = [], [], [], []
    for l in range(DEPTH):
        sc1, scl1, gc1, sc2, scl2, gc2 = ada_mod(cond_ctx, ada_w[l], ada_b[l])
        sl1, sll1, gl1, sl2, sll2, gl2 = ada_mod(c, ada_w[l], ada_b[l])
        hc = modulate(rmsnorm(xc, norm_mix[l]), sc1, scl1)
        hl = modulate(rmsnorm(xl, norm_mix[l]), sl1, sll1)
        i = l // 2
        if l % 2 == 0:
            oc, k_c, v_c, s_f, s_b = even_mixer_ctx(hc, ev_w_in[i], ev_w_out[i], gla_dec_w[i], gla_dec_b[i],
                                                    gla_norm[i], att_sink[i])
            ol = even_mixer_lat(hl, ev_w_in[i], ev_w_out[i], gla_dec_w[i], gla_dec_b[i], gla_norm[i], att_sink[i],
                                rope, cache_attn_k[:, i], cache_attn_v[:, i], state_gla[:, i, 0], state_gla[:, i, 1])
            ks_out.append(k_c)
            vs_out.append(v_c)
            gla_out.append(jnp.stack([s_f, s_b], axis=1))
        else:
            rw = (rw_mu[i], rw_wr[i], rw_wk[i], rw_wv[i], rw_wo[i], rw_w0[i], rw_w1[i], rw_w2[i], rw_a0[i],
                  rw_a1[i], rw_a2[i], rw_g1[i], rw_g2[i], rw_kk[i], rw_ka[i], rw_rk[i], rw_lnx_w[i], rw_lnx_b[i])
            zero = jnp.zeros((xc.shape[0], RWKV_HEADS, RWKV_HEAD, RWKV_HEAD), f32)
            oc, s_f, s_b = rwkv_mix(hc, *rw, zero, zero)
            ol, _, _ = rwkv_mix(hl, *rw, state_rwkv[:, i, 0], state_rwkv[:, i, 1])
            rwkv_out.append(jnp.stack([s_f, s_b], axis=1))
        xc = xc + gc1 * oc
        xl = xl + gl1 * ol
        moe = (moe_w_grp[l], moe_b_grp[l], moe_w_exp[l], moe_b_exp[l], moe_w_gate[l], moe_w_up[l], moe_w_down[l])
        xc = xc + gc2 * hier_moe(modulate(rmsnorm(xc, norm_ffn[l]), sc2, scl2), *moe)
        xl = xl + gl2 * hier_moe(modulate(rmsnorm(xl, norm_ffn[l]), sl2, sll2), *moe)

    y_prompt = rmsnorm(xc, norm_out)
    y_sample = rmsnorm(xl, norm_out)
    new_attn_k = jnp.stack(ks_out, axis=1)
    new_attn_v = jnp.stack(vs_out, axis=1)
    new_gla = jnp.stack(gla_out, axis=1)
    new_rwkv = jnp.stack(rwkv_out, axis=1)
    return (y_prompt, y_sample, new_attn_k, new_attn_v, new_gla, new_rwkv)
```

```python
import functools

import jax
import jax.numpy as jnp
import numpy as np
from jax import lax
from jax.experimental import pallas as pl
from jax.experimental.pallas import tpu as pltpu

D_MODEL = 1024
DEPTH = 4
GRID_W = 64
EPS = 1e-6
GLA_HEADS = 4
GLA_DK = 64
GLA_DV = 128
GLA_LOWRANK = 16
GLA_GATE_TEMP = 16.0
GLA_CHUNK = 64
ATT_HEADS = 8
ATT_KV_HEADS = 2
ATT_GROUP = ATT_HEADS // ATT_KV_HEADS
ATT_HD = 64
WINDOW = 128
ATT_BLOCK = 128
ROPE_BASE = 10000.0
ROPE_AXIS_DIMS = ATT_HD // 2
NEG = -1e30
IN_WIDTHS = (GLA_HEADS * GLA_DK, GLA_HEADS * GLA_DK, GLA_HEADS * GLA_DV, GLA_HEADS * GLA_DV,
             ATT_HEADS * ATT_HD, ATT_KV_HEADS * ATT_HD, ATT_KV_HEADS * ATT_HD, 2 * GLA_LOWRANK)
IN_SPLITS = tuple(int(v) for v in np.cumsum(IN_WIDTHS)[:-1])
RWKV_HEAD = 64
RWKV_HEADS = D_MODEL // RWKV_HEAD
RWKV_LN_EPS = 64e-5
N_GROUPS = 4
EXP_PER_GROUP = 4
N_EXPERTS = N_GROUPS * EXP_PER_GROUP
TOP_K = 2
D_EXPERT = 512

V7X_VMEM_BYTES = 64 * 1024 * 1024
MOE_TILE_M = 256

f32 = jnp.float32
bf16 = jnp.bfloat16


def rmsnorm(x, w):
    y = x * lax.rsqrt(jnp.mean(x * x, axis=-1, keepdims=True) + EPS)
    return y * w


def ada_mod(cond, w, b):
    mod = jax.nn.silu(cond) @ w + b
    return [m[:, None, :] for m in jnp.split(mod, 6, axis=-1)]


def modulate(x, shift, scale):
    return x * (1.0 + scale) + shift


def rope_tables(row_pos, col_pos):
    freqs = ROPE_BASE ** (-jnp.arange(0, ROPE_AXIS_DIMS, 2, dtype=f32) / ROPE_AXIS_DIMS)
    ang_r = row_pos.astype(f32)[:, None] * freqs[None, :]
    ang_c = col_pos.astype(f32)[:, None] * freqs[None, :]
    return jnp.cos(ang_r), jnp.sin(ang_r), jnp.cos(ang_c), jnp.sin(ang_c)


def rotate(x, cos, sin):
    x1, x2 = jnp.split(x, 2, axis=-1)
    cos = cos[None, :, None, :]
    sin = sin[None, :, None, :]
    return jnp.concatenate([x1 * cos - x2 * sin, x1 * sin + x2 * cos], axis=-1)


def apply_axial_rope(x, tables):
    cr, sr, cc, sc = tables
    xr, xc = jnp.split(x, 2, axis=-1)
    return jnp.concatenate([rotate(xr, cr, sr), rotate(xc, cc, sc)], axis=-1)


def gla_scan(q, k, v, logdecay, s0):
    B, L, H, _ = q.shape
    n = L // GLA_CHUNK

    def to_chunks(a):
        return a.reshape(B, n, GLA_CHUNK, H, a.shape[-1]).transpose(1, 0, 3, 2, 4)

    qc, kc, vc, gc = to_chunks(q), to_chunks(k), to_chunks(v), to_chunks(logdecay)
    causal = jnp.tril(jnp.ones((GLA_CHUNK, GLA_CHUNK), bool))

    def step(s, inp):
        qi, ki, vi, gi = inp
        b = jnp.cumsum(gi, axis=-2)
        b_last = b[..., -1:, :]
        q_e = qi * jnp.exp(b)
        att = jnp.einsum('bhik,bhjk->bhij', q_e, ki * jnp.exp(-b))
        att = jnp.where(causal, att, 0.0)
        o = jnp.einsum('bhik,bhkv->bhiv', q_e, s) + jnp.einsum('bhij,bhjv->bhiv', att, vi)
        s_new = jnp.exp(b_last[..., 0, :])[..., None] * s + jnp.einsum('bhjk,bhjv->bhkv', ki * jnp.exp(b_last - b), vi)
        return s_new, o

    s_fin, o = lax.scan(step, s0, (qc, kc, vc, gc))
    o = o.transpose(1, 0, 3, 2, 4).reshape(B, L, H, v.shape[-1])
    return o, s_fin


def gla_bidir(q, k, v, ld_f, ld_b, s_f0, s_b0):
    o_f, s_f = gla_scan(q, k, v, ld_f, s_f0)
    o_b, s_b = gla_scan(q[:, ::-1], k[:, ::-1], v[:, ::-1], ld_b[:, ::-1], s_b0)
    return o_f + o_b[:, ::-1], s_f, s_b


def ctx_attn(q, k, v, sink):
    B, L = q.shape[:2]
    scale = ATT_HD ** -0.5
    qg = q.reshape(B, L, ATT_KV_HEADS, ATT_GROUP, ATT_HD)
    sink_b = sink.reshape(ATT_KV_HEADS, ATT_GROUP)[None, :, :, None, None]
    s = jnp.einsum('bqhgd,bkhd->bhgqk', qg, k) * scale
    s_sink = jnp.broadcast_to(sink_b, s.shape[:-1] + (1,))
    p = jax.nn.softmax(jnp.concatenate([s_sink, s], axis=-1), axis=-1)[..., 1:]
    return jnp.einsum('bhgqk,bkhd->bqhgd', p, v).reshape(B, L, ATT_HEADS * ATT_HD)


def window_attn_latent(q, k, v, k_ctx, v_ctx, sink):
    B, S = q.shape[:2]
    nb = S // ATT_BLOCK
    Lc = k_ctx.shape[1]
    scale = ATT_HD ** -0.5
    qb = q.reshape(B, nb, ATT_BLOCK, ATT_KV_HEADS, ATT_GROUP, ATT_HD).transpose(1, 0, 2, 3, 4, 5)
    pad = ((0, 0), (ATT_BLOCK, ATT_BLOCK), (0, 0), (0, 0))
    kp = jnp.pad(k, pad)
    vp = jnp.pad(v, pad)
    sink_b = sink.reshape(ATT_KV_HEADS, ATT_GROUP)[None, :, :, None, None]
    offs_q = jnp.arange(ATT_BLOCK)
    offs_k = jnp.arange(3 * ATT_BLOCK) - ATT_BLOCK

    def block(args):
        bi, q_blk = args
        k_band = lax.dynamic_slice_in_dim(kp, bi * ATT_BLOCK, 3 * ATT_BLOCK, axis=1)
        v_band = lax.dynamic_slice_in_dim(vp, bi * ATT_BLOCK, 3 * ATT_BLOCK, axis=1)
        qpos = bi * ATT_BLOCK + offs_q
        kpos = bi * ATT_BLOCK + offs_k
        valid = (jnp.abs(qpos[:, None] - kpos[None, :]) <= WINDOW) & (kpos >= 0)[None, :] & (kpos < S)[None, :]
        s_lat = jnp.einsum('bqhgd,bkhd->bhgqk', q_blk, k_band) * scale
        s_lat = jnp.where(valid, s_lat, NEG)
        s_ctx = jnp.einsum('bqhgd,bchd->bhgqc', q_blk, k_ctx) * scale
        s_sink = jnp.broadcast_to(sink_b, s_ctx.shape[:-1] + (1,))
        p = jax.nn.softmax(jnp.concatenate([s_sink, s_ctx, s_lat], axis=-1), axis=-1)
        o = (jnp.einsum('bhgqc,bchd->bqhgd', p[..., 1:1 + Lc], v_ctx)
             + jnp.einsum('bhgqk,bkhd->bqhgd', p[..., 1 + Lc:], v_band))
        return o.reshape(B, ATT_BLOCK, ATT_HEADS * ATT_HD)

    o = lax.map(block, (jnp.arange(nb), qb))
    return o.transpose(1, 0, 2, 3).reshape(B, S, ATT_HEADS * ATT_HD)


def even_projections(h, w_in, dec_w, dec_b):
    B, L, _ = h.shape
    gq, gk, gv, gg, aq, ak, av, lr = jnp.split(h @ w_in, IN_SPLITS, axis=-1)
    gq = gq.reshape(B, L, GLA_HEADS, GLA_DK) * (GLA_DK ** -0.5)
    gk = gk.reshape(B, L, GLA_HEADS, GLA_DK)
    gv = gv.reshape(B, L, GLA_HEADS, GLA_DV)
    lr_f, lr_b = jnp.split(lr, 2, axis=-1)

    def logdecay(lr_d, w, b):
        return (jax.nn.log_sigmoid(lr_d @ w + b) / GLA_GATE_TEMP).reshape(B, L, GLA_HEADS, GLA_DK)

    ld_f = logdecay(lr_f, dec_w[0], dec_b[0])
    ld_b = logdecay(lr_b, dec_w[1], dec_b[1])
    aq = aq.reshape(B, L, ATT_HEADS, ATT_HD)
    ak = ak.reshape(B, L, ATT_KV_HEADS, ATT_HD)
    av = av.reshape(B, L, ATT_KV_HEADS, ATT_HD)
    return gq, gk, gv, gg, ld_f, ld_b, aq, ak, av


def even_output(o_gla, gg, o_att, gla_norm, w_out):
    B, L = o_gla.shape[:2]
    o_gla = rmsnorm(o_gla, gla_norm).reshape(B, L, GLA_HEADS * GLA_DV) * jax.nn.silu(gg)
    return jnp.concatenate([o_gla, o_att], axis=-1) @ w_out


def even_mixer_ctx(h, w_in, w_out, dec_w, dec_b, gla_norm, sink):
    gq, gk, gv, gg, ld_f, ld_b, aq, ak, av = even_projections(h, w_in, dec_w, dec_b)
    zero = jnp.zeros((h.shape[0], GLA_HEADS, GLA_DK, GLA_DV), f32)
    o_gla, s_f, s_b = gla_bidir(gq, gk, gv, ld_f, ld_b, zero, zero)
    o_att = ctx_attn(aq, ak, av, sink)
    return even_output(o_gla, gg, o_att, gla_norm, w_out), ak, av, s_f, s_b


def even_mixer_lat(h, w_in, w_out, dec_w, dec_b, gla_norm, sink, rope, k_ctx, v_ctx, s_f0, s_b0):
    gq, gk, gv, gg, ld_f, ld_b, aq, ak, av = even_projections(h, w_in, dec_w, dec_b)
    o_gla, _, _ = gla_bidir(gq, gk, gv, ld_f, ld_b, s_f0, s_b0)
    o_att = window_attn_latent(apply_axial_rope(aq, rope), apply_axial_rope(ak, rope), av, k_ctx, v_ctx, sink)
    return even_output(o_gla, gg, o_att, gla_norm, w_out)


def centred_shift(x):
    xp = jnp.pad(x, ((0, 0), (1, 1), (0, 0)))
    return 0.5 * (xp[:, :-2] + xp[:, 2:])


def rwkv_scan(r, w, k, v, kk, a, s0):
    def step(s, inp):
        r_t, w_t, k_t, v_t, kk_t, a_t = inp
        sa = jnp.einsum('bhvk,bhk->bhv', s, -kk_t)
        s = s * w_t[:, :, None, :] + sa[..., None] * (kk_t * a_t)[:, :, None, :] + v_t[..., None] * k_t[:, :, None, :]
        return s, jnp.einsum('bhvk,bhk->bhv', s, r_t)

    xs = tuple(t.transpose(1, 0, 2, 3) for t in (r, w, k, v, kk, a))
    s_fin, out = lax.scan(step, s0, xs)
    return out.transpose(1, 0, 2, 3), s_fin


def rwkv_mix(h, mu, wr, wk, wv, wo, w0, w1, w2, a0, a1, a2, g1, g2, k_k, k_a, r_k, lnx_w, lnx_b, s_f0, s_b0):
    B, L, D = h.shape
    H, N = RWKV_HEADS, RWKV_HEAD
    xx = centred_shift(h) - h
    xr, xw, xk, xv, xa, xg = [h + xx * mu[j] for j in range(6)]
    r = (xr @ wr).reshape(B, L, H, N)
    k = (xk @ wk).reshape(B, L, H, N)
    v = (xv @ wv).reshape(B, L, H, N)
    g = jax.nn.sigmoid(xg @ g1) @ g2
    kk = k * k_k.reshape(H, N)
    kk = kk * lax.rsqrt(jnp.sum(kk * kk, axis=-1, keepdims=True) + 1e-12)

    def direction(d, flip, s0):
        w_raw = -jax.nn.softplus(-(w0[d] + jnp.tanh(xw @ w1[d]) @ w2[d])) - 0.5
        decay = jnp.exp(-jnp.exp(w_raw)).reshape(B, L, H, N)
        a = jax.nn.sigmoid(a0[d] + (xa @ a1[d]) @ a2[d]).reshape(B, L, H, N)
        k_d = k * (1.0 + (a - 1.0) * k_a.reshape(H, N))
        seqs = (r, decay, k_d, v, kk, a)
        if flip:
            seqs = tuple(t[:, ::-1] for t in seqs)
        o, s = rwkv_scan(*seqs, s0)
        if flip:
            o = o[:, ::-1]
        return o, s, k_d

    o_f, s_f, k_f = direction(0, False, s_f0)
    o_b, s_b, k_b = direction(1, True, s_b0)
    o = o_f + o_b
    m = jnp.mean(o, axis=-1, keepdims=True)
    var = jnp.mean(jnp.square(o - m), axis=-1, keepdims=True)
    o = (o - m) * lax.rsqrt(var + RWKV_LN_EPS) * lnx_w.reshape(H, N) + lnx_b.reshape(H, N)
    bonus = jnp.sum(r * 0.5 * (k_f + k_b) * r_k, axis=-1, keepdims=True) * v
    o = (o + bonus).reshape(B, L, D)
    return (o * g) @ wo, s_f, s_b


def _moe_expert_kernel(tile_expert_ref, n_tiles_ref, xs_ref, ws_ref, wg_ref, wu_ref, wd_ref, ys_ref,
                       wg_bf, wu_bf, wd_bf):
    i = pl.program_id(0)
    prev_expert = tile_expert_ref[jnp.maximum(i - 1, 0)]
    expert_changed = jnp.logical_or(i == 0, tile_expert_ref[i] != prev_expert)

    @pl.when(expert_changed)
    def _():
        wg_bf[...] = wg_ref[0].astype(bf16)
        wu_bf[...] = wu_ref[0].astype(bf16)
        wd_bf[...] = wd_ref[0].astype(bf16)

    @pl.when(i < n_tiles_ref[0])
    def _():
        x = xs_ref[...]
        g = jnp.dot(x, wg_bf[...], preferred_element_type=f32)
        u = jnp.dot(x, wu_bf[...], preferred_element_type=f32)
        hid = (g * jax.nn.sigmoid(g)) * u * ws_ref[...]
        ys_ref[...] = jnp.dot(hid.astype(bf16), wd_bf[...], preferred_element_type=f32)

    @pl.when(i >= n_tiles_ref[0])
    def _():
        ys_ref[...] = jnp.zeros_like(ys_ref)


def moe_experts(xs, ws, tile_expert, n_tiles, w_gate, w_up, w_down):
    P, D = xs.shape
    max_tiles = P // MOE_TILE_M
    weight_bytes = 3 * D * D_EXPERT * (2 * 4 + 2)
    tile_bytes = 2 * MOE_TILE_M * (D * 2 + D * 4 + 128 * 4)
    grid_spec = pltpu.PrefetchScalarGridSpec(
        num_scalar_prefetch=2,
        grid=(max_tiles,),
        in_specs=[
            pl.BlockSpec((MOE_TILE_M, D), lambda i, te, nt: (i, 0)),
            pl.BlockSpec((MOE_TILE_M, 1), lambda i, te, nt: (i, 0)),
            pl.BlockSpec((1, D, D_EXPERT), lambda i, te, nt: (te[i], 0, 0)),
            pl.BlockSpec((1, D, D_EXPERT), lambda i, te, nt: (te[i], 0, 0)),
            pl.BlockSpec((1, D_EXPERT, D), lambda i, te, nt: (te[i], 0, 0)),
        ],
        out_specs=pl.BlockSpec((MOE_TILE_M, D), lambda i, te, nt: (i, 0)),
        scratch_shapes=[
            pltpu.VMEM((D, D_EXPERT), bf16),
            pltpu.VMEM((D, D_EXPERT), bf16),
            pltpu.VMEM((D_EXPERT, D), bf16),
        ],
    )
    return pl.pallas_call(
        _moe_expert_kernel,
        grid_spec=grid_spec,
        out_shape=jax.ShapeDtypeStruct((P, D), f32),
        compiler_params=pltpu.CompilerParams(
            dimension_semantics=("arbitrary",),
            vmem_limit_bytes=min(V7X_VMEM_BYTES, weight_bytes + tile_bytes + (8 << 20)),
        ),
        name="moe_experts",
    )(tile_expert, n_tiles, xs, ws, w_gate, w_up, w_down)


def hier_moe(t, w_grp, b_grp, w_exp, b_exp, w_gate, w_up, w_down):
    T, D = t.shape
    hp = lax.Precision.HIGHEST
    grp_logits = jnp.dot(t, w_grp, precision=hp) + b_grp
    grp_prob = jax.nn.softmax(grp_logits, axis=-1)
    g_top = jnp.argmax(grp_logits, axis=-1)
    p_g = jnp.take_along_axis(grp_prob, g_top[:, None], axis=1)
    exp_logits = (jnp.dot(t, w_exp, precision=hp) + b_exp).reshape(-1, N_GROUPS, EXP_PER_GROUP)
    sel = jnp.take_along_axis(exp_logits, g_top[:, None, None], axis=1)[:, 0]
    top_v, top_i = lax.top_k(sel, TOP_K)
    wts = p_g * jax.nn.softmax(top_v, axis=-1)
    idx = (g_top[:, None] * EXP_PER_GROUP + top_i).astype(jnp.int32)

    e_flat = idx.reshape(-1)
    onehot = (e_flat[:, None] == jnp.arange(N_EXPERTS, dtype=jnp.int32)[None, :]).astype(jnp.int32)
    rank = jnp.take_along_axis(jnp.cumsum(onehot, axis=0), e_flat[:, None], axis=1)[:, 0] - 1
    counts = jnp.sum(onehot, axis=0)
    tiles_per = (counts + MOE_TILE_M - 1) // MOE_TILE_M
    tile_end = jnp.cumsum(tiles_per)
    start_padded = (tile_end - tiles_per) * MOE_TILE_M
    dest = start_padded[e_flat] + rank
    max_tiles = (TOP_K * T) // MOE_TILE_M + N_EXPERTS
    P = max_tiles * MOE_TILE_M
    tok = jnp.arange(TOP_K * T, dtype=jnp.int32) // TOP_K
    row_tok = jnp.zeros((P,), jnp.int32).at[dest].set(tok)
    row_w = jnp.zeros((P,), f32).at[dest].set(wts.reshape(-1))
    n_tiles = tile_end[-1:].astype(jnp.int32)
    tile_ids = jnp.arange(max_tiles, dtype=jnp.int32)
    tile_expert = jnp.minimum(jnp.sum((tile_ids[:, None] >= tile_end[None, :]).astype(jnp.int32), axis=1),
                              N_EXPERTS - 1).astype(jnp.int32)
    last_expert = tile_expert[jnp.maximum(n_tiles[0] - 1, 0)]
    tile_expert = jnp.where(tile_ids < n_tiles[0], tile_expert, last_expert)

    xs = jnp.take(t.astype(bf16), row_tok, axis=0)
    ys = moe_experts(xs, row_w[:, None], tile_expert, n_tiles, w_gate, w_up, w_down)
    pair = jnp.take(ys, dest, axis=0).reshape(T, TOP_K, D)
    return pair[:, 0] + pair[:, 1]


def kernel(x_prompt, x_sample, c, cache_attn_k, cache_attn_v, state_gla, state_rwkv, c_ctx, ada_w, ada_b, norm_mix, norm_ffn, norm_out, ev_w_in, ev_w_out, gla_dec_w, gla_dec_b, gla_norm, att_sink, rw_mu, rw_wr, rw_wk, rw_wv, rw_wo, rw_w0, rw_w1, rw_w2, rw_a0, rw_a1, rw_a2, rw_g1, rw_g2, rw_kk, rw_ka, rw_rk, rw_lnx_w, rw_lnx_b, moe_w_grp, moe_b_grp, moe_w_exp, moe_b_exp, moe_w_gate, moe_w_up, moe_w_down):
    n_lat = x_sample.shape[1]
    rows = n_lat // GRID_W
    row_pos = jnp.repeat(jnp.arange(rows), GRID_W)
    col_pos = jnp.tile(jnp.arange(GRID_W), rows)
    rope = rope_tables(row_pos, col_pos)
    Bc, Lc, D = x_prompt.shape
    Bl, Ll, _ = x_sample.shape

    xc, xl = x_prompt, x_sample
    cond_ctx = c_ctx[None, :]
    ks_out, vs_out, gla_out, rwkv_out = [], [], [], []
    for l in range(DEPTH):
        sc1, scl1, gc1, sc2, scl2, gc2 = ada_mod(cond_ctx, ada_w[l], ada_b[l])
        sl1, sll1, gl1, sl2, sll2, gl2 = ada_mod(c, ada_w[l], ada_b[l])
        hc = modulate(rmsnorm(xc, norm_mix[l]), sc1, scl1)
        hl = modulate(rmsnorm(xl, norm_mix[l]), sl1, sll1)
        i = l // 2
        if l % 2 == 0:
            oc, k_c, v_c, s_f, s_b = even_mixer_ctx(hc, ev_w_in[i], ev_w_out[i], gla_dec_w[i], gla_dec_b[i],
                                                    gla_norm[i], att_sink[i])
            ol = even_mixer_lat(hl, ev_w_in[i], ev_w_out[i], gla_dec_w[i], gla_dec_b[i], gla_norm[i], att_sink[i],
                                rope, cache_attn_k[:, i], cache_attn_v[:, i], state_gla[:, i, 0], state_gla[:, i, 1])
            ks_out.append(k_c)
            vs_out.append(v_c)
            gla_out.append(jnp.stack([s_f, s_b], axis=1))
        else:
            rw = (rw_mu[i], rw_wr[i], rw_wk[i], rw_wv[i], rw_wo[i], rw_w0[i], rw_w1[i], rw_w2[i], rw_a0[i],
                  rw_a1[i], rw_a2[i], rw_g1[i], rw_g2[i], rw_kk[i], rw_ka[i], rw_rk[i], rw_lnx_w[i], rw_lnx_b[i])
            zero = jnp.zeros((Bc, RWKV_HEADS, RWKV_HEAD, RWKV_HEAD), f32)
            oc, s_f, s_b = rwkv_mix(hc, *rw, zero, zero)
            ol, _, _ = rwkv_mix(hl, *rw, state_rwkv[:, i, 0], state_rwkv[:, i, 1])
            rwkv_out.append(jnp.stack([s_f, s_b], axis=1))
        xc = xc + gc1 * oc
        xl = xl + gl1 * ol
        tc = modulate(rmsnorm(xc, norm_ffn[l]), sc2, scl2).reshape(Bc * Lc, D)
        tl = modulate(rmsnorm(xl, norm_ffn[l]), sl2, sll2).reshape(Bl * Ll, D)
        y = hier_moe(jnp.concatenate([tc, tl], axis=0), moe_w_grp[l], moe_b_grp[l], moe_w_exp[l], moe_b_exp[l],
                     moe_w_gate[l], moe_w_up[l], moe_w_down[l])
        xc = xc + gc2 * y[:Bc * Lc].reshape(Bc, Lc, D)
        xl = xl + gl2 * y[Bc * Lc:].reshape(Bl, Ll, D)

    y_prompt = rmsnorm(xc, norm_out)
    y_sample = rmsnorm(xl, norm_out)
    new_attn_k = jnp.stack(ks_out, axis=1)
    new_attn_v = jnp.stack(vs_out, axis=1)
    new_gla = jnp.stack(gla_out, axis=1)
    new_rwkv = jnp.stack(rwkv_out, axis=1)
    return (y_prompt, y_sample, new_attn_k, new_attn_v, new_gla, new_rwkv)
```

```python
import functools

import jax
import jax.numpy as jnp
import numpy as np
from jax import lax
from jax.experimental import pallas as pl
from jax.experimental.pallas import tpu as pltpu

D_MODEL = 1024
DEPTH = 4
GRID_W = 64
EPS = 1e-6
GLA_HEADS = 4
GLA_DK = 64
GLA_DV = 128
GLA_LOWRANK = 16
GLA_GATE_TEMP = 16.0
GLA_CHUNK = 64
ATT_HEADS = 8
ATT_KV_HEADS = 2
ATT_GROUP = ATT_HEADS // ATT_KV_HEADS
ATT_HD = 64
WINDOW = 128
ATT_BLOCK = 128
ROPE_BASE = 10000.0
ROPE_AXIS_DIMS = ATT_HD // 2
NEG = -1e30
IN_WIDTHS = (GLA_HEADS * GLA_DK, GLA_HEADS * GLA_DK, GLA_HEADS * GLA_DV, GLA_HEADS * GLA_DV,
             ATT_HEADS * ATT_HD, ATT_KV_HEADS * ATT_HD, ATT_KV_HEADS * ATT_HD, 2 * GLA_LOWRANK)
IN_SPLITS = tuple(int(v) for v in np.cumsum(IN_WIDTHS)[:-1])
RWKV_HEAD = 64
RWKV_HEADS = D_MODEL // RWKV_HEAD
RWKV_LN_EPS = 64e-5
N_GROUPS = 4
EXP_PER_GROUP = 4
N_EXPERTS = N_GROUPS * EXP_PER_GROUP
TOP_K = 2
D_EXPERT = 512

V7X_VMEM_BYTES = 64 * 1024 * 1024
MOE_TILE_M = 256

f32 = jnp.float32
bf16 = jnp.bfloat16


def rmsnorm(x, w):
    y = x * lax.rsqrt(jnp.mean(x * x, axis=-1, keepdims=True) + EPS)
    return y * w


def ada_mod(cond, w, b):
    mod = jax.nn.silu(cond) @ w + b
    return [m[:, None, :] for m in jnp.split(mod, 6, axis=-1)]


def modulate(x, shift, scale):
    return x * (1.0 + scale) + shift


def rope_tables(row_pos, col_pos):
    freqs = ROPE_BASE ** (-jnp.arange(0, ROPE_AXIS_DIMS, 2, dtype=f32) / ROPE_AXIS_DIMS)
    ang_r = row_pos.astype(f32)[:, None] * freqs[None, :]
    ang_c = col_pos.astype(f32)[:, None] * freqs[None, :]
    return jnp.cos(ang_r), jnp.sin(ang_r), jnp.cos(ang_c), jnp.sin(ang_c)


def rotate(x, cos, sin):
    x1, x2 = jnp.split(x, 2, axis=-1)
    cos = cos[None, :, None, :]
    sin = sin[None, :, None, :]
    return jnp.concatenate([x1 * cos - x2 * sin, x1 * sin + x2 * cos], axis=-1)


def apply_axial_rope(x, tables):
    cr, sr, cc, sc = tables
    xr, xc = jnp.split(x, 2, axis=-1)
    return jnp.concatenate([rotate(xr, cr, sr), rotate(xc, cc, sc)], axis=-1)


def gla_scan(q, k, v, logdecay, s0):
    B, L, H, _ = q.shape
    n = L // GLA_CHUNK

    def to_chunks(a):
        return a.reshape(B, n, GLA_CHUNK, H, a.shape[-1]).transpose(1, 0, 3, 2, 4)

    qc, kc, vc, gc = to_chunks(q), to_chunks(k), to_chunks(v), to_chunks(logdecay)
    causal = jnp.tril(jnp.ones((GLA_CHUNK, GLA_CHUNK), bool))

    def step(s, inp):
        qi, ki, vi, gi = inp
        b = jnp.cumsum(gi, axis=-2)
        b_last = b[..., -1:, :]
        q_e = qi * jnp.exp(b)
        att = jnp.einsum('bhik,bhjk->bhij', q_e, ki * jnp.exp(-b))
        att = jnp.where(causal, att, 0.0)
        o = jnp.einsum('bhik,bhkv->bhiv', q_e, s) + jnp.einsum('bhij,bhjv->bhiv', att, vi)
        s_new = jnp.exp(b_last[..., 0, :])[..., None] * s + jnp.einsum('bhjk,bhjv->bhkv', ki * jnp.exp(b_last - b), vi)
        return s_new, o

    s_fin, o = lax.scan(step, s0, (qc, kc, vc, gc))
    o = o.transpose(1, 0, 3, 2, 4).reshape(B, L, H, v.shape[-1])
    return o, s_fin


def gla_bidir(q, k, v, ld_f, ld_b, s_f0, s_b0):
    o_f, s_f = gla_scan(q, k, v, ld_f, s_f0)
    o_b, s_b = gla_scan(q[:, ::-1], k[:, ::-1], v[:, ::-1], ld_b[:, ::-1], s_b0)
    return o_f + o_b[:, ::-1], s_f, s_b


def ctx_attn(q, k, v, sink):
    B, L = q.shape[:2]
    scale = ATT_HD ** -0.5
    qg = q.reshape(B, L, ATT_KV_HEADS, ATT_GROUP, ATT_HD)
    sink_b = sink.reshape(ATT_KV_HEADS, ATT_GROUP)[None, :, :, None, None]
    s = jnp.einsum('bqhgd,bkhd->bhgqk', qg, k) * scale
    s_sink = jnp.broadcast_to(sink_b, s.shape[:-1] + (1,))
    p = jax.nn.softmax(jnp.concatenate([s_sink, s], axis=-1), axis=-1)[..., 1:]
    return jnp.einsum('bhgqk,bkhd->bqhgd', p, v).reshape(B, L, ATT_HEADS * ATT_HD)


def window_attn_latent(q, k, v, k_ctx, v_ctx, sink):
    B, S = q.shape[:2]
    nb = S // ATT_BLOCK
    Lc = k_ctx.shape[1]
    scale = ATT_HD ** -0.5
    qb = q.reshape(B, nb, ATT_BLOCK, ATT_KV_HEADS, ATT_GROUP, ATT_HD).transpose(1, 0, 2, 3, 4, 5)
    pad = ((0, 0), (ATT_BLOCK, ATT_BLOCK), (0, 0), (0, 0))
    kp = jnp.pad(k, pad)
    vp = jnp.pad(v, pad)
    sink_b = sink.reshape(ATT_KV_HEADS, ATT_GROUP)[None, :, :, None, None]
    offs_q = jnp.arange(ATT_BLOCK)
    offs_k = jnp.arange(3 * ATT_BLOCK) - ATT_BLOCK

    def block(args):
        bi, q_blk = args
        k_band = lax.dynamic_slice_in_dim(kp, bi * ATT_BLOCK, 3 * ATT_BLOCK, axis=1)
        v_band = lax.dynamic_slice_in_dim(vp, bi * ATT_BLOCK, 3 * ATT_BLOCK, axis=1)
        qpos = bi * ATT_BLOCK + offs_q
        kpos = bi * ATT_BLOCK + offs_k
        valid = (jnp.abs(qpos[:, None] - kpos[None, :]) <= WINDOW) & (kpos >= 0)[None, :] & (kpos < S)[None, :]
        s_lat = jnp.einsum('bqhgd,bkhd->bhgqk', q_blk, k_band) * scale
        s_lat = jnp.where(valid, s_lat, NEG)
        s_ctx = jnp.einsum('bqhgd,bchd->bhgqc', q_blk, k_ctx) * scale
        s_sink = jnp.broadcast_to(sink_b, s_ctx.shape[:-1] + (1,))
        p = jax.nn.softmax(jnp.concatenate([s_sink, s_ctx, s_lat], axis=-1), axis=-1)
        o = (jnp.einsum('bhgqc,bchd->bqhgd', p[..., 1:1 + Lc], v_ctx)
             + jnp.einsum('bhgqk,bkhd->bqhgd', p[..., 1 + Lc:], v_band))
        return o.reshape(B, ATT_BLOCK, ATT_HEADS * ATT_HD)

    o = lax.map(block, (jnp.arange(nb), qb))
    return o.transpose(1, 0, 2, 3).reshape(B, S, ATT_HEADS * ATT_HD)


def even_projections(h, w_in, dec_w, dec_b):
    B, L, _ = h.shape
    gq, gk, gv, gg, aq, ak, av, lr = jnp.split(h @ w_in, IN_SPLITS, axis=-1)
    gq = gq.reshape(B, L, GLA_HEADS, GLA_DK) * (GLA_DK ** -0.5)
    gk = gk.reshape(B, L, GLA_HEADS, GLA_DK)
    gv = gv.reshape(B, L, GLA_HEADS, GLA_DV)
    lr_f, lr_b = jnp.split(lr, 2, axis=-1)

    def logdecay(lr_d, w, b):
        return (jax.nn.log_sigmoid(lr_d @ w + b) / GLA_GATE_TEMP).reshape(B, L, GLA_HEADS, GLA_DK)

    ld_f = logdecay(lr_f, dec_w[0], dec_b[0])
    ld_b = logdecay(lr_b, dec_w[1], dec_b[1])
    aq = aq.reshape(B, L, ATT_HEADS, ATT_HD)
    ak = ak.reshape(B, L, ATT_KV_HEADS, ATT_HD)
    av = av.reshape(B, L, ATT_KV_HEADS, ATT_HD)
    return gq, gk, gv, gg, ld_f, ld_b, aq, ak, av


def even_output(o_gla, gg, o_att, gla_norm, w_out):
    B, L = o_gla.shape[:2]
    o_gla = rmsnorm(o_gla, gla_norm).reshape(B, L, GLA_HEADS * GLA_DV) * jax.nn.silu(gg)
    return jnp.concatenate([o_gla, o_att], axis=-1) @ w_out


def even_mixer_ctx(h, w_in, w_out, dec_w, dec_b, gla_norm, sink):
    gq, gk, gv, gg, ld_f, ld_b, aq, ak, av = even_projections(h, w_in, dec_w, dec_b)
    zero = jnp.zeros((h.shape[0], GLA_HEADS, GLA_DK, GLA_DV), f32)
    o_gla, s_f, s_b = gla_bidir(gq, gk, gv, ld_f, ld_b, zero, zero)
    o_att = ctx_attn(aq, ak, av, sink)
    return even_output(o_gla, gg, o_att, gla_norm, w_out), ak, av, s_f, s_b


def even_mixer_lat(h, w_in, w_out, dec_w, dec_b, gla_norm, sink, rope, k_ctx, v_ctx, s_f0, s_b0):
    gq, gk, gv, gg, ld_f, ld_b, aq, ak, av = even_projections(h, w_in, dec_w, dec_b)
    o_gla, _, _ = gla_bidir(gq, gk, gv, ld_f, ld_b, s_f0, s_b0)
    o_att = window_attn_latent(apply_axial_rope(aq, rope), apply_axial_rope(ak, rope), av, k_ctx, v_ctx, sink)
    return even_output(o_gla, gg, o_att, gla_norm, w_out)


def centred_shift(x):
    xp = jnp.pad(x, ((0, 0), (1, 1), (0, 0)))
    return 0.5 * (xp[:, :-2] + xp[:, 2:])


RWKV_SUB = 64
LANES = 128
SUBLANES = 8
HEAD_PAIRS = D_MODEL // LANES


def _rwkv_scan_kernel(rf, vf, af, wf, kf, bf_, rb, vb, ab, wb, kb, bb, s0, of, ob, s, acc, *, tc):
    j = pl.program_id(1)

    @pl.when(j == 0)
    def _():
        s[...] = s0[...]

    row_head = lax.broadcasted_iota(jnp.int32, (2 * LANES, 2 * LANES), 0) // RWKV_HEAD
    col_head = lax.broadcasted_iota(jnp.int32, (2 * LANES, 2 * LANES), 1) // RWKV_HEAD
    ones_bd = (row_head == col_head).astype(bf16)
    sub = lax.broadcasted_iota(jnp.int32, (RWKV_HEAD, LANES), 0)
    lane_in_head = lax.broadcasted_iota(jnp.int32, (RWKV_HEAD, LANES), 1) % RWKV_HEAD
    eye = (sub == lane_in_head).astype(bf16)
    dirs = ((rf, vf, af, wf, kf, bf_), (rb, vb, ab, wb, kb, bb))

    def row_sums(x):
        return jnp.dot(x, ones_bd, preferred_element_type=f32)

    def row(ref, t, hp, reps):
        return jnp.broadcast_to(ref[t, pl.ds(hp, 1), :], (reps, LANES))

    def sub_chunk(sc, carry):
        def step(tt, carry):
            t_f = sc * RWKV_SUB + tt
            pos = ((t_f, jnp.maximum(t_f - 1, 0), tt - 1),
                   (tc - 1 - t_f, jnp.minimum(tc - t_f, tc - 1), RWKV_SUB - tt))
            for d in range(2):
                r_ref, v_ref, a_ref, w_ref, k_ref, b_ref = dirs[d]
                t_now, t_prev, out_lane = pos[d]
                lhs, tiles = [], []
                for hp2 in range(0, HEAD_PAIRS, 2):
                    v_diag = []
                    for hp in (hp2, hp2 + 1):
                        s_t = s[d, :, hp * LANES:(hp + 1) * LANES]
                        lhs.append(jnp.concatenate([(s_t * row(a_ref, t_now, hp, RWKV_HEAD)).astype(bf16),
                                                    (s_t * row(r_ref, t_prev, hp, RWKV_HEAD)).astype(bf16)], axis=1))
                        v_diag.append(eye * jnp.concatenate([row(v_ref, t_now, hp, 16).astype(bf16)] * 4, axis=0))
                        tiles.append(s_t)
                    lhs.append(jnp.concatenate(v_diag, axis=1))
                res = row_sums(jnp.concatenate(lhs, axis=0))
                for hp, s_t in enumerate(tiles):
                    ls = slice(hp * LANES, (hp + 1) * LANES)
                    base = (hp // 2) * 3 * RWKV_HEAD
                    own = res[base + (hp % 2) * RWKV_HEAD:base + (hp % 2 + 1) * RWKV_HEAD]
                    sa, out_prev = own[:, :LANES], own[:, LANES:]
                    v_col = res[base + 2 * RWKV_HEAD:base + 3 * RWKV_HEAD, (hp % 2) * LANES:(hp % 2 + 1) * LANES]
                    s[d, :, ls] = (s_t * row(w_ref, t_now, hp, RWKV_HEAD) + sa * row(b_ref, t_now, hp, RWKV_HEAD)
                                   + v_col * row(k_ref, t_now, hp, RWKV_HEAD))
                    acc[d, :, ls] = jnp.where(lane_in_head == out_lane, out_prev, acc[d, :, ls])
            return carry

        lax.fori_loop(0, RWKV_SUB, step, 0, unroll=4)
        t_last = sc * RWKV_SUB + RWKV_SUB - 1
        last = ((t_last, RWKV_SUB - 1), (tc - 1 - t_last, 0))
        for d in range(2):
            r_ref = dirs[d][0]
            t_now, out_lane = last[d]
            out_mask = jnp.concatenate([lane_in_head == out_lane] * 2, axis=1)
            for hp in range(0, HEAD_PAIRS, 2):
                ls = slice(hp * LANES, (hp + 2) * LANES)
                r_t = jnp.concatenate([row(r_ref, t_now, hp, RWKV_HEAD), row(r_ref, t_now, hp + 1, RWKV_HEAD)], axis=1)
                out_t = row_sums((s[d, :, ls] * r_t).astype(bf16))
                acc[d, :, ls] = jnp.where(out_mask, out_t, acc[d, :, ls])
        of[sc] = acc[0]
        ob[tc // RWKV_SUB - 1 - sc] = acc[1]
        return carry

    lax.fori_loop(0, tc // RWKV_SUB, sub_chunk, 0)


def rwkv_scan_bidir(r, v, nkk, w_f, k_f, b_f, w_b, k_b, b_b, s0):
    B, L, D = r.shape
    tc = min(L, 256)
    n = L // tc
    fwd = pl.BlockSpec((None, tc, HEAD_PAIRS, LANES), lambda b, j: (b, j, 0, 0))
    bwd = pl.BlockSpec((None, tc, HEAD_PAIRS, LANES), lambda b, j: (b, n - 1 - j, 0, 0))
    r, v, nkk, w_f, k_f, b_f, w_b, k_b, b_b = (t.reshape(B, L, HEAD_PAIRS, LANES)
                                               for t in (r, v, nkk, w_f, k_f, b_f, w_b, k_b, b_b))
    state = pl.BlockSpec((None, 2, RWKV_HEAD, D), lambda b, j: (b, 0, 0, 0))
    nsub = tc // RWKV_SUB
    block_bytes = tc * D * 4
    o_f, o_b, s_fin = pl.pallas_call(
        functools.partial(_rwkv_scan_kernel, tc=tc),
        grid=(B, n),
        in_specs=[fwd] * 6 + [bwd] * 6 + [state],
        out_specs=[pl.BlockSpec((None, nsub, RWKV_HEAD, D), lambda b, j: (b, j, 0, 0)),
                   pl.BlockSpec((None, nsub, RWKV_HEAD, D), lambda b, j: (b, n - 1 - j, 0, 0)),
                   state],
        out_shape=[jax.ShapeDtypeStruct((B, L // RWKV_SUB, RWKV_HEAD, D), f32),
                   jax.ShapeDtypeStruct((B, L // RWKV_SUB, RWKV_HEAD, D), f32),
                   jax.ShapeDtypeStruct((B, 2, RWKV_HEAD, D), f32)],
        scratch_shapes=[pltpu.VMEM((2, RWKV_HEAD, D), f32)],
        compiler_params=pltpu.CompilerParams(
            dimension_semantics=("arbitrary", "arbitrary"),
            vmem_limit_bytes=min(V7X_VMEM_BYTES, 2 * 14 * block_bytes + (8 << 20)),
        ),
        name="rwkv_scan",
    )(r, v, nkk, w_f, k_f, b_f, r, v, nkk, w_b, k_b, b_b, s0)
    o = (o_f + o_b).reshape(B, L // RWKV_SUB, RWKV_HEAD, RWKV_HEADS, RWKV_SUB)
    return o.transpose(0, 1, 4, 3, 2).reshape(B, L, D), s_fin


def rwkv_mix(h, mu, wr, wk, wv, wo, w0, w1, w2, a0, a1, a2, g1, g2, k_k, k_a, r_k, lnx_w, lnx_b, s_f0, s_b0):
    B, L, D = h.shape
    H, N = RWKV_HEADS, RWKV_HEAD
    xx = centred_shift(h) - h
    xr, xw, xk, xv, xa, xg = [h + xx * mu[j] for j in range(6)]
    r = (xr @ wr).reshape(B, L, H, N)
    k = (xk @ wk).reshape(B, L, H, N)
    v = (xv @ wv).reshape(B, L, H, N)
    g = jax.nn.sigmoid(xg @ g1) @ g2
    kk = k * k_k.reshape(H, N)
    kk = kk * lax.rsqrt(jnp.sum(kk * kk, axis=-1, keepdims=True) + 1e-12)

    def direction(d):
        w_raw = -jax.nn.softplus(-(w0[d] + jnp.tanh(xw @ w1[d]) @ w2[d])) - 0.5
        decay = jnp.exp(-jnp.exp(w_raw))
        a = jax.nn.sigmoid(a0[d] + (xa @ a1[d]) @ a2[d]).reshape(B, L, H, N)
        k_d = k * (1.0 + (a - 1.0) * k_a.reshape(H, N))
        return decay, k_d, kk * a

    w_f, k_f, b_f = direction(0)
    w_b, k_b, b_b = direction(1)

    def rows(t):
        return t.reshape(B, L, D)

    def to_kernel_state(s0):
        return s0.transpose(0, 2, 1, 3).reshape(B, N, D)

    o, s_fin = rwkv_scan_bidir(rows(r), rows(v), rows(-kk), w_f, rows(k_f), rows(b_f), w_b, rows(k_b), rows(b_b),
                               jnp.stack([to_kernel_state(s_f0), to_kernel_state(s_b0)], axis=1))
    s_fin = s_fin.reshape(B, 2, N, H, N).transpose(0, 1, 3, 2, 4)
    s_f, s_b = s_fin[:, 0], s_fin[:, 1]
    o = o.reshape(B, L, H, N)
    m = jnp.mean(o, axis=-1, keepdims=True)
    var = jnp.mean(jnp.square(o - m), axis=-1, keepdims=True)
    o = (o - m) * lax.rsqrt(var + RWKV_LN_EPS) * lnx_w.reshape(H, N) + lnx_b.reshape(H, N)
    bonus = jnp.sum(r * 0.5 * (k_f + k_b) * r_k, axis=-1, keepdims=True) * v
    o = (o + bonus).reshape(B, L, D)
    return (o * g) @ wo, s_f, s_b


def _moe_expert_kernel(tile_expert_ref, n_tiles_ref, xs_ref, ws_ref, wg_ref, wu_ref, wd_ref, ys_ref,
                       wg_bf, wu_bf, wd_bf):
    i = pl.program_id(0)
    prev_expert = tile_expert_ref[jnp.maximum(i - 1, 0)]
    expert_changed = jnp.logical_or(i == 0, tile_expert_ref[i] != prev_expert)

    @pl.when(expert_changed)
    def _():
        wg_bf[...] = wg_ref[0].astype(bf16)
        wu_bf[...] = wu_ref[0].astype(bf16)
        wd_bf[...] = wd_ref[0].astype(bf16)

    @pl.when(i < n_tiles_ref[0])
    def _():
        x = xs_ref[...]
        g = jnp.dot(x, wg_bf[...], preferred_element_type=f32)
        u = jnp.dot(x, wu_bf[...], preferred_element_type=f32)
        hid = (g * jax.nn.sigmoid(g)) * u * ws_ref[...]
        ys_ref[...] = jnp.dot(hid.astype(bf16), wd_bf[...], preferred_element_type=f32).astype(ys_ref.dtype)

    @pl.when(i >= n_tiles_ref[0])
    def _():
        ys_ref[...] = jnp.zeros_like(ys_ref)


def moe_experts(xs, ws, tile_expert, n_tiles, w_gate, w_up, w_down):
    P, D = xs.shape
    max_tiles = P // MOE_TILE_M
    weight_bytes = 3 * D * D_EXPERT * (2 * 4 + 2)
    tile_bytes = 2 * MOE_TILE_M * (D * 2 + D * 2 + LANES * 4)
    grid_spec = pltpu.PrefetchScalarGridSpec(
        num_scalar_prefetch=2,
        grid=(max_tiles,),
        in_specs=[
            pl.BlockSpec((MOE_TILE_M, D), lambda i, te, nt: (i, 0)),
            pl.BlockSpec((MOE_TILE_M, 1), lambda i, te, nt: (i, 0)),
            pl.BlockSpec((1, D, D_EXPERT), lambda i, te, nt: (te[i], 0, 0)),
            pl.BlockSpec((1, D, D_EXPERT), lambda i, te, nt: (te[i], 0, 0)),
            pl.BlockSpec((1, D_EXPERT, D), lambda i, te, nt: (te[i], 0, 0)),
        ],
        out_specs=pl.BlockSpec((MOE_TILE_M, D), lambda i, te, nt: (i, 0)),
        scratch_shapes=[
            pltpu.VMEM((D, D_EXPERT), bf16),
            pltpu.VMEM((D, D_EXPERT), bf16),
            pltpu.VMEM((D_EXPERT, D), bf16),
        ],
    )
    return pl.pallas_call(
        _moe_expert_kernel,
        grid_spec=grid_spec,
        out_shape=jax.ShapeDtypeStruct((P, D), bf16),
        compiler_params=pltpu.CompilerParams(
            dimension_semantics=("arbitrary",),
            vmem_limit_bytes=min(V7X_VMEM_BYTES, weight_bytes + tile_bytes + (8 << 20)),
        ),
        name="moe_experts",
    )(tile_expert, n_tiles, xs, ws, w_gate, w_up, w_down)


def hier_moe(t, w_grp, b_grp, w_exp, b_exp, w_gate, w_up, w_down):
    T, D = t.shape
    hp = lax.Precision.HIGHEST
    grp_logits = jnp.dot(t, w_grp, precision=hp) + b_grp
    grp_prob = jax.nn.softmax(grp_logits, axis=-1)
    g_top = jnp.argmax(grp_logits, axis=-1)
    p_g = jnp.take_along_axis(grp_prob, g_top[:, None], axis=1)
    exp_logits = (jnp.dot(t, w_exp, precision=hp) + b_exp).reshape(-1, N_GROUPS, EXP_PER_GROUP)
    sel = jnp.take_along_axis(exp_logits, g_top[:, None, None], axis=1)[:, 0]
    top_v, top_i = lax.top_k(sel, TOP_K)
    wts = p_g * jax.nn.softmax(top_v, axis=-1)
    idx = (g_top[:, None] * EXP_PER_GROUP + top_i).astype(jnp.int32)

    e_flat = idx.reshape(-1)
    onehot = (e_flat[:, None] == jnp.arange(N_EXPERTS, dtype=jnp.int32)[None, :]).astype(jnp.int32)
    rank = jnp.take_along_axis(jnp.cumsum(onehot, axis=0), e_flat[:, None], axis=1)[:, 0] - 1
    counts = jnp.sum(onehot, axis=0)
    tiles_per = (counts + MOE_TILE_M - 1) // MOE_TILE_M
    tile_end = jnp.cumsum(tiles_per)
    start_padded = (tile_end - tiles_per) * MOE_TILE_M
    dest = start_padded[e_flat] + rank
    max_tiles = (TOP_K * T) // MOE_TILE_M + N_EXPERTS
    P = max_tiles * MOE_TILE_M
    row_pair = jnp.full((P,), -1, jnp.int32).at[dest].set(jnp.arange(TOP_K * T, dtype=jnp.int32),
                                                          unique_indices=True)
    row_tok = jnp.maximum(row_pair, 0) // TOP_K
    row_w = jnp.where(row_pair >= 0, jnp.take(wts.reshape(-1), jnp.maximum(row_pair, 0)), 0.0)
    n_tiles = tile_end[-1:].astype(jnp.int32)
    tile_ids = jnp.arange(max_tiles, dtype=jnp.int32)
    tile_expert = jnp.minimum(jnp.sum((tile_ids[:, None] >= tile_end[None, :]).astype(jnp.int32), axis=1),
                              N_EXPERTS - 1).astype(jnp.int32)
    last_expert = tile_expert[jnp.maximum(n_tiles[0] - 1, 0)]
    tile_expert = jnp.where(tile_ids < n_tiles[0], tile_expert, last_expert)

    xs = jnp.take(t.astype(bf16), row_tok, axis=0)
    ys = moe_experts(xs, row_w[:, None], tile_expert, n_tiles, w_gate, w_up, w_down)
    pair = jnp.take(ys, dest, axis=0).reshape(T, TOP_K, D).astype(f32)
    return pair[:, 0] + pair[:, 1]


def kernel(x_prompt, x_sample, c, cache_attn_k, cache_attn_v, state_gla, state_rwkv, c_ctx, ada_w, ada_b, norm_mix, norm_ffn, norm_out, ev_w_in, ev_w_out, gla_dec_w, gla_dec_b, gla_norm, att_sink, rw_mu, rw_wr, rw_wk, rw_wv, rw_wo, rw_w0, rw_w1, rw_w2, rw_a0, rw_a1, rw_a2, rw_g1, rw_g2, rw_kk, rw_ka, rw_rk, rw_lnx_w, rw_lnx_b, moe_w_grp, moe_b_grp, moe_w_exp, moe_b_exp, moe_w_gate, moe_w_up, moe_w_down):
    n_lat = x_sample.shape[1]
    rows = n_lat // GRID_W
    row_pos = jnp.repeat(jnp.arange(rows), GRID_W)
    col_pos = jnp.tile(jnp.arange(GRID_W), rows)
    rope = rope_tables(row_pos, col_pos)
    Bc, Lc, D = x_prompt.shape
    Bl, Ll, _ = x_sample.shape

    xc, xl = x_prompt, x_sample
    cond_ctx = c_ctx[None, :]
    ks_out, vs_out, gla_out, rwkv_out = [], [], [], []
    for l in range(DEPTH):
        sc1, scl1, gc1, sc2, scl2, gc2 = ada_mod(cond_ctx, ada_w[l], ada_b[l])
        sl1, sll1, gl1, sl2, sll2, gl2 = ada_mod(c, ada_w[l], ada_b[l])
        hc = modulate(rmsnorm(xc, norm_mix[l]), sc1, scl1)
        hl = modulate(rmsnorm(xl, norm_mix[l]), sl1, sll1)
        i = l // 2
        if l % 2 == 0:
            oc, k_c, v_c, s_f, s_b = even_mixer_ctx(hc, ev_w_in[i], ev_w_out[i], gla_dec_w[i], gla_dec_b[i],
                                                    gla_norm[i], att_sink[i])
            ol = even_mixer_lat(hl, ev_w_in[i], ev_w_out[i], gla_dec_w[i], gla_dec_b[i], gla_norm[i], att_sink[i],
                                rope, cache_attn_k[:, i], cache_attn_v[:, i], state_gla[:, i, 0], state_gla[:, i, 1])
            ks_out.append(k_c)
            vs_out.append(v_c)
            gla_out.append(jnp.stack([s_f, s_b], axis=1))
        else:
            rw = (rw_mu[i], rw_wr[i], rw_wk[i], rw_wv[i], rw_wo[i], rw_w0[i], rw_w1[i], rw_w2[i], rw_a0[i],
                  rw_a1[i], rw_a2[i], rw_g1[i], rw_g2[i], rw_kk[i], rw_ka[i], rw_rk[i], rw_lnx_w[i], rw_lnx_b[i])
            zero = jnp.zeros((Bc, RWKV_HEADS, RWKV_HEAD, RWKV_HEAD), f32)
            oc, s_f, s_b = rwkv_mix(hc, *rw, zero, zero)
            ol, _, _ = rwkv_mix(hl, *rw, state_rwkv[:, i, 0], state_rwkv[:, i, 1])
            rwkv_out.append(jnp.stack([s_f, s_b], axis=1))
        xc = xc + gc1 * oc
        xl = xl + gl1 * ol
        tc = modulate(rmsnorm(xc, norm_ffn[l]), sc2, scl2).reshape(Bc * Lc, D)
        tl = modulate(rmsnorm(xl, norm_ffn[l]), sl2, sll2).reshape(Bl * Ll, D)
        y = hier_moe(jnp.concatenate([tc, tl], axis=0), moe_w_grp[l], moe_b_grp[l], moe_w_exp[l], moe_b_exp[l],
                     moe_w_gate[l], moe_w_up[l], moe_w_down[l])
        xc = xc + gc2 * y[:Bc * Lc].reshape(Bc, Lc, D)
        xl = xl + gl2 * y[Bc * Lc:].reshape(Bl, Ll, D)

    y_prompt = rmsnorm(xc, norm_out)
    y_sample = rmsnorm(xl, norm_out)
    new_attn_k = jnp.stack(ks_out, axis=1)
    new_attn_v = jnp.stack(vs_out, axis=1)
    new_gla = jnp.stack(gla_out, axis=1)
    new_rwkv = jnp.stack(rwkv_out, axis=1)
    return (y_prompt, y_sample, new_attn_k, new_attn_v, new_gla, new_rwkv)
```

```python
import functools

import jax
import jax.numpy as jnp
import numpy as np
from jax import lax
from jax.experimental import pallas as pl
from jax.experimental.pallas import tpu as pltpu

D_MODEL = 1024
DEPTH = 4
GRID_W = 64
EPS = 1e-6
GLA_HEADS = 4
GLA_DK = 64
GLA_DV = 128
GLA_LOWRANK = 16
GLA_GATE_TEMP = 16.0
GLA_CHUNK = 64
ATT_HEADS = 8
ATT_KV_HEADS = 2
ATT_GROUP = ATT_HEADS // ATT_KV_HEADS
ATT_HD = 64
WINDOW = 128
ATT_BLOCK = 128
ROPE_BASE = 10000.0
ROPE_AXIS_DIMS = ATT_HD // 2
NEG = -1e30
IN_WIDTHS = (GLA_HEADS * GLA_DK, GLA_HEADS * GLA_DK, GLA_HEADS * GLA_DV, GLA_HEADS * GLA_DV,
             ATT_HEADS * ATT_HD, ATT_KV_HEADS * ATT_HD, ATT_KV_HEADS * ATT_HD, 2 * GLA_LOWRANK)
IN_SPLITS = tuple(int(v) for v in np.cumsum(IN_WIDTHS)[:-1])
RWKV_HEAD = 64
RWKV_HEADS = D_MODEL // RWKV_HEAD
RWKV_LN_EPS = 64e-5
N_GROUPS = 4
EXP_PER_GROUP = 4
N_EXPERTS = N_GROUPS * EXP_PER_GROUP
TOP_K = 2
D_EXPERT = 512

V7X_VMEM_BYTES = 64 * 1024 * 1024
MOE_TILE_M = 256

f32 = jnp.float32
bf16 = jnp.bfloat16


def rmsnorm(x, w):
    y = x * lax.rsqrt(jnp.mean(x * x, axis=-1, keepdims=True) + EPS)
    return y * w


def ada_mod(cond, w, b):
    mod = jax.nn.silu(cond) @ w + b
    return [m[:, None, :] for m in jnp.split(mod, 6, axis=-1)]


def modulate(x, shift, scale):
    return x * (1.0 + scale) + shift


def rope_tables(row_pos, col_pos):
    freqs = ROPE_BASE ** (-jnp.arange(0, ROPE_AXIS_DIMS, 2, dtype=f32) / ROPE_AXIS_DIMS)
    ang_r = row_pos.astype(f32)[:, None] * freqs[None, :]
    ang_c = col_pos.astype(f32)[:, None] * freqs[None, :]
    return jnp.cos(ang_r), jnp.sin(ang_r), jnp.cos(ang_c), jnp.sin(ang_c)


def rotate(x, cos, sin):
    x1, x2 = jnp.split(x, 2, axis=-1)
    cos = cos[None, :, None, :]
    sin = sin[None, :, None, :]
    return jnp.concatenate([x1 * cos - x2 * sin, x1 * sin + x2 * cos], axis=-1)


def apply_axial_rope(x, tables):
    cr, sr, cc, sc = tables
    xr, xc = jnp.split(x, 2, axis=-1)
    return jnp.concatenate([rotate(xr, cr, sr), rotate(xc, cc, sc)], axis=-1)


def gla_scan(q, k, v, logdecay, s0):
    B, L, H, _ = q.shape
    n = L // GLA_CHUNK

    def to_chunks(a):
        return a.reshape(B, n, GLA_CHUNK, H, a.shape[-1]).transpose(1, 0, 3, 2, 4)

    qc, kc, vc, gc = to_chunks(q), to_chunks(k), to_chunks(v), to_chunks(logdecay)
    causal = jnp.tril(jnp.ones((GLA_CHUNK, GLA_CHUNK), bool))

    def step(s, inp):
        qi, ki, vi, gi = inp
        b = jnp.cumsum(gi, axis=-2)
        b_last = b[..., -1:, :]
        q_e = qi * jnp.exp(b)
        att = jnp.einsum('bhik,bhjk->bhij', q_e, ki * jnp.exp(-b))
        att = jnp.where(causal, att, 0.0)
        o = jnp.einsum('bhik,bhkv->bhiv', q_e, s) + jnp.einsum('bhij,bhjv->bhiv', att, vi)
        s_new = jnp.exp(b_last[..., 0, :])[..., None] * s + jnp.einsum('bhjk,bhjv->bhkv', ki * jnp.exp(b_last - b), vi)
        return s_new, o

    s_fin, o = lax.scan(step, s0, (qc, kc, vc, gc))
    o = o.transpose(1, 0, 3, 2, 4).reshape(B, L, H, v.shape[-1])
    return o, s_fin


def gla_bidir(q, k, v, ld_f, ld_b, s_f0, s_b0):
    o_f, s_f = gla_scan(q, k, v, ld_f, s_f0)
    o_b, s_b = gla_scan(q[:, ::-1], k[:, ::-1], v[:, ::-1], ld_b[:, ::-1], s_b0)
    return o_f + o_b[:, ::-1], s_f, s_b


def ctx_attn(q, k, v, sink):
    B, L = q.shape[:2]
    scale = ATT_HD ** -0.5
    qg = q.reshape(B, L, ATT_KV_HEADS, ATT_GROUP, ATT_HD)
    sink_b = sink.reshape(ATT_KV_HEADS, ATT_GROUP)[None, :, :, None, None]
    s = jnp.einsum('bqhgd,bkhd->bhgqk', qg, k) * scale
    s_sink = jnp.broadcast_to(sink_b, s.shape[:-1] + (1,))
    p = jax.nn.softmax(jnp.concatenate([s_sink, s], axis=-1), axis=-1)[..., 1:]
    return jnp.einsum('bhgqk,bkhd->bqhgd', p, v).reshape(B, L, ATT_HEADS * ATT_HD)


def window_attn_latent(q, k, v, k_ctx, v_ctx, sink):
    B, S = q.shape[:2]
    nb = S // ATT_BLOCK
    Lc = k_ctx.shape[1]
    scale = ATT_HD ** -0.5
    qb = q.reshape(B, nb, ATT_BLOCK, ATT_KV_HEADS, ATT_GROUP, ATT_HD).transpose(1, 0, 2, 3, 4, 5)
    pad = ((0, 0), (ATT_BLOCK, ATT_BLOCK), (0, 0), (0, 0))
    kp = jnp.pad(k, pad)
    vp = jnp.pad(v, pad)
    sink_b = sink.reshape(ATT_KV_HEADS, ATT_GROUP)[None, :, :, None, None]
    offs_q = jnp.arange(ATT_BLOCK)
    offs_k = jnp.arange(3 * ATT_BLOCK) - ATT_BLOCK

    def block(args):
        bi, q_blk = args
        k_band = lax.dynamic_slice_in_dim(kp, bi * ATT_BLOCK, 3 * ATT_BLOCK, axis=1)
        v_band = lax.dynamic_slice_in_dim(vp, bi * ATT_BLOCK, 3 * ATT_BLOCK, axis=1)
        qpos = bi * ATT_BLOCK + offs_q
        kpos = bi * ATT_BLOCK + offs_k
        valid = (jnp.abs(qpos[:, None] - kpos[None, :]) <= WINDOW) & (kpos >= 0)[None, :] & (kpos < S)[None, :]
        s_lat = jnp.einsum('bqhgd,bkhd->bhgqk', q_blk, k_band) * scale
        s_lat = jnp.where(valid, s_lat, NEG)
        s_ctx = jnp.einsum('bqhgd,bchd->bhgqc', q_blk, k_ctx) * scale
        s_sink = jnp.broadcast_to(sink_b, s_ctx.shape[:-1] + (1,))
        p = jax.nn.softmax(jnp.concatenate([s_sink, s_ctx, s_lat], axis=-1), axis=-1)
        o = (jnp.einsum('bhgqc,bchd->bqhgd', p[..., 1:1 + Lc], v_ctx)
             + jnp.einsum('bhgqk,bkhd->bqhgd', p[..., 1 + Lc:], v_band))
        return o.reshape(B, ATT_BLOCK, ATT_HEADS * ATT_HD)

    o = lax.map(block, (jnp.arange(nb), qb))
    return o.transpose(1, 0, 2, 3).reshape(B, S, ATT_HEADS * ATT_HD)


def even_projections(h, w_in, dec_w, dec_b):
    B, L, _ = h.shape
    gq, gk, gv, gg, aq, ak, av, lr = jnp.split(h @ w_in, IN_SPLITS, axis=-1)
    gq = gq.reshape(B, L, GLA_HEADS, GLA_DK) * (GLA_DK ** -0.5)
    gk = gk.reshape(B, L, GLA_HEADS, GLA_DK)
    gv = gv.reshape(B, L, GLA_HEADS, GLA_DV)
    lr_f, lr_b = jnp.split(lr, 2, axis=-1)

    def logdecay(lr_d, w, b):
        return (jax.nn.log_sigmoid(lr_d @ w + b) / GLA_GATE_TEMP).reshape(B, L, GLA_HEADS, GLA_DK)

    ld_f = logdecay(lr_f, dec_w[0], dec_b[0])
    ld_b = logdecay(lr_b, dec_w[1], dec_b[1])
    aq = aq.reshape(B, L, ATT_HEADS, ATT_HD)
    ak = ak.reshape(B, L, ATT_KV_HEADS, ATT_HD)
    av = av.reshape(B, L, ATT_KV_HEADS, ATT_HD)
    return gq, gk, gv, gg, ld_f, ld_b, aq, ak, av


def even_output(o_gla, gg, o_att, gla_norm, w_out):
    B, L = o_gla.shape[:2]
    o_gla = rmsnorm(o_gla, gla_norm).reshape(B, L, GLA_HEADS * GLA_DV) * jax.nn.silu(gg)
    return jnp.concatenate([o_gla, o_att], axis=-1) @ w_out


def even_mixer_ctx(h, w_in, w_out, dec_w, dec_b, gla_norm, sink):
    gq, gk, gv, gg, ld_f, ld_b, aq, ak, av = even_projections(h, w_in, dec_w, dec_b)
    zero = jnp.zeros((h.shape[0], GLA_HEADS, GLA_DK, GLA_DV), f32)
    o_gla, s_f, s_b = gla_bidir(gq, gk, gv, ld_f, ld_b, zero, zero)
    o_att = ctx_attn(aq, ak, av, sink)
    return even_output(o_gla, gg, o_att, gla_norm, w_out), ak, av, s_f, s_b


def even_mixer_lat(h, w_in, w_out, dec_w, dec_b, gla_norm, sink, rope, k_ctx, v_ctx, s_f0, s_b0):
    gq, gk, gv, gg, ld_f, ld_b, aq, ak, av = even_projections(h, w_in, dec_w, dec_b)
    o_gla, _, _ = gla_bidir(gq, gk, gv, ld_f, ld_b, s_f0, s_b0)
    o_att = window_attn_latent(apply_axial_rope(aq, rope), apply_axial_rope(ak, rope), av, k_ctx, v_ctx, sink)
    return even_output(o_gla, gg, o_att, gla_norm, w_out)


def centred_shift(x):
    xp = jnp.pad(x, ((0, 0), (1, 1), (0, 0)))
    return 0.5 * (xp[:, :-2] + xp[:, 2:])


RWKV_SUB = 64
LANES = 128
SUBLANES = 8
HEAD_PAIRS = D_MODEL // LANES


def _rwkv_scan_kernel(rf, vf, af, wf, kf, bf_, rb, vb, ab, wb, kb, bb, s0, of, ob, s, acc, *, tc):
    j = pl.program_id(1)

    @pl.when(j == 0)
    def _():
        s[...] = s0[...]

    row_head = lax.broadcasted_iota(jnp.int32, (2 * LANES, 2 * LANES), 0) // RWKV_HEAD
    col_head = lax.broadcasted_iota(jnp.int32, (2 * LANES, 2 * LANES), 1) // RWKV_HEAD
    ones_bd = (row_head == col_head).astype(bf16)
    sub = lax.broadcasted_iota(jnp.int32, (RWKV_HEAD, LANES), 0)
    lane_in_head = lax.broadcasted_iota(jnp.int32, (RWKV_HEAD, LANES), 1) % RWKV_HEAD
    eye = (sub == lane_in_head).astype(bf16)
    dirs = ((rf, vf, af, wf, kf, bf_), (rb, vb, ab, wb, kb, bb))

    def row_sums(x):
        return jnp.dot(x, ones_bd, preferred_element_type=f32)

    def row(ref, t, hp, reps):
        return jnp.broadcast_to(ref[t, pl.ds(hp, 1), :], (reps, LANES))

    def sub_chunk(sc, carry):
        def step(tt, carry):
            t_f = sc * RWKV_SUB + tt
            pos = ((t_f, jnp.maximum(t_f - 1, 0), tt - 1),
                   (tc - 1 - t_f, jnp.minimum(tc - t_f, tc - 1), RWKV_SUB - tt))
            for d in range(2):
                r_ref, v_ref, a_ref, w_ref, k_ref, b_ref = dirs[d]
                t_now, t_prev, out_lane = pos[d]
                lhs, tiles = [], []
                for hp2 in range(0, HEAD_PAIRS, 2):
                    v_diag = []
                    for hp in (hp2, hp2 + 1):
                        s_t = s[d, :, hp * LANES:(hp + 1) * LANES]
                        lhs.append(jnp.concatenate([(s_t * row(a_ref, t_now, hp, RWKV_HEAD)).astype(bf16),
                                                    (s_t * row(r_ref, t_prev, hp, RWKV_HEAD)).astype(bf16)], axis=1))
                        v_diag.append(eye * jnp.concatenate([row(v_ref, t_now, hp, 16).astype(bf16)] * 4, axis=0))
                        tiles.append(s_t)
                    lhs.append(jnp.concatenate(v_diag, axis=1))
                res = row_sums(jnp.concatenate(lhs, axis=0))
                for hp, s_t in enumerate(tiles):
                    ls = slice(hp * LANES, (hp + 1) * LANES)
                    base = (hp // 2) * 3 * RWKV_HEAD
                    own = res[base + (hp % 2) * RWKV_HEAD:base + (hp % 2 + 1) * RWKV_HEAD]
                    sa, out_prev = own[:, :LANES], own[:, LANES:]
                    v_col = res[base + 2 * RWKV_HEAD:base + 3 * RWKV_HEAD, (hp % 2) * LANES:(hp % 2 + 1) * LANES]
                    s[d, :, ls] = (s_t * row(w_ref, t_now, hp, RWKV_HEAD) + sa * row(b_ref, t_now, hp, RWKV_HEAD)
                                   + v_col * row(k_ref, t_now, hp, RWKV_HEAD))
                    acc[d, :, ls] = jnp.where(lane_in_head == out_lane, out_prev, acc[d, :, ls])
            return carry

        lax.fori_loop(0, RWKV_SUB, step, 0, unroll=4)
        t_last = sc * RWKV_SUB + RWKV_SUB - 1
        last = ((t_last, RWKV_SUB - 1), (tc - 1 - t_last, 0))
        for d in range(2):
            r_ref = dirs[d][0]
            t_now, out_lane = last[d]
            out_mask = jnp.concatenate([lane_in_head == out_lane] * 2, axis=1)
            for hp in range(0, HEAD_PAIRS, 2):
                ls = slice(hp * LANES, (hp + 2) * LANES)
                r_t = jnp.concatenate([row(r_ref, t_now, hp, RWKV_HEAD), row(r_ref, t_now, hp + 1, RWKV_HEAD)], axis=1)
                out_t = row_sums((s[d, :, ls] * r_t).astype(bf16))
                acc[d, :, ls] = jnp.where(out_mask, out_t, acc[d, :, ls])
        of[sc] = acc[0]
        ob[tc // RWKV_SUB - 1 - sc] = acc[1]
        return carry

    lax.fori_loop(0, tc // RWKV_SUB, sub_chunk, 0)


def rwkv_scan_bidir(r, v, nkk, w_f, k_f, b_f, w_b, k_b, b_b, s0):
    B, L, D = r.shape
    tc = min(L, 256)
    n = L // tc
    fwd = pl.BlockSpec((None, tc, HEAD_PAIRS, LANES), lambda b, j: (b, j, 0, 0))
    bwd = pl.BlockSpec((None, tc, HEAD_PAIRS, LANES), lambda b, j: (b, n - 1 - j, 0, 0))
    r, v, nkk, w_f, k_f, b_f, w_b, k_b, b_b = (t.reshape(B, L, HEAD_PAIRS, LANES)
                                               for t in (r, v, nkk, w_f, k_f, b_f, w_b, k_b, b_b))
    state = pl.BlockSpec((None, 2, RWKV_HEAD, D), lambda b, j: (b, 0, 0, 0))
    nsub = tc // RWKV_SUB
    block_bytes = tc * D * 4
    o_f, o_b, s_fin = pl.pallas_call(
        functools.partial(_rwkv_scan_kernel, tc=tc),
        grid=(B, n),
        in_specs=[fwd] * 6 + [bwd] * 6 + [state],
        out_specs=[pl.BlockSpec((None, nsub, RWKV_HEAD, D), lambda b, j: (b, j, 0, 0)),
                   pl.BlockSpec((None, nsub, RWKV_HEAD, D), lambda b, j: (b, n - 1 - j, 0, 0)),
                   state],
        out_shape=[jax.ShapeDtypeStruct((B, L // RWKV_SUB, RWKV_HEAD, D), f32),
                   jax.ShapeDtypeStruct((B, L // RWKV_SUB, RWKV_HEAD, D), f32),
                   jax.ShapeDtypeStruct((B, 2, RWKV_HEAD, D), f32)],
        scratch_shapes=[pltpu.VMEM((2, RWKV_HEAD, D), f32)],
        compiler_params=pltpu.CompilerParams(
            dimension_semantics=("arbitrary", "arbitrary"),
            vmem_limit_bytes=min(V7X_VMEM_BYTES, 2 * 14 * block_bytes + (8 << 20)),
        ),
        name="rwkv_scan",
    )(r, v, nkk, w_f, k_f, b_f, r, v, nkk, w_b, k_b, b_b, s0)
    o = (o_f + o_b).reshape(B, L // RWKV_SUB, RWKV_HEAD, RWKV_HEADS, RWKV_SUB)
    return o.transpose(0, 1, 4, 3, 2).reshape(B, L, D), s_fin


def rwkv_mix(h, mu, wr, wk, wv, wo, w0, w1, w2, a0, a1, a2, g1, g2, k_k, k_a, r_k, lnx_w, lnx_b, s_f0, s_b0):
    B, L, D = h.shape
    H, N = RWKV_HEADS, RWKV_HEAD
    xx = centred_shift(h) - h
    xr, xw, xk, xv, xa, xg = [h + xx * mu[j] for j in range(6)]
    r = (xr @ wr).reshape(B, L, H, N)
    k = (xk @ wk).reshape(B, L, H, N)
    v = (xv @ wv).reshape(B, L, H, N)
    g = jax.nn.sigmoid(xg @ g1) @ g2
    kk = k * k_k.reshape(H, N)
    kk = kk * lax.rsqrt(jnp.sum(kk * kk, axis=-1, keepdims=True) + 1e-12)

    def direction(d):
        w_raw = -jax.nn.softplus(-(w0[d] + jnp.tanh(xw @ w1[d]) @ w2[d])) - 0.5
        decay = jnp.exp(-jnp.exp(w_raw))
        a = jax.nn.sigmoid(a0[d] + (xa @ a1[d]) @ a2[d]).reshape(B, L, H, N)
        k_d = k * (1.0 + (a - 1.0) * k_a.reshape(H, N))
        return decay, k_d, kk * a

    w_f, k_f, b_f = direction(0)
    w_b, k_b, b_b = direction(1)

    def rows(t):
        return t.reshape(B, L, D)

    def to_kernel_state(s0):
        return s0.transpose(0, 2, 1, 3).reshape(B, N, D)

    o, s_fin = rwkv_scan_bidir(rows(r), rows(v), rows(-kk), w_f, rows(k_f), rows(b_f), w_b, rows(k_b), rows(b_b),
                               jnp.stack([to_kernel_state(s_f0), to_kernel_state(s_b0)], axis=1))
    s_fin = s_fin.reshape(B, 2, N, H, N).transpose(0, 1, 3, 2, 4)
    s_f, s_b = s_fin[:, 0], s_fin[:, 1]
    o = o.reshape(B, L, H, N)
    m = jnp.mean(o, axis=-1, keepdims=True)
    var = jnp.mean(jnp.square(o - m), axis=-1, keepdims=True)
    o = (o - m) * lax.rsqrt(var + RWKV_LN_EPS) * lnx_w.reshape(H, N) + lnx_b.reshape(H, N)
    bonus = jnp.sum(r * 0.5 * (k_f + k_b) * r_k, axis=-1, keepdims=True) * v
    o = (o + bonus).reshape(B, L, D)
    return (o * g) @ wo, s_f, s_b


GROUP_HID = EXP_PER_GROUP * D_EXPERT


def _moe_group_kernel(tile_group_ref, n_tiles_ref, xs_ref, gates_ref, wg_ref, wu_ref, wd_ref, ys_ref,
                      wg_bf, wu_bf, wd_bf):
    i = pl.program_id(0)
    prev_group = tile_group_ref[jnp.maximum(i - 1, 0)]
    group_changed = jnp.logical_or(i == 0, tile_group_ref[i] != prev_group)

    @pl.when(group_changed)
    def _():
        for e in range(EXP_PER_GROUP):
            hs = slice(e * D_EXPERT, (e + 1) * D_EXPERT)
            wg_bf[:, hs] = wg_ref[e].astype(bf16)
            wu_bf[:, hs] = wu_ref[e].astype(bf16)
            wd_bf[hs, :] = wd_ref[e].astype(bf16)

    @pl.when(i < n_tiles_ref[0])
    def _():
        x = xs_ref[...].astype(bf16)
        g = jnp.dot(x, wg_bf[...], preferred_element_type=f32)
        u = jnp.dot(x, wu_bf[...], preferred_element_type=f32)
        gate = jnp.concatenate([jnp.broadcast_to(gates_ref[:, e:e + 1], (MOE_TILE_M, D_EXPERT))
                                for e in range(EXP_PER_GROUP)], axis=1)
        hid = (g * jax.nn.sigmoid(g)) * u * gate
        ys_ref[...] = jnp.dot(hid.astype(bf16), wd_bf[...], preferred_element_type=f32)

    @pl.when(i >= n_tiles_ref[0])
    def _():
        ys_ref[...] = jnp.zeros_like(ys_ref)


def moe_group_experts(xs, gates, tile_group, n_tiles, w_gate, w_up, w_down):
    P, D = xs.shape
    max_tiles = P // MOE_TILE_M
    weight_bytes = 3 * EXP_PER_GROUP * D * D_EXPERT * (4 + 2)
    tile_bytes = 2 * MOE_TILE_M * (2 * D * 4 + LANES * 4) + 4 * MOE_TILE_M * GROUP_HID * 4
    once = pl.Buffered(1)
    grid_spec = pltpu.PrefetchScalarGridSpec(
        num_scalar_prefetch=2,
        grid=(max_tiles,),
        in_specs=[
            pl.BlockSpec((MOE_TILE_M, D), lambda i, tg, nt: (i, 0)),
            pl.BlockSpec((MOE_TILE_M, EXP_PER_GROUP), lambda i, tg, nt: (i, 0)),
            pl.BlockSpec((EXP_PER_GROUP, D, D_EXPERT), lambda i, tg, nt: (tg[i], 0, 0), pipeline_mode=once),
            pl.BlockSpec((EXP_PER_GROUP, D, D_EXPERT), lambda i, tg, nt: (tg[i], 0, 0), pipeline_mode=once),
            pl.BlockSpec((EXP_PER_GROUP, D_EXPERT, D), lambda i, tg, nt: (tg[i], 0, 0), pipeline_mode=once),
        ],
        out_specs=pl.BlockSpec((MOE_TILE_M, D), lambda i, tg, nt: (i, 0)),
        scratch_shapes=[
            pltpu.VMEM((D, GROUP_HID), bf16),
            pltpu.VMEM((D, GROUP_HID), bf16),
            pltpu.VMEM((GROUP_HID, D), bf16),
        ],
    )
    return pl.pallas_call(
        _moe_group_kernel,
        grid_spec=grid_spec,
        out_shape=jax.ShapeDtypeStruct((P, D), f32),
        compiler_params=pltpu.CompilerParams(
            dimension_semantics=("arbitrary",),
            vmem_limit_bytes=min(V7X_VMEM_BYTES - (4 << 20), weight_bytes + tile_bytes + (8 << 20)),
        ),
        name="moe_group_experts",
    )(tile_group, n_tiles, xs, gates, w_gate, w_up, w_down)


def hier_moe(t, w_grp, b_grp, w_exp, b_exp, w_gate, w_up, w_down):
    T, D = t.shape
    logits = jnp.dot(t, jnp.concatenate([w_grp, w_exp], axis=1), precision=lax.Precision.HIGHEST)
    grp_logits = logits[:, :N_GROUPS] + b_grp
    grp_prob = jax.nn.softmax(grp_logits, axis=-1)
    g_top = jnp.argmax(grp_logits, axis=-1).astype(jnp.int32)
    in_group = g_top[:, None] == jnp.arange(N_GROUPS, dtype=jnp.int32)[None, :]
    p_g = jnp.sum(jnp.where(in_group, grp_prob, 0.0), axis=1, keepdims=True)
    exp_logits = (logits[:, N_GROUPS:] + b_exp).reshape(-1, N_GROUPS, EXP_PER_GROUP)
    sel = jnp.sum(jnp.where(in_group[:, :, None], exp_logits, 0.0), axis=1)
    top_v, top_i = lax.top_k(sel, TOP_K)
    wts = p_g * jax.nn.softmax(top_v, axis=-1)
    gates = jnp.sum(jax.nn.one_hot(top_i, EXP_PER_GROUP, dtype=f32) * wts[..., None], axis=1)

    order = jnp.argsort(g_top, stable=True).astype(jnp.int32)
    grp_i32 = in_group.astype(jnp.int32)
    counts = jnp.sum(grp_i32, axis=0)
    rank = jnp.sum(jnp.where(in_group, jnp.cumsum(grp_i32, axis=0), 0), axis=1) - 1
    tiles_per = (counts + MOE_TILE_M - 1) // MOE_TILE_M
    tile_end = jnp.cumsum(tiles_per)
    start_padded = (tile_end - tiles_per) * MOE_TILE_M
    start_sorted = jnp.cumsum(counts) - counts
    max_tiles = T // MOE_TILE_M + N_GROUPS
    n_tiles = tile_end[-1:].astype(jnp.int32)
    tile_ids = jnp.arange(max_tiles, dtype=jnp.int32)
    tile_group = jnp.minimum(jnp.sum((tile_ids[:, None] >= tile_end[None, :]).astype(jnp.int32), axis=1),
                             N_GROUPS - 1).astype(jnp.int32)
    tile_group = jnp.where(tile_ids < n_tiles[0], tile_group, tile_group[jnp.maximum(n_tiles[0] - 1, 0)])
    row_group = jnp.repeat(tile_group, MOE_TILE_M)
    row_in_group = jnp.arange(max_tiles * MOE_TILE_M, dtype=jnp.int32) - start_padded[row_group]
    row_valid = (row_in_group >= 0) & (row_in_group < counts[row_group])
    row_tok = jnp.where(row_valid, order[jnp.clip(start_sorted[row_group] + row_in_group, 0, T - 1)], 0)
    pos = start_padded[g_top] + rank

    xs = jnp.take(t, row_tok, axis=0)
    row_gates = jnp.where(row_valid[:, None], jnp.take(gates, row_tok, axis=0), 0.0)
    ys = moe_group_experts(xs, row_gates, tile_group, n_tiles, w_gate, w_up, w_down)
    return jnp.take(ys, pos, axis=0)


def kernel(x_prompt, x_sample, c, cache_attn_k, cache_attn_v, state_gla, state_rwkv, c_ctx, ada_w, ada_b, norm_mix, norm_ffn, norm_out, ev_w_in, ev_w_out, gla_dec_w, gla_dec_b, gla_norm, att_sink, rw_mu, rw_wr, rw_wk, rw_wv, rw_wo, rw_w0, rw_w1, rw_w2, rw_a0, rw_a1, rw_a2, rw_g1, rw_g2, rw_kk, rw_ka, rw_rk, rw_lnx_w, rw_lnx_b, moe_w_grp, moe_b_grp, moe_w_exp, moe_b_exp, moe_w_gate, moe_w_up, moe_w_down):
    n_lat = x_sample.shape[1]
    rows = n_lat // GRID_W
    row_pos = jnp.repeat(jnp.arange(rows), GRID_W)
    col_pos = jnp.tile(jnp.arange(GRID_W), rows)
    rope = rope_tables(row_pos, col_pos)
    Bc, Lc, D = x_prompt.shape
    Bl, Ll, _ = x_sample.shape

    xc, xl = x_prompt, x_sample
    cond_ctx = c_ctx[None, :]
    ks_out, vs_out, gla_out, rwkv_out = [], [], [], []
    for l in range(DEPTH):
        sc1, scl1, gc1, sc2, scl2, gc2 = ada_mod(cond_ctx, ada_w[l], ada_b[l])
        sl1, sll1, gl1, sl2, sll2, gl2 = ada_mod(c, ada_w[l], ada_b[l])
        hc = modulate(rmsnorm(xc, norm_mix[l]), sc1, scl1)
        hl = modulate(rmsnorm(xl, norm_mix[l]), sl1, sll1)
        i = l // 2
        if l % 2 == 0:
            oc, k_c, v_c, s_f, s_b = even_mixer_ctx(hc, ev_w_in[i], ev_w_out[i], gla_dec_w[i], gla_dec_b[i],
                                                    gla_norm[i], att_sink[i])
            ol = even_mixer_lat(hl, ev_w_in[i], ev_w_out[i], gla_dec_w[i], gla_dec_b[i], gla_norm[i], att_sink[i],
                                rope, cache_attn_k[:, i], cache_attn_v[:, i], state_gla[:, i, 0], state_gla[:, i, 1])
            ks_out.append(k_c)
            vs_out.append(v_c)
            gla_out.append(jnp.stack([s_f, s_b], axis=1))
        else:
            rw = (rw_mu[i], rw_wr[i], rw_wk[i], rw_wv[i], rw_wo[i], rw_w0[i], rw_w1[i], rw_w2[i], rw_a0[i],
                  rw_a1[i], rw_a2[i], rw_g1[i], rw_g2[i], rw_kk[i], rw_ka[i], rw_rk[i], rw_lnx_w[i], rw_lnx_b[i])
            zero = jnp.zeros((Bc, RWKV_HEADS, RWKV_HEAD, RWKV_HEAD), f32)
            oc, s_f, s_b = rwkv_mix(hc, *rw, zero, zero)
            ol, _, _ = rwkv_mix(hl, *rw, state_rwkv[:, i, 0], state_rwkv[:, i, 1])
            rwkv_out.append(jnp.stack([s_f, s_b], axis=1))
        xc = xc + gc1 * oc
        xl = xl + gl1 * ol
        tc = modulate(rmsnorm(xc, norm_ffn[l]), sc2, scl2).reshape(Bc * Lc, D)
        tl = modulate(rmsnorm(xl, norm_ffn[l]), sl2, sll2).reshape(Bl * Ll, D)
        y = hier_moe(jnp.concatenate([tc, tl], axis=0), moe_w_grp[l], moe_b_grp[l], moe_w_exp[l], moe_b_exp[l],
                     moe_w_gate[l], moe_w_up[l], moe_w_down[l])
        xc = xc + gc2 * y[:Bc * Lc].reshape(Bc, Lc, D)
        xl = xl + gl2 * y[Bc * Lc:].reshape(Bl, Ll, D)

    y_prompt = rmsnorm(xc, norm_out)
    y_sample = rmsnorm(xl, norm_out)
    new_attn_k = jnp.stack(ks_out, axis=1)
    new_attn_v = jnp.stack(vs_out, axis=1)
    new_gla = jnp.stack(gla_out, axis=1)
    new_rwkv = jnp.stack(rwkv_out, axis=1)
    return (y_prompt, y_sample, new_attn_k, new_attn_v, new_gla, new_rwkv)
```

```python
import functools

import jax
import jax.numpy as jnp
import numpy as np
from jax import lax
from jax.experimental import pallas as pl
from jax.experimental.pallas import tpu as pltpu

D_MODEL = 1024
DEPTH = 4
GRID_W = 64
EPS = 1e-6
GLA_HEADS = 4
GLA_DK = 64
GLA_DV = 128
GLA_LOWRANK = 16
GLA_GATE_TEMP = 16.0
GLA_CHUNK = 64
ATT_HEADS = 8
ATT_KV_HEADS = 2
ATT_GROUP = ATT_HEADS // ATT_KV_HEADS
ATT_HD = 64
WINDOW = 128
ATT_BLOCK = 128
ROPE_BASE = 10000.0
ROPE_AXIS_DIMS = ATT_HD // 2
NEG = -1e30
IN_WIDTHS = (GLA_HEADS * GLA_DK, GLA_HEADS * GLA_DK, GLA_HEADS * GLA_DV, GLA_HEADS * GLA_DV,
             ATT_HEADS * ATT_HD, ATT_KV_HEADS * ATT_HD, ATT_KV_HEADS * ATT_HD, 2 * GLA_LOWRANK)
IN_SPLITS = tuple(int(v) for v in np.cumsum(IN_WIDTHS)[:-1])
RWKV_HEAD = 64
RWKV_HEADS = D_MODEL // RWKV_HEAD
RWKV_LN_EPS = 64e-5
N_GROUPS = 4
EXP_PER_GROUP = 4
N_EXPERTS = N_GROUPS * EXP_PER_GROUP
TOP_K = 2
D_EXPERT = 512

V7X_VMEM_BYTES = 64 * 1024 * 1024
MOE_TILE_M = 256

f32 = jnp.float32
bf16 = jnp.bfloat16


def rmsnorm(x, w):
    y = x * lax.rsqrt(jnp.mean(x * x, axis=-1, keepdims=True) + EPS)
    return y * w


def ada_mod(cond, w, b):
    mod = jax.nn.silu(cond) @ w + b
    return [m[:, None, :] for m in jnp.split(mod, 6, axis=-1)]


def modulate(x, shift, scale):
    return x * (1.0 + scale) + shift


def rope_tables(row_pos, col_pos):
    freqs = ROPE_BASE ** (-jnp.arange(0, ROPE_AXIS_DIMS, 2, dtype=f32) / ROPE_AXIS_DIMS)
    ang_r = row_pos.astype(f32)[:, None] * freqs[None, :]
    ang_c = col_pos.astype(f32)[:, None] * freqs[None, :]
    return jnp.cos(ang_r), jnp.sin(ang_r), jnp.cos(ang_c), jnp.sin(ang_c)


def rotate(x, cos, sin):
    x1, x2 = jnp.split(x, 2, axis=-1)
    cos = cos[None, :, None, :]
    sin = sin[None, :, None, :]
    return jnp.concatenate([x1 * cos - x2 * sin, x1 * sin + x2 * cos], axis=-1)


def apply_axial_rope(x, tables):
    cr, sr, cc, sc = tables
    xr, xc = jnp.split(x, 2, axis=-1)
    return jnp.concatenate([rotate(xr, cr, sr), rotate(xc, cc, sc)], axis=-1)


GLA_QK = GLA_HEADS * GLA_DK
GLA_V = GLA_HEADS * GLA_DV


def _gla_kernel(q_ref, v_ref, gg_ref, ldf_ref, ldb_ref, kt_ref, ldft_ref, ldbt_ref, s0_ref, norm_ref,
                o_ref, s_ref, *, seq_len):
    n = seq_len // GLA_CHUNK
    C = GLA_CHUNK
    ti = lax.broadcasted_iota(jnp.int32, (C, C), 0)
    tj = lax.broadcasted_iota(jnp.int32, (C, C), 1)
    keep = (tj <= ti, tj >= ti)
    tri = tuple(m.astype(bf16) for m in keep)
    tri_t = (tri[1], tri[0])
    ones = jnp.ones((C, GLA_DV), bf16)
    s_ref[...] = s0_ref[...]
    o_ref[...] = jnp.zeros_like(o_ref)
    ld_refs = ((ldf_ref, ldft_ref), (ldb_ref, ldbt_ref))

    def split3(x):
        hi = x.astype(bf16)
        rest = x - hi.astype(f32)
        mid = rest.astype(bf16)
        return hi, mid, (rest - mid.astype(f32)).astype(bf16)

    def sums(a, b):
        if isinstance(b, tuple):
            return sum(jnp.dot(a, p, preferred_element_type=f32) for p in b)
        return sum(jnp.dot(p, b, preferred_element_type=f32) for p in a)

    def chunk(c, carry):
        pending = []
        for d in range(2):
            cc = c if d == 0 else n - 1 - c
            rows = pl.ds(pl.multiple_of(cc * C, C), C)
            ld_ref, ldt_ref = ld_refs[d]
            ld_t3 = split3(ldt_ref[cc])
            b_all = sums(tri[d], split3(ld_ref[rows, :]))
            bt_all = sums(ld_t3, tri_t[d])
            total_all = sums(ld_t3, ones)
            for h in range(GLA_HEADS):
                ks = slice(h * GLA_DK, (h + 1) * GLA_DK)
                vs = slice(h * GLA_DV, (h + 1) * GLA_DV)
                q = q_ref[rows, ks] * (GLA_DK ** -0.5)
                v = v_ref[rows, vs].astype(bf16)
                k_t = kt_ref[cc, ks, :]
                b, b_t, total = b_all[:, ks], bt_all[ks], total_all[ks]
                q_e = (q * jnp.exp(b)).astype(bf16)
                k_e = (k_t * jnp.exp(-b_t)).astype(bf16)
                k_s = (k_t * jnp.exp(total[:, :C] - b_t)).astype(bf16)
                att = jnp.where(keep[d], jnp.dot(q_e, k_e, preferred_element_type=f32), 0.0).astype(bf16)
                s = s_ref[d, h]
                o = (jnp.dot(q_e, s.astype(bf16), preferred_element_type=f32)
                     + jnp.dot(att, v, preferred_element_type=f32))
                s_new = s * jnp.exp(total) + jnp.dot(k_s, v, preferred_element_type=f32)
                pending.append((d, h, rows, vs, o_ref[rows, vs] + o, s_new))
        for d, h, rows, vs, o, s_new in pending:
            o_ref[rows, vs] = o
            s_ref[d, h] = s_new
        return carry

    lax.fori_loop(0, n, chunk, 0)
    for h in range(GLA_HEADS):
        vs = slice(h * GLA_DV, (h + 1) * GLA_DV)
        x = o_ref[:, vs]
        y = x * lax.rsqrt(jnp.mean(x * x, axis=-1, keepdims=True) + EPS) * norm_ref[...]
        g = gg_ref[:, vs]
        o_ref[:, vs] = y * (g * jax.nn.sigmoid(g))


def gla_bidir_gated(z, ld_f, ld_b, s0, gla_norm):
    B, L, _ = z.shape
    n = L // GLA_CHUNK
    assert n % 2 == 0

    def per_chunk_t(a):
        return a.reshape(B, n, GLA_CHUNK, GLA_QK).transpose(0, 1, 3, 2)

    state = pl.BlockSpec((None, 2, GLA_HEADS, GLA_DK, GLA_DV), lambda b: (b, 0, 0, 0, 0))
    ld = pl.BlockSpec((None, L, GLA_QK), lambda b: (b, 0, 0))
    ld_t = pl.BlockSpec((None, n, GLA_QK, GLA_CHUNK), lambda b: (b, 0, 0, 0))
    return pl.pallas_call(
        functools.partial(_gla_kernel, seq_len=L),
        grid=(B,),
        in_specs=[pl.BlockSpec((None, L, GLA_QK), lambda b: (b, 0, 0)),
                  pl.BlockSpec((None, L, GLA_V), lambda b: (b, 0, 1)),
                  pl.BlockSpec((None, L, GLA_V), lambda b: (b, 0, 2)),
                  ld, ld, ld_t, ld_t, ld_t, state,
                  pl.BlockSpec((1, GLA_DV), lambda b: (0, 0))],
        out_specs=[pl.BlockSpec((None, L, GLA_V), lambda b: (b, 0, 0)), state],
        out_shape=[jax.ShapeDtypeStruct((B, L, GLA_V), f32),
                   jax.ShapeDtypeStruct((B, 2, GLA_HEADS, GLA_DK, GLA_DV), f32)],
        compiler_params=pltpu.CompilerParams(
            dimension_semantics=("arbitrary",),
            vmem_limit_bytes=min(V7X_VMEM_BYTES, 2 * L * (9 * GLA_QK + 3 * GLA_V) * 4 + (16 << 20)),
        ),
        name="gla_bidir",
    )(z, z, z, ld_f, ld_b, per_chunk_t(z[..., GLA_QK:2 * GLA_QK]), per_chunk_t(ld_f), per_chunk_t(ld_b),
      s0, gla_norm.reshape(1, GLA_DV))


def ctx_attn(q, k, v, sink):
    B, L = q.shape[:2]
    scale = ATT_HD ** -0.5
    qg = q.reshape(B, L, ATT_KV_HEADS, ATT_GROUP, ATT_HD)
    sink_b = sink.reshape(ATT_KV_HEADS, ATT_GROUP)[None, :, :, None, None]
    s = jnp.einsum('bqhgd,bkhd->bhgqk', qg, k) * scale
    s_sink = jnp.broadcast_to(sink_b, s.shape[:-1] + (1,))
    p = jax.nn.softmax(jnp.concatenate([s_sink, s], axis=-1), axis=-1)[..., 1:]
    return jnp.einsum('bhgqk,bkhd->bqhgd', p, v).reshape(B, L, ATT_HEADS * ATT_HD)


def window_attn_latent(q, k, v, k_ctx, v_ctx, sink):
    B, S = q.shape[:2]
    nb = S // ATT_BLOCK
    Lc = k_ctx.shape[1]
    scale = ATT_HD ** -0.5
    qb = q.reshape(B, nb, ATT_BLOCK, ATT_KV_HEADS, ATT_GROUP, ATT_HD).transpose(1, 0, 2, 3, 4, 5)
    pad = ((0, 0), (ATT_BLOCK, ATT_BLOCK), (0, 0), (0, 0))
    kp = jnp.pad(k, pad)
    vp = jnp.pad(v, pad)
    sink_b = sink.reshape(ATT_KV_HEADS, ATT_GROUP)[None, :, :, None, None]
    offs_q = jnp.arange(ATT_BLOCK)
    offs_k = jnp.arange(3 * ATT_BLOCK) - ATT_BLOCK

    def block(args):
        bi, q_blk = args
        k_band = lax.dynamic_slice_in_dim(kp, bi * ATT_BLOCK, 3 * ATT_BLOCK, axis=1)
        v_band = lax.dynamic_slice_in_dim(vp, bi * ATT_BLOCK, 3 * ATT_BLOCK, axis=1)
        qpos = bi * ATT_BLOCK + offs_q
        kpos = bi * ATT_BLOCK + offs_k
        valid = (jnp.abs(qpos[:, None] - kpos[None, :]) <= WINDOW) & (kpos >= 0)[None, :] & (kpos < S)[None, :]
        s_lat = jnp.einsum('bqhgd,bkhd->bhgqk', q_blk, k_band) * scale
        s_lat = jnp.where(valid, s_lat, NEG)
        s_ctx = jnp.einsum('bqhgd,bchd->bhgqc', q_blk, k_ctx) * scale
        s_sink = jnp.broadcast_to(sink_b, s_ctx.shape[:-1] + (1,))
        p = jax.nn.softmax(jnp.concatenate([s_sink, s_ctx, s_lat], axis=-1), axis=-1)
        o = (jnp.einsum('bhgqc,bchd->bqhgd', p[..., 1:1 + Lc], v_ctx)
             + jnp.einsum('bhgqk,bkhd->bqhgd', p[..., 1 + Lc:], v_band))
        return o.reshape(B, ATT_BLOCK, ATT_HEADS * ATT_HD)

    o = lax.map(block, (jnp.arange(nb), qb))
    return o.transpose(1, 0, 2, 3).reshape(B, S, ATT_HEADS * ATT_HD)


def even_projections(h, w_in, dec_w, dec_b):
    B, L, _ = h.shape
    z = h @ w_in
    aq, ak, av, lr = jnp.split(z[..., IN_SPLITS[3]:], [s - IN_SPLITS[3] for s in IN_SPLITS[4:]], axis=-1)
    lr_f, lr_b = jnp.split(lr, 2, axis=-1)

    def logdecay(lr_d, w, b):
        return jax.nn.log_sigmoid(lr_d @ w + b) / GLA_GATE_TEMP

    ld_f = logdecay(lr_f, dec_w[0], dec_b[0])
    ld_b = logdecay(lr_b, dec_w[1], dec_b[1])
    aq = aq.reshape(B, L, ATT_HEADS, ATT_HD)
    ak = ak.reshape(B, L, ATT_KV_HEADS, ATT_HD)
    av = av.reshape(B, L, ATT_KV_HEADS, ATT_HD)
    return z, ld_f, ld_b, aq, ak, av


def even_mixer_ctx(h, w_in, w_out, dec_w, dec_b, gla_norm, sink):
    z, ld_f, ld_b, aq, ak, av = even_projections(h, w_in, dec_w, dec_b)
    zero = jnp.zeros((h.shape[0], 2, GLA_HEADS, GLA_DK, GLA_DV), f32)
    o_gla, s_fin = gla_bidir_gated(z, ld_f, ld_b, zero, gla_norm)
    o_att = ctx_attn(aq, ak, av, sink)
    return jnp.concatenate([o_gla, o_att], axis=-1) @ w_out, ak, av, s_fin


def even_mixer_lat(h, w_in, w_out, dec_w, dec_b, gla_norm, sink, rope, k_ctx, v_ctx, s0):
    z, ld_f, ld_b, aq, ak, av = even_projections(h, w_in, dec_w, dec_b)
    o_gla, _ = gla_bidir_gated(z, ld_f, ld_b, s0, gla_norm)
    o_att = window_attn_latent(apply_axial_rope(aq, rope), apply_axial_rope(ak, rope), av, k_ctx, v_ctx, sink)
    return jnp.concatenate([o_gla, o_att], axis=-1) @ w_out


def centred_shift(x):
    xp = jnp.pad(x, ((0, 0), (1, 1), (0, 0)))
    return 0.5 * (xp[:, :-2] + xp[:, 2:])


RWKV_SUB = 64
LANES = 128
SUBLANES = 8
HEAD_PAIRS = D_MODEL // LANES


def _rwkv_scan_kernel(rf, vf, af, wf, kf, bf_, rb, vb, ab, wb, kb, bb, s0, of, ob, s, acc, *, tc):
    j = pl.program_id(1)

    @pl.when(j == 0)
    def _():
        s[...] = s0[...]

    row_head = lax.broadcasted_iota(jnp.int32, (2 * LANES, 2 * LANES), 0) // RWKV_HEAD
    col_head = lax.broadcasted_iota(jnp.int32, (2 * LANES, 2 * LANES), 1) // RWKV_HEAD
    ones_bd = (row_head == col_head).astype(bf16)
    sub = lax.broadcasted_iota(jnp.int32, (RWKV_HEAD, LANES), 0)
    lane_in_head = lax.broadcasted_iota(jnp.int32, (RWKV_HEAD, LANES), 1) % RWKV_HEAD
    eye = (sub == lane_in_head).astype(bf16)
    dirs = ((rf, vf, af, wf, kf, bf_), (rb, vb, ab, wb, kb, bb))

    def row_sums(x):
        return jnp.dot(x, ones_bd, preferred_element_type=f32)

    def row(ref, t, hp, reps):
        return jnp.broadcast_to(ref[t, pl.ds(hp, 1), :], (reps, LANES))

    def sub_chunk(sc, carry):
        def step(tt, carry):
            t_f = sc * RWKV_SUB + tt
            pos = ((t_f, jnp.maximum(t_f - 1, 0), tt - 1),
                   (tc - 1 - t_f, jnp.minimum(tc - t_f, tc - 1), RWKV_SUB - tt))
            for d in range(2):
                r_ref, v_ref, a_ref, w_ref, k_ref, b_ref = dirs[d]
                t_now, t_prev, out_lane = pos[d]
                lhs, tiles = [], []
                for hp2 in range(0, HEAD_PAIRS, 2):
                    v_diag = []
                    for hp in (hp2, hp2 + 1):
                        s_t = s[d, :, hp * LANES:(hp + 1) * LANES]
                        lhs.append(jnp.concatenate([(s_t * row(a_ref, t_now, hp, RWKV_HEAD)).astype(bf16),
                                                    (s_t * row(r_ref, t_prev, hp, RWKV_HEAD)).astype(bf16)], axis=1))
                        v_diag.append(eye * jnp.concatenate([row(v_ref, t_now, hp, 16).astype(bf16)] * 4, axis=0))
                        tiles.append(s_t)
                    lhs.append(jnp.concatenate(v_diag, axis=1))
                res = row_sums(jnp.concatenate(lhs, axis=0))
                for hp, s_t in enumerate(tiles):
                    ls = slice(hp * LANES, (hp + 1) * LANES)
                    base = (hp // 2) * 3 * RWKV_HEAD
                    own = res[base + (hp % 2) * RWKV_HEAD:base + (hp % 2 + 1) * RWKV_HEAD]
                    sa, out_prev = own[:, :LANES], own[:, LANES:]
                    v_col = res[base + 2 * RWKV_HEAD:base + 3 * RWKV_HEAD, (hp % 2) * LANES:(hp % 2 + 1) * LANES]
                    s[d, :, ls] = (s_t * row(w_ref, t_now, hp, RWKV_HEAD) + sa * row(b_ref, t_now, hp, RWKV_HEAD)
                                   + v_col * row(k_ref, t_now, hp, RWKV_HEAD))
                    acc[d, :, ls] = jnp.where(lane_in_head == out_lane, out_prev, acc[d, :, ls])
            return carry

        lax.fori_loop(0, RWKV_SUB, step, 0, unroll=8)
        t_last = sc * RWKV_SUB + RWKV_SUB - 1
        last = ((t_last, RWKV_SUB - 1), (tc - 1 - t_last, 0))
        for d in range(2):
            r_ref = dirs[d][0]
            t_now, out_lane = last[d]
            out_mask = jnp.concatenate([lane_in_head == out_lane] * 2, axis=1)
            for hp in range(0, HEAD_PAIRS, 2):
                ls = slice(hp * LANES, (hp + 2) * LANES)
                r_t = jnp.concatenate([row(r_ref, t_now, hp, RWKV_HEAD), row(r_ref, t_now, hp + 1, RWKV_HEAD)], axis=1)
                out_t = row_sums((s[d, :, ls] * r_t).astype(bf16))
                acc[d, :, ls] = jnp.where(out_mask, out_t, acc[d, :, ls])
        of[sc] = acc[0]
        ob[tc // RWKV_SUB - 1 - sc] = acc[1]
        return carry

    lax.fori_loop(0, tc // RWKV_SUB, sub_chunk, 0)


def rwkv_scan_bidir(r, v, nkk, w_f, k_f, b_f, w_b, k_b, b_b, s0):
    B, L, D = r.shape
    tc = min(L, 256)
    n = L // tc
    fwd = pl.BlockSpec((None, tc, HEAD_PAIRS, LANES), lambda b, j: (b, j, 0, 0))
    bwd = pl.BlockSpec((None, tc, HEAD_PAIRS, LANES), lambda b, j: (b, n - 1 - j, 0, 0))
    r, v, nkk, w_f, k_f, b_f, w_b, k_b, b_b = (t.reshape(B, L, HEAD_PAIRS, LANES)
                                               for t in (r, v, nkk, w_f, k_f, b_f, w_b, k_b, b_b))
    state = pl.BlockSpec((None, 2, RWKV_HEAD, D), lambda b, j: (b, 0, 0, 0))
    nsub = tc // RWKV_SUB
    block_bytes = tc * D * 4
    o_f, o_b, s_fin = pl.pallas_call(
        functools.partial(_rwkv_scan_kernel, tc=tc),
        grid=(B, n),
        in_specs=[fwd] * 6 + [bwd] * 6 + [state],
        out_specs=[pl.BlockSpec((None, nsub, RWKV_HEAD, D), lambda b, j: (b, j, 0, 0)),
                   pl.BlockSpec((None, nsub, RWKV_HEAD, D), lambda b, j: (b, n - 1 - j, 0, 0)),
                   state],
        out_shape=[jax.ShapeDtypeStruct((B, L // RWKV_SUB, RWKV_HEAD, D), f32),
                   jax.ShapeDtypeStruct((B, L // RWKV_SUB, RWKV_HEAD, D), f32),
                   jax.ShapeDtypeStruct((B, 2, RWKV_HEAD, D), f32)],
        scratch_shapes=[pltpu.VMEM((2, RWKV_HEAD, D), f32)],
        compiler_params=pltpu.CompilerParams(
            dimension_semantics=("arbitrary", "arbitrary"),
            vmem_limit_bytes=min(V7X_VMEM_BYTES, 2 * 14 * block_bytes + (8 << 20)),
        ),
        name="rwkv_scan",
    )(r, v, nkk, w_f, k_f, b_f, r, v, nkk, w_b, k_b, b_b, s0)
    o = (o_f + o_b).reshape(B, L // RWKV_SUB, RWKV_HEAD, RWKV_HEADS, RWKV_SUB)
    return o.transpose(0, 1, 4, 3, 2).reshape(B, L, D), s_fin


def rwkv_mix(h, mu, wr, wk, wv, wo, w0, w1, w2, a0, a1, a2, g1, g2, k_k, k_a, r_k, lnx_w, lnx_b, s_f0, s_b0):
    B, L, D = h.shape
    H, N = RWKV_HEADS, RWKV_HEAD
    xx = centred_shift(h) - h
    xr, xw, xk, xv, xa, xg = [h + xx * mu[j] for j in range(6)]
    r = (xr @ wr).reshape(B, L, H, N)
    k = (xk @ wk).reshape(B, L, H, N)
    v = (xv @ wv).reshape(B, L, H, N)
    g = jax.nn.sigmoid(xg @ g1) @ g2
    kk = k * k_k.reshape(H, N)
    kk = kk * lax.rsqrt(jnp.sum(kk * kk, axis=-1, keepdims=True) + 1e-12)

    def direction(d):
        w_raw = -jax.nn.softplus(-(w0[d] + jnp.tanh(xw @ w1[d]) @ w2[d])) - 0.5
        decay = jnp.exp(-jnp.exp(w_raw))
        a = jax.nn.sigmoid(a0[d] + (xa @ a1[d]) @ a2[d]).reshape(B, L, H, N)
        k_d = k * (1.0 + (a - 1.0) * k_a.reshape(H, N))
        return decay, k_d, kk * a

    w_f, k_f, b_f = direction(0)
    w_b, k_b, b_b = direction(1)

    def rows(t):
        return t.reshape(B, L, D)

    def to_kernel_state(s0):
        return s0.transpose(0, 2, 1, 3).reshape(B, N, D)

    o, s_fin = rwkv_scan_bidir(rows(r), rows(v), rows(-kk), w_f, rows(k_f), rows(b_f), w_b, rows(k_b), rows(b_b),
                               jnp.stack([to_kernel_state(s_f0), to_kernel_state(s_b0)], axis=1))
    s_fin = s_fin.reshape(B, 2, N, H, N).transpose(0, 1, 3, 2, 4)
    s_f, s_b = s_fin[:, 0], s_fin[:, 1]
    o = o.reshape(B, L, H, N)
    m = jnp.mean(o, axis=-1, keepdims=True)
    var = jnp.mean(jnp.square(o - m), axis=-1, keepdims=True)
    o = (o - m) * lax.rsqrt(var + RWKV_LN_EPS) * lnx_w.reshape(H, N) + lnx_b.reshape(H, N)
    bonus = jnp.sum(r * 0.5 * (k_f + k_b) * r_k, axis=-1, keepdims=True) * v
    o = (o + bonus).reshape(B, L, D)
    return (o * g) @ wo, s_f, s_b


GROUP_HID = EXP_PER_GROUP * D_EXPERT


def _moe_group_kernel(tile_group_ref, n_tiles_ref, xs_ref, gates_ref, wg_ref, wu_ref, wd_ref, ys_ref,
                      wg_bf, wu_bf, wd_bf):
    i = pl.program_id(0)
    prev_group = tile_group_ref[jnp.maximum(i - 1, 0)]
    group_changed = jnp.logical_or(i == 0, tile_group_ref[i] != prev_group)

    @pl.when(group_changed)
    def _():
        for e in range(EXP_PER_GROUP):
            hs = slice(e * D_EXPERT, (e + 1) * D_EXPERT)
            wg_bf[:, hs] = wg_ref[e].astype(bf16)
            wu_bf[:, hs] = wu_ref[e].astype(bf16)
            wd_bf[hs, :] = wd_ref[e].astype(bf16)

    @pl.when(i < n_tiles_ref[0])
    def _():
        x = xs_ref[...].astype(bf16)
        g = jnp.dot(x, wg_bf[...], preferred_element_type=f32)
        u = jnp.dot(x, wu_bf[...], preferred_element_type=f32)
        gate = jnp.concatenate([jnp.broadcast_to(gates_ref[:, e:e + 1], (MOE_TILE_M, D_EXPERT))
                                for e in range(EXP_PER_GROUP)], axis=1)
        hid = (g * jax.nn.sigmoid(g)) * u * gate
        ys_ref[...] = jnp.dot(hid.astype(bf16), wd_bf[...], preferred_element_type=f32)

    @pl.when(i >= n_tiles_ref[0])
    def _():
        ys_ref[...] = jnp.zeros_like(ys_ref)


def moe_group_experts(xs, gates, tile_group, n_tiles, w_gate, w_up, w_down):
    P, D = xs.shape
    max_tiles = P // MOE_TILE_M
    weight_bytes = 3 * EXP_PER_GROUP * D * D_EXPERT * (4 + 2)
    tile_bytes = 2 * MOE_TILE_M * (2 * D * 4 + LANES * 4) + 4 * MOE_TILE_M * GROUP_HID * 4
    once = pl.Buffered(1)
    grid_spec = pltpu.PrefetchScalarGridSpec(
        num_scalar_prefetch=2,
        grid=(max_tiles,),
        in_specs=[
            pl.BlockSpec((MOE_TILE_M, D), lambda i, tg, nt: (i, 0)),
            pl.BlockSpec((MOE_TILE_M, EXP_PER_GROUP), lambda i, tg, nt: (i, 0)),
            pl.BlockSpec((EXP_PER_GROUP, D, D_EXPERT), lambda i, tg, nt: (tg[i], 0, 0), pipeline_mode=once),
            pl.BlockSpec((EXP_PER_GROUP, D, D_EXPERT), lambda i, tg, nt: (tg[i], 0, 0), pipeline_mode=once),
            pl.BlockSpec((EXP_PER_GROUP, D_EXPERT, D), lambda i, tg, nt: (tg[i], 0, 0), pipeline_mode=once),
        ],
        out_specs=pl.BlockSpec((MOE_TILE_M, D), lambda i, tg, nt: (i, 0)),
        scratch_shapes=[
            pltpu.VMEM((D, GROUP_HID), bf16),
            pltpu.VMEM((D, GROUP_HID), bf16),
            pltpu.VMEM((GROUP_HID, D), bf16),
        ],
    )
    return pl.pallas_call(
        _moe_group_kernel,
        grid_spec=grid_spec,
        out_shape=jax.ShapeDtypeStruct((P, D), f32),
        compiler_params=pltpu.CompilerParams(
            dimension_semantics=("arbitrary",),
            vmem_limit_bytes=min(V7X_VMEM_BYTES - (4 << 20), weight_bytes + tile_bytes + (8 << 20)),
        ),
        name="moe_group_experts",
    )(tile_group, n_tiles, xs, gates, w_gate, w_up, w_down)


def hier_moe(t, w_grp, b_grp, w_exp, b_exp, w_gate, w_up, w_down):
    T, D = t.shape
    logits = jnp.dot(t, jnp.concatenate([w_grp, w_exp], axis=1), precision=lax.Precision.HIGHEST)
    grp_logits = logits[:, :N_GROUPS] + b_grp
    grp_prob = jax.nn.softmax(grp_logits, axis=-1)
    g_top = jnp.argmax(grp_logits, axis=-1).astype(jnp.int32)
    in_group = g_top[:, None] == jnp.arange(N_GROUPS, dtype=jnp.int32)[None, :]
    p_g = jnp.sum(jnp.where(in_group, grp_prob, 0.0), axis=1, keepdims=True)
    exp_logits = (logits[:, N_GROUPS:] + b_exp).reshape(-1, N_GROUPS, EXP_PER_GROUP)
    sel = jnp.sum(jnp.where(in_group[:, :, None], exp_logits, 0.0), axis=1)
    top_v, top_i = lax.top_k(sel, TOP_K)
    wts = p_g * jax.nn.softmax(top_v, axis=-1)
    gates = jnp.sum(jax.nn.one_hot(top_i, EXP_PER_GROUP, dtype=f32) * wts[..., None], axis=1)

    order = jnp.argsort(g_top, stable=True).astype(jnp.int32)
    grp_i32 = in_group.astype(jnp.int32)
    counts = jnp.sum(grp_i32, axis=0)
    rank = jnp.sum(jnp.where(in_group, jnp.cumsum(grp_i32, axis=0), 0), axis=1) - 1
    tiles_per = (counts + MOE_TILE_M - 1) // MOE_TILE_M
    tile_end = jnp.cumsum(tiles_per)
    start_padded = (tile_end - tiles_per) * MOE_TILE_M
    start_sorted = jnp.cumsum(counts) - counts
    max_tiles = T // MOE_TILE_M + N_GROUPS
    n_tiles = tile_end[-1:].astype(jnp.int32)
    tile_ids = jnp.arange(max_tiles, dtype=jnp.int32)
    tile_group = jnp.minimum(jnp.sum((tile_ids[:, None] >= tile_end[None, :]).astype(jnp.int32), axis=1),
                             N_GROUPS - 1).astype(jnp.int32)
    tile_group = jnp.where(tile_ids < n_tiles[0], tile_group, tile_group[jnp.maximum(n_tiles[0] - 1, 0)])
    row_group = jnp.repeat(tile_group, MOE_TILE_M)
    row_in_group = jnp.arange(max_tiles * MOE_TILE_M, dtype=jnp.int32) - start_padded[row_group]
    row_valid = (row_in_group >= 0) & (row_in_group < counts[row_group])
    row_tok = jnp.where(row_valid, order[jnp.clip(start_sorted[row_group] + row_in_group, 0, T - 1)], 0)
    pos = start_padded[g_top] + rank

    xs = jnp.take(t, row_tok, axis=0)
    row_gates = jnp.where(row_valid[:, None], jnp.take(gates, row_tok, axis=0), 0.0)
    ys = moe_group_experts(xs, row_gates, tile_group, n_tiles, w_gate, w_up, w_down)
    return jnp.take(ys, pos, axis=0)


def kernel(x_prompt, x_sample, c, cache_attn_k, cache_attn_v, state_gla, state_rwkv, c_ctx, ada_w, ada_b, norm_mix, norm_ffn, norm_out, ev_w_in, ev_w_out, gla_dec_w, gla_dec_b, gla_norm, att_sink, rw_mu, rw_wr, rw_wk, rw_wv, rw_wo, rw_w0, rw_w1, rw_w2, rw_a0, rw_a1, rw_a2, rw_g1, rw_g2, rw_kk, rw_ka, rw_rk, rw_lnx_w, rw_lnx_b, moe_w_grp, moe_b_grp, moe_w_exp, moe_b_exp, moe_w_gate, moe_w_up, moe_w_down):
    n_lat = x_sample.shape[1]
    rows = n_lat // GRID_W
    row_pos = jnp.repeat(jnp.arange(rows), GRID_W)
    col_pos = jnp.tile(jnp.arange(GRID_W), rows)
    rope = rope_tables(row_pos, col_pos)
    Bc, Lc, D = x_prompt.shape
    Bl, Ll, _ = x_sample.shape

    xc, xl = x_prompt, x_sample
    cond_ctx = c_ctx[None, :]
    ks_out, vs_out, gla_out, rwkv_out = [], [], [], []
    for l in range(DEPTH):
        sc1, scl1, gc1, sc2, scl2, gc2 = ada_mod(cond_ctx, ada_w[l], ada_b[l])
        sl1, sll1, gl1, sl2, sll2, gl2 = ada_mod(c, ada_w[l], ada_b[l])
        hc = modulate(rmsnorm(xc, norm_mix[l]), sc1, scl1)
        hl = modulate(rmsnorm(xl, norm_mix[l]), sl1, sll1)
        i = l // 2
        if l % 2 == 0:
            oc, k_c, v_c, s_gla = even_mixer_ctx(hc, ev_w_in[i], ev_w_out[i], gla_dec_w[i], gla_dec_b[i],
                                                 gla_norm[i], att_sink[i])
            ol = even_mixer_lat(hl, ev_w_in[i], ev_w_out[i], gla_dec_w[i], gla_dec_b[i], gla_norm[i], att_sink[i],
                                rope, cache_attn_k[:, i], cache_attn_v[:, i], state_gla[:, i])
            ks_out.append(k_c)
            vs_out.append(v_c)
            gla_out.append(s_gla)
        else:
            rw = (rw_mu[i], rw_wr[i], rw_wk[i], rw_wv[i], rw_wo[i], rw_w0[i], rw_w1[i], rw_w2[i], rw_a0[i],
                  rw_a1[i], rw_a2[i], rw_g1[i], rw_g2[i], rw_kk[i], rw_ka[i], rw_rk[i], rw_lnx_w[i], rw_lnx_b[i])
            zero = jnp.zeros((Bc, RWKV_HEADS, RWKV_HEAD, RWKV_HEAD), f32)
            oc, s_f, s_b = rwkv_mix(hc, *rw, zero, zero)
            ol, _, _ = rwkv_mix(hl, *rw, state_rwkv[:, i, 0], state_rwkv[:, i, 1])
            rwkv_out.append(jnp.stack([s_f, s_b], axis=1))
        xc = xc + gc1 * oc
        xl = xl + gl1 * ol
        tc = modulate(rmsnorm(xc, norm_ffn[l]), sc2, scl2).reshape(Bc * Lc, D)
        tl = modulate(rmsnorm(xl, norm_ffn[l]), sl2, sll2).reshape(Bl * Ll, D)
        y = hier_moe(jnp.concatenate([tc, tl], axis=0), moe_w_grp[l], moe_b_grp[l], moe_w_exp[l], moe_b_exp[l],
                     moe_w_gate[l], moe_w_up[l], moe_w_down[l])
        xc = xc + gc2 * y[:Bc * Lc].reshape(Bc, Lc, D)
        xl = xl + gl2 * y[Bc * Lc:].reshape(Bl, Ll, D)

    y_prompt = rmsnorm(xc, norm_out)
    y_sample = rmsnorm(xl, norm_out)
    new_attn_k = jnp.stack(ks_out, axis=1)
    new_attn_v = jnp.stack(vs_out, axis=1)
    new_gla = jnp.stack(gla_out, axis=1)
    new_rwkv = jnp.stack(rwkv_out, axis=1)
    return (y_prompt, y_sample, new_attn_k, new_attn_v, new_gla, new_rwkv)
```

```python
import functools

import jax
import jax.numpy as jnp
import numpy as np
from jax import lax
from jax.experimental import pallas as pl
from jax.experimental.pallas import tpu as pltpu

D_MODEL = 1024
DEPTH = 4
GRID_W = 64
EPS = 1e-6
GLA_HEADS = 4
GLA_DK = 64
GLA_DV = 128
GLA_LOWRANK = 16
GLA_GATE_TEMP = 16.0
GLA_CHUNK = 64
ATT_HEADS = 8
ATT_KV_HEADS = 2
ATT_GROUP = ATT_HEADS // ATT_KV_HEADS
ATT_HD = 64
WINDOW = 128
ATT_BLOCK = 128
ROPE_BASE = 10000.0
ROPE_AXIS_DIMS = ATT_HD // 2
NEG = -1e30
IN_WIDTHS = (GLA_HEADS * GLA_DK, GLA_HEADS * GLA_DK, GLA_HEADS * GLA_DV, GLA_HEADS * GLA_DV,
             ATT_HEADS * ATT_HD, ATT_KV_HEADS * ATT_HD, ATT_KV_HEADS * ATT_HD, 2 * GLA_LOWRANK)
IN_SPLITS = tuple(int(v) for v in np.cumsum(IN_WIDTHS)[:-1])
RWKV_HEAD = 64
RWKV_HEADS = D_MODEL // RWKV_HEAD
RWKV_LN_EPS = 64e-5
N_GROUPS = 4
EXP_PER_GROUP = 4
N_EXPERTS = N_GROUPS * EXP_PER_GROUP
TOP_K = 2
D_EXPERT = 512

V7X_VMEM_BYTES = 64 * 1024 * 1024
MOE_TILE_M = 256

f32 = jnp.float32
bf16 = jnp.bfloat16


def rmsnorm(x, w):
    y = x * lax.rsqrt(jnp.mean(x * x, axis=-1, keepdims=True) + EPS)
    return y * w


def ada_mod(cond, w, b):
    mod = jax.nn.silu(cond) @ w + b
    return [m[:, None, :] for m in jnp.split(mod, 6, axis=-1)]


def modulate(x, shift, scale):
    return x * (1.0 + scale) + shift


def rope_tables(row_pos, col_pos):
    freqs = ROPE_BASE ** (-jnp.arange(0, ROPE_AXIS_DIMS, 2, dtype=f32) / ROPE_AXIS_DIMS)
    ang_r = row_pos.astype(f32)[:, None] * freqs[None, :]
    ang_c = col_pos.astype(f32)[:, None] * freqs[None, :]
    return jnp.cos(ang_r), jnp.sin(ang_r), jnp.cos(ang_c), jnp.sin(ang_c)


def rotate(x, cos, sin):
    x1, x2 = jnp.split(x, 2, axis=-1)
    cos = cos[None, :, None, :]
    sin = sin[None, :, None, :]
    return jnp.concatenate([x1 * cos - x2 * sin, x1 * sin + x2 * cos], axis=-1)


def apply_axial_rope(x, tables):
    cr, sr, cc, sc = tables
    xr, xc = jnp.split(x, 2, axis=-1)
    return jnp.concatenate([rotate(xr, cr, sr), rotate(xc, cc, sc)], axis=-1)


GLA_QK = GLA_HEADS * GLA_DK
GLA_V = GLA_HEADS * GLA_DV


def _gla_kernel(q_ref, v_ref, gg_ref, ldf_ref, ldb_ref, kt_ref, ldft_ref, ldbt_ref, s0_ref, norm_ref,
                o_ref, s_ref, *, seq_len):
    n = seq_len // GLA_CHUNK
    C = GLA_CHUNK
    ti = lax.broadcasted_iota(jnp.int32, (C, C), 0)
    tj = lax.broadcasted_iota(jnp.int32, (C, C), 1)
    keep = (tj <= ti, tj >= ti)
    tri = tuple(m.astype(bf16) for m in keep)
    tri_t = (tri[1], tri[0])
    ones = jnp.ones((C, GLA_DV), bf16)
    s_ref[...] = s0_ref[...]
    o_ref[...] = jnp.zeros_like(o_ref)
    ld_refs = ((ldf_ref, ldft_ref), (ldb_ref, ldbt_ref))

    def split3(x):
        hi = x.astype(bf16)
        rest = x - hi.astype(f32)
        mid = rest.astype(bf16)
        return hi, mid, (rest - mid.astype(f32)).astype(bf16)

    def sums(a, b):
        if isinstance(b, tuple):
            return sum(jnp.dot(a, p, preferred_element_type=f32) for p in b)
        return sum(jnp.dot(p, b, preferred_element_type=f32) for p in a)

    def chunk(c, carry):
        pending = []
        for d in range(2):
            cc = c if d == 0 else n - 1 - c
            rows = pl.ds(pl.multiple_of(cc * C, C), C)
            ld_ref, ldt_ref = ld_refs[d]
            ld_t3 = split3(ldt_ref[cc])
            b_all = sums(tri[d], split3(ld_ref[rows, :]))
            bt_all = sums(ld_t3, tri_t[d])
            total_all = sums(ld_t3, ones)
            for h in range(GLA_HEADS):
                ks = slice(h * GLA_DK, (h + 1) * GLA_DK)
                vs = slice(h * GLA_DV, (h + 1) * GLA_DV)
                q = q_ref[rows, ks] * (GLA_DK ** -0.5)
                v = v_ref[rows, vs].astype(bf16)
                k_t = kt_ref[cc, ks, :]
                b, b_t, total = b_all[:, ks], bt_all[ks], total_all[ks]
                q_e = (q * jnp.exp(b)).astype(bf16)
                k_e = (k_t * jnp.exp(-b_t)).astype(bf16)
                k_s = (k_t * jnp.exp(total[:, :C] - b_t)).astype(bf16)
                att = jnp.where(keep[d], jnp.dot(q_e, k_e, preferred_element_type=f32), 0.0).astype(bf16)
                s = s_ref[d, h]
                o = (jnp.dot(q_e, s.astype(bf16), preferred_element_type=f32)
                     + jnp.dot(att, v, preferred_element_type=f32))
                s_new = s * jnp.exp(total) + jnp.dot(k_s, v, preferred_element_type=f32)
                pending.append((d, h, rows, vs, o_ref[rows, vs] + o, s_new))
        for d, h, rows, vs, o, s_new in pending:
            o_ref[rows, vs] = o
            s_ref[d, h] = s_new
        return carry

    lax.fori_loop(0, n, chunk, 0)
    for h in range(GLA_HEADS):
        vs = slice(h * GLA_DV, (h + 1) * GLA_DV)
        x = o_ref[:, vs]
        y = x * lax.rsqrt(jnp.mean(x * x, axis=-1, keepdims=True) + EPS) * norm_ref[...]
        g = gg_ref[:, vs]
        o_ref[:, vs] = y * (g * jax.nn.sigmoid(g))


def gla_bidir_gated(z, ld_f, ld_b, s0, gla_norm):
    B, L, _ = z.shape
    n = L // GLA_CHUNK
    assert n % 2 == 0

    def per_chunk_t(a):
        return a.reshape(B, n, GLA_CHUNK, GLA_QK).transpose(0, 1, 3, 2)

    state = pl.BlockSpec((None, 2, GLA_HEADS, GLA_DK, GLA_DV), lambda b: (b, 0, 0, 0, 0))
    ld = pl.BlockSpec((None, L, GLA_QK), lambda b: (b, 0, 0))
    ld_t = pl.BlockSpec((None, n, GLA_QK, GLA_CHUNK), lambda b: (b, 0, 0, 0))
    return pl.pallas_call(
        functools.partial(_gla_kernel, seq_len=L),
        grid=(B,),
        in_specs=[pl.BlockSpec((None, L, GLA_QK), lambda b: (b, 0, 0)),
                  pl.BlockSpec((None, L, GLA_V), lambda b: (b, 0, 1)),
                  pl.BlockSpec((None, L, GLA_V), lambda b: (b, 0, 2)),
                  ld, ld, ld_t, ld_t, ld_t, state,
                  pl.BlockSpec((1, GLA_DV), lambda b: (0, 0))],
        out_specs=[pl.BlockSpec((None, L, GLA_V), lambda b: (b, 0, 0)), state],
        out_shape=[jax.ShapeDtypeStruct((B, L, GLA_V), f32),
                   jax.ShapeDtypeStruct((B, 2, GLA_HEADS, GLA_DK, GLA_DV), f32)],
        compiler_params=pltpu.CompilerParams(
            dimension_semantics=("arbitrary",),
            vmem_limit_bytes=min(V7X_VMEM_BYTES, 2 * L * (9 * GLA_QK + 3 * GLA_V) * 4 + (16 << 20)),
        ),
        name="gla_bidir",
    )(z, z, z, ld_f, ld_b, per_chunk_t(z[..., GLA_QK:2 * GLA_QK]), per_chunk_t(ld_f), per_chunk_t(ld_b),
      s0, gla_norm.reshape(1, GLA_DV))


def ctx_attn(q, k, v, sink):
    B, L = q.shape[:2]
    scale = ATT_HD ** -0.5
    qg = q.reshape(B, L, ATT_KV_HEADS, ATT_GROUP, ATT_HD)
    sink_b = sink.reshape(ATT_KV_HEADS, ATT_GROUP)[None, :, :, None, None]
    s = jnp.einsum('bqhgd,bkhd->bhgqk', qg, k) * scale
    s_sink = jnp.broadcast_to(sink_b, s.shape[:-1] + (1,))
    p = jax.nn.softmax(jnp.concatenate([s_sink, s], axis=-1), axis=-1)[..., 1:]
    return jnp.einsum('bhgqk,bkhd->bqhgd', p, v).reshape(B, L, ATT_HEADS * ATT_HD)


def window_attn_latent(q, k, v, k_ctx, v_ctx, sink):
    B, S = q.shape[:2]
    nb = S // ATT_BLOCK
    Lc = k_ctx.shape[1]
    scale = ATT_HD ** -0.5
    qb = q.reshape(B, nb, ATT_BLOCK, ATT_KV_HEADS, ATT_GROUP, ATT_HD).transpose(1, 0, 2, 3, 4, 5)
    pad = ((0, 0), (ATT_BLOCK, ATT_BLOCK), (0, 0), (0, 0))
    kp = jnp.pad(k, pad)
    vp = jnp.pad(v, pad)
    sink_b = sink.reshape(ATT_KV_HEADS, ATT_GROUP)[None, :, :, None, None]
    offs_q = jnp.arange(ATT_BLOCK)
    offs_k = jnp.arange(3 * ATT_BLOCK) - ATT_BLOCK

    def block(args):
        bi, q_blk = args
        k_band = lax.dynamic_slice_in_dim(kp, bi * ATT_BLOCK, 3 * ATT_BLOCK, axis=1)
        v_band = lax.dynamic_slice_in_dim(vp, bi * ATT_BLOCK, 3 * ATT_BLOCK, axis=1)
        qpos = bi * ATT_BLOCK + offs_q
        kpos = bi * ATT_BLOCK + offs_k
        valid = (jnp.abs(qpos[:, None] - kpos[None, :]) <= WINDOW) & (kpos >= 0)[None, :] & (kpos < S)[None, :]
        s_lat = jnp.einsum('bqhgd,bkhd->bhgqk', q_blk, k_band) * scale
        s_lat = jnp.where(valid, s_lat, NEG)
        s_ctx = jnp.einsum('bqhgd,bchd->bhgqc', q_blk, k_ctx) * scale
        s_sink = jnp.broadcast_to(sink_b, s_ctx.shape[:-1] + (1,))
        p = jax.nn.softmax(jnp.concatenate([s_sink, s_ctx, s_lat], axis=-1), axis=-1)
        o = (jnp.einsum('bhgqc,bchd->bqhgd', p[..., 1:1 + Lc], v_ctx)
             + jnp.einsum('bhgqk,bkhd->bqhgd', p[..., 1 + Lc:], v_band))
        return o.reshape(B, ATT_BLOCK, ATT_HEADS * ATT_HD)

    o = lax.map(block, (jnp.arange(nb), qb))
    return o.transpose(1, 0, 2, 3).reshape(B, S, ATT_HEADS * ATT_HD)


def even_projections(h, w_in, dec_w, dec_b):
    B, L, _ = h.shape
    z = h @ w_in
    aq, ak, av, lr = jnp.split(z[..., IN_SPLITS[3]:], [s - IN_SPLITS[3] for s in IN_SPLITS[4:]], axis=-1)
    lr_f, lr_b = jnp.split(lr, 2, axis=-1)

    def logdecay(lr_d, w, b):
        return jax.nn.log_sigmoid(lr_d @ w + b) / GLA_GATE_TEMP

    ld_f = logdecay(lr_f, dec_w[0], dec_b[0])
    ld_b = logdecay(lr_b, dec_w[1], dec_b[1])
    aq = aq.reshape(B, L, ATT_HEADS, ATT_HD)
    ak = ak.reshape(B, L, ATT_KV_HEADS, ATT_HD)
    av = av.reshape(B, L, ATT_KV_HEADS, ATT_HD)
    return z, ld_f, ld_b, aq, ak, av


def even_mixer_ctx(h, w_in, w_out, dec_w, dec_b, gla_norm, sink):
    z, ld_f, ld_b, aq, ak, av = even_projections(h, w_in, dec_w, dec_b)
    zero = jnp.zeros((h.shape[0], 2, GLA_HEADS, GLA_DK, GLA_DV), f32)
    o_gla, s_fin = gla_bidir_gated(z, ld_f, ld_b, zero, gla_norm)
    o_att = ctx_attn(aq, ak, av, sink)
    return jnp.concatenate([o_gla, o_att], axis=-1) @ w_out, ak, av, s_fin


def even_mixer_lat(h, w_in, w_out, dec_w, dec_b, gla_norm, sink, rope, k_ctx, v_ctx, s0):
    z, ld_f, ld_b, aq, ak, av = even_projections(h, w_in, dec_w, dec_b)
    o_gla, _ = gla_bidir_gated(z, ld_f, ld_b, s0, gla_norm)
    o_att = window_attn_latent(apply_axial_rope(aq, rope), apply_axial_rope(ak, rope), av, k_ctx, v_ctx, sink)
    return jnp.concatenate([o_gla, o_att], axis=-1) @ w_out


def centred_shift(x):
    xp = jnp.pad(x, ((0, 0), (1, 1), (0, 0)))
    return 0.5 * (xp[:, :-2] + xp[:, 2:])


LANES = 128
SUBLANES = 8
RWKV_TC = 32


def _rwkv_scan_kernel(r_ref, v_ref, a_ref, w_ref, k_ref, b_ref, s0_ref, o_ref, s_ref, *, tc, nv):
    @pl.when(pl.program_id(1) == 0)
    def _():
        s_ref[...] = s0_ref[...]

    def step(t, carry):
        alpha, w, beta, kd, r = a_ref[t], w_ref[t], b_ref[t], k_ref[t], r_ref[t]
        for vb in range(nv // SUBLANES):
            outs = []
            for v in range(vb * SUBLANES, (vb + 1) * SUBLANES):
                s = s_ref[v]
                sa = jnp.sum(s * alpha, axis=0, keepdims=True)
                s_new = s * w + sa * beta + v_ref[t, pl.ds(v, 1), :] * kd
                s_ref[v] = s_new
                outs.append(jnp.sum(s_new * r, axis=0, keepdims=True))
            o_ref[t, vb * SUBLANES:(vb + 1) * SUBLANES, :] = jnp.concatenate(outs, axis=0)
        return carry

    lax.fori_loop(0, tc, step, 0)


def rwkv_scan_lanes(r, v, nkk, w, k, b, s0):
    G, L, nv, _ = v.shape
    tc = RWKV_TC
    key_blk = pl.BlockSpec((None, tc, RWKV_HEAD, LANES), lambda g, j: (g, j, 0, 0))
    val_blk = pl.BlockSpec((None, tc, nv, LANES), lambda g, j: (g, j, 0, 0))
    state = pl.BlockSpec((None, nv, RWKV_HEAD, LANES), lambda g, j: (g, 0, 0, 0))
    block_bytes = tc * RWKV_HEAD * LANES * 4
    return pl.pallas_call(
        functools.partial(_rwkv_scan_kernel, tc=tc, nv=nv),
        grid=(G, L // tc),
        in_specs=[key_blk, val_blk, key_blk, key_blk, key_blk, key_blk, state],
        out_specs=[val_blk, state],
        out_shape=[jax.ShapeDtypeStruct((G, L, nv, LANES), f32),
                   jax.ShapeDtypeStruct((G, nv, RWKV_HEAD, LANES), f32)],
        compiler_params=pltpu.CompilerParams(
            dimension_semantics=("arbitrary", "arbitrary"),
            vmem_limit_bytes=2 * 7 * block_bytes + 4 * nv * RWKV_HEAD * LANES * 4 + (8 << 20),
        ),
        name="rwkv_scan",
    )(r, v, nkk, w, k, b, s0)


def rwkv_scan_bidir(fwd, bwd, s_f0, s_b0):
    B, L, H, N = fwd[0].shape
    S = 2 * B * H
    halves = 2 if S < LANES else 1
    G = S * halves // LANES
    nv = N // halves

    def seq_last(x_f, x_b):
        x = jnp.stack([x_f, x_b[:, ::-1]], axis=0)
        return x.transpose(2, 4, 0, 1, 3).reshape(L, N, S)

    def key_rows(x_f, x_b):
        x = seq_last(x_f, x_b)
        if halves == 2:
            return jnp.concatenate([x, x], axis=-1)[None]
        return x.reshape(L, N, G, LANES).transpose(2, 0, 1, 3)

    def value_rows(x):
        T = x.shape[0]
        if halves == 2:
            return x.reshape(T, 2, nv, S).transpose(0, 2, 1, 3).reshape(1, T, nv, LANES)
        return x.reshape(T, N, G, LANES).transpose(2, 0, 1, 3)

    def value_rows_inv(y):
        T = y.shape[1]
        if halves == 2:
            return y.reshape(T, nv, 2, S).transpose(0, 2, 1, 3).reshape(T, N, S)
        return y.transpose(1, 2, 0, 3).reshape(T, N, S)

    r, v, nkk, w, k, b = (seq_last(f, g) if i == 1 else key_rows(f, g) for i, (f, g) in enumerate(zip(fwd, bwd)))
    s0 = jnp.stack([s_f0, s_b0], axis=0).transpose(3, 4, 0, 1, 2).reshape(N, N, S)
    s0 = value_rows(s0.transpose(1, 0, 2)).transpose(0, 2, 1, 3)
    out, s_fin = rwkv_scan_lanes(r, value_rows(v), nkk, w, k, b, s0)
    o = value_rows_inv(out).reshape(L, N, 2, B, H).transpose(2, 3, 0, 4, 1)
    s_fin = value_rows_inv(s_fin.transpose(0, 2, 1, 3)).transpose(1, 0, 2)
    s_fin = s_fin.reshape(N, N, 2, B, H).transpose(2, 3, 4, 0, 1)
    return o[0] + o[1][:, ::-1], s_fin[0], s_fin[1]


def rwkv_mix(h, mu, wr, wk, wv, wo, w0, w1, w2, a0, a1, a2, g1, g2, k_k, k_a, r_k, lnx_w, lnx_b, s_f0, s_b0):
    B, L, D = h.shape
    H, N = RWKV_HEADS, RWKV_HEAD
    xx = centred_shift(h) - h
    xr, xw, xk, xv, xa, xg = [h + xx * mu[j] for j in range(6)]
    r = (xr @ wr).reshape(B, L, H, N)
    k = (xk @ wk).reshape(B, L, H, N)
    v = (xv @ wv).reshape(B, L, H, N)
    g = jax.nn.sigmoid(xg @ g1) @ g2
    kk = k * k_k.reshape(H, N)
    kk = kk * lax.rsqrt(jnp.sum(kk * kk, axis=-1, keepdims=True) + 1e-12)

    def direction(d):
        w_raw = -jax.nn.softplus(-(w0[d] + jnp.tanh(xw @ w1[d]) @ w2[d])) - 0.5
        decay = jnp.exp(-jnp.exp(w_raw)).reshape(B, L, H, N)
        a = jax.nn.sigmoid(a0[d] + (xa @ a1[d]) @ a2[d]).reshape(B, L, H, N)
        k_d = k * (1.0 + (a - 1.0) * k_a.reshape(H, N))
        return decay, k_d, kk * a

    w_f, k_f, b_f = direction(0)
    w_b, k_b, b_b = direction(1)
    o, s_f, s_b = rwkv_scan_bidir((r, v, -kk, w_f, k_f, b_f), (r, v, -kk, w_b, k_b, b_b), s_f0, s_b0)
    m = jnp.mean(o, axis=-1, keepdims=True)
    var = jnp.mean(jnp.square(o - m), axis=-1, keepdims=True)
    o = (o - m) * lax.rsqrt(var + RWKV_LN_EPS) * lnx_w.reshape(H, N) + lnx_b.reshape(H, N)
    bonus = jnp.sum(r * 0.5 * (k_f + k_b) * r_k, axis=-1, keepdims=True) * v
    o = (o + bonus).reshape(B, L, D)
    return (o * g) @ wo, s_f, s_b


GROUP_HID = EXP_PER_GROUP * D_EXPERT


def _moe_group_kernel(tile_group_ref, n_tiles_ref, xs_ref, gates_ref, wg_ref, wu_ref, wd_ref, ys_ref,
                      wg_bf, wu_bf, wd_bf):
    i = pl.program_id(0)
    prev_group = tile_group_ref[jnp.maximum(i - 1, 0)]
    group_changed = jnp.logical_or(i == 0, tile_group_ref[i] != prev_group)

    @pl.when(group_changed)
    def _():
        for e in range(EXP_PER_GROUP):
            hs = slice(e * D_EXPERT, (e + 1) * D_EXPERT)
            wg_bf[:, hs] = wg_ref[e].astype(bf16)
            wu_bf[:, hs] = wu_ref[e].astype(bf16)
            wd_bf[hs, :] = wd_ref[e].astype(bf16)

    @pl.when(i < n_tiles_ref[0])
    def _():
        x = xs_ref[...].astype(bf16)
        g = jnp.dot(x, wg_bf[...], preferred_element_type=f32)
        u = jnp.dot(x, wu_bf[...], preferred_element_type=f32)
        gate = jnp.concatenate([jnp.broadcast_to(gates_ref[:, e:e + 1], (MOE_TILE_M, D_EXPERT))
                                for e in range(EXP_PER_GROUP)], axis=1)
        hid = (g * jax.nn.sigmoid(g)) * u * gate
        ys_ref[...] = jnp.dot(hid.astype(bf16), wd_bf[...], preferred_element_type=f32)

    @pl.when(i >= n_tiles_ref[0])
    def _():
        ys_ref[...] = jnp.zeros_like(ys_ref)


def moe_group_experts(xs, gates, tile_group, n_tiles, w_gate, w_up, w_down):
    P, D = xs.shape
    max_tiles = P // MOE_TILE_M
    weight_bytes = 3 * EXP_PER_GROUP * D * D_EXPERT * (4 + 2)
    tile_bytes = 2 * MOE_TILE_M * (2 * D * 4 + LANES * 4) + 4 * MOE_TILE_M * GROUP_HID * 4
    once = pl.Buffered(1)
    grid_spec = pltpu.PrefetchScalarGridSpec(
        num_scalar_prefetch=2,
        grid=(max_tiles,),
        in_specs=[
            pl.BlockSpec((MOE_TILE_M, D), lambda i, tg, nt: (i, 0)),
            pl.BlockSpec((MOE_TILE_M, EXP_PER_GROUP), lambda i, tg, nt: (i, 0)),
            pl.BlockSpec((EXP_PER_GROUP, D, D_EXPERT), lambda i, tg, nt: (tg[i], 0, 0), pipeline_mode=once),
            pl.BlockSpec((EXP_PER_GROUP, D, D_EXPERT), lambda i, tg, nt: (tg[i], 0, 0), pipeline_mode=once),
            pl.BlockSpec((EXP_PER_GROUP, D_EXPERT, D), lambda i, tg, nt: (tg[i], 0, 0), pipeline_mode=once),
        ],
        out_specs=pl.BlockSpec((MOE_TILE_M, D), lambda i, tg, nt: (i, 0)),
        scratch_shapes=[
            pltpu.VMEM((D, GROUP_HID), bf16),
            pltpu.VMEM((D, GROUP_HID), bf16),
            pltpu.VMEM((GROUP_HID, D), bf16),
        ],
    )
    return pl.pallas_call(
        _moe_group_kernel,
        grid_spec=grid_spec,
        out_shape=jax.ShapeDtypeStruct((P, D), f32),
        compiler_params=pltpu.CompilerParams(
            dimension_semantics=("arbitrary",),
            vmem_limit_bytes=min(V7X_VMEM_BYTES - (4 << 20), weight_bytes + tile_bytes + (8 << 20)),
        ),
        name="moe_group_experts",
    )(tile_group, n_tiles, xs, gates, w_gate, w_up, w_down)


def hier_moe(t, w_grp, b_grp, w_exp, b_exp, w_gate, w_up, w_down):
    T, D = t.shape
    logits = jnp.dot(t, jnp.concatenate([w_grp, w_exp], axis=1), precision=lax.Precision.HIGHEST)
    grp_logits = logits[:, :N_GROUPS] + b_grp
    grp_prob = jax.nn.softmax(grp_logits, axis=-1)
    g_top = jnp.argmax(grp_logits, axis=-1).astype(jnp.int32)
    in_group = g_top[:, None] == jnp.arange(N_GROUPS, dtype=jnp.int32)[None, :]
    p_g = jnp.sum(jnp.where(in_group, grp_prob, 0.0), axis=1, keepdims=True)
    exp_logits = (logits[:, N_GROUPS:] + b_exp).reshape(-1, N_GROUPS, EXP_PER_GROUP)
    sel = jnp.sum(jnp.where(in_group[:, :, None], exp_logits, 0.0), axis=1)
    top_v, top_i = lax.top_k(sel, TOP_K)
    wts = p_g * jax.nn.softmax(top_v, axis=-1)
    gates = jnp.sum(jax.nn.one_hot(top_i, EXP_PER_GROUP, dtype=f32) * wts[..., None], axis=1)

    order = jnp.argsort(g_top, stable=True).astype(jnp.int32)
    grp_i32 = in_group.astype(jnp.int32)
    counts = jnp.sum(grp_i32, axis=0)
    rank = jnp.sum(jnp.where(in_group, jnp.cumsum(grp_i32, axis=0), 0), axis=1) - 1
    tiles_per = (counts + MOE_TILE_M - 1) // MOE_TILE_M
    tile_end = jnp.cumsum(tiles_per)
    start_padded = (tile_end - tiles_per) * MOE_TILE_M
    start_sorted = jnp.cumsum(counts) - counts
    max_tiles = T // MOE_TILE_M + N_GROUPS
    n_tiles = tile_end[-1:].astype(jnp.int32)
    tile_ids = jnp.arange(max_tiles, dtype=jnp.int32)
    tile_group = jnp.minimum(jnp.sum((tile_ids[:, None] >= tile_end[None, :]).astype(jnp.int32), axis=1),
                             N_GROUPS - 1).astype(jnp.int32)
    tile_group = jnp.where(tile_ids < n_tiles[0], tile_group, tile_group[jnp.maximum(n_tiles[0] - 1, 0)])
    row_group = jnp.repeat(tile_group, MOE_TILE_M)
    row_in_group = jnp.arange(max_tiles * MOE_TILE_M, dtype=jnp.int32) - start_padded[row_group]
    row_valid = (row_in_group >= 0) & (row_in_group < counts[row_group])
    row_tok = jnp.where(row_valid, order[jnp.clip(start_sorted[row_group] + row_in_group, 0, T - 1)], 0)
    pos = start_padded[g_top] + rank

    xs = jnp.take(t, row_tok, axis=0)
    row_gates = jnp.where(row_valid[:, None], jnp.take(gates, row_tok, axis=0), 0.0)
    ys = moe_group_experts(xs, row_gates, tile_group, n_tiles, w_gate, w_up, w_down)
    return jnp.take(ys, pos, axis=0)


def kernel(x_prompt, x_sample, c, cache_attn_k, cache_attn_v, state_gla, state_rwkv, c_ctx, ada_w, ada_b, norm_mix, norm_ffn, norm_out, ev_w_in, ev_w_out, gla_dec_w, gla_dec_b, gla_norm, att_sink, rw_mu, rw_wr, rw_wk, rw_wv, rw_wo, rw_w0, rw_w1, rw_w2, rw_a0, rw_a1, rw_a2, rw_g1, rw_g2, rw_kk, rw_ka, rw_rk, rw_lnx_w, rw_lnx_b, moe_w_grp, moe_b_grp, moe_w_exp, moe_b_exp, moe_w_gate, moe_w_up, moe_w_down):
    n_lat = x_sample.shape[1]
    rows = n_lat // GRID_W
    row_pos = jnp.repeat(jnp.arange(rows), GRID_W)
    col_pos = jnp.tile(jnp.arange(GRID_W), rows)
    rope = rope_tables(row_pos, col_pos)
    Bc, Lc, D = x_prompt.shape
    Bl, Ll, _ = x_sample.shape

    xc, xl = x_prompt, x_sample
    cond_ctx = c_ctx[None, :]
    ks_out, vs_out, gla_out, rwkv_out = [], [], [], []
    for l in range(DEPTH):
        sc1, scl1, gc1, sc2, scl2, gc2 = ada_mod(cond_ctx, ada_w[l], ada_b[l])
        sl1, sll1, gl1, sl2, sll2, gl2 = ada_mod(c, ada_w[l], ada_b[l])
        hc = modulate(rmsnorm(xc, norm_mix[l]), sc1, scl1)
        hl = modulate(rmsnorm(xl, norm_mix[l]), sl1, sll1)
        i = l // 2
        if l % 2 == 0:
            oc, k_c, v_c, s_gla = even_mixer_ctx(hc, ev_w_in[i], ev_w_out[i], gla_dec_w[i], gla_dec_b[i],
                                                 gla_norm[i], att_sink[i])
            ol = even_mixer_lat(hl, ev_w_in[i], ev_w_out[i], gla_dec_w[i], gla_dec_b[i], gla_norm[i], att_sink[i],
                                rope, cache_attn_k[:, i], cache_attn_v[:, i], state_gla[:, i])
            ks_out.append(k_c)
            vs_out.append(v_c)
            gla_out.append(s_gla)
        else:
            rw = (rw_mu[i], rw_wr[i], rw_wk[i], rw_wv[i], rw_wo[i], rw_w0[i], rw_w1[i], rw_w2[i], rw_a0[i],
                  rw_a1[i], rw_a2[i], rw_g1[i], rw_g2[i], rw_kk[i], rw_ka[i], rw_rk[i], rw_lnx_w[i], rw_lnx_b[i])
            zero = jnp.zeros((Bc, RWKV_HEADS, RWKV_HEAD, RWKV_HEAD), f32)
            oc, s_f, s_b = rwkv_mix(hc, *rw, zero, zero)
            ol, _, _ = rwkv_mix(hl, *rw, state_rwkv[:, i, 0], state_rwkv[:, i, 1])
            rwkv_out.append(jnp.stack([s_f, s_b], axis=1))
        xc = xc + gc1 * oc
        xl = xl + gl1 * ol
        tc = modulate(rmsnorm(xc, norm_ffn[l]), sc2, scl2).reshape(Bc * Lc, D)
        tl = modulate(rmsnorm(xl, norm_ffn[l]), sl2, sll2).reshape(Bl * Ll, D)
        y = hier_moe(jnp.concatenate([tc, tl], axis=0), moe_w_grp[l], moe_b_grp[l], moe_w_exp[l], moe_b_exp[l],
                     moe_w_gate[l], moe_w_up[l], moe_w_down[l])
        xc = xc + gc2 * y[:Bc * Lc].reshape(Bc, Lc, D)
        xl = xl + gl2 * y[Bc * Lc:].reshape(Bl, Ll, D)

    y_prompt = rmsnorm(xc, norm_out)
    y_sample = rmsnorm(xl, norm_out)
    new_attn_k = jnp.stack(ks_out, axis=1)
    new_attn_v = jnp.stack(vs_out, axis=1)
    new_gla = jnp.stack(gla_out, axis=1)
    new_rwkv = jnp.stack(rwkv_out, axis=1)
    return (y_prompt, y_sample, new_attn_k, new_attn_v, new_gla, new_rwkv)
```

```python
import functools

import jax
import jax.numpy as jnp
import numpy as np
from jax import lax
from jax.experimental import pallas as pl
from jax.experimental.pallas import tpu as pltpu

D_MODEL = 1024
DEPTH = 4
GRID_W = 64
EPS = 1e-6
GLA_HEADS = 4
GLA_DK = 64
GLA_DV = 128
GLA_LOWRANK = 16
GLA_GATE_TEMP = 16.0
GLA_CHUNK = 64
ATT_HEADS = 8
ATT_KV_HEADS = 2
ATT_GROUP = ATT_HEADS // ATT_KV_HEADS
ATT_HD = 64
WINDOW = 128
ATT_BLOCK = 128
ROPE_BASE = 10000.0
ROPE_AXIS_DIMS = ATT_HD // 2
NEG = -1e30
IN_WIDTHS = (GLA_HEADS * GLA_DK, GLA_HEADS * GLA_DK, GLA_HEADS * GLA_DV, GLA_HEADS * GLA_DV,
             ATT_HEADS * ATT_HD, ATT_KV_HEADS * ATT_HD, ATT_KV_HEADS * ATT_HD, 2 * GLA_LOWRANK)
IN_SPLITS = tuple(int(v) for v in np.cumsum(IN_WIDTHS)[:-1])
RWKV_HEAD = 64
RWKV_HEADS = D_MODEL // RWKV_HEAD
RWKV_LN_EPS = 64e-5
N_GROUPS = 4
EXP_PER_GROUP = 4
N_EXPERTS = N_GROUPS * EXP_PER_GROUP
TOP_K = 2
D_EXPERT = 512

V7X_VMEM_BYTES = 64 * 1024 * 1024
MOE_TILE_M = 256

f32 = jnp.float32
bf16 = jnp.bfloat16


def rmsnorm(x, w):
    y = x * lax.rsqrt(jnp.mean(x * x, axis=-1, keepdims=True) + EPS)
    return y * w


def ada_mod(cond, w, b):
    mod = jax.nn.silu(cond) @ w + b
    return [m[:, None, :] for m in jnp.split(mod, 6, axis=-1)]


def modulate(x, shift, scale):
    return x * (1.0 + scale) + shift


def rope_tables(row_pos, col_pos):
    freqs = ROPE_BASE ** (-jnp.arange(0, ROPE_AXIS_DIMS, 2, dtype=f32) / ROPE_AXIS_DIMS)
    ang_r = row_pos.astype(f32)[:, None] * freqs[None, :]
    ang_c = col_pos.astype(f32)[:, None] * freqs[None, :]
    return jnp.cos(ang_r), jnp.sin(ang_r), jnp.cos(ang_c), jnp.sin(ang_c)


def rotate(x, cos, sin):
    x1, x2 = jnp.split(x, 2, axis=-1)
    cos = cos[None, :, None, :]
    sin = sin[None, :, None, :]
    return jnp.concatenate([x1 * cos - x2 * sin, x1 * sin + x2 * cos], axis=-1)


def apply_axial_rope(x, tables):
    cr, sr, cc, sc = tables
    xr, xc = jnp.split(x, 2, axis=-1)
    return jnp.concatenate([rotate(xr, cr, sr), rotate(xc, cc, sc)], axis=-1)


GLA_QK = GLA_HEADS * GLA_DK
GLA_V = GLA_HEADS * GLA_DV


def _gla_kernel(q_ref, v_ref, gg_ref, ldf_ref, ldb_ref, kt_ref, ldft_ref, ldbt_ref, s0_ref, norm_ref,
                o_ref, s_ref, *, seq_len):
    n = seq_len // GLA_CHUNK
    C = GLA_CHUNK
    ti = lax.broadcasted_iota(jnp.int32, (C, C), 0)
    tj = lax.broadcasted_iota(jnp.int32, (C, C), 1)
    keep = (tj <= ti, tj >= ti)
    tri = tuple(m.astype(bf16) for m in keep)
    tri_t = (tri[1], tri[0])
    ones = jnp.ones((C, GLA_DV), bf16)
    s_ref[...] = s0_ref[...]
    o_ref[...] = jnp.zeros_like(o_ref)
    ld_refs = ((ldf_ref, ldft_ref), (ldb_ref, ldbt_ref))

    def split3(x):
        hi = x.astype(bf16)
        rest = x - hi.astype(f32)
        mid = rest.astype(bf16)
        return hi, mid, (rest - mid.astype(f32)).astype(bf16)

    def sums(a, b):
        if isinstance(b, tuple):
            return sum(jnp.dot(a, p, preferred_element_type=f32) for p in b)
        return sum(jnp.dot(p, b, preferred_element_type=f32) for p in a)

    def chunk(c, carry):
        pending = []
        for d in range(2):
            cc = c if d == 0 else n - 1 - c
            rows = pl.ds(pl.multiple_of(cc * C, C), C)
            ld_ref, ldt_ref = ld_refs[d]
            ld_t3 = split3(ldt_ref[cc])
            b_all = sums(tri[d], split3(ld_ref[rows, :]))
            bt_all = sums(ld_t3, tri_t[d])
            total_all = sums(ld_t3, ones)
            for h in range(GLA_HEADS):
                ks = slice(h * GLA_DK, (h + 1) * GLA_DK)
                vs = slice(h * GLA_DV, (h + 1) * GLA_DV)
                q = q_ref[rows, ks] * (GLA_DK ** -0.5)
                v = v_ref[rows, vs].astype(bf16)
                k_t = kt_ref[cc, ks, :]
                b, b_t, total = b_all[:, ks], bt_all[ks], total_all[ks]
                q_e = (q * jnp.exp(b)).astype(bf16)
                k_e = (k_t * jnp.exp(-b_t)).astype(bf16)
                k_s = (k_t * jnp.exp(total[:, :C] - b_t)).astype(bf16)
                att = jnp.where(keep[d], jnp.dot(q_e, k_e, preferred_element_type=f32), 0.0).astype(bf16)
                s = s_ref[d, h]
                o = (jnp.dot(q_e, s.astype(bf16), preferred_element_type=f32)
                     + jnp.dot(att, v, preferred_element_type=f32))
                s_new = s * jnp.exp(total) + jnp.dot(k_s, v, preferred_element_type=f32)
                pending.append((d, h, rows, vs, o_ref[rows, vs] + o, s_new))
        for d, h, rows, vs, o, s_new in pending:
            o_ref[rows, vs] = o
            s_ref[d, h] = s_new
        return carry

    lax.fori_loop(0, n, chunk, 0)
    for h in range(GLA_HEADS):
        vs = slice(h * GLA_DV, (h + 1) * GLA_DV)
        x = o_ref[:, vs]
        y = x * lax.rsqrt(jnp.mean(x * x, axis=-1, keepdims=True) + EPS) * norm_ref[...]
        g = gg_ref[:, vs]
        o_ref[:, vs] = y * (g * jax.nn.sigmoid(g))


def gla_bidir_gated(z, ld_f, ld_b, s0, gla_norm):
    B, L, _ = z.shape
    n = L // GLA_CHUNK
    assert n % 2 == 0

    def per_chunk_t(a):
        return a.reshape(B, n, GLA_CHUNK, GLA_QK).transpose(0, 1, 3, 2)

    state = pl.BlockSpec((None, 2, GLA_HEADS, GLA_DK, GLA_DV), lambda b: (b, 0, 0, 0, 0))
    ld = pl.BlockSpec((None, L, GLA_QK), lambda b: (b, 0, 0))
    ld_t = pl.BlockSpec((None, n, GLA_QK, GLA_CHUNK), lambda b: (b, 0, 0, 0))
    return pl.pallas_call(
        functools.partial(_gla_kernel, seq_len=L),
        grid=(B,),
        in_specs=[pl.BlockSpec((None, L, GLA_QK), lambda b: (b, 0, 0)),
                  pl.BlockSpec((None, L, GLA_V), lambda b: (b, 0, 1)),
                  pl.BlockSpec((None, L, GLA_V), lambda b: (b, 0, 2)),
                  ld, ld, ld_t, ld_t, ld_t, state,
                  pl.BlockSpec((1, GLA_DV), lambda b: (0, 0))],
        out_specs=[pl.BlockSpec((None, L, GLA_V), lambda b: (b, 0, 0)), state],
        out_shape=[jax.ShapeDtypeStruct((B, L, GLA_V), f32),
                   jax.ShapeDtypeStruct((B, 2, GLA_HEADS, GLA_DK, GLA_DV), f32)],
        compiler_params=pltpu.CompilerParams(
            dimension_semantics=("arbitrary",),
            vmem_limit_bytes=min(V7X_VMEM_BYTES, 2 * L * (9 * GLA_QK + 3 * GLA_V) * 4 + (16 << 20)),
        ),
        name="gla_bidir",
    )(z, z, z, ld_f, ld_b, per_chunk_t(z[..., GLA_QK:2 * GLA_QK]), per_chunk_t(ld_f), per_chunk_t(ld_b),
      s0, gla_norm.reshape(1, GLA_DV))


def ctx_attn(q, k, v, sink):
    B, L = q.shape[:2]
    scale = ATT_HD ** -0.5
    qg = q.reshape(B, L, ATT_KV_HEADS, ATT_GROUP, ATT_HD)
    sink_b = sink.reshape(ATT_KV_HEADS, ATT_GROUP)[None, :, :, None, None]
    s = jnp.einsum('bqhgd,bkhd->bhgqk', qg, k) * scale
    s_sink = jnp.broadcast_to(sink_b, s.shape[:-1] + (1,))
    p = jax.nn.softmax(jnp.concatenate([s_sink, s], axis=-1), axis=-1)[..., 1:]
    return jnp.einsum('bhgqk,bkhd->bqhgd', p, v).reshape(B, L, ATT_HEADS * ATT_HD)


def window_attn_latent(q, k, v, k_ctx, v_ctx, sink):
    B, S = q.shape[:2]
    nb = S // ATT_BLOCK
    Lc = k_ctx.shape[1]
    scale = ATT_HD ** -0.5
    qb = q.reshape(B, nb, ATT_BLOCK, ATT_KV_HEADS, ATT_GROUP, ATT_HD).transpose(1, 0, 2, 3, 4, 5)
    pad = ((0, 0), (ATT_BLOCK, ATT_BLOCK), (0, 0), (0, 0))
    kp = jnp.pad(k, pad)
    vp = jnp.pad(v, pad)
    sink_b = sink.reshape(ATT_KV_HEADS, ATT_GROUP)[None, :, :, None, None]
    offs_q = jnp.arange(ATT_BLOCK)
    offs_k = jnp.arange(3 * ATT_BLOCK) - ATT_BLOCK

    def block(args):
        bi, q_blk = args
        k_band = lax.dynamic_slice_in_dim(kp, bi * ATT_BLOCK, 3 * ATT_BLOCK, axis=1)
        v_band = lax.dynamic_slice_in_dim(vp, bi * ATT_BLOCK, 3 * ATT_BLOCK, axis=1)
        qpos = bi * ATT_BLOCK + offs_q
        kpos = bi * ATT_BLOCK + offs_k
        valid = (jnp.abs(qpos[:, None] - kpos[None, :]) <= WINDOW) & (kpos >= 0)[None, :] & (kpos < S)[None, :]
        s_lat = jnp.einsum('bqhgd,bkhd->bhgqk', q_blk, k_band) * scale
        s_lat = jnp.where(valid, s_lat, NEG)
        s_ctx = jnp.einsum('bqhgd,bchd->bhgqc', q_blk, k_ctx) * scale
        s_sink = jnp.broadcast_to(sink_b, s_ctx.shape[:-1] + (1,))
        p = jax.nn.softmax(jnp.concatenate([s_sink, s_ctx, s_lat], axis=-1), axis=-1)
        o = (jnp.einsum('bhgqc,bchd->bqhgd', p[..., 1:1 + Lc], v_ctx)
             + jnp.einsum('bhgqk,bkhd->bqhgd', p[..., 1 + Lc:], v_band))
        return o.reshape(B, ATT_BLOCK, ATT_HEADS * ATT_HD)

    o = lax.map(block, (jnp.arange(nb), qb))
    return o.transpose(1, 0, 2, 3).reshape(B, S, ATT_HEADS * ATT_HD)


def even_projections(h, w_in, dec_w, dec_b):
    B, L, _ = h.shape
    z = h @ w_in
    aq, ak, av, lr = jnp.split(z[..., IN_SPLITS[3]:], [s - IN_SPLITS[3] for s in IN_SPLITS[4:]], axis=-1)
    lr_f, lr_b = jnp.split(lr, 2, axis=-1)

    def logdecay(lr_d, w, b):
        return jax.nn.log_sigmoid(lr_d @ w + b) / GLA_GATE_TEMP

    ld_f = logdecay(lr_f, dec_w[0], dec_b[0])
    ld_b = logdecay(lr_b, dec_w[1], dec_b[1])
    aq = aq.reshape(B, L, ATT_HEADS, ATT_HD)
    ak = ak.reshape(B, L, ATT_KV_HEADS, ATT_HD)
    av = av.reshape(B, L, ATT_KV_HEADS, ATT_HD)
    return z, ld_f, ld_b, aq, ak, av


def even_mixer_ctx(h, w_in, w_out, dec_w, dec_b, gla_norm, sink):
    z, ld_f, ld_b, aq, ak, av = even_projections(h, w_in, dec_w, dec_b)
    zero = jnp.zeros((h.shape[0], 2, GLA_HEADS, GLA_DK, GLA_DV), f32)
    o_gla, s_fin = gla_bidir_gated(z, ld_f, ld_b, zero, gla_norm)
    o_att = ctx_attn(aq, ak, av, sink)
    return jnp.concatenate([o_gla, o_att], axis=-1) @ w_out, ak, av, s_fin


def even_mixer_lat(h, w_in, w_out, dec_w, dec_b, gla_norm, sink, rope, k_ctx, v_ctx, s0):
    z, ld_f, ld_b, aq, ak, av = even_projections(h, w_in, dec_w, dec_b)
    o_gla, _ = gla_bidir_gated(z, ld_f, ld_b, s0, gla_norm)
    o_att = window_attn_latent(apply_axial_rope(aq, rope), apply_axial_rope(ak, rope), av, k_ctx, v_ctx, sink)
    return jnp.concatenate([o_gla, o_att], axis=-1) @ w_out


def centred_shift(x):
    xp = jnp.pad(x, ((0, 0), (1, 1), (0, 0)))
    return 0.5 * (xp[:, :-2] + xp[:, 2:])


LANES = 128
SUBLANES = 8
RWKV_TC = 32


def _rwkv_scan_kernel(r_ref, v_ref, a_ref, w_ref, k_ref, b_ref, s0_ref, o_ref, s_ref, *, tc, nv, n_dir_groups):
    backward = pl.program_id(0) >= n_dir_groups

    @pl.when(pl.program_id(1) == 0)
    def _():
        s_ref[...] = s0_ref[...]

    def step(i, carry):
        t = jnp.where(backward, tc - 1 - i, i)
        alpha, w, beta, kd, r = a_ref[t], w_ref[t], b_ref[t], k_ref[t], r_ref[t]
        for vb in range(nv // SUBLANES):
            outs = []
            for v in range(vb * SUBLANES, (vb + 1) * SUBLANES):
                s = s_ref[v]
                sa = jnp.sum(s * alpha, axis=0, keepdims=True)
                s_new = s * w + sa * beta + v_ref[t, pl.ds(v, 1), :] * kd
                s_ref[v] = s_new
                outs.append(jnp.sum(s_new * r, axis=0, keepdims=True))
            o_ref[t, vb * SUBLANES:(vb + 1) * SUBLANES, :] = jnp.concatenate(outs, axis=0)
        return carry

    lax.fori_loop(0, tc, step, 0)


def rwkv_scan_lanes(r, v, nkk, w, k, b, s0):
    gd, L, nv, _ = v.shape
    tc = RWKV_TC
    n = L // tc

    def time_block(g, j):
        return jnp.where(g >= gd, n - 1 - j, j)

    def blk(rows, shared):
        return pl.BlockSpec((None, tc, rows, LANES), lambda g, j: (g % gd if shared else g, time_block(g, j), 0, 0))

    state = pl.BlockSpec((None, nv, RWKV_HEAD, LANES), lambda g, j: (g, 0, 0, 0))
    block_bytes = tc * RWKV_HEAD * LANES * 4
    return pl.pallas_call(
        functools.partial(_rwkv_scan_kernel, tc=tc, nv=nv, n_dir_groups=gd),
        grid=(2 * gd, n),
        in_specs=[blk(RWKV_HEAD, True), blk(nv, True), blk(RWKV_HEAD, True),
                  blk(RWKV_HEAD, False), blk(RWKV_HEAD, False), blk(RWKV_HEAD, False), state],
        out_specs=[blk(nv, False), state],
        out_shape=[jax.ShapeDtypeStruct((2 * gd, L, nv, LANES), f32),
                   jax.ShapeDtypeStruct((2 * gd, nv, RWKV_HEAD, LANES), f32)],
        compiler_params=pltpu.CompilerParams(
            dimension_semantics=("arbitrary", "arbitrary"),
            vmem_limit_bytes=2 * 7 * block_bytes + 4 * nv * RWKV_HEAD * LANES * 4 + (8 << 20),
        ),
        name="rwkv_scan",
    )(r, v, nkk, w, k, b, s0)


def rwkv_scan_bidir(r, v, nkk, fwd, bwd, s_f0, s_b0):
    B, L, H, N = r.shape
    S = B * H
    parts = max(1, LANES // S)
    gd = S * parts // LANES
    nv = N // parts

    def seq_last(x):
        return x.transpose(1, 3, 0, 2).reshape(L, N, S)

    def key_rows(x):
        if parts > 1:
            return jnp.concatenate([x] * parts, axis=-1)[None]
        return x.reshape(x.shape[0], N, gd, LANES).transpose(2, 0, 1, 3)

    def value_rows(x):
        if parts > 1:
            return x.reshape(x.shape[0], parts, nv, S).transpose(0, 2, 1, 3).reshape(1, x.shape[0], nv, LANES)
        return key_rows(x)

    def value_rows_inv(y):
        T = y.shape[1]
        if parts > 1:
            return y.reshape(T, nv, parts, S).transpose(0, 2, 1, 3).reshape(T, N, S)
        return y.transpose(1, 2, 0, 3).reshape(T, N, S)

    def state_in(s):
        return value_rows(s.transpose(3, 2, 0, 1).reshape(N, N, S)).transpose(0, 2, 1, 3)

    def state_out(s):
        return value_rows_inv(s.transpose(0, 2, 1, 3)).reshape(N, N, B, H).transpose(2, 3, 1, 0)

    w, k, b = (jnp.concatenate([key_rows(seq_last(f)), key_rows(seq_last(g))], axis=0) for f, g in zip(fwd, bwd))
    out, s_fin = rwkv_scan_lanes(key_rows(seq_last(r)), value_rows(seq_last(v)), key_rows(seq_last(nkk)), w, k, b,
                                 jnp.concatenate([state_in(s_f0), state_in(s_b0)], axis=0))
    o = value_rows_inv(out[:gd]) + value_rows_inv(out[gd:])
    return o.reshape(L, N, B, H).transpose(2, 0, 3, 1), state_out(s_fin[:gd]), state_out(s_fin[gd:])


def rwkv_mix(h, mu, wr, wk, wv, wo, w0, w1, w2, a0, a1, a2, g1, g2, k_k, k_a, r_k, lnx_w, lnx_b, s_f0, s_b0):
    B, L, D = h.shape
    H, N = RWKV_HEADS, RWKV_HEAD
    xx = centred_shift(h) - h
    xr, xw, xk, xv, xa, xg = [h + xx * mu[j] for j in range(6)]
    r = (xr @ wr).reshape(B, L, H, N)
    k = (xk @ wk).reshape(B, L, H, N)
    v = (xv @ wv).reshape(B, L, H, N)
    g = jax.nn.sigmoid(xg @ g1) @ g2
    kk = k * k_k.reshape(H, N)
    kk = kk * lax.rsqrt(jnp.sum(kk * kk, axis=-1, keepdims=True) + 1e-12)

    def direction(d):
        w_raw = -jax.nn.softplus(-(w0[d] + jnp.tanh(xw @ w1[d]) @ w2[d])) - 0.5
        decay = jnp.exp(-jnp.exp(w_raw)).reshape(B, L, H, N)
        a = jax.nn.sigmoid(a0[d] + (xa @ a1[d]) @ a2[d]).reshape(B, L, H, N)
        k_d = k * (1.0 + (a - 1.0) * k_a.reshape(H, N))
        return decay, k_d, kk * a

    w_f, k_f, b_f = direction(0)
    w_b, k_b, b_b = direction(1)
    o, s_f, s_b = rwkv_scan_bidir(r, v, -kk, (w_f, k_f, b_f), (w_b, k_b, b_b), s_f0, s_b0)
    m = jnp.mean(o, axis=-1, keepdims=True)
    var = jnp.mean(jnp.square(o - m), axis=-1, keepdims=True)
    o = (o - m) * lax.rsqrt(var + RWKV_LN_EPS) * lnx_w.reshape(H, N) + lnx_b.reshape(H, N)
    bonus = jnp.sum(r * 0.5 * (k_f + k_b) * r_k, axis=-1, keepdims=True) * v
    o = (o + bonus).reshape(B, L, D)
    return (o * g) @ wo, s_f, s_b


GROUP_HID = EXP_PER_GROUP * D_EXPERT


def _moe_group_kernel(tile_group_ref, n_tiles_ref, xs_ref, gates_ref, wg_ref, wu_ref, wd_ref, ys_ref,
                      wg_bf, wu_bf, wd_bf):
    i = pl.program_id(0)
    prev_group = tile_group_ref[jnp.maximum(i - 1, 0)]
    group_changed = jnp.logical_or(i == 0, tile_group_ref[i] != prev_group)

    @pl.when(group_changed)
    def _():
        for e in range(EXP_PER_GROUP):
            hs = slice(e * D_EXPERT, (e + 1) * D_EXPERT)
            wg_bf[:, hs] = wg_ref[e].astype(bf16)
            wu_bf[:, hs] = wu_ref[e].astype(bf16)
            wd_bf[hs, :] = wd_ref[e].astype(bf16)

    @pl.when(i < n_tiles_ref[0])
    def _():
        x = xs_ref[...].astype(bf16)
        g = jnp.dot(x, wg_bf[...], preferred_element_type=f32)
        u = jnp.dot(x, wu_bf[...], preferred_element_type=f32)
        gate = jnp.concatenate([jnp.broadcast_to(gates_ref[:, e:e + 1], (MOE_TILE_M, D_EXPERT))
                                for e in range(EXP_PER_GROUP)], axis=1)
        hid = (g * jax.nn.sigmoid(g)) * u * gate
        ys_ref[...] = jnp.dot(hid.astype(bf16), wd_bf[...], preferred_element_type=f32)

    @pl.when(i >= n_tiles_ref[0])
    def _():
        ys_ref[...] = jnp.zeros_like(ys_ref)


def moe_group_experts(xs, gates, tile_group, n_tiles, w_gate, w_up, w_down):
    P, D = xs.shape
    max_tiles = P // MOE_TILE_M
    weight_bytes = 3 * EXP_PER_GROUP * D * D_EXPERT * (4 + 2)
    tile_bytes = 2 * MOE_TILE_M * (2 * D * 4 + LANES * 4) + 4 * MOE_TILE_M * GROUP_HID * 4
    once = pl.Buffered(1)
    grid_spec = pltpu.PrefetchScalarGridSpec(
        num_scalar_prefetch=2,
        grid=(max_tiles,),
        in_specs=[
            pl.BlockSpec((MOE_TILE_M, D), lambda i, tg, nt: (i, 0)),
            pl.BlockSpec((MOE_TILE_M, EXP_PER_GROUP), lambda i, tg, nt: (i, 0)),
            pl.BlockSpec((EXP_PER_GROUP, D, D_EXPERT), lambda i, tg, nt: (tg[i], 0, 0), pipeline_mode=once),
            pl.BlockSpec((EXP_PER_GROUP, D, D_EXPERT), lambda i, tg, nt: (tg[i], 0, 0), pipeline_mode=once),
            pl.BlockSpec((EXP_PER_GROUP, D_EXPERT, D), lambda i, tg, nt: (tg[i], 0, 0), pipeline_mode=once),
        ],
        out_specs=pl.BlockSpec((MOE_TILE_M, D), lambda i, tg, nt: (i, 0)),
        scratch_shapes=[
            pltpu.VMEM((D, GROUP_HID), bf16),
            pltpu.VMEM((D, GROUP_HID), bf16),
            pltpu.VMEM((GROUP_HID, D), bf16),
        ],
    )
    return pl.pallas_call(
        _moe_group_kernel,
        grid_spec=grid_spec,
        out_shape=jax.ShapeDtypeStruct((P, D), f32),
        compiler_params=pltpu.CompilerParams(
            dimension_semantics=("arbitrary",),
            vmem_limit_bytes=min(V7X_VMEM_BYTES - (4 << 20), weight_bytes + tile_bytes + (8 << 20)),
        ),
        name="moe_group_experts",
    )(tile_group, n_tiles, xs, gates, w_gate, w_up, w_down)


def hier_moe(t, w_grp, b_grp, w_exp, b_exp, w_gate, w_up, w_down):
    T, D = t.shape
    logits = jnp.dot(t, jnp.concatenate([w_grp, w_exp], axis=1), precision=lax.Precision.HIGHEST)
    grp_logits = logits[:, :N_GROUPS] + b_grp
    grp_prob = jax.nn.softmax(grp_logits, axis=-1)
    g_top = jnp.argmax(grp_logits, axis=-1).astype(jnp.int32)
    in_group = g_top[:, None] == jnp.arange(N_GROUPS, dtype=jnp.int32)[None, :]
    p_g = jnp.sum(jnp.where(in_group, grp_prob, 0.0), axis=1, keepdims=True)
    exp_logits = (logits[:, N_GROUPS:] + b_exp).reshape(-1, N_GROUPS, EXP_PER_GROUP)
    sel = jnp.sum(jnp.where(in_group[:, :, None], exp_logits, 0.0), axis=1)
    top_v, top_i = lax.top_k(sel, TOP_K)
    wts = p_g * jax.nn.softmax(top_v, axis=-1)
    gates = jnp.sum(jax.nn.one_hot(top_i, EXP_PER_GROUP, dtype=f32) * wts[..., None], axis=1)

    order = jnp.argsort(g_top, stable=True).astype(jnp.int32)
    grp_i32 = in_group.astype(jnp.int32)
    counts = jnp.sum(grp_i32, axis=0)
    rank = jnp.sum(jnp.where(in_group, jnp.cumsum(grp_i32, axis=0), 0), axis=1) - 1
    tiles_per = (counts + MOE_TILE_M - 1) // MOE_TILE_M
    tile_end = jnp.cumsum(tiles_per)
    start_padded = (tile_end - tiles_per) * MOE_TILE_M
    start_sorted = jnp.cumsum(counts) - counts
    max_tiles = T // MOE_TILE_M + N_GROUPS
    n_tiles = tile_end[-1:].astype(jnp.int32)
    tile_ids = jnp.arange(max_tiles, dtype=jnp.int32)
    tile_group = jnp.minimum(jnp.sum((tile_ids[:, None] >= tile_end[None, :]).astype(jnp.int32), axis=1),
                             N_GROUPS - 1).astype(jnp.int32)
    tile_group = jnp.where(tile_ids < n_tiles[0], tile_group, tile_group[jnp.maximum(n_tiles[0] - 1, 0)])
    row_group = jnp.repeat(tile_group, MOE_TILE_M)
    row_in_group = jnp.arange(max_tiles * MOE_TILE_M, dtype=jnp.int32) - start_padded[row_group]
    row_valid = (row_in_group >= 0) & (row_in_group < counts[row_group])
    row_tok = jnp.where(row_valid, order[jnp.clip(start_sorted[row_group] + row_in_group, 0, T - 1)], 0)
    pos = start_padded[g_top] + rank

    xs = jnp.take(t, row_tok, axis=0)
    row_gates = jnp.where(row_valid[:, None], jnp.take(gates, row_tok, axis=0), 0.0)
    ys = moe_group_experts(xs, row_gates, tile_group, n_tiles, w_gate, w_up, w_down)
    return jnp.take(ys, pos, axis=0)


def kernel(x_prompt, x_sample, c, cache_attn_k, cache_attn_v, state_gla, state_rwkv, c_ctx, ada_w, ada_b, norm_mix, norm_ffn, norm_out, ev_w_in, ev_w_out, gla_dec_w, gla_dec_b, gla_norm, att_sink, rw_mu, rw_wr, rw_wk, rw_wv, rw_wo, rw_w0, rw_w1, rw_w2, rw_a0, rw_a1, rw_a2, rw_g1, rw_g2, rw_kk, rw_ka, rw_rk, rw_lnx_w, rw_lnx_b, moe_w_grp, moe_b_grp, moe_w_exp, moe_b_exp, moe_w_gate, moe_w_up, moe_w_down):
    n_lat = x_sample.shape[1]
    rows = n_lat // GRID_W
    row_pos = jnp.repeat(jnp.arange(rows), GRID_W)
    col_pos = jnp.tile(jnp.arange(GRID_W), rows)
    rope = rope_tables(row_pos, col_pos)
    Bc, Lc, D = x_prompt.shape
    Bl, Ll, _ = x_sample.shape

    xc, xl = x_prompt, x_sample
    cond_ctx = c_ctx[None, :]
    ks_out, vs_out, gla_out, rwkv_out = [], [], [], []
    for l in range(DEPTH):
        sc1, scl1, gc1, sc2, scl2, gc2 = ada_mod(cond_ctx, ada_w[l], ada_b[l])
        sl1, sll1, gl1, sl2, sll2, gl2 = ada_mod(c, ada_w[l], ada_b[l])
        hc = modulate(rmsnorm(xc, norm_mix[l]), sc1, scl1)
        hl = modulate(rmsnorm(xl, norm_mix[l]), sl1, sll1)
        i = l // 2
        if l % 2 == 0:
            oc, k_c, v_c, s_gla = even_mixer_ctx(hc, ev_w_in[i], ev_w_out[i], gla_dec_w[i], gla_dec_b[i],
                                                 gla_norm[i], att_sink[i])
            ol = even_mixer_lat(hl, ev_w_in[i], ev_w_out[i], gla_dec_w[i], gla_dec_b[i], gla_norm[i], att_sink[i],
                                rope, cache_attn_k[:, i], cache_attn_v[:, i], state_gla[:, i])
            ks_out.append(k_c)
            vs_out.append(v_c)
            gla_out.append(s_gla)
        else:
            rw = (rw_mu[i], rw_wr[i], rw_wk[i], rw_wv[i], rw_wo[i], rw_w0[i], rw_w1[i], rw_w2[i], rw_a0[i],
                  rw_a1[i], rw_a2[i], rw_g1[i], rw_g2[i], rw_kk[i], rw_ka[i], rw_rk[i], rw_lnx_w[i], rw_lnx_b[i])
            zero = jnp.zeros((Bc, RWKV_HEADS, RWKV_HEAD, RWKV_HEAD), f32)
            oc, s_f, s_b = rwkv_mix(hc, *rw, zero, zero)
            ol, _, _ = rwkv_mix(hl, *rw, state_rwkv[:, i, 0], state_rwkv[:, i, 1])
            rwkv_out.append(jnp.stack([s_f, s_b], axis=1))
        xc = xc + gc1 * oc
        xl = xl + gl1 * ol
        tc = modulate(rmsnorm(xc, norm_ffn[l]), sc2, scl2).reshape(Bc * Lc, D)
        tl = modulate(rmsnorm(xl, norm_ffn[l]), sl2, sll2).reshape(Bl * Ll, D)
        y = hier_moe(jnp.concatenate([tc, tl], axis=0), moe_w_grp[l], moe_b_grp[l], moe_w_exp[l], moe_b_exp[l],
                     moe_w_gate[l], moe_w_up[l], moe_w_down[l])
        xc = xc + gc2 * y[:Bc * Lc].reshape(Bc, Lc, D)
        xl = xl + gl2 * y[Bc * Lc:].reshape(Bl, Ll, D)

    y_prompt = rmsnorm(xc, norm_out)
    y_sample = rmsnorm(xl, norm_out)
    new_attn_k = jnp.stack(ks_out, axis=1)
    new_attn_v = jnp.stack(vs_out, axis=1)
    new_gla = jnp.stack(gla_out, axis=1)
    new_rwkv = jnp.stack(rwkv_out, axis=1)
    return (y_prompt, y_sample, new_attn_k, new_attn_v, new_gla, new_rwkv)
```

```python
import functools

import jax
import jax.numpy as jnp
import numpy as np
from jax import lax
from jax.experimental import pallas as pl
from jax.experimental.pallas import tpu as pltpu

D_MODEL = 1024
DEPTH = 4
GRID_W = 64
EPS = 1e-6
GLA_HEADS = 4
GLA_DK = 64
GLA_DV = 128
GLA_LOWRANK = 16
GLA_GATE_TEMP = 16.0
GLA_CHUNK = 64
ATT_HEADS = 8
ATT_KV_HEADS = 2
ATT_GROUP = ATT_HEADS // ATT_KV_HEADS
ATT_HD = 64
WINDOW = 128
ATT_BLOCK = 128
ROPE_BASE = 10000.0
ROPE_AXIS_DIMS = ATT_HD // 2
NEG = -1e30
IN_WIDTHS = (GLA_HEADS * GLA_DK, GLA_HEADS * GLA_DK, GLA_HEADS * GLA_DV, GLA_HEADS * GLA_DV,
             ATT_HEADS * ATT_HD, ATT_KV_HEADS * ATT_HD, ATT_KV_HEADS * ATT_HD, 2 * GLA_LOWRANK)
IN_SPLITS = tuple(int(v) for v in np.cumsum(IN_WIDTHS)[:-1])
RWKV_HEAD = 64
RWKV_HEADS = D_MODEL // RWKV_HEAD
RWKV_LN_EPS = 64e-5
N_GROUPS = 4
EXP_PER_GROUP = 4
N_EXPERTS = N_GROUPS * EXP_PER_GROUP
TOP_K = 2
D_EXPERT = 512

V7X_VMEM_BYTES = 64 * 1024 * 1024
MOE_TILE_M = 256

f32 = jnp.float32
bf16 = jnp.bfloat16


def rmsnorm(x, w):
    y = x * lax.rsqrt(jnp.mean(x * x, axis=-1, keepdims=True) + EPS)
    return y * w


def ada_mod(cond, w, b):
    mod = jax.nn.silu(cond) @ w + b
    return [m[:, None, :] for m in jnp.split(mod, 6, axis=-1)]


def modulate(x, shift, scale):
    return x * (1.0 + scale) + shift


def rope_tables(row_pos, col_pos):
    freqs = ROPE_BASE ** (-jnp.arange(0, ROPE_AXIS_DIMS, 2, dtype=f32) / ROPE_AXIS_DIMS)
    ang_r = row_pos.astype(f32)[:, None] * freqs[None, :]
    ang_c = col_pos.astype(f32)[:, None] * freqs[None, :]
    return jnp.cos(ang_r), jnp.sin(ang_r), jnp.cos(ang_c), jnp.sin(ang_c)


def rotate(x, cos, sin):
    x1, x2 = jnp.split(x, 2, axis=-1)
    cos = cos[None, :, None, :]
    sin = sin[None, :, None, :]
    return jnp.concatenate([x1 * cos - x2 * sin, x1 * sin + x2 * cos], axis=-1)


def apply_axial_rope(x, tables):
    cr, sr, cc, sc = tables
    xr, xc = jnp.split(x, 2, axis=-1)
    return jnp.concatenate([rotate(xr, cr, sr), rotate(xc, cc, sc)], axis=-1)


GLA_QK = GLA_HEADS * GLA_DK
GLA_V = GLA_HEADS * GLA_DV


def _gla_kernel(q_ref, v_ref, gg_ref, ldf_ref, ldb_ref, kt_ref, ldft_ref, ldbt_ref, s0_ref, norm_ref,
                o_ref, s_ref, *, seq_len):
    n = seq_len // GLA_CHUNK
    C = GLA_CHUNK
    ti = lax.broadcasted_iota(jnp.int32, (C, C), 0)
    tj = lax.broadcasted_iota(jnp.int32, (C, C), 1)
    keep = (tj <= ti, tj >= ti)
    tri = tuple(m.astype(bf16) for m in keep)
    tri_t = (tri[1], tri[0])
    ones = jnp.ones((C, GLA_DV), bf16)
    s_ref[...] = s0_ref[...]
    o_ref[...] = jnp.zeros_like(o_ref)
    ld_refs = ((ldf_ref, ldft_ref), (ldb_ref, ldbt_ref))

    def split3(x):
        hi = x.astype(bf16)
        rest = x - hi.astype(f32)
        mid = rest.astype(bf16)
        return hi, mid, (rest - mid.astype(f32)).astype(bf16)

    def sums(a, b):
        if isinstance(b, tuple):
            return sum(jnp.dot(a, p, preferred_element_type=f32) for p in b)
        return sum(jnp.dot(p, b, preferred_element_type=f32) for p in a)

    def chunk(c, carry):
        pending = []
        for d in range(2):
            cc = c if d == 0 else n - 1 - c
            rows = pl.ds(pl.multiple_of(cc * C, C), C)
            ld_ref, ldt_ref = ld_refs[d]
            ld_t3 = split3(ldt_ref[cc])
            b_all = sums(tri[d], split3(ld_ref[rows, :]))
            bt_all = sums(ld_t3, tri_t[d])
            total_all = sums(ld_t3, ones)
            for h in range(GLA_HEADS):
                ks = slice(h * GLA_DK, (h + 1) * GLA_DK)
                vs = slice(h * GLA_DV, (h + 1) * GLA_DV)
                q = q_ref[rows, ks] * (GLA_DK ** -0.5)
                v = v_ref[rows, vs].astype(bf16)
                k_t = kt_ref[cc, ks, :]
                b, b_t, total = b_all[:, ks], bt_all[ks], total_all[ks]
                q_e = (q * jnp.exp(b)).astype(bf16)
                k_e = (k_t * jnp.exp(-b_t)).astype(bf16)
                k_s = (k_t * jnp.exp(total[:, :C] - b_t)).astype(bf16)
                att = jnp.where(keep[d], jnp.dot(q_e, k_e, preferred_element_type=f32), 0.0).astype(bf16)
                s = s_ref[d, h]
                o = (jnp.dot(q_e, s.astype(bf16), preferred_element_type=f32)
                     + jnp.dot(att, v, preferred_element_type=f32))
                s_new = s * jnp.exp(total) + jnp.dot(k_s, v, preferred_element_type=f32)
                pending.append((d, h, rows, vs, o_ref[rows, vs] + o, s_new))
        for d, h, rows, vs, o, s_new in pending:
            o_ref[rows, vs] = o
            s_ref[d, h] = s_new
        return carry

    lax.fori_loop(0, n, chunk, 0)
    for h in range(GLA_HEADS):
        vs = slice(h * GLA_DV, (h + 1) * GLA_DV)
        x = o_ref[:, vs]
        y = x * lax.rsqrt(jnp.mean(x * x, axis=-1, keepdims=True) + EPS) * norm_ref[...]
        g = gg_ref[:, vs]
        o_ref[:, vs] = y * (g * jax.nn.sigmoid(g))


def gla_bidir_gated(z, ld_f, ld_b, s0, gla_norm):
    B, L, _ = z.shape
    n = L // GLA_CHUNK
    assert n % 2 == 0

    def per_chunk_t(a):
        return a.reshape(B, n, GLA_CHUNK, GLA_QK).transpose(0, 1, 3, 2)

    state = pl.BlockSpec((None, 2, GLA_HEADS, GLA_DK, GLA_DV), lambda b: (b, 0, 0, 0, 0))
    ld = pl.BlockSpec((None, L, GLA_QK), lambda b: (b, 0, 0))
    ld_t = pl.BlockSpec((None, n, GLA_QK, GLA_CHUNK), lambda b: (b, 0, 0, 0))
    return pl.pallas_call(
        functools.partial(_gla_kernel, seq_len=L),
        grid=(B,),
        in_specs=[pl.BlockSpec((None, L, GLA_QK), lambda b: (b, 0, 0)),
                  pl.BlockSpec((None, L, GLA_V), lambda b: (b, 0, 1)),
                  pl.BlockSpec((None, L, GLA_V), lambda b: (b, 0, 2)),
                  ld, ld, ld_t, ld_t, ld_t, state,
                  pl.BlockSpec((1, GLA_DV), lambda b: (0, 0))],
        out_specs=[pl.BlockSpec((None, L, GLA_V), lambda b: (b, 0, 0)), state],
        out_shape=[jax.ShapeDtypeStruct((B, L, GLA_V), f32),
                   jax.ShapeDtypeStruct((B, 2, GLA_HEADS, GLA_DK, GLA_DV), f32)],
        compiler_params=pltpu.CompilerParams(
            dimension_semantics=("arbitrary",),
            vmem_limit_bytes=min(V7X_VMEM_BYTES, 2 * L * (9 * GLA_QK + 3 * GLA_V) * 4 + (16 << 20)),
        ),
        name="gla_bidir",
    )(z, z, z, ld_f, ld_b, per_chunk_t(z[..., GLA_QK:2 * GLA_QK]), per_chunk_t(ld_f), per_chunk_t(ld_b),
      s0, gla_norm.reshape(1, GLA_DV))


def ctx_attn(q, k, v, sink):
    B, L = q.shape[:2]
    scale = ATT_HD ** -0.5
    qg = q.reshape(B, L, ATT_KV_HEADS, ATT_GROUP, ATT_HD)
    sink_b = sink.reshape(ATT_KV_HEADS, ATT_GROUP)[None, :, :, None, None]
    s = jnp.einsum('bqhgd,bkhd->bhgqk', qg, k) * scale
    s_sink = jnp.broadcast_to(sink_b, s.shape[:-1] + (1,))
    p = jax.nn.softmax(jnp.concatenate([s_sink, s], axis=-1), axis=-1)[..., 1:]
    return jnp.einsum('bhgqk,bkhd->bqhgd', p, v).reshape(B, L, ATT_HEADS * ATT_HD)


def window_attn_latent(q, k, v, k_ctx, v_ctx, sink):
    B, S = q.shape[:2]
    nb = S // ATT_BLOCK
    Lc = k_ctx.shape[1]
    scale = ATT_HD ** -0.5
    qb = q.reshape(B, nb, ATT_BLOCK, ATT_KV_HEADS, ATT_GROUP, ATT_HD).transpose(1, 0, 2, 3, 4, 5)
    pad = ((0, 0), (ATT_BLOCK, ATT_BLOCK), (0, 0), (0, 0))
    kp = jnp.pad(k, pad)
    vp = jnp.pad(v, pad)
    sink_b = sink.reshape(ATT_KV_HEADS, ATT_GROUP)[None, :, :, None, None]
    offs_q = jnp.arange(ATT_BLOCK)
    offs_k = jnp.arange(3 * ATT_BLOCK) - ATT_BLOCK

    def block(args):
        bi, q_blk = args
        k_band = lax.dynamic_slice_in_dim(kp, bi * ATT_BLOCK, 3 * ATT_BLOCK, axis=1)
        v_band = lax.dynamic_slice_in_dim(vp, bi * ATT_BLOCK, 3 * ATT_BLOCK, axis=1)
        qpos = bi * ATT_BLOCK + offs_q
        kpos = bi * ATT_BLOCK + offs_k
        valid = (jnp.abs(qpos[:, None] - kpos[None, :]) <= WINDOW) & (kpos >= 0)[None, :] & (kpos < S)[None, :]
        s_lat = jnp.einsum('bqhgd,bkhd->bhgqk', q_blk, k_band) * scale
        s_lat = jnp.where(valid, s_lat, NEG)
        s_ctx = jnp.einsum('bqhgd,bchd->bhgqc', q_blk, k_ctx) * scale
        s_sink = jnp.broadcast_to(sink_b, s_ctx.shape[:-1] + (1,))
        p = jax.nn.softmax(jnp.concatenate([s_sink, s_ctx, s_lat], axis=-1), axis=-1)
        o = (jnp.einsum('bhgqc,bchd->bqhgd', p[..., 1:1 + Lc], v_ctx)
             + jnp.einsum('bhgqk,bkhd->bqhgd', p[..., 1 + Lc:], v_band))
        return o.reshape(B, ATT_BLOCK, ATT_HEADS * ATT_HD)

    o = lax.map(block, (jnp.arange(nb), qb))
    return o.transpose(1, 0, 2, 3).reshape(B, S, ATT_HEADS * ATT_HD)


def even_projections(h, w_in, dec_w, dec_b):
    B, L, _ = h.shape
    z = h @ w_in
    aq, ak, av, lr = jnp.split(z[..., IN_SPLITS[3]:], [s - IN_SPLITS[3] for s in IN_SPLITS[4:]], axis=-1)
    lr_f, lr_b = jnp.split(lr, 2, axis=-1)

    def logdecay(lr_d, w, b):
        return jax.nn.log_sigmoid(lr_d @ w + b) / GLA_GATE_TEMP

    ld_f = logdecay(lr_f, dec_w[0], dec_b[0])
    ld_b = logdecay(lr_b, dec_w[1], dec_b[1])
    aq = aq.reshape(B, L, ATT_HEADS, ATT_HD)
    ak = ak.reshape(B, L, ATT_KV_HEADS, ATT_HD)
    av = av.reshape(B, L, ATT_KV_HEADS, ATT_HD)
    return z, ld_f, ld_b, aq, ak, av


def even_mixer_ctx(h, w_in, w_out, dec_w, dec_b, gla_norm, sink):
    z, ld_f, ld_b, aq, ak, av = even_projections(h, w_in, dec_w, dec_b)
    zero = jnp.zeros((h.shape[0], 2, GLA_HEADS, GLA_DK, GLA_DV), f32)
    o_gla, s_fin = gla_bidir_gated(z, ld_f, ld_b, zero, gla_norm)
    o_att = ctx_attn(aq, ak, av, sink)
    return jnp.concatenate([o_gla, o_att], axis=-1) @ w_out, ak, av, s_fin


def even_mixer_lat(h, w_in, w_out, dec_w, dec_b, gla_norm, sink, rope, k_ctx, v_ctx, s0):
    z, ld_f, ld_b, aq, ak, av = even_projections(h, w_in, dec_w, dec_b)
    o_gla, _ = gla_bidir_gated(z, ld_f, ld_b, s0, gla_norm)
    o_att = window_attn_latent(apply_axial_rope(aq, rope), apply_axial_rope(ak, rope), av, k_ctx, v_ctx, sink)
    return jnp.concatenate([o_gla, o_att], axis=-1) @ w_out


def centred_shift(x):
    xp = jnp.pad(x, ((0, 0), (1, 1), (0, 0)))
    return 0.5 * (xp[:, :-2] + xp[:, 2:])


LANES = 128
SUBLANES = 8
RWKV_TC = 32


def _rwkv_scan_kernel(r_ref, v_ref, a_ref, w_ref, k_ref, b_ref, s0_ref, o_ref, s_ref, *, tc, nv, n_dir_groups):
    backward = pl.program_id(0) >= n_dir_groups

    @pl.when(pl.program_id(1) == 0)
    def _():
        s_ref[...] = s0_ref[...]

    def step(i, carry):
        t = jnp.where(backward, tc - 1 - i, i)
        alpha, w, beta, kd, r = a_ref[t], w_ref[t], b_ref[t], k_ref[t], r_ref[t]
        for vb in range(nv // SUBLANES):
            outs = []
            for v in range(vb * SUBLANES, (vb + 1) * SUBLANES):
                s = s_ref[v]
                sa = jnp.sum(s * alpha, axis=0, keepdims=True)
                s_new = s * w + sa * beta + v_ref[t, pl.ds(v, 1), :] * kd
                s_ref[v] = s_new
                outs.append(jnp.sum(s_new * r, axis=0, keepdims=True))
            o_ref[t, vb * SUBLANES:(vb + 1) * SUBLANES, :] = jnp.concatenate(outs, axis=0)
        return carry

    lax.fori_loop(0, tc, step, 0)


def rwkv_scan_lanes(r, v, nkk, w, k, b, s0):
    gd, L, nv, _ = v.shape
    tc = RWKV_TC
    n = L // tc

    def time_block(g, j):
        return jnp.where(g >= gd, n - 1 - j, j)

    def blk(rows, shared):
        return pl.BlockSpec((None, tc, rows, LANES), lambda g, j: (g % gd if shared else g, time_block(g, j), 0, 0))

    state = pl.BlockSpec((None, nv, RWKV_HEAD, LANES), lambda g, j: (g, 0, 0, 0))
    block_bytes = tc * RWKV_HEAD * LANES * 4
    return pl.pallas_call(
        functools.partial(_rwkv_scan_kernel, tc=tc, nv=nv, n_dir_groups=gd),
        grid=(2 * gd, n),
        in_specs=[blk(RWKV_HEAD, True), blk(nv, True), blk(RWKV_HEAD, True),
                  blk(RWKV_HEAD, False), blk(RWKV_HEAD, False), blk(RWKV_HEAD, False), state],
        out_specs=[blk(nv, False), state],
        out_shape=[jax.ShapeDtypeStruct((2 * gd, L, nv, LANES), f32),
                   jax.ShapeDtypeStruct((2 * gd, nv, RWKV_HEAD, LANES), f32)],
        compiler_params=pltpu.CompilerParams(
            dimension_semantics=("arbitrary", "arbitrary"),
            vmem_limit_bytes=2 * 7 * block_bytes + 4 * nv * RWKV_HEAD * LANES * 4 + (8 << 20),
        ),
        name="rwkv_scan",
    )(r, v, nkk, w, k, b, s0)


RELAYOUT_HEADS = 4
RELAYOUT_TB = 8
RELAYOUT_W = RELAYOUT_HEADS * RWKV_HEAD


def _to_lanes_kernel(*refs, n_arrays, batch):
    lane_q = lax.broadcasted_iota(jnp.int32, (RWKV_HEAD, LANES), 1) // batch
    for x_ref, o_ref in zip(refs[:n_arrays], refs[n_arrays:]):
        for t0 in range(0, x_ref.shape[1], RELAYOUT_HEADS):
            x4 = jnp.concatenate([x_ref[:, t0 + i, :] for i in range(RELAYOUT_HEADS)], axis=0)
            xt = x4.T
            for i in range(RELAYOUT_HEADS):
                tile = None
                for q in range(RELAYOUT_HEADS):
                    p = xt[q * RWKV_HEAD:(q + 1) * RWKV_HEAD, :]
                    shift = ((q - i) * batch) % LANES
                    p = pltpu.roll(p, shift, axis=1) if shift else p
                    tile = p if tile is None else jnp.where(lane_q == q, p, tile)
                o_ref[t0 + i] = tile


def to_lanes(xs):
    B, L, D = xs[0].shape
    n = len(xs)
    return pl.pallas_call(
        functools.partial(_to_lanes_kernel, n_arrays=n, batch=B),
        grid=(D // RELAYOUT_W, L // RELAYOUT_TB),
        in_specs=[pl.BlockSpec((B, RELAYOUT_TB, RELAYOUT_W), lambda g, j: (0, j, g))] * n,
        out_specs=[pl.BlockSpec((None, RELAYOUT_TB, RWKV_HEAD, LANES), lambda g, j: (g, j, 0, 0))] * n,
        out_shape=[jax.ShapeDtypeStruct((D // RELAYOUT_W, L, RWKV_HEAD, LANES), f32)] * n,
        compiler_params=pltpu.CompilerParams(dimension_semantics=("arbitrary", "arbitrary")),
        name="to_lanes",
    )(*xs)


def _from_lanes_kernel(f_ref, b_ref, o_ref, *, batch):
    lane_q = lax.broadcasted_iota(jnp.int32, (RWKV_HEAD, LANES), 1) // batch
    for t0 in range(0, f_ref.shape[0], RELAYOUT_HEADS):
        tiles = [f_ref[t0 + i] + b_ref[t0 + i] for i in range(RELAYOUT_HEADS)]
        rows = []
        for q in range(RELAYOUT_HEADS):
            acc = None
            for i in range(RELAYOUT_HEADS):
                shift = ((i - q) * batch) % LANES
                p = pltpu.roll(tiles[i], shift, axis=1) if shift else tiles[i]
                acc = p if acc is None else jnp.where(lane_q == i, p, acc)
            rows.append(acc)
        x4 = jnp.concatenate(rows, axis=0).T
        for i in range(RELAYOUT_HEADS):
            o_ref[:, t0 + i, :] = x4[i * batch:(i + 1) * batch]


def from_lanes(out, gd, B):
    L = out.shape[1]
    return pl.pallas_call(
        functools.partial(_from_lanes_kernel, batch=B),
        grid=(gd, L // RELAYOUT_TB),
        in_specs=[pl.BlockSpec((None, RELAYOUT_TB, RWKV_HEAD, LANES), lambda g, j: (g, j, 0, 0)),
                  pl.BlockSpec((None, RELAYOUT_TB, RWKV_HEAD, LANES), lambda g, j: (g + gd, j, 0, 0))],
        out_specs=pl.BlockSpec((B, RELAYOUT_TB, RELAYOUT_W), lambda g, j: (0, j, g)),
        out_shape=jax.ShapeDtypeStruct((B, L, gd * RELAYOUT_W), f32),
        compiler_params=pltpu.CompilerParams(dimension_semantics=("arbitrary", "arbitrary")),
        name="from_lanes",
    )(out, out)


def rwkv_scan_bidir(r, v, nkk, fwd, bwd, s_f0, s_b0):
    B, L, H, N = r.shape
    S = B * H
    parts = max(1, LANES // S)
    gd = S * parts // LANES
    nv = N // parts

    def seq_last(x):
        return x.transpose(1, 3, 2, 0).reshape(L, N, S)

    def key_rows(x):
        if parts > 1:
            return jnp.concatenate([x] * parts, axis=-1)[None]
        return x.reshape(x.shape[0], N, gd, LANES).transpose(2, 0, 1, 3)

    def value_rows(x):
        if parts > 1:
            return x.reshape(x.shape[0], parts, nv, S).transpose(0, 2, 1, 3).reshape(1, x.shape[0], nv, LANES)
        return key_rows(x)

    def value_rows_inv(y):
        T = y.shape[1]
        if parts > 1:
            return y.reshape(T, nv, parts, S).transpose(0, 2, 1, 3).reshape(T, N, S)
        return y.transpose(1, 2, 0, 3).reshape(T, N, S)

    def state_in(s):
        return value_rows(s.transpose(3, 2, 1, 0).reshape(N, N, S)).transpose(0, 2, 1, 3)

    def state_out(s):
        return value_rows_inv(s.transpose(0, 2, 1, 3)).reshape(N, N, H, B).transpose(3, 2, 1, 0)

    s0 = jnp.concatenate([state_in(s_f0), state_in(s_b0)], axis=0)
    if parts == 1 and B * RELAYOUT_HEADS == LANES:
        flat = [x.reshape(B, L, H * N) for x in (r, v, nkk, *fwd, *bwd)]
        r_l, v_l, nkk_l, wf, kf, bf_, wb, kb, bb = to_lanes(flat)
        w, k, b = (jnp.concatenate([f, g], axis=0) for f, g in ((wf, wb), (kf, kb), (bf_, bb)))
        out, s_fin = rwkv_scan_lanes(r_l, v_l, nkk_l, w, k, b, s0)
        o = from_lanes(out, gd, B).reshape(B, L, H, N)
    else:
        w, k, b = (jnp.concatenate([key_rows(seq_last(f)), key_rows(seq_last(g))], axis=0)
                   for f, g in zip(fwd, bwd))
        out, s_fin = rwkv_scan_lanes(key_rows(seq_last(r)), value_rows(seq_last(v)), key_rows(seq_last(nkk)),
                                     w, k, b, s0)
        o = value_rows_inv(out[:gd]) + value_rows_inv(out[gd:])
        o = o.reshape(L, N, H, B).transpose(3, 0, 2, 1)
    return o, state_out(s_fin[:gd]), state_out(s_fin[gd:])


def rwkv_mix(h, mu, wr, wk, wv, wo, w0, w1, w2, a0, a1, a2, g1, g2, k_k, k_a, r_k, lnx_w, lnx_b, s_f0, s_b0):
    B, L, D = h.shape
    H, N = RWKV_HEADS, RWKV_HEAD
    xx = centred_shift(h) - h
    xr, xw, xk, xv, xa, xg = [h + xx * mu[j] for j in range(6)]
    r = (xr @ wr).reshape(B, L, H, N)
    k = (xk @ wk).reshape(B, L, H, N)
    v = (xv @ wv).reshape(B, L, H, N)
    g = jax.nn.sigmoid(xg @ g1) @ g2
    kk = k * k_k.reshape(H, N)
    kk = kk * lax.rsqrt(jnp.sum(kk * kk, axis=-1, keepdims=True) + 1e-12)

    def direction(d):
        w_raw = -jax.nn.softplus(-(w0[d] + jnp.tanh(xw @ w1[d]) @ w2[d])) - 0.5
        decay = jnp.exp(-jnp.exp(w_raw)).reshape(B, L, H, N)
        a = jax.nn.sigmoid(a0[d] + (xa @ a1[d]) @ a2[d]).reshape(B, L, H, N)
        k_d = k * (1.0 + (a - 1.0) * k_a.reshape(H, N))
        return decay, k_d, kk * a

    w_f, k_f, b_f = direction(0)
    w_b, k_b, b_b = direction(1)
    o, s_f, s_b = rwkv_scan_bidir(r, v, -kk, (w_f, k_f, b_f), (w_b, k_b, b_b), s_f0, s_b0)
    m = jnp.mean(o, axis=-1, keepdims=True)
    var = jnp.mean(jnp.square(o - m), axis=-1, keepdims=True)
    o = (o - m) * lax.rsqrt(var + RWKV_LN_EPS) * lnx_w.reshape(H, N) + lnx_b.reshape(H, N)
    bonus = jnp.sum(r * 0.5 * (k_f + k_b) * r_k, axis=-1, keepdims=True) * v
    o = (o + bonus).reshape(B, L, D)
    return (o * g) @ wo, s_f, s_b


GROUP_HID = EXP_PER_GROUP * D_EXPERT


def _moe_group_kernel(tile_group_ref, n_tiles_ref, xs_ref, gates_ref, wg_ref, wu_ref, wd_ref, ys_ref,
                      wg_bf, wu_bf, wd_bf):
    i = pl.program_id(0)
    prev_group = tile_group_ref[jnp.maximum(i - 1, 0)]
    group_changed = jnp.logical_or(i == 0, tile_group_ref[i] != prev_group)

    @pl.when(group_changed)
    def _():
        for e in range(EXP_PER_GROUP):
            hs = slice(e * D_EXPERT, (e + 1) * D_EXPERT)
            wg_bf[:, hs] = wg_ref[e].astype(bf16)
            wu_bf[:, hs] = wu_ref[e].astype(bf16)
            wd_bf[hs, :] = wd_ref[e].astype(bf16)

    @pl.when(i < n_tiles_ref[0])
    def _():
        x = xs_ref[...].astype(bf16)
        g = jnp.dot(x, wg_bf[...], preferred_element_type=f32)
        u = jnp.dot(x, wu_bf[...], preferred_element_type=f32)
        gate = jnp.concatenate([jnp.broadcast_to(gates_ref[:, e:e + 1], (MOE_TILE_M, D_EXPERT))
                                for e in range(EXP_PER_GROUP)], axis=1)
        hid = (g * jax.nn.sigmoid(g)) * u * gate
        ys_ref[...] = jnp.dot(hid.astype(bf16), wd_bf[...], preferred_element_type=f32)

    @pl.when(i >= n_tiles_ref[0])
    def _():
        ys_ref[...] = jnp.zeros_like(ys_ref)


def moe_group_experts(xs, gates, tile_group, n_tiles, w_gate, w_up, w_down):
    P, D = xs.shape
    max_tiles = P // MOE_TILE_M
    weight_bytes = 3 * EXP_PER_GROUP * D * D_EXPERT * (4 + 2)
    tile_bytes = 2 * MOE_TILE_M * (2 * D * 4 + LANES * 4) + 4 * MOE_TILE_M * GROUP_HID * 4
    once = pl.Buffered(1)
    grid_spec = pltpu.PrefetchScalarGridSpec(
        num_scalar_prefetch=2,
        grid=(max_tiles,),
        in_specs=[
            pl.BlockSpec((MOE_TILE_M, D), lambda i, tg, nt: (i, 0)),
            pl.BlockSpec((MOE_TILE_M, EXP_PER_GROUP), lambda i, tg, nt: (i, 0)),
            pl.BlockSpec((EXP_PER_GROUP, D, D_EXPERT), lambda i, tg, nt: (tg[i], 0, 0), pipeline_mode=once),
            pl.BlockSpec((EXP_PER_GROUP, D, D_EXPERT), lambda i, tg, nt: (tg[i], 0, 0), pipeline_mode=once),
            pl.BlockSpec((EXP_PER_GROUP, D_EXPERT, D), lambda i, tg, nt: (tg[i], 0, 0), pipeline_mode=once),
        ],
        out_specs=pl.BlockSpec((MOE_TILE_M, D), lambda i, tg, nt: (i, 0)),
        scratch_shapes=[
            pltpu.VMEM((D, GROUP_HID), bf16),
            pltpu.VMEM((D, GROUP_HID), bf16),
            pltpu.VMEM((GROUP_HID, D), bf16),
        ],
    )
    return pl.pallas_call(
        _moe_group_kernel,
        grid_spec=grid_spec,
        out_shape=jax.ShapeDtypeStruct((P, D), f32),
        compiler_params=pltpu.CompilerParams(
            dimension_semantics=("arbitrary",),
            vmem_limit_bytes=min(V7X_VMEM_BYTES - (4 << 20), weight_bytes + tile_bytes + (8 << 20)),
        ),
        name="moe_group_experts",
    )(tile_group, n_tiles, xs, gates, w_gate, w_up, w_down)


def hier_moe(t, w_grp, b_grp, w_exp, b_exp, w_gate, w_up, w_down):
    T, D = t.shape
    logits = jnp.dot(t, jnp.concatenate([w_grp, w_exp], axis=1), precision=lax.Precision.HIGHEST)
    grp_logits = logits[:, :N_GROUPS] + b_grp
    grp_prob = jax.nn.softmax(grp_logits, axis=-1)
    g_top = jnp.argmax(grp_logits, axis=-1).astype(jnp.int32)
    in_group = g_top[:, None] == jnp.arange(N_GROUPS, dtype=jnp.int32)[None, :]
    p_g = jnp.sum(jnp.where(in_group, grp_prob, 0.0), axis=1, keepdims=True)
    exp_logits = (logits[:, N_GROUPS:] + b_exp).reshape(-1, N_GROUPS, EXP_PER_GROUP)
    sel = jnp.sum(jnp.where(in_group[:, :, None], exp_logits, 0.0), axis=1)
    top_v, top_i = lax.top_k(sel, TOP_K)
    wts = p_g * jax.nn.softmax(top_v, axis=-1)
    gates = jnp.sum(jax.nn.one_hot(top_i, EXP_PER_GROUP, dtype=f32) * wts[..., None], axis=1)

    order = jnp.argsort(g_top, stable=True).astype(jnp.int32)
    grp_i32 = in_group.astype(jnp.int32)
    counts = jnp.sum(grp_i32, axis=0)
    rank = jnp.sum(jnp.where(in_group, jnp.cumsum(grp_i32, axis=0), 0), axis=1) - 1
    tiles_per = (counts + MOE_TILE_M - 1) // MOE_TILE_M
    tile_end = jnp.cumsum(tiles_per)
    start_padded = (tile_end - tiles_per) * MOE_TILE_M
    start_sorted = jnp.cumsum(counts) - counts
    max_tiles = T // MOE_TILE_M + N_GROUPS
    n_tiles = tile_end[-1:].astype(jnp.int32)
    tile_ids = jnp.arange(max_tiles, dtype=jnp.int32)
    tile_group = jnp.minimum(jnp.sum((tile_ids[:, None] >= tile_end[None, :]).astype(jnp.int32), axis=1),
                             N_GROUPS - 1).astype(jnp.int32)
    tile_group = jnp.where(tile_ids < n_tiles[0], tile_group, tile_group[jnp.maximum(n_tiles[0] - 1, 0)])
    row_group = jnp.repeat(tile_group, MOE_TILE_M)
    row_in_group = jnp.arange(max_tiles * MOE_TILE_M, dtype=jnp.int32) - start_padded[row_group]
    row_valid = (row_in_group >= 0) & (row_in_group < counts[row_group])
    row_tok = jnp.where(row_valid, order[jnp.clip(start_sorted[row_group] + row_in_group, 0, T - 1)], 0)
    pos = start_padded[g_top] + rank

    xs = jnp.take(t, row_tok, axis=0)
    row_gates = jnp.where(row_valid[:, None], jnp.take(gates, row_tok, axis=0), 0.0)
    ys = moe_group_experts(xs, row_gates, tile_group, n_tiles, w_gate, w_up, w_down)
    return jnp.take(ys, pos, axis=0)


def kernel(x_prompt, x_sample, c, cache_attn_k, cache_attn_v, state_gla, state_rwkv, c_ctx, ada_w, ada_b, norm_mix, norm_ffn, norm_out, ev_w_in, ev_w_out, gla_dec_w, gla_dec_b, gla_norm, att_sink, rw_mu, rw_wr, rw_wk, rw_wv, rw_wo, rw_w0, rw_w1, rw_w2, rw_a0, rw_a1, rw_a2, rw_g1, rw_g2, rw_kk, rw_ka, rw_rk, rw_lnx_w, rw_lnx_b, moe_w_grp, moe_b_grp, moe_w_exp, moe_b_exp, moe_w_gate, moe_w_up, moe_w_down):
    n_lat = x_sample.shape[1]
    rows = n_lat // GRID_W
    row_pos = jnp.repeat(jnp.arange(rows), GRID_W)
    col_pos = jnp.tile(jnp.arange(GRID_W), rows)
    rope = rope_tables(row_pos, col_pos)
    Bc, Lc, D = x_prompt.shape
    Bl, Ll, _ = x_sample.shape

    xc, xl = x_prompt, x_sample
    cond_ctx = c_ctx[None, :]
    ks_out, vs_out, gla_out, rwkv_out = [], [], [], []
    for l in range(DEPTH):
        sc1, scl1, gc1, sc2, scl2, gc2 = ada_mod(cond_ctx, ada_w[l], ada_b[l])
        sl1, sll1, gl1, sl2, sll2, gl2 = ada_mod(c, ada_w[l], ada_b[l])
        hc = modulate(rmsnorm(xc, norm_mix[l]), sc1, scl1)
        hl = modulate(rmsnorm(xl, norm_mix[l]), sl1, sll1)
        i = l // 2
        if l % 2 == 0:
            oc, k_c, v_c, s_gla = even_mixer_ctx(hc, ev_w_in[i], ev_w_out[i], gla_dec_w[i], gla_dec_b[i],
                                                 gla_norm[i], att_sink[i])
            ol = even_mixer_lat(hl, ev_w_in[i], ev_w_out[i], gla_dec_w[i], gla_dec_b[i], gla_norm[i], att_sink[i],
                                rope, cache_attn_k[:, i], cache_attn_v[:, i], state_gla[:, i])
            ks_out.append(k_c)
            vs_out.append(v_c)
            gla_out.append(s_gla)
        else:
            rw = (rw_mu[i], rw_wr[i], rw_wk[i], rw_wv[i], rw_wo[i], rw_w0[i], rw_w1[i], rw_w2[i], rw_a0[i],
                  rw_a1[i], rw_a2[i], rw_g1[i], rw_g2[i], rw_kk[i], rw_ka[i], rw_rk[i], rw_lnx_w[i], rw_lnx_b[i])
            zero = jnp.zeros((Bc, RWKV_HEADS, RWKV_HEAD, RWKV_HEAD), f32)
            oc, s_f, s_b = rwkv_mix(hc, *rw, zero, zero)
            ol, _, _ = rwkv_mix(hl, *rw, state_rwkv[:, i, 0], state_rwkv[:, i, 1])
            rwkv_out.append(jnp.stack([s_f, s_b], axis=1))
        xc = xc + gc1 * oc
        xl = xl + gl1 * ol
        tc = modulate(rmsnorm(xc, norm_ffn[l]), sc2, scl2).reshape(Bc * Lc, D)
        tl = modulate(rmsnorm(xl, norm_ffn[l]), sl2, sll2).reshape(Bl * Ll, D)
        y = hier_moe(jnp.concatenate([tc, tl], axis=0), moe_w_grp[l], moe_b_grp[l], moe_w_exp[l], moe_b_exp[l],
                     moe_w_gate[l], moe_w_up[l], moe_w_down[l])
        xc = xc + gc2 * y[:Bc * Lc].reshape(Bc, Lc, D)
        xl = xl + gl2 * y[Bc * Lc:].reshape(Bl, Ll, D)

    y_prompt = rmsnorm(xc, norm_out)
    y_sample = rmsnorm(xl, norm_out)
    new_attn_k = jnp.stack(ks_out, axis=1)
    new_attn_v = jnp.stack(vs_out, axis=1)
    new_gla = jnp.stack(gla_out, axis=1)
    new_rwkv = jnp.stack(rwkv_out, axis=1)
    return (y_prompt, y_sample, new_attn_k, new_attn_v, new_gla, new_rwkv)
```

```python
import functools

import jax
import jax.numpy as jnp
import numpy as np
from jax import lax
from jax.experimental import pallas as pl
from jax.experimental.pallas import tpu as pltpu

D_MODEL = 1024
DEPTH = 4
GRID_W = 64
EPS = 1e-6
GLA_HEADS = 4
GLA_DK = 64
GLA_DV = 128
GLA_LOWRANK = 16
GLA_GATE_TEMP = 16.0
GLA_CHUNK = 64
ATT_HEADS = 8
ATT_KV_HEADS = 2
ATT_GROUP = ATT_HEADS // ATT_KV_HEADS
ATT_HD = 64
WINDOW = 128
ATT_BLOCK = 128
ROPE_BASE = 10000.0
ROPE_AXIS_DIMS = ATT_HD // 2
NEG = -1e30
IN_WIDTHS = (GLA_HEADS * GLA_DK, GLA_HEADS * GLA_DK, GLA_HEADS * GLA_DV, GLA_HEADS * GLA_DV,
             ATT_HEADS * ATT_HD, ATT_KV_HEADS * ATT_HD, ATT_KV_HEADS * ATT_HD, 2 * GLA_LOWRANK)
IN_SPLITS = tuple(int(v) for v in np.cumsum(IN_WIDTHS)[:-1])
RWKV_HEAD = 64
RWKV_HEADS = D_MODEL // RWKV_HEAD
RWKV_LN_EPS = 64e-5
N_GROUPS = 4
EXP_PER_GROUP = 4
N_EXPERTS = N_GROUPS * EXP_PER_GROUP
TOP_K = 2
D_EXPERT = 512

V7X_VMEM_BYTES = 64 * 1024 * 1024
MOE_TILE_M = 256

f32 = jnp.float32
bf16 = jnp.bfloat16


def rmsnorm(x, w):
    y = x * lax.rsqrt(jnp.mean(x * x, axis=-1, keepdims=True) + EPS)
    return y * w


def ada_mod(cond, w, b):
    mod = jax.nn.silu(cond) @ w + b
    return [m[:, None, :] for m in jnp.split(mod, 6, axis=-1)]


def modulate(x, shift, scale):
    return x * (1.0 + scale) + shift


def rope_tables(row_pos, col_pos):
    freqs = ROPE_BASE ** (-jnp.arange(0, ROPE_AXIS_DIMS, 2, dtype=f32) / ROPE_AXIS_DIMS)
    ang_r = row_pos.astype(f32)[:, None] * freqs[None, :]
    ang_c = col_pos.astype(f32)[:, None] * freqs[None, :]
    return jnp.cos(ang_r), jnp.sin(ang_r), jnp.cos(ang_c), jnp.sin(ang_c)


def rotate(x, cos, sin):
    x1, x2 = jnp.split(x, 2, axis=-1)
    cos = cos[None, :, None, :]
    sin = sin[None, :, None, :]
    return jnp.concatenate([x1 * cos - x2 * sin, x1 * sin + x2 * cos], axis=-1)


def apply_axial_rope(x, tables):
    cr, sr, cc, sc = tables
    xr, xc = jnp.split(x, 2, axis=-1)
    return jnp.concatenate([rotate(xr, cr, sr), rotate(xc, cc, sc)], axis=-1)


GLA_QK = GLA_HEADS * GLA_DK
GLA_V = GLA_HEADS * GLA_DV


def _gla_kernel(q_ref, v_ref, gg_ref, ldf_ref, ldb_ref, kt_ref, ldft_ref, ldbt_ref, s0_ref, norm_ref,
                o_ref, s_ref, *, seq_len):
    n = seq_len // GLA_CHUNK
    C = GLA_CHUNK
    ti = lax.broadcasted_iota(jnp.int32, (C, C), 0)
    tj = lax.broadcasted_iota(jnp.int32, (C, C), 1)
    keep = (tj <= ti, tj >= ti)
    tri = tuple(m.astype(bf16) for m in keep)
    tri_t = (tri[1], tri[0])
    ones = jnp.ones((C, GLA_DV), bf16)
    s_ref[...] = s0_ref[...]
    o_ref[...] = jnp.zeros_like(o_ref)
    ld_refs = ((ldf_ref, ldft_ref), (ldb_ref, ldbt_ref))

    def split3(x):
        hi = x.astype(bf16)
        rest = x - hi.astype(f32)
        mid = rest.astype(bf16)
        return hi, mid, (rest - mid.astype(f32)).astype(bf16)

    def sums(a, b):
        if isinstance(b, tuple):
            return sum(jnp.dot(a, p, preferred_element_type=f32) for p in b)
        return sum(jnp.dot(p, b, preferred_element_type=f32) for p in a)

    def chunk(c, carry):
        pending = []
        for d in range(2):
            cc = c if d == 0 else n - 1 - c
            rows = pl.ds(pl.multiple_of(cc * C, C), C)
            ld_ref, ldt_ref = ld_refs[d]
            ld_t3 = split3(ldt_ref[cc])
            b_all = sums(tri[d], split3(ld_ref[rows, :]))
            bt_all = sums(ld_t3, tri_t[d])
            total_all = sums(ld_t3, ones)
            for h in range(GLA_HEADS):
                ks = slice(h * GLA_DK, (h + 1) * GLA_DK)
                vs = slice(h * GLA_DV, (h + 1) * GLA_DV)
                q = q_ref[rows, ks] * (GLA_DK ** -0.5)
                v = v_ref[rows, vs].astype(bf16)
                k_t = kt_ref[cc, ks, :]
                b, b_t, total = b_all[:, ks], bt_all[ks], total_all[ks]
                q_e = (q * jnp.exp(b)).astype(bf16)
                k_e = (k_t * jnp.exp(-b_t)).astype(bf16)
                k_s = (k_t * jnp.exp(total[:, :C] - b_t)).astype(bf16)
                att = jnp.where(keep[d], jnp.dot(q_e, k_e, preferred_element_type=f32), 0.0).astype(bf16)
                s = s_ref[d, h]
                o = (jnp.dot(q_e, s.astype(bf16), preferred_element_type=f32)
                     + jnp.dot(att, v, preferred_element_type=f32))
                s_new = s * jnp.exp(total) + jnp.dot(k_s, v, preferred_element_type=f32)
                pending.append((d, h, rows, vs, o_ref[rows, vs] + o, s_new))
        for d, h, rows, vs, o, s_new in pending:
            o_ref[rows, vs] = o
            s_ref[d, h] = s_new
        return carry

    lax.fori_loop(0, n, chunk, 0)
    for h in range(GLA_HEADS):
        vs = slice(h * GLA_DV, (h + 1) * GLA_DV)
        x = o_ref[:, vs]
        y = x * lax.rsqrt(jnp.mean(x * x, axis=-1, keepdims=True) + EPS) * norm_ref[...]
        g = gg_ref[:, vs]
        o_ref[:, vs] = y * (g * jax.nn.sigmoid(g))


def gla_bidir_gated(z, ld_f, ld_b, s0, gla_norm):
    B, L, _ = z.shape
    n = L // GLA_CHUNK
    assert n % 2 == 0

    def per_chunk_t(a):
        return a.reshape(B, n, GLA_CHUNK, GLA_QK).transpose(0, 1, 3, 2)

    state = pl.BlockSpec((None, 2, GLA_HEADS, GLA_DK, GLA_DV), lambda b: (b, 0, 0, 0, 0))
    ld = pl.BlockSpec((None, L, GLA_QK), lambda b: (b, 0, 0))
    ld_t = pl.BlockSpec((None, n, GLA_QK, GLA_CHUNK), lambda b: (b, 0, 0, 0))
    return pl.pallas_call(
        functools.partial(_gla_kernel, seq_len=L),
        grid=(B,),
        in_specs=[pl.BlockSpec((None, L, GLA_QK), lambda b: (b, 0, 0)),
                  pl.BlockSpec((None, L, GLA_V), lambda b: (b, 0, 1)),
                  pl.BlockSpec((None, L, GLA_V), lambda b: (b, 0, 2)),
                  ld, ld, ld_t, ld_t, ld_t, state,
                  pl.BlockSpec((1, GLA_DV), lambda b: (0, 0))],
        out_specs=[pl.BlockSpec((None, L, GLA_V), lambda b: (b, 0, 0)), state],
        out_shape=[jax.ShapeDtypeStruct((B, L, GLA_V), f32),
                   jax.ShapeDtypeStruct((B, 2, GLA_HEADS, GLA_DK, GLA_DV), f32)],
        compiler_params=pltpu.CompilerParams(
            dimension_semantics=("arbitrary",),
            vmem_limit_bytes=min(V7X_VMEM_BYTES, 2 * L * (9 * GLA_QK + 3 * GLA_V) * 4 + (16 << 20)),
        ),
        name="gla_bidir",
    )(z, z, z, ld_f, ld_b, per_chunk_t(z[..., GLA_QK:2 * GLA_QK]), per_chunk_t(ld_f), per_chunk_t(ld_b),
      s0, gla_norm.reshape(1, GLA_DV))


def ctx_attn(q, k, v, sink):
    B, L = q.shape[:2]
    scale = ATT_HD ** -0.5
    qg = q.reshape(B, L, ATT_KV_HEADS, ATT_GROUP, ATT_HD)
    sink_b = sink.reshape(ATT_KV_HEADS, ATT_GROUP)[None, :, :, None, None]
    s = jnp.einsum('bqhgd,bkhd->bhgqk', qg, k) * scale
    s_sink = jnp.broadcast_to(sink_b, s.shape[:-1] + (1,))
    p = jax.nn.softmax(jnp.concatenate([s_sink, s], axis=-1), axis=-1)[..., 1:]
    return jnp.einsum('bhgqk,bkhd->bqhgd', p, v).reshape(B, L, ATT_HEADS * ATT_HD)


def window_attn_latent(q, k, v, k_ctx, v_ctx, sink):
    B, S = q.shape[:2]
    nb = S // ATT_BLOCK
    Lc = k_ctx.shape[1]
    scale = ATT_HD ** -0.5
    qb = q.reshape(B, nb, ATT_BLOCK, ATT_KV_HEADS, ATT_GROUP, ATT_HD).transpose(1, 0, 2, 3, 4, 5)
    pad = ((0, 0), (ATT_BLOCK, ATT_BLOCK), (0, 0), (0, 0))
    kp = jnp.pad(k, pad)
    vp = jnp.pad(v, pad)
    sink_b = sink.reshape(ATT_KV_HEADS, ATT_GROUP)[None, :, :, None, None]
    offs_q = jnp.arange(ATT_BLOCK)
    offs_k = jnp.arange(3 * ATT_BLOCK) - ATT_BLOCK

    def block(args):
        bi, q_blk = args
        k_band = lax.dynamic_slice_in_dim(kp, bi * ATT_BLOCK, 3 * ATT_BLOCK, axis=1)
        v_band = lax.dynamic_slice_in_dim(vp, bi * ATT_BLOCK, 3 * ATT_BLOCK, axis=1)
        qpos = bi * ATT_BLOCK + offs_q
        kpos = bi * ATT_BLOCK + offs_k
        valid = (jnp.abs(qpos[:, None] - kpos[None, :]) <= WINDOW) & (kpos >= 0)[None, :] & (kpos < S)[None, :]
        s_lat = jnp.einsum('bqhgd,bkhd->bhgqk', q_blk, k_band) * scale
        s_lat = jnp.where(valid, s_lat, NEG)
        s_ctx = jnp.einsum('bqhgd,bchd->bhgqc', q_blk, k_ctx) * scale
        s_sink = jnp.broadcast_to(sink_b, s_ctx.shape[:-1] + (1,))
        p = jax.nn.softmax(jnp.concatenate([s_sink, s_ctx, s_lat], axis=-1), axis=-1)
        o = (jnp.einsum('bhgqc,bchd->bqhgd', p[..., 1:1 + Lc], v_ctx)
             + jnp.einsum('bhgqk,bkhd->bqhgd', p[..., 1 + Lc:], v_band))
        return o.reshape(B, ATT_BLOCK, ATT_HEADS * ATT_HD)

    o = lax.map(block, (jnp.arange(nb), qb))
    return o.transpose(1, 0, 2, 3).reshape(B, S, ATT_HEADS * ATT_HD)


def even_projections(h, w_in, dec_w, dec_b):
    B, L, _ = h.shape
    z = h @ w_in
    aq, ak, av, lr = jnp.split(z[..., IN_SPLITS[3]:], [s - IN_SPLITS[3] for s in IN_SPLITS[4:]], axis=-1)
    lr_f, lr_b = jnp.split(lr, 2, axis=-1)

    def logdecay(lr_d, w, b):
        return jax.nn.log_sigmoid(lr_d @ w + b) / GLA_GATE_TEMP

    ld_f = logdecay(lr_f, dec_w[0], dec_b[0])
    ld_b = logdecay(lr_b, dec_w[1], dec_b[1])
    aq = aq.reshape(B, L, ATT_HEADS, ATT_HD)
    ak = ak.reshape(B, L, ATT_KV_HEADS, ATT_HD)
    av = av.reshape(B, L, ATT_KV_HEADS, ATT_HD)
    return z, ld_f, ld_b, aq, ak, av


def even_mixer_ctx(h, w_in, w_out, dec_w, dec_b, gla_norm, sink):
    z, ld_f, ld_b, aq, ak, av = even_projections(h, w_in, dec_w, dec_b)
    zero = jnp.zeros((h.shape[0], 2, GLA_HEADS, GLA_DK, GLA_DV), f32)
    o_gla, s_fin = gla_bidir_gated(z, ld_f, ld_b, zero, gla_norm)
    o_att = ctx_attn(aq, ak, av, sink)
    return jnp.concatenate([o_gla, o_att], axis=-1) @ w_out, ak, av, s_fin


def even_mixer_lat(h, w_in, w_out, dec_w, dec_b, gla_norm, sink, rope, k_ctx, v_ctx, s0):
    z, ld_f, ld_b, aq, ak, av = even_projections(h, w_in, dec_w, dec_b)
    o_gla, _ = gla_bidir_gated(z, ld_f, ld_b, s0, gla_norm)
    o_att = window_attn_latent(apply_axial_rope(aq, rope), apply_axial_rope(ak, rope), av, k_ctx, v_ctx, sink)
    return jnp.concatenate([o_gla, o_att], axis=-1) @ w_out


def centred_shift(x):
    xp = jnp.pad(x, ((0, 0), (1, 1), (0, 0)))
    return 0.5 * (xp[:, :-2] + xp[:, 2:])


LANES = 128
SUBLANES = 8
RWKV_TC = 32


def _rwkv_scan_kernel(r_ref, v_ref, a_ref, w_ref, k_ref, b_ref, s0_ref, o_ref, s_ref, *, tc, nv, n_dir_groups):
    backward = pl.program_id(0) >= n_dir_groups

    @pl.when(pl.program_id(1) == 0)
    def _():
        s_ref[...] = s0_ref[...]

    def step(i, carry):
        t = jnp.where(backward, tc - 1 - i, i)
        alpha, w, beta, kd, r = a_ref[t], w_ref[t], b_ref[t], k_ref[t], r_ref[t]
        for vb in range(nv // SUBLANES):
            outs = []
            for v in range(vb * SUBLANES, (vb + 1) * SUBLANES):
                s = s_ref[v]
                sa = jnp.sum(s * alpha, axis=0, keepdims=True)
                s_new = s * w + sa * beta + v_ref[t, pl.ds(v, 1), :] * kd
                s_ref[v] = s_new
                outs.append(jnp.sum(s_new * r, axis=0, keepdims=True))
            o_ref[t, vb * SUBLANES:(vb + 1) * SUBLANES, :] = jnp.concatenate(outs, axis=0)
        return carry

    lax.fori_loop(0, tc, step, 0)


def rwkv_scan_lanes(r, v, nkk, w, k, b, s0):
    gd, L, nv, _ = v.shape
    tc = RWKV_TC
    n = L // tc

    def time_block(g, j):
        return jnp.where(g >= gd, n - 1 - j, j)

    def blk(rows, shared):
        return pl.BlockSpec((None, tc, rows, LANES), lambda g, j: (g % gd if shared else g, time_block(g, j), 0, 0))

    state = pl.BlockSpec((None, nv, RWKV_HEAD, LANES), lambda g, j: (g, 0, 0, 0))
    block_bytes = tc * RWKV_HEAD * LANES * 4
    return pl.pallas_call(
        functools.partial(_rwkv_scan_kernel, tc=tc, nv=nv, n_dir_groups=gd),
        grid=(2 * gd, n),
        in_specs=[blk(RWKV_HEAD, True), blk(nv, True), blk(RWKV_HEAD, True),
                  blk(RWKV_HEAD, False), blk(RWKV_HEAD, False), blk(RWKV_HEAD, False), state],
        out_specs=[blk(nv, False), state],
        out_shape=[jax.ShapeDtypeStruct((2 * gd, L, nv, LANES), f32),
                   jax.ShapeDtypeStruct((2 * gd, nv, RWKV_HEAD, LANES), f32)],
        compiler_params=pltpu.CompilerParams(
            dimension_semantics=("arbitrary", "arbitrary"),
            vmem_limit_bytes=2 * 7 * block_bytes + 4 * nv * RWKV_HEAD * LANES * 4 + (8 << 20),
        ),
        name="rwkv_scan",
    )(r, v, nkk, w, k, b, s0)


def rwkv_scan_bidir(r, v, nkk, w, k, b, s0):
    B, L, H, N = r.shape
    parts = max(1, LANES // (B * H))
    hg = LANES // (parts * B)
    gd = H // hg
    nv = N // parts

    def key_rows(x):
        lead = x.shape[:-4]
        nl = len(lead)
        T = x.shape[-3]
        x = jnp.broadcast_to(x.reshape(*lead, 1, B, T, gd, hg, N), (*lead, parts, B, T, gd, hg, N))
        perm = tuple(range(nl)) + tuple(nl + i for i in (3, 2, 5, 0, 4, 1))
        return x.transpose(perm).reshape(-1, T, N, LANES)

    def value_rows(x):
        lead = x.shape[:-4]
        nl = len(lead)
        T = x.shape[-3]
        x = x.reshape(*lead, B, T, gd, hg, parts, nv)
        perm = tuple(range(nl)) + tuple(nl + i for i in (2, 1, 5, 4, 3, 0))
        return x.transpose(perm).reshape(-1, T, nv, LANES)

    def value_rows_inv(y, lead):
        nl = len(lead)
        T = y.shape[1]
        y = y.reshape(*lead, gd, T, nv, parts, hg, B)
        perm = tuple(range(nl)) + tuple(nl + i for i in (5, 1, 0, 4, 3, 2))
        return y.transpose(perm).reshape(*lead, B, T, H, N)

    s_in = value_rows(s0.transpose(0, 1, 4, 2, 3)).transpose(0, 2, 1, 3)
    out, s_fin = rwkv_scan_lanes(key_rows(r), value_rows(v), key_rows(nkk), key_rows(w), key_rows(k), key_rows(b),
                                 s_in)
    o = value_rows_inv(out, (2,))
    s_fin = value_rows_inv(s_fin.transpose(0, 2, 1, 3), (2,)).transpose(0, 1, 3, 4, 2)
    return o[0] + o[1], s_fin


def rwkv_mix(h, mu, wr, wk, wv, wo, w0, w1, w2, a0, a1, a2, g1, g2, k_k, k_a, r_k, lnx_w, lnx_b, s_f0, s_b0):
    B, L, D = h.shape
    H, N = RWKV_HEADS, RWKV_HEAD
    xx = centred_shift(h) - h
    xr, xw, xk, xv, xa, xg = [h + xx * mu[j] for j in range(6)]
    r = (xr @ wr).reshape(B, L, H, N)
    k = (xk @ wk).reshape(B, L, H, N)
    v = (xv @ wv).reshape(B, L, H, N)
    g = jax.nn.sigmoid(xg @ g1) @ g2
    kk = k * k_k.reshape(H, N)
    kk = kk * lax.rsqrt(jnp.sum(kk * kk, axis=-1, keepdims=True) + 1e-12)

    lora_w = jnp.einsum('zblr,zrd->zbld', jnp.tanh(jnp.einsum('bld,zdr->zblr', xw, w1)), w2)
    w_raw = -jax.nn.softplus(-(w0[:, None, None, :] + lora_w)) - 0.5
    decay = jnp.exp(-jnp.exp(w_raw)).reshape(2, B, L, H, N)
    lora_a = jnp.einsum('zblr,zrd->zbld', jnp.einsum('bld,zdr->zblr', xa, a1), a2)
    a = jax.nn.sigmoid(a0[:, None, None, :] + lora_a).reshape(2, B, L, H, N)
    k_d = k * (1.0 + (a - 1.0) * k_a.reshape(H, N))
    o, s_fin = rwkv_scan_bidir(r, v, -kk, decay, k_d, kk * a, jnp.stack([s_f0, s_b0], axis=0))
    m = jnp.mean(o, axis=-1, keepdims=True)
    var = jnp.mean(jnp.square(o - m), axis=-1, keepdims=True)
    o = (o - m) * lax.rsqrt(var + RWKV_LN_EPS) * lnx_w.reshape(H, N) + lnx_b.reshape(H, N)
    bonus = jnp.sum(r * 0.5 * (k_d[0] + k_d[1]) * r_k, axis=-1, keepdims=True) * v
    o = (o + bonus).reshape(B, L, D)
    return (o * g) @ wo, s_fin[0], s_fin[1]


GROUP_HID = EXP_PER_GROUP * D_EXPERT


def _moe_group_kernel(tile_group_ref, n_tiles_ref, xs_ref, gates_ref, wg_ref, wu_ref, wd_ref, ys_ref,
                      wg_bf, wu_bf, wd_bf):
    i = pl.program_id(0)
    prev_group = tile_group_ref[jnp.maximum(i - 1, 0)]
    group_changed = jnp.logical_or(i == 0, tile_group_ref[i] != prev_group)

    @pl.when(group_changed)
    def _():
        for e in range(EXP_PER_GROUP):
            hs = slice(e * D_EXPERT, (e + 1) * D_EXPERT)
            wg_bf[:, hs] = wg_ref[e].astype(bf16)
            wu_bf[:, hs] = wu_ref[e].astype(bf16)
            wd_bf[hs, :] = wd_ref[e].astype(bf16)

    @pl.when(i < n_tiles_ref[0])
    def _():
        x = xs_ref[...].astype(bf16)
        g = jnp.dot(x, wg_bf[...], preferred_element_type=f32)
        u = jnp.dot(x, wu_bf[...], preferred_element_type=f32)
        gate = jnp.concatenate([jnp.broadcast_to(gates_ref[:, e:e + 1], (MOE_TILE_M, D_EXPERT))
                                for e in range(EXP_PER_GROUP)], axis=1)
        hid = (g * jax.nn.sigmoid(g)) * u * gate
        ys_ref[...] = jnp.dot(hid.astype(bf16), wd_bf[...], preferred_element_type=f32)

    @pl.when(i >= n_tiles_ref[0])
    def _():
        ys_ref[...] = jnp.zeros_like(ys_ref)


def moe_group_experts(xs, gates, tile_group, n_tiles, w_gate, w_up, w_down):
    P, D = xs.shape
    max_tiles = P // MOE_TILE_M
    weight_bytes = 3 * EXP_PER_GROUP * D * D_EXPERT * (4 + 2)
    tile_bytes = 2 * MOE_TILE_M * (2 * D * 4 + LANES * 4) + 4 * MOE_TILE_M * GROUP_HID * 4
    once = pl.Buffered(1)
    grid_spec = pltpu.PrefetchScalarGridSpec(
        num_scalar_prefetch=2,
        grid=(max_tiles,),
        in_specs=[
            pl.BlockSpec((MOE_TILE_M, D), lambda i, tg, nt: (i, 0)),
            pl.BlockSpec((MOE_TILE_M, EXP_PER_GROUP), lambda i, tg, nt: (i, 0)),
            pl.BlockSpec((EXP_PER_GROUP, D, D_EXPERT), lambda i, tg, nt: (tg[i], 0, 0), pipeline_mode=once),
            pl.BlockSpec((EXP_PER_GROUP, D, D_EXPERT), lambda i, tg, nt: (tg[i], 0, 0), pipeline_mode=once),
            pl.BlockSpec((EXP_PER_GROUP, D_EXPERT, D), lambda i, tg, nt: (tg[i], 0, 0), pipeline_mode=once),
        ],
        out_specs=pl.BlockSpec((MOE_TILE_M, D), lambda i, tg, nt: (i, 0)),
        scratch_shapes=[
            pltpu.VMEM((D, GROUP_HID), bf16),
            pltpu.VMEM((D, GROUP_HID), bf16),
            pltpu.VMEM((GROUP_HID, D), bf16),
        ],
    )
    return pl.pallas_call(
        _moe_group_kernel,
        grid_spec=grid_spec,
        out_shape=jax.ShapeDtypeStruct((P, D), f32),
        compiler_params=pltpu.CompilerParams(
            dimension_semantics=("arbitrary",),
            vmem_limit_bytes=min(V7X_VMEM_BYTES - (4 << 20), weight_bytes + tile_bytes + (8 << 20)),
        ),
        name="moe_group_experts",
    )(tile_group, n_tiles, xs, gates, w_gate, w_up, w_down)


def hier_moe(t, w_grp, b_grp, w_exp, b_exp, w_gate, w_up, w_down):
    T, D = t.shape
    logits = jnp.dot(t, jnp.concatenate([w_grp, w_exp], axis=1), precision=lax.Precision.HIGHEST)
    grp_logits = logits[:, :N_GROUPS] + b_grp
    grp_prob = jax.nn.softmax(grp_logits, axis=-1)
    g_top = jnp.argmax(grp_logits, axis=-1).astype(jnp.int32)
    in_group = g_top[:, None] == jnp.arange(N_GROUPS, dtype=jnp.int32)[None, :]
    p_g = jnp.sum(jnp.where(in_group, grp_prob, 0.0), axis=1, keepdims=True)
    exp_logits = (logits[:, N_GROUPS:] + b_exp).reshape(-1, N_GROUPS, EXP_PER_GROUP)
    sel = jnp.sum(jnp.where(in_group[:, :, None], exp_logits, 0.0), axis=1)
    top_v, top_i = lax.top_k(sel, TOP_K)
    wts = p_g * jax.nn.softmax(top_v, axis=-1)
    gates = jnp.sum(jax.nn.one_hot(top_i, EXP_PER_GROUP, dtype=f32) * wts[..., None], axis=1)

    order = jnp.argsort(g_top, stable=True).astype(jnp.int32)
    grp_i32 = in_group.astype(jnp.int32)
    counts = jnp.sum(grp_i32, axis=0)
    rank = jnp.sum(jnp.where(in_group, jnp.cumsum(grp_i32, axis=0), 0), axis=1) - 1
    tiles_per = (counts + MOE_TILE_M - 1) // MOE_TILE_M
    tile_end = jnp.cumsum(tiles_per)
    start_padded = (tile_end - tiles_per) * MOE_TILE_M
    start_sorted = jnp.cumsum(counts) - counts
    max_tiles = T // MOE_TILE_M + N_GROUPS
    n_tiles = tile_end[-1:].astype(jnp.int32)
    tile_ids = jnp.arange(max_tiles, dtype=jnp.int32)
    tile_group = jnp.minimum(jnp.sum((tile_ids[:, None] >= tile_end[None, :]).astype(jnp.int32), axis=1),
                             N_GROUPS - 1).astype(jnp.int32)
    tile_group = jnp.where(tile_ids < n_tiles[0], tile_group, tile_group[jnp.maximum(n_tiles[0] - 1, 0)])
    row_group = jnp.repeat(tile_group, MOE_TILE_M)
    row_in_group = jnp.arange(max_tiles * MOE_TILE_M, dtype=jnp.int32) - start_padded[row_group]
    row_valid = (row_in_group >= 0) & (row_in_group < counts[row_group])
    row_tok = jnp.where(row_valid, order[jnp.clip(start_sorted[row_group] + row_in_group, 0, T - 1)], 0)
    pos = start_padded[g_top] + rank

    xs = jnp.take(t, row_tok, axis=0)
    row_gates = jnp.where(row_valid[:, None], jnp.take(gates, row_tok, axis=0), 0.0)
    ys = moe_group_experts(xs, row_gates, tile_group, n_tiles, w_gate, w_up, w_down)
    return jnp.take(ys, pos, axis=0)


def kernel(x_prompt, x_sample, c, cache_attn_k, cache_attn_v, state_gla, state_rwkv, c_ctx, ada_w, ada_b, norm_mix, norm_ffn, norm_out, ev_w_in, ev_w_out, gla_dec_w, gla_dec_b, gla_norm, att_sink, rw_mu, rw_wr, rw_wk, rw_wv, rw_wo, rw_w0, rw_w1, rw_w2, rw_a0, rw_a1, rw_a2, rw_g1, rw_g2, rw_kk, rw_ka, rw_rk, rw_lnx_w, rw_lnx_b, moe_w_grp, moe_b_grp, moe_w_exp, moe_b_exp, moe_w_gate, moe_w_up, moe_w_down):
    n_lat = x_sample.shape[1]
    rows = n_lat // GRID_W
    row_pos = jnp.repeat(jnp.arange(rows), GRID_W)
    col_pos = jnp.tile(jnp.arange(GRID_W), rows)
    rope = rope_tables(row_pos, col_pos)
    Bc, Lc, D = x_prompt.shape
    Bl, Ll, _ = x_sample.shape

    xc, xl = x_prompt, x_sample
    cond_ctx = c_ctx[None, :]
    ks_out, vs_out, gla_out, rwkv_out = [], [], [], []
    for l in range(DEPTH):
        sc1, scl1, gc1, sc2, scl2, gc2 = ada_mod(cond_ctx, ada_w[l], ada_b[l])
        sl1, sll1, gl1, sl2, sll2, gl2 = ada_mod(c, ada_w[l], ada_b[l])
        hc = modulate(rmsnorm(xc, norm_mix[l]), sc1, scl1)
        hl = modulate(rmsnorm(xl, norm_mix[l]), sl1, sll1)
        i = l // 2
        if l % 2 == 0:
            oc, k_c, v_c, s_gla = even_mixer_ctx(hc, ev_w_in[i], ev_w_out[i], gla_dec_w[i], gla_dec_b[i],
                                                 gla_norm[i], att_sink[i])
            ol = even_mixer_lat(hl, ev_w_in[i], ev_w_out[i], gla_dec_w[i], gla_dec_b[i], gla_norm[i], att_sink[i],
                                rope, cache_attn_k[:, i], cache_attn_v[:, i], state_gla[:, i])
            ks_out.append(k_c)
            vs_out.append(v_c)
            gla_out.append(s_gla)
        else:
            rw = (rw_mu[i], rw_wr[i], rw_wk[i], rw_wv[i], rw_wo[i], rw_w0[i], rw_w1[i], rw_w2[i], rw_a0[i],
                  rw_a1[i], rw_a2[i], rw_g1[i], rw_g2[i], rw_kk[i], rw_ka[i], rw_rk[i], rw_lnx_w[i], rw_lnx_b[i])
            zero = jnp.zeros((Bc, RWKV_HEADS, RWKV_HEAD, RWKV_HEAD), f32)
            oc, s_f, s_b = rwkv_mix(hc, *rw, zero, zero)
            ol, _, _ = rwkv_mix(hl, *rw, state_rwkv[:, i, 0], state_rwkv[:, i, 1])
            rwkv_out.append(jnp.stack([s_f, s_b], axis=1))
        xc = xc + gc1 * oc
        xl = xl + gl1 * ol
        tc = modulate(rmsnorm(xc, norm_ffn[l]), sc2, scl2).reshape(Bc * Lc, D)
        tl = modulate(rmsnorm(xl, norm_ffn[l]), sl2, sll2).reshape(Bl * Ll, D)
        y = hier_moe(jnp.concatenate([tc, tl], axis=0), moe_w_grp[l], moe_b_grp[l], moe_w_exp[l], moe_b_exp[l],
                     moe_w_gate[l], moe_w_up[l], moe_w_down[l])
        xc = xc + gc2 * y[:Bc * Lc].reshape(Bc, Lc, D)
        xl = xl + gl2 * y[Bc * Lc:].reshape(Bl, Ll, D)

    y_prompt = rmsnorm(xc, norm_out)
    y_sample = rmsnorm(xl, norm_out)
    new_attn_k = jnp.stack(ks_out, axis=1)
    new_attn_v = jnp.stack(vs_out, axis=1)
    new_gla = jnp.stack(gla_out, axis=1)
    new_rwkv = jnp.stack(rwkv_out, axis=1)
    return (y_prompt, y_sample, new_attn_k, new_attn_v, new_gla, new_rwkv)
```

```python
import functools

import jax
import jax.numpy as jnp
import numpy as np
from jax import lax
from jax.experimental import pallas as pl
from jax.experimental.pallas import tpu as pltpu

D_MODEL = 1024
DEPTH = 4
GRID_W = 64
EPS = 1e-6
GLA_HEADS = 4
GLA_DK = 64
GLA_DV = 128
GLA_LOWRANK = 16
GLA_GATE_TEMP = 16.0
GLA_CHUNK = 64
ATT_HEADS = 8
ATT_KV_HEADS = 2
ATT_GROUP = ATT_HEADS // ATT_KV_HEADS
ATT_HD = 64
WINDOW = 128
ATT_BLOCK = 128
ROPE_BASE = 10000.0
ROPE_AXIS_DIMS = ATT_HD // 2
NEG = -1e30
IN_WIDTHS = (GLA_HEADS * GLA_DK, GLA_HEADS * GLA_DK, GLA_HEADS * GLA_DV, GLA_HEADS * GLA_DV,
             ATT_HEADS * ATT_HD, ATT_KV_HEADS * ATT_HD, ATT_KV_HEADS * ATT_HD, 2 * GLA_LOWRANK)
IN_SPLITS = tuple(int(v) for v in np.cumsum(IN_WIDTHS)[:-1])
RWKV_HEAD = 64
RWKV_HEADS = D_MODEL // RWKV_HEAD
RWKV_LN_EPS = 64e-5
N_GROUPS = 4
EXP_PER_GROUP = 4
N_EXPERTS = N_GROUPS * EXP_PER_GROUP
TOP_K = 2
D_EXPERT = 512

V7X_VMEM_BYTES = 64 * 1024 * 1024
MOE_TILE_M = 256

f32 = jnp.float32
bf16 = jnp.bfloat16


def rmsnorm(x, w):
    y = x * lax.rsqrt(jnp.mean(x * x, axis=-1, keepdims=True) + EPS)
    return y * w


def ada_mod(cond, w, b):
    mod = jax.nn.silu(cond) @ w + b
    return [m[:, None, :] for m in jnp.split(mod, 6, axis=-1)]


def modulate(x, shift, scale):
    return x * (1.0 + scale) + shift


def rope_tables(row_pos, col_pos):
    freqs = ROPE_BASE ** (-jnp.arange(0, ROPE_AXIS_DIMS, 2, dtype=f32) / ROPE_AXIS_DIMS)
    ang_r = row_pos.astype(f32)[:, None] * freqs[None, :]
    ang_c = col_pos.astype(f32)[:, None] * freqs[None, :]
    return jnp.cos(ang_r), jnp.sin(ang_r), jnp.cos(ang_c), jnp.sin(ang_c)


def rotate(x, cos, sin):
    x1, x2 = jnp.split(x, 2, axis=-1)
    cos = cos[None, :, None, :]
    sin = sin[None, :, None, :]
    return jnp.concatenate([x1 * cos - x2 * sin, x1 * sin + x2 * cos], axis=-1)


def apply_axial_rope(x, tables):
    cr, sr, cc, sc = tables
    xr, xc = jnp.split(x, 2, axis=-1)
    return jnp.concatenate([rotate(xr, cr, sr), rotate(xc, cc, sc)], axis=-1)


GLA_QK = GLA_HEADS * GLA_DK
GLA_V = GLA_HEADS * GLA_DV


def _gla_kernel(q_ref, v_ref, gg_ref, ldf_ref, ldb_ref, kt_ref, ldft_ref, ldbt_ref, s0_ref, norm_ref,
                o_ref, s_ref, *, seq_len):
    n = seq_len // GLA_CHUNK
    C = GLA_CHUNK
    ti = lax.broadcasted_iota(jnp.int32, (C, C), 0)
    tj = lax.broadcasted_iota(jnp.int32, (C, C), 1)
    keep = (tj <= ti, tj >= ti)
    tri = tuple(m.astype(bf16) for m in keep)
    tri_t = (tri[1], tri[0])
    ones = jnp.ones((C, GLA_DV), bf16)
    s_ref[...] = s0_ref[...]
    o_ref[...] = jnp.zeros_like(o_ref)
    ld_refs = ((ldf_ref, ldft_ref), (ldb_ref, ldbt_ref))

    def split3(x):
        hi = x.astype(bf16)
        rest = x - hi.astype(f32)
        mid = rest.astype(bf16)
        return hi, mid, (rest - mid.astype(f32)).astype(bf16)

    def sums(a, b):
        if isinstance(b, tuple):
            return sum(jnp.dot(a, p, preferred_element_type=f32) for p in b)
        return sum(jnp.dot(p, b, preferred_element_type=f32) for p in a)

    def chunk(c, carry):
        pending = []
        for d in range(2):
            cc = c if d == 0 else n - 1 - c
            rows = pl.ds(pl.multiple_of(cc * C, C), C)
            ld_ref, ldt_ref = ld_refs[d]
            ld_t3 = split3(ldt_ref[cc])
            b_all = sums(tri[d], split3(ld_ref[rows, :]))
            bt_all = sums(ld_t3, tri_t[d])
            total_all = sums(ld_t3, ones)
            for h in range(GLA_HEADS):
                ks = slice(h * GLA_DK, (h + 1) * GLA_DK)
                vs = slice(h * GLA_DV, (h + 1) * GLA_DV)
                q = q_ref[rows, ks] * (GLA_DK ** -0.5)
                v = v_ref[rows, vs].astype(bf16)
                k_t = kt_ref[cc, ks, :]
                b, b_t, total = b_all[:, ks], bt_all[ks], total_all[ks]
                q_e = (q * jnp.exp(b)).astype(bf16)
                k_e = (k_t * jnp.exp(-b_t)).astype(bf16)
                k_s = (k_t * jnp.exp(total[:, :C] - b_t)).astype(bf16)
                att = jnp.where(keep[d], jnp.dot(q_e, k_e, preferred_element_type=f32), 0.0).astype(bf16)
                s = s_ref[d, h]
                o = (jnp.dot(q_e, s.astype(bf16), preferred_element_type=f32)
                     + jnp.dot(att, v, preferred_element_type=f32))
                s_new = s * jnp.exp(total) + jnp.dot(k_s, v, preferred_element_type=f32)
                pending.append((d, h, rows, vs, o_ref[rows, vs] + o, s_new))
        for d, h, rows, vs, o, s_new in pending:
            o_ref[rows, vs] = o
            s_ref[d, h] = s_new
        return carry

    lax.fori_loop(0, n, chunk, 0)
    for h in range(GLA_HEADS):
        vs = slice(h * GLA_DV, (h + 1) * GLA_DV)
        x = o_ref[:, vs]
        y = x * lax.rsqrt(jnp.mean(x * x, axis=-1, keepdims=True) + EPS) * norm_ref[...]
        g = gg_ref[:, vs]
        o_ref[:, vs] = y * (g * jax.nn.sigmoid(g))


def gla_bidir_gated(z, ld_f, ld_b, s0, gla_norm):
    B, L, _ = z.shape
    n = L // GLA_CHUNK
    assert n % 2 == 0

    def per_chunk_t(a):
        return a.reshape(B, n, GLA_CHUNK, GLA_QK).transpose(0, 1, 3, 2)

    state = pl.BlockSpec((None, 2, GLA_HEADS, GLA_DK, GLA_DV), lambda b: (b, 0, 0, 0, 0))
    ld = pl.BlockSpec((None, L, GLA_QK), lambda b: (b, 0, 0))
    ld_t = pl.BlockSpec((None, n, GLA_QK, GLA_CHUNK), lambda b: (b, 0, 0, 0))
    return pl.pallas_call(
        functools.partial(_gla_kernel, seq_len=L),
        grid=(B,),
        in_specs=[pl.BlockSpec((None, L, GLA_QK), lambda b: (b, 0, 0)),
                  pl.BlockSpec((None, L, GLA_V), lambda b: (b, 0, 1)),
                  pl.BlockSpec((None, L, GLA_V), lambda b: (b, 0, 2)),
                  ld, ld, ld_t, ld_t, ld_t, state,
                  pl.BlockSpec((1, GLA_DV), lambda b: (0, 0))],
        out_specs=[pl.BlockSpec((None, L, GLA_V), lambda b: (b, 0, 0)), state],
        out_shape=[jax.ShapeDtypeStruct((B, L, GLA_V), f32),
                   jax.ShapeDtypeStruct((B, 2, GLA_HEADS, GLA_DK, GLA_DV), f32)],
        compiler_params=pltpu.CompilerParams(
            dimension_semantics=("arbitrary",),
            vmem_limit_bytes=min(V7X_VMEM_BYTES, 2 * L * (9 * GLA_QK + 3 * GLA_V) * 4 + (16 << 20)),
        ),
        name="gla_bidir",
    )(z, z, z, ld_f, ld_b, per_chunk_t(z[..., GLA_QK:2 * GLA_QK]), per_chunk_t(ld_f), per_chunk_t(ld_b),
      s0, gla_norm.reshape(1, GLA_DV))


def ctx_attn(q, k, v, sink):
    B, L = q.shape[:2]
    scale = ATT_HD ** -0.5
    qg = q.reshape(B, L, ATT_KV_HEADS, ATT_GROUP, ATT_HD)
    sink_b = sink.reshape(ATT_KV_HEADS, ATT_GROUP)[None, :, :, None, None]
    s = jnp.einsum('bqhgd,bkhd->bhgqk', qg, k) * scale
    s_sink = jnp.broadcast_to(sink_b, s.shape[:-1] + (1,))
    p = jax.nn.softmax(jnp.concatenate([s_sink, s], axis=-1), axis=-1)[..., 1:]
    return jnp.einsum('bhgqk,bkhd->bqhgd', p, v).reshape(B, L, ATT_HEADS * ATT_HD)


def window_attn_latent(q, k, v, k_ctx, v_ctx, sink):
    B, S = q.shape[:2]
    nb = S // ATT_BLOCK
    Lc = k_ctx.shape[1]
    scale = ATT_HD ** -0.5
    qb = q.reshape(B, nb, ATT_BLOCK, ATT_KV_HEADS, ATT_GROUP, ATT_HD).transpose(1, 0, 2, 3, 4, 5)
    pad = ((0, 0), (ATT_BLOCK, ATT_BLOCK), (0, 0), (0, 0))
    kp = jnp.pad(k, pad)
    vp = jnp.pad(v, pad)
    sink_b = sink.reshape(ATT_KV_HEADS, ATT_GROUP)[None, :, :, None, None]
    offs_q = jnp.arange(ATT_BLOCK)
    offs_k = jnp.arange(3 * ATT_BLOCK) - ATT_BLOCK

    def block(args):
        bi, q_blk = args
        k_band = lax.dynamic_slice_in_dim(kp, bi * ATT_BLOCK, 3 * ATT_BLOCK, axis=1)
        v_band = lax.dynamic_slice_in_dim(vp, bi * ATT_BLOCK, 3 * ATT_BLOCK, axis=1)
        qpos = bi * ATT_BLOCK + offs_q
        kpos = bi * ATT_BLOCK + offs_k
        valid = (jnp.abs(qpos[:, None] - kpos[None, :]) <= WINDOW) & (kpos >= 0)[None, :] & (kpos < S)[None, :]
        s_lat = jnp.einsum('bqhgd,bkhd->bhgqk', q_blk, k_band) * scale
        s_lat = jnp.where(valid, s_lat, NEG)
        s_ctx = jnp.einsum('bqhgd,bchd->bhgqc', q_blk, k_ctx) * scale
        s_sink = jnp.broadcast_to(sink_b, s_ctx.shape[:-1] + (1,))
        p = jax.nn.softmax(jnp.concatenate([s_sink, s_ctx, s_lat], axis=-1), axis=-1)
        o = (jnp.einsum('bhgqc,bchd->bqhgd', p[..., 1:1 + Lc], v_ctx)
             + jnp.einsum('bhgqk,bkhd->bqhgd', p[..., 1 + Lc:], v_band))
        return o.reshape(B, ATT_BLOCK, ATT_HEADS * ATT_HD)

    o = lax.map(block, (jnp.arange(nb), qb))
    return o.transpose(1, 0, 2, 3).reshape(B, S, ATT_HEADS * ATT_HD)


def even_projections(h, w_in, dec_w, dec_b):
    B, L, _ = h.shape
    z = h @ w_in
    aq, ak, av, lr = jnp.split(z[..., IN_SPLITS[3]:], [s - IN_SPLITS[3] for s in IN_SPLITS[4:]], axis=-1)
    lr_f, lr_b = jnp.split(lr, 2, axis=-1)

    def logdecay(lr_d, w, b):
        return jax.nn.log_sigmoid(lr_d @ w + b) / GLA_GATE_TEMP

    ld_f = logdecay(lr_f, dec_w[0], dec_b[0])
    ld_b = logdecay(lr_b, dec_w[1], dec_b[1])
    aq = aq.reshape(B, L, ATT_HEADS, ATT_HD)
    ak = ak.reshape(B, L, ATT_KV_HEADS, ATT_HD)
    av = av.reshape(B, L, ATT_KV_HEADS, ATT_HD)
    return z, ld_f, ld_b, aq, ak, av


def even_mixer_ctx(h, w_in, w_out, dec_w, dec_b, gla_norm, sink):
    z, ld_f, ld_b, aq, ak, av = even_projections(h, w_in, dec_w, dec_b)
    zero = jnp.zeros((h.shape[0], 2, GLA_HEADS, GLA_DK, GLA_DV), f32)
    o_gla, s_fin = gla_bidir_gated(z, ld_f, ld_b, zero, gla_norm)
    o_att = ctx_attn(aq, ak, av, sink)
    return jnp.concatenate([o_gla, o_att], axis=-1) @ w_out, ak, av, s_fin


def even_mixer_lat(h, w_in, w_out, dec_w, dec_b, gla_norm, sink, rope, k_ctx, v_ctx, s0):
    z, ld_f, ld_b, aq, ak, av = even_projections(h, w_in, dec_w, dec_b)
    o_gla, _ = gla_bidir_gated(z, ld_f, ld_b, s0, gla_norm)
    o_att = window_attn_latent(apply_axial_rope(aq, rope), apply_axial_rope(ak, rope), av, k_ctx, v_ctx, sink)
    return jnp.concatenate([o_gla, o_att], axis=-1) @ w_out


def centred_shift(x):
    xp = jnp.pad(x, ((0, 0), (1, 1), (0, 0)))
    return 0.5 * (xp[:, :-2] + xp[:, 2:])


LANES = 128
SUBLANES = 8
RWKV_TC = 32


def _rwkv_scan_kernel(r_ref, v_ref, a_ref, w_ref, k_ref, b_ref, s0_ref, o_ref, s_ref, *, tc, nv, n_dir_groups):
    backward = pl.program_id(0) >= n_dir_groups

    @pl.when(pl.program_id(1) == 0)
    def _():
        s_ref[...] = s0_ref[...]

    def step(i, carry):
        t = jnp.where(backward, tc - 1 - i, i)
        alpha, w, beta, kd, r = a_ref[t], w_ref[t], b_ref[t], k_ref[t], r_ref[t]
        for vb in range(nv // SUBLANES):
            outs = []
            for v in range(vb * SUBLANES, (vb + 1) * SUBLANES):
                s = s_ref[v]
                sa = jnp.sum(s * alpha, axis=0, keepdims=True)
                s_new = s * w + sa * beta + v_ref[t, pl.ds(v, 1), :] * kd
                s_ref[v] = s_new
                outs.append(jnp.sum(s_new * r, axis=0, keepdims=True))
            o_ref[t, vb * SUBLANES:(vb + 1) * SUBLANES, :] = jnp.concatenate(outs, axis=0)
        return carry

    lax.fori_loop(0, tc, step, 0)


def rwkv_scan_lanes(r, v, nkk, w, k, b, s0):
    gd, L, nv, _ = v.shape
    tc = RWKV_TC
    n = L // tc

    def time_block(g, j):
        return jnp.where(g >= gd, n - 1 - j, j)

    def blk(rows, shared):
        return pl.BlockSpec((None, tc, rows, LANES), lambda g, j: (g % gd if shared else g, time_block(g, j), 0, 0))

    state = pl.BlockSpec((None, nv, RWKV_HEAD, LANES), lambda g, j: (g, 0, 0, 0))
    block_bytes = tc * RWKV_HEAD * LANES * 4
    return pl.pallas_call(
        functools.partial(_rwkv_scan_kernel, tc=tc, nv=nv, n_dir_groups=gd),
        grid=(2 * gd, n),
        in_specs=[blk(RWKV_HEAD, True), blk(nv, True), blk(RWKV_HEAD, True),
                  blk(RWKV_HEAD, False), blk(RWKV_HEAD, False), blk(RWKV_HEAD, False), state],
        out_specs=[blk(nv, False), state],
        out_shape=[jax.ShapeDtypeStruct((2 * gd, L, nv, LANES), f32),
                   jax.ShapeDtypeStruct((2 * gd, nv, RWKV_HEAD, LANES), f32)],
        compiler_params=pltpu.CompilerParams(
            dimension_semantics=("arbitrary", "arbitrary"),
            vmem_limit_bytes=2 * 7 * block_bytes + 4 * nv * RWKV_HEAD * LANES * 4 + (8 << 20),
        ),
        name="rwkv_scan",
    )(r, v, nkk, w, k, b, s0)


def rwkv_scan_bidir(r, v, nkk, fwd, bwd, s_f0, s_b0):
    B, L, H, N = r.shape
    S = B * H
    gd = S // LANES

    def rows(x):
        T = x.shape[1]
        return x.transpose(1, 3, 0, 2).reshape(T, N, gd, LANES).transpose(2, 0, 1, 3)

    def rows_inv(y):
        return y.transpose(1, 2, 0, 3).reshape(y.shape[1], N, B, H)

    def state_in(s):
        return rows(s.transpose(0, 3, 1, 2)).transpose(0, 2, 1, 3)

    def state_out(s):
        return rows_inv(s.transpose(0, 2, 1, 3)).transpose(2, 3, 1, 0)

    w, k, b = (jnp.concatenate([rows(f), rows(g)], axis=0) for f, g in zip(fwd, bwd))
    out, s_fin = rwkv_scan_lanes(rows(r), rows(v), rows(nkk), w, k, b,
                                 jnp.concatenate([state_in(s_f0), state_in(s_b0)], axis=0))
    o = rows_inv(out[:gd]) + rows_inv(out[gd:])
    return o.transpose(2, 0, 3, 1), state_out(s_fin[:gd]), state_out(s_fin[gd:])


RWKV_SUB = 64
HEAD_PAIRS = D_MODEL // LANES


def _rwkv_rowsum_kernel(rf, vf, af, wf, kf, bf_, rb, vb, ab, wb, kb, bb, s0, of, ob, s, acc, *, tc):
    j = pl.program_id(1)

    @pl.when(j == 0)
    def _():
        s[...] = s0[...]

    row_head = lax.broadcasted_iota(jnp.int32, (2 * LANES, 2 * LANES), 0) // RWKV_HEAD
    col_head = lax.broadcasted_iota(jnp.int32, (2 * LANES, 2 * LANES), 1) // RWKV_HEAD
    ones_bd = (row_head == col_head).astype(bf16)
    sub = lax.broadcasted_iota(jnp.int32, (RWKV_HEAD, LANES), 0)
    lane_in_head = lax.broadcasted_iota(jnp.int32, (RWKV_HEAD, LANES), 1) % RWKV_HEAD
    eye = (sub == lane_in_head).astype(bf16)
    dirs = ((rf, vf, af, wf, kf, bf_), (rb, vb, ab, wb, kb, bb))

    def row_sums(x):
        return jnp.dot(x, ones_bd, preferred_element_type=f32)

    def row(ref, t, hp, reps):
        return jnp.broadcast_to(ref[t, pl.ds(hp, 1), :], (reps, LANES))

    def sub_chunk(sc, carry):
        def step(tt, carry):
            t_f = sc * RWKV_SUB + tt
            pos = ((t_f, jnp.maximum(t_f - 1, 0), tt - 1),
                   (tc - 1 - t_f, jnp.minimum(tc - t_f, tc - 1), RWKV_SUB - tt))
            for d in range(2):
                r_ref, v_ref, a_ref, w_ref, k_ref, b_ref = dirs[d]
                t_now, t_prev, out_lane = pos[d]
                lhs, tiles = [], []
                for hp2 in range(0, HEAD_PAIRS, 2):
                    v_diag = []
                    for hp in (hp2, hp2 + 1):
                        s_t = s[d, :, hp * LANES:(hp + 1) * LANES]
                        lhs.append(jnp.concatenate([(s_t * row(a_ref, t_now, hp, RWKV_HEAD)).astype(bf16),
                                                    (s_t * row(r_ref, t_prev, hp, RWKV_HEAD)).astype(bf16)], axis=1))
                        v_diag.append(eye * jnp.concatenate([row(v_ref, t_now, hp, 16).astype(bf16)] * 4, axis=0))
                        tiles.append(s_t)
                    lhs.append(jnp.concatenate(v_diag, axis=1))
                res = row_sums(jnp.concatenate(lhs, axis=0))
                for hp, s_t in enumerate(tiles):
                    ls = slice(hp * LANES, (hp + 1) * LANES)
                    base = (hp // 2) * 3 * RWKV_HEAD
                    own = res[base + (hp % 2) * RWKV_HEAD:base + (hp % 2 + 1) * RWKV_HEAD]
                    sa, out_prev = own[:, :LANES], own[:, LANES:]
                    v_col = res[base + 2 * RWKV_HEAD:base + 3 * RWKV_HEAD, (hp % 2) * LANES:(hp % 2 + 1) * LANES]
                    s[d, :, ls] = (s_t * row(w_ref, t_now, hp, RWKV_HEAD) + sa * row(b_ref, t_now, hp, RWKV_HEAD)
                                   + v_col * row(k_ref, t_now, hp, RWKV_HEAD))
                    acc[d, :, ls] = jnp.where(lane_in_head == out_lane, out_prev, acc[d, :, ls])
            return carry

        lax.fori_loop(0, RWKV_SUB, step, 0, unroll=8)
        t_last = sc * RWKV_SUB + RWKV_SUB - 1
        last = ((t_last, RWKV_SUB - 1), (tc - 1 - t_last, 0))
        for d in range(2):
            r_ref = dirs[d][0]
            t_now, out_lane = last[d]
            out_mask = jnp.concatenate([lane_in_head == out_lane] * 2, axis=1)
            for hp in range(0, HEAD_PAIRS, 2):
                ls = slice(hp * LANES, (hp + 2) * LANES)
                r_t = jnp.concatenate([row(r_ref, t_now, hp, RWKV_HEAD), row(r_ref, t_now, hp + 1, RWKV_HEAD)], axis=1)
                out_t = row_sums((s[d, :, ls] * r_t).astype(bf16))
                acc[d, :, ls] = jnp.where(out_mask, out_t, acc[d, :, ls])
        of[sc] = acc[0]
        ob[tc // RWKV_SUB - 1 - sc] = acc[1]
        return carry

    lax.fori_loop(0, tc // RWKV_SUB, sub_chunk, 0)


def rwkv_scan_rowsum(r, v, nkk, fwd, bwd, s_f0, s_b0):
    B, L, H, N = r.shape
    D = H * N
    tc = min(L, 256)
    n = L // tc
    fwd_blk = pl.BlockSpec((None, tc, HEAD_PAIRS, LANES), lambda b, j: (b, j, 0, 0))
    bwd_blk = pl.BlockSpec((None, tc, HEAD_PAIRS, LANES), lambda b, j: (b, n - 1 - j, 0, 0))
    r, v, nkk, w_f, k_f, b_f, w_b, k_b, b_b = (t.reshape(B, L, HEAD_PAIRS, LANES) for t in (r, v, nkk, *fwd, *bwd))
    state = pl.BlockSpec((None, 2, RWKV_HEAD, D), lambda b, j: (b, 0, 0, 0))
    nsub = tc // RWKV_SUB
    block_bytes = tc * D * 4
    s0 = jnp.stack([s.transpose(0, 2, 1, 3).reshape(B, N, D) for s in (s_f0, s_b0)], axis=1)
    o_f, o_b, s_fin = pl.pallas_call(
        functools.partial(_rwkv_rowsum_kernel, tc=tc),
        grid=(B, n),
        in_specs=[fwd_blk] * 6 + [bwd_blk] * 6 + [state],
        out_specs=[pl.BlockSpec((None, nsub, RWKV_HEAD, D), lambda b, j: (b, j, 0, 0)),
                   pl.BlockSpec((None, nsub, RWKV_HEAD, D), lambda b, j: (b, n - 1 - j, 0, 0)),
                   state],
        out_shape=[jax.ShapeDtypeStruct((B, L // RWKV_SUB, RWKV_HEAD, D), f32),
                   jax.ShapeDtypeStruct((B, L // RWKV_SUB, RWKV_HEAD, D), f32),
                   jax.ShapeDtypeStruct((B, 2, RWKV_HEAD, D), f32)],
        scratch_shapes=[pltpu.VMEM((2, RWKV_HEAD, D), f32)],
        compiler_params=pltpu.CompilerParams(
            dimension_semantics=("arbitrary", "arbitrary"),
            vmem_limit_bytes=min(V7X_VMEM_BYTES, 2 * 14 * block_bytes + (8 << 20)),
        ),
        name="rwkv_scan_rowsum",
    )(r, v, nkk, w_f, k_f, b_f, r, v, nkk, w_b, k_b, b_b, s0)
    o = (o_f + o_b).reshape(B, L // RWKV_SUB, RWKV_HEAD, H, RWKV_SUB).transpose(0, 1, 4, 3, 2).reshape(B, L, H, N)
    s_fin = s_fin.reshape(B, 2, N, H, N).transpose(0, 1, 3, 2, 4)
    return o, s_fin[:, 0], s_fin[:, 1]


def rwkv_mix(h, mu, wr, wk, wv, wo, w0, w1, w2, a0, a1, a2, g1, g2, k_k, k_a, r_k, lnx_w, lnx_b, s_f0, s_b0):
    B, L, D = h.shape
    H, N = RWKV_HEADS, RWKV_HEAD
    xx = centred_shift(h) - h
    xr, xw, xk, xv, xa, xg = [h + xx * mu[j] for j in range(6)]
    r = (xr @ wr).reshape(B, L, H, N)
    k = (xk @ wk).reshape(B, L, H, N)
    v = (xv @ wv).reshape(B, L, H, N)
    g = jax.nn.sigmoid(xg @ g1) @ g2
    kk = k * k_k.reshape(H, N)
    kk = kk * lax.rsqrt(jnp.sum(kk * kk, axis=-1, keepdims=True) + 1e-12)

    def direction(d):
        w_raw = -jax.nn.softplus(-(w0[d] + jnp.tanh(xw @ w1[d]) @ w2[d])) - 0.5
        decay = jnp.exp(-jnp.exp(w_raw)).reshape(B, L, H, N)
        a = jax.nn.sigmoid(a0[d] + (xa @ a1[d]) @ a2[d]).reshape(B, L, H, N)
        k_d = k * (1.0 + (a - 1.0) * k_a.reshape(H, N))
        return decay, k_d, kk * a

    fwd, bwd = direction(0), direction(1)
    k_f, k_b = fwd[1], bwd[1]
    scan = rwkv_scan_bidir if B * H >= LANES else rwkv_scan_rowsum
    o, s_f, s_b = scan(r, v, -kk, fwd, bwd, s_f0, s_b0)
    m = jnp.mean(o, axis=-1, keepdims=True)
    var = jnp.mean(jnp.square(o - m), axis=-1, keepdims=True)
    o = (o - m) * lax.rsqrt(var + RWKV_LN_EPS) * lnx_w.reshape(H, N) + lnx_b.reshape(H, N)
    bonus = jnp.sum(r * 0.5 * (k_f + k_b) * r_k, axis=-1, keepdims=True) * v
    o = (o + bonus).reshape(B, L, D)
    return (o * g) @ wo, s_f, s_b


GROUP_HID = EXP_PER_GROUP * D_EXPERT


def _moe_group_kernel(tile_group_ref, n_tiles_ref, xs_ref, gates_ref, wg_ref, wu_ref, wd_ref, ys_ref,
                      wg_bf, wu_bf, wd_bf):
    i = pl.program_id(0)
    prev_group = tile_group_ref[jnp.maximum(i - 1, 0)]
    group_changed = jnp.logical_or(i == 0, tile_group_ref[i] != prev_group)

    @pl.when(group_changed)
    def _():
        for e in range(EXP_PER_GROUP):
            hs = slice(e * D_EXPERT, (e + 1) * D_EXPERT)
            wg_bf[:, hs] = wg_ref[e].astype(bf16)
            wu_bf[:, hs] = wu_ref[e].astype(bf16)
            wd_bf[hs, :] = wd_ref[e].astype(bf16)

    @pl.when(i < n_tiles_ref[0])
    def _():
        x = xs_ref[...].astype(bf16)
        g = jnp.dot(x, wg_bf[...], preferred_element_type=f32)
        u = jnp.dot(x, wu_bf[...], preferred_element_type=f32)
        gate = jnp.concatenate([jnp.broadcast_to(gates_ref[:, e:e + 1], (MOE_TILE_M, D_EXPERT))
                                for e in range(EXP_PER_GROUP)], axis=1)
        hid = (g * jax.nn.sigmoid(g)) * u * gate
        ys_ref[...] = jnp.dot(hid.astype(bf16), wd_bf[...], preferred_element_type=f32)

    @pl.when(i >= n_tiles_ref[0])
    def _():
        ys_ref[...] = jnp.zeros_like(ys_ref)


def moe_group_experts(xs, gates, tile_group, n_tiles, w_gate, w_up, w_down):
    P, D = xs.shape
    max_tiles = P // MOE_TILE_M
    weight_bytes = 3 * EXP_PER_GROUP * D * D_EXPERT * (4 + 2)
    tile_bytes = 2 * MOE_TILE_M * (2 * D * 4 + LANES * 4) + 4 * MOE_TILE_M * GROUP_HID * 4
    once = pl.Buffered(1)
    grid_spec = pltpu.PrefetchScalarGridSpec(
        num_scalar_prefetch=2,
        grid=(max_tiles,),
        in_specs=[
            pl.BlockSpec((MOE_TILE_M, D), lambda i, tg, nt: (i, 0)),
            pl.BlockSpec((MOE_TILE_M, EXP_PER_GROUP), lambda i, tg, nt: (i, 0)),
            pl.BlockSpec((EXP_PER_GROUP, D, D_EXPERT), lambda i, tg, nt: (tg[i], 0, 0), pipeline_mode=once),
            pl.BlockSpec((EXP_PER_GROUP, D, D_EXPERT), lambda i, tg, nt: (tg[i], 0, 0), pipeline_mode=once),
            pl.BlockSpec((EXP_PER_GROUP, D_EXPERT, D), lambda i, tg, nt: (tg[i], 0, 0), pipeline_mode=once),
        ],
        out_specs=pl.BlockSpec((MOE_TILE_M, D), lambda i, tg, nt: (i, 0)),
        scratch_shapes=[
            pltpu.VMEM((D, GROUP_HID), bf16),
            pltpu.VMEM((D, GROUP_HID), bf16),
            pltpu.VMEM((GROUP_HID, D), bf16),
        ],
    )
    return pl.pallas_call(
        _moe_group_kernel,
        grid_spec=grid_spec,
        out_shape=jax.ShapeDtypeStruct((P, D), f32),
        compiler_params=pltpu.CompilerParams(
            dimension_semantics=("arbitrary",),
            vmem_limit_bytes=min(V7X_VMEM_BYTES - (4 << 20), weight_bytes + tile_bytes + (8 << 20)),
        ),
        name="moe_group_experts",
    )(tile_group, n_tiles, xs, gates, w_gate, w_up, w_down)


def hier_moe(t, w_grp, b_grp, w_exp, b_exp, w_gate, w_up, w_down):
    T, D = t.shape
    logits = jnp.dot(t, jnp.concatenate([w_grp, w_exp], axis=1), precision=lax.Precision.HIGHEST)
    grp_logits = logits[:, :N_GROUPS] + b_grp
    grp_prob = jax.nn.softmax(grp_logits, axis=-1)
    g_top = jnp.argmax(grp_logits, axis=-1).astype(jnp.int32)
    in_group = g_top[:, None] == jnp.arange(N_GROUPS, dtype=jnp.int32)[None, :]
    p_g = jnp.sum(jnp.where(in_group, grp_prob, 0.0), axis=1, keepdims=True)
    exp_logits = (logits[:, N_GROUPS:] + b_exp).reshape(-1, N_GROUPS, EXP_PER_GROUP)
    sel = jnp.sum(jnp.where(in_group[:, :, None], exp_logits, 0.0), axis=1)
    top_v, top_i = lax.top_k(sel, TOP_K)
    wts = p_g * jax.nn.softmax(top_v, axis=-1)
    gates = jnp.sum(jax.nn.one_hot(top_i, EXP_PER_GROUP, dtype=f32) * wts[..., None], axis=1)

    order = jnp.argsort(g_top, stable=True).astype(jnp.int32)
    grp_i32 = in_group.astype(jnp.int32)
    counts = jnp.sum(grp_i32, axis=0)
    rank = jnp.sum(jnp.where(in_group, jnp.cumsum(grp_i32, axis=0), 0), axis=1) - 1
    tiles_per = (counts + MOE_TILE_M - 1) // MOE_TILE_M
    tile_end = jnp.cumsum(tiles_per)
    start_padded = (tile_end - tiles_per) * MOE_TILE_M
    start_sorted = jnp.cumsum(counts) - counts
    max_tiles = T // MOE_TILE_M + N_GROUPS
    n_tiles = tile_end[-1:].astype(jnp.int32)
    tile_ids = jnp.arange(max_tiles, dtype=jnp.int32)
    tile_group = jnp.minimum(jnp.sum((tile_ids[:, None] >= tile_end[None, :]).astype(jnp.int32), axis=1),
                             N_GROUPS - 1).astype(jnp.int32)
    tile_group = jnp.where(tile_ids < n_tiles[0], tile_group, tile_group[jnp.maximum(n_tiles[0] - 1, 0)])
    row_group = jnp.repeat(tile_group, MOE_TILE_M)
    row_in_group = jnp.arange(max_tiles * MOE_TILE_M, dtype=jnp.int32) - start_padded[row_group]
    row_valid = (row_in_group >= 0) & (row_in_group < counts[row_group])
    row_tok = jnp.where(row_valid, order[jnp.clip(start_sorted[row_group] + row_in_group, 0, T - 1)], 0)
    pos = start_padded[g_top] + rank

    xs = jnp.take(t, row_tok, axis=0)
    row_gates = jnp.where(row_valid[:, None], jnp.take(gates, row_tok, axis=0), 0.0)
    ys = moe_group_experts(xs, row_gates, tile_group, n_tiles, w_gate, w_up, w_down)
    return jnp.take(ys, pos, axis=0)


def kernel(x_prompt, x_sample, c, cache_attn_k, cache_attn_v, state_gla, state_rwkv, c_ctx, ada_w, ada_b, norm_mix, norm_ffn, norm_out, ev_w_in, ev_w_out, gla_dec_w, gla_dec_b, gla_norm, att_sink, rw_mu, rw_wr, rw_wk, rw_wv, rw_wo, rw_w0, rw_w1, rw_w2, rw_a0, rw_a1, rw_a2, rw_g1, rw_g2, rw_kk, rw_ka, rw_rk, rw_lnx_w, rw_lnx_b, moe_w_grp, moe_b_grp, moe_w_exp, moe_b_exp, moe_w_gate, moe_w_up, moe_w_down):
    n_lat = x_sample.shape[1]
    rows = n_lat // GRID_W
    row_pos = jnp.repeat(jnp.arange(rows), GRID_W)
    col_pos = jnp.tile(jnp.arange(GRID_W), rows)
    rope = rope_tables(row_pos, col_pos)
    Bc, Lc, D = x_prompt.shape
    Bl, Ll, _ = x_sample.shape

    xc, xl = x_prompt, x_sample
    cond_ctx = c_ctx[None, :]
    ks_out, vs_out, gla_out, rwkv_out = [], [], [], []
    for l in range(DEPTH):
        sc1, scl1, gc1, sc2, scl2, gc2 = ada_mod(cond_ctx, ada_w[l], ada_b[l])
        sl1, sll1, gl1, sl2, sll2, gl2 = ada_mod(c, ada_w[l], ada_b[l])
        hc = modulate(rmsnorm(xc, norm_mix[l]), sc1, scl1)
        hl = modulate(rmsnorm(xl, norm_mix[l]), sl1, sll1)
        i = l // 2
        if l % 2 == 0:
            oc, k_c, v_c, s_gla = even_mixer_ctx(hc, ev_w_in[i], ev_w_out[i], gla_dec_w[i], gla_dec_b[i],
                                                 gla_norm[i], att_sink[i])
            ol = even_mixer_lat(hl, ev_w_in[i], ev_w_out[i], gla_dec_w[i], gla_dec_b[i], gla_norm[i], att_sink[i],
                                rope, cache_attn_k[:, i], cache_attn_v[:, i], state_gla[:, i])
            ks_out.append(k_c)
            vs_out.append(v_c)
            gla_out.append(s_gla)
        else:
            rw = (rw_mu[i], rw_wr[i], rw_wk[i], rw_wv[i], rw_wo[i], rw_w0[i], rw_w1[i], rw_w2[i], rw_a0[i],
                  rw_a1[i], rw_a2[i], rw_g1[i], rw_g2[i], rw_kk[i], rw_ka[i], rw_rk[i], rw_lnx_w[i], rw_lnx_b[i])
            zero = jnp.zeros((Bc, RWKV_HEADS, RWKV_HEAD, RWKV_HEAD), f32)
            oc, s_f, s_b = rwkv_mix(hc, *rw, zero, zero)
            ol, _, _ = rwkv_mix(hl, *rw, state_rwkv[:, i, 0], state_rwkv[:, i, 1])
            rwkv_out.append(jnp.stack([s_f, s_b], axis=1))
        xc = xc + gc1 * oc
        xl = xl + gl1 * ol
        tc = modulate(rmsnorm(xc, norm_ffn[l]), sc2, scl2).reshape(Bc * Lc, D)
        tl = modulate(rmsnorm(xl, norm_ffn[l]), sl2, sll2).reshape(Bl * Ll, D)
        y = hier_moe(jnp.concatenate([tc, tl], axis=0), moe_w_grp[l], moe_b_grp[l], moe_w_exp[l], moe_b_exp[l],
                     moe_w_gate[l], moe_w_up[l], moe_w_down[l])
        xc = xc + gc2 * y[:Bc * Lc].reshape(Bc, Lc, D)
        xl = xl + gl2 * y[Bc * Lc:].reshape(Bl, Ll, D)

    y_prompt = rmsnorm(xc, norm_out)
    y_sample = rmsnorm(xl, norm_out)
    new_attn_k = jnp.stack(ks_out, axis=1)
    new_attn_v = jnp.stack(vs_out, axis=1)
    new_gla = jnp.stack(gla_out, axis=1)
    new_rwkv = jnp.stack(rwkv_out, axis=1)
    return (y_prompt, y_sample, new_attn_k, new_attn_v, new_gla, new_rwkv)
```

```python
import functools

import jax
import jax.numpy as jnp
import numpy as np
from jax import lax
from jax.experimental import pallas as pl
from jax.experimental.pallas import tpu as pltpu

D_MODEL = 1024
DEPTH = 4
GRID_W = 64
EPS = 1e-6
GLA_HEADS = 4
GLA_DK = 64
GLA_DV = 128
GLA_LOWRANK = 16
GLA_GATE_TEMP = 16.0
GLA_CHUNK = 64
ATT_HEADS = 8
ATT_KV_HEADS = 2
ATT_GROUP = ATT_HEADS // ATT_KV_HEADS
ATT_HD = 64
WINDOW = 128
ATT_BLOCK = 128
ROPE_BASE = 10000.0
ROPE_AXIS_DIMS = ATT_HD // 2
NEG = -1e30
IN_WIDTHS = (GLA_HEADS * GLA_DK, GLA_HEADS * GLA_DK, GLA_HEADS * GLA_DV, GLA_HEADS * GLA_DV,
             ATT_HEADS * ATT_HD, ATT_KV_HEADS * ATT_HD, ATT_KV_HEADS * ATT_HD, 2 * GLA_LOWRANK)
IN_SPLITS = tuple(int(v) for v in np.cumsum(IN_WIDTHS)[:-1])
RWKV_HEAD = 64
RWKV_HEADS = D_MODEL // RWKV_HEAD
RWKV_LN_EPS = 64e-5
N_GROUPS = 4
EXP_PER_GROUP = 4
N_EXPERTS = N_GROUPS * EXP_PER_GROUP
TOP_K = 2
D_EXPERT = 512

V7X_VMEM_BYTES = 64 * 1024 * 1024
MOE_TILE_M = 256

f32 = jnp.float32
bf16 = jnp.bfloat16


def rmsnorm(x, w):
    y = x * lax.rsqrt(jnp.mean(x * x, axis=-1, keepdims=True) + EPS)
    return y * w


def ada_mod(cond, w, b):
    mod = jax.nn.silu(cond) @ w + b
    return [m[:, None, :] for m in jnp.split(mod, 6, axis=-1)]


def modulate(x, shift, scale):
    return x * (1.0 + scale) + shift


def rope_tables(row_pos, col_pos):
    freqs = ROPE_BASE ** (-jnp.arange(0, ROPE_AXIS_DIMS, 2, dtype=f32) / ROPE_AXIS_DIMS)
    ang_r = row_pos.astype(f32)[:, None] * freqs[None, :]
    ang_c = col_pos.astype(f32)[:, None] * freqs[None, :]
    return jnp.cos(ang_r), jnp.sin(ang_r), jnp.cos(ang_c), jnp.sin(ang_c)


def rotate(x, cos, sin):
    x1, x2 = jnp.split(x, 2, axis=-1)
    cos = cos[None, :, None, :]
    sin = sin[None, :, None, :]
    return jnp.concatenate([x1 * cos - x2 * sin, x1 * sin + x2 * cos], axis=-1)


def apply_axial_rope(x, tables):
    cr, sr, cc, sc = tables
    xr, xc = jnp.split(x, 2, axis=-1)
    return jnp.concatenate([rotate(xr, cr, sr), rotate(xc, cc, sc)], axis=-1)


GLA_QK = GLA_HEADS * GLA_DK
GLA_V = GLA_HEADS * GLA_DV


def _gla_kernel(q_ref, v_ref, gg_ref, ldf_ref, ldb_ref, kt_ref, ldft_ref, ldbt_ref, s0_ref, norm_ref,
                o_ref, s_ref, *, seq_len):
    n = seq_len // GLA_CHUNK
    C = GLA_CHUNK
    ti = lax.broadcasted_iota(jnp.int32, (C, C), 0)
    tj = lax.broadcasted_iota(jnp.int32, (C, C), 1)
    keep = (tj <= ti, tj >= ti)
    tri = tuple(m.astype(bf16) for m in keep)
    tri_t = (tri[1], tri[0])
    ones = jnp.ones((C, GLA_DV), bf16)
    s_ref[...] = s0_ref[...]
    o_ref[...] = jnp.zeros_like(o_ref)
    ld_refs = ((ldf_ref, ldft_ref), (ldb_ref, ldbt_ref))

    def split3(x):
        hi = x.astype(bf16)
        rest = x - hi.astype(f32)
        mid = rest.astype(bf16)
        return hi, mid, (rest - mid.astype(f32)).astype(bf16)

    def sums(a, b):
        if isinstance(b, tuple):
            return sum(jnp.dot(a, p, preferred_element_type=f32) for p in b)
        return sum(jnp.dot(p, b, preferred_element_type=f32) for p in a)

    def chunk(c, carry):
        pending = []
        for d in range(2):
            cc = c if d == 0 else n - 1 - c
            rows = pl.ds(pl.multiple_of(cc * C, C), C)
            ld_ref, ldt_ref = ld_refs[d]
            ld_t3 = split3(ldt_ref[cc])
            b_all = sums(tri[d], split3(ld_ref[rows, :]))
            bt_all = sums(ld_t3, tri_t[d])
            total_all = sums(ld_t3, ones)
            for h in range(GLA_HEADS):
                ks = slice(h * GLA_DK, (h + 1) * GLA_DK)
                vs = slice(h * GLA_DV, (h + 1) * GLA_DV)
                q = q_ref[rows, ks] * (GLA_DK ** -0.5)
                v = v_ref[rows, vs].astype(bf16)
                k_t = kt_ref[cc, ks, :]
                b, b_t, total = b_all[:, ks], bt_all[ks], total_all[ks]
                q_e = (q * jnp.exp(b)).astype(bf16)
                k_e = (k_t * jnp.exp(-b_t)).astype(bf16)
                k_s = (k_t * jnp.exp(total[:, :C] - b_t)).astype(bf16)
                att = jnp.where(keep[d], jnp.dot(q_e, k_e, preferred_element_type=f32), 0.0).astype(bf16)
                s = s_ref[d, h]
                o = (jnp.dot(q_e, s.astype(bf16), preferred_element_type=f32)
                     + jnp.dot(att, v, preferred_element_type=f32))
                s_new = s * jnp.exp(total) + jnp.dot(k_s, v, preferred_element_type=f32)
                pending.append((d, h, rows, vs, o_ref[rows, vs] + o, s_new))
        for d, h, rows, vs, o, s_new in pending:
            o_ref[rows, vs] = o
            s_ref[d, h] = s_new
        return carry

    lax.fori_loop(0, n, chunk, 0)
    for h in range(GLA_HEADS):
        vs = slice(h * GLA_DV, (h + 1) * GLA_DV)
        x = o_ref[:, vs]
        y = x * lax.rsqrt(jnp.mean(x * x, axis=-1, keepdims=True) + EPS) * norm_ref[...]
        g = gg_ref[:, vs]
        o_ref[:, vs] = y * (g * jax.nn.sigmoid(g))


def gla_bidir_gated(z, ld_f, ld_b, s0, gla_norm):
    B, L, _ = z.shape
    n = L // GLA_CHUNK
    assert n % 2 == 0

    def per_chunk_t(a):
        return a.reshape(B, n, GLA_CHUNK, GLA_QK).transpose(0, 1, 3, 2)

    state = pl.BlockSpec((None, 2, GLA_HEADS, GLA_DK, GLA_DV), lambda b: (b, 0, 0, 0, 0))
    ld = pl.BlockSpec((None, L, GLA_QK), lambda b: (b, 0, 0))
    ld_t = pl.BlockSpec((None, n, GLA_QK, GLA_CHUNK), lambda b: (b, 0, 0, 0))
    return pl.pallas_call(
        functools.partial(_gla_kernel, seq_len=L),
        grid=(B,),
        in_specs=[pl.BlockSpec((None, L, GLA_QK), lambda b: (b, 0, 0)),
                  pl.BlockSpec((None, L, GLA_V), lambda b: (b, 0, 1)),
                  pl.BlockSpec((None, L, GLA_V), lambda b: (b, 0, 2)),
                  ld, ld, ld_t, ld_t, ld_t, state,
                  pl.BlockSpec((1, GLA_DV), lambda b: (0, 0))],
        out_specs=[pl.BlockSpec((None, L, GLA_V), lambda b: (b, 0, 0)), state],
        out_shape=[jax.ShapeDtypeStruct((B, L, GLA_V), f32),
                   jax.ShapeDtypeStruct((B, 2, GLA_HEADS, GLA_DK, GLA_DV), f32)],
        compiler_params=pltpu.CompilerParams(
            dimension_semantics=("arbitrary",),
            vmem_limit_bytes=min(V7X_VMEM_BYTES, 2 * L * (9 * GLA_QK + 3 * GLA_V) * 4 + (16 << 20)),
        ),
        name="gla_bidir",
    )(z, z, z, ld_f, ld_b, per_chunk_t(z[..., GLA_QK:2 * GLA_QK]), per_chunk_t(ld_f), per_chunk_t(ld_b),
      s0, gla_norm.reshape(1, GLA_DV))


def ctx_attn(q, k, v, sink):
    B, L = q.shape[:2]
    scale = ATT_HD ** -0.5
    qg = q.reshape(B, L, ATT_KV_HEADS, ATT_GROUP, ATT_HD)
    sink_b = sink.reshape(ATT_KV_HEADS, ATT_GROUP)[None, :, :, None, None]
    s = jnp.einsum('bqhgd,bkhd->bhgqk', qg, k) * scale
    s_sink = jnp.broadcast_to(sink_b, s.shape[:-1] + (1,))
    p = jax.nn.softmax(jnp.concatenate([s_sink, s], axis=-1), axis=-1)[..., 1:]
    return jnp.einsum('bhgqk,bkhd->bqhgd', p, v).reshape(B, L, ATT_HEADS * ATT_HD)


def window_attn_latent(q, k, v, k_ctx, v_ctx, sink):
    B, S = q.shape[:2]
    nb = S // ATT_BLOCK
    Lc = k_ctx.shape[1]
    scale = ATT_HD ** -0.5
    qb = q.reshape(B, nb, ATT_BLOCK, ATT_KV_HEADS, ATT_GROUP, ATT_HD).transpose(1, 0, 2, 3, 4, 5)
    pad = ((0, 0), (ATT_BLOCK, ATT_BLOCK), (0, 0), (0, 0))
    kp = jnp.pad(k, pad)
    vp = jnp.pad(v, pad)
    sink_b = sink.reshape(ATT_KV_HEADS, ATT_GROUP)[None, :, :, None, None]
    offs_q = jnp.arange(ATT_BLOCK)
    offs_k = jnp.arange(3 * ATT_BLOCK) - ATT_BLOCK

    def block(args):
        bi, q_blk = args
        k_band = lax.dynamic_slice_in_dim(kp, bi * ATT_BLOCK, 3 * ATT_BLOCK, axis=1)
        v_band = lax.dynamic_slice_in_dim(vp, bi * ATT_BLOCK, 3 * ATT_BLOCK, axis=1)
        qpos = bi * ATT_BLOCK + offs_q
        kpos = bi * ATT_BLOCK + offs_k
        valid = (jnp.abs(qpos[:, None] - kpos[None, :]) <= WINDOW) & (kpos >= 0)[None, :] & (kpos < S)[None, :]
        s_lat = jnp.einsum('bqhgd,bkhd->bhgqk', q_blk, k_band) * scale
        s_lat = jnp.where(valid, s_lat, NEG)
        s_ctx = jnp.einsum('bqhgd,bchd->bhgqc', q_blk, k_ctx) * scale
        s_sink = jnp.broadcast_to(sink_b, s_ctx.shape[:-1] + (1,))
        p = jax.nn.softmax(jnp.concatenate([s_sink, s_ctx, s_lat], axis=-1), axis=-1)
        o = (jnp.einsum('bhgqc,bchd->bqhgd', p[..., 1:1 + Lc], v_ctx)
             + jnp.einsum('bhgqk,bkhd->bqhgd', p[..., 1 + Lc:], v_band))
        return o.reshape(B, ATT_BLOCK, ATT_HEADS * ATT_HD)

    o = lax.map(block, (jnp.arange(nb), qb))
    return o.transpose(1, 0, 2, 3).reshape(B, S, ATT_HEADS * ATT_HD)


def even_projections(h, w_in, dec_w, dec_b):
    B, L, _ = h.shape
    z = h @ w_in
    aq, ak, av, lr = jnp.split(z[..., IN_SPLITS[3]:], [s - IN_SPLITS[3] for s in IN_SPLITS[4:]], axis=-1)
    lr_f, lr_b = jnp.split(lr, 2, axis=-1)

    def logdecay(lr_d, w, b):
        return jax.nn.log_sigmoid(lr_d @ w + b) / GLA_GATE_TEMP

    ld_f = logdecay(lr_f, dec_w[0], dec_b[0])
    ld_b = logdecay(lr_b, dec_w[1], dec_b[1])
    aq = aq.reshape(B, L, ATT_HEADS, ATT_HD)
    ak = ak.reshape(B, L, ATT_KV_HEADS, ATT_HD)
    av = av.reshape(B, L, ATT_KV_HEADS, ATT_HD)
    return z, ld_f, ld_b, aq, ak, av


def even_mixer_ctx(h, w_in, w_out, dec_w, dec_b, gla_norm, sink):
    z, ld_f, ld_b, aq, ak, av = even_projections(h, w_in, dec_w, dec_b)
    zero = jnp.zeros((h.shape[0], 2, GLA_HEADS, GLA_DK, GLA_DV), f32)
    o_gla, s_fin = gla_bidir_gated(z, ld_f, ld_b, zero, gla_norm)
    o_att = ctx_attn(aq, ak, av, sink)
    return jnp.concatenate([o_gla, o_att], axis=-1) @ w_out, ak, av, s_fin


def even_mixer_lat(h, w_in, w_out, dec_w, dec_b, gla_norm, sink, rope, k_ctx, v_ctx, s0):
    z, ld_f, ld_b, aq, ak, av = even_projections(h, w_in, dec_w, dec_b)
    o_gla, _ = gla_bidir_gated(z, ld_f, ld_b, s0, gla_norm)
    o_att = window_attn_latent(apply_axial_rope(aq, rope), apply_axial_rope(ak, rope), av, k_ctx, v_ctx, sink)
    return jnp.concatenate([o_gla, o_att], axis=-1) @ w_out


def centred_shift(x):
    xp = jnp.pad(x, ((0, 0), (1, 1), (0, 0)))
    return 0.5 * (xp[:, :-2] + xp[:, 2:])


LANES = 128
SUBLANES = 8
RWKV_TC = 32


def _rwkv_scan_kernel(r_ref, v_ref, k_ref, zw_ref, za_ref, kk_ref, ka_ref, rk_ref, s0_ref, o_ref, bonus_ref, s_ref,
                      *, tc, nv, n_dir_groups):
    backward = pl.program_id(0) >= n_dir_groups

    @pl.when(pl.program_id(1) == 0)
    def _():
        s_ref[...] = s0_ref[...]

    def step(i, carry):
        t = jnp.where(backward, tc - 1 - i, i)
        k_t, r = k_ref[t], r_ref[t]
        kk = k_t * kk_ref[0]
        kk = kk * lax.rsqrt(jnp.sum(kk * kk, axis=0, keepdims=True) + 1e-12)
        a = jax.nn.sigmoid(za_ref[t])
        z = -zw_ref[t]
        softplus = jnp.maximum(z, 0.0) + jnp.log1p(jnp.exp(-jnp.abs(z)))
        w = jnp.exp(-jnp.exp(-softplus - 0.5))
        alpha, beta = -kk, kk * a
        kd = k_t * (1.0 + (a - 1.0) * ka_ref[0])
        bonus_ref[pl.ds(t, 1), :] = jnp.sum(r * 0.5 * kd * rk_ref[0], axis=0, keepdims=True)
        for vb in range(nv // SUBLANES):
            outs = []
            for v in range(vb * SUBLANES, (vb + 1) * SUBLANES):
                s = s_ref[v]
                sa = jnp.sum(s * alpha, axis=0, keepdims=True)
                s_new = s * w + sa * beta + v_ref[t, pl.ds(v, 1), :] * kd
                s_ref[v] = s_new
                outs.append(jnp.sum(s_new * r, axis=0, keepdims=True))
            o_ref[t, vb * SUBLANES:(vb + 1) * SUBLANES, :] = jnp.concatenate(outs, axis=0)
        return carry

    lax.fori_loop(0, tc, step, 0)


def rwkv_scan_lanes(r, v, k, zw, za, k_k, k_a, r_k, s0):
    gd, L, nv, _ = v.shape
    tc = RWKV_TC
    n = L // tc

    def time_block(g, j):
        return jnp.where(g >= gd, n - 1 - j, j)

    def blk(rows, shared):
        return pl.BlockSpec((None, tc, rows, LANES), lambda g, j: (g % gd if shared else g, time_block(g, j), 0, 0))

    const = pl.BlockSpec((None, 1, RWKV_HEAD, LANES), lambda g, j: (g % gd, 0, 0, 0))
    state = pl.BlockSpec((None, nv, RWKV_HEAD, LANES), lambda g, j: (g, 0, 0, 0))
    block_bytes = tc * RWKV_HEAD * LANES * 4
    return pl.pallas_call(
        functools.partial(_rwkv_scan_kernel, tc=tc, nv=nv, n_dir_groups=gd),
        grid=(2 * gd, n),
        in_specs=[blk(RWKV_HEAD, True), blk(nv, True), blk(RWKV_HEAD, True),
                  blk(RWKV_HEAD, False), blk(RWKV_HEAD, False), const, const, const, state],
        out_specs=[blk(nv, False), pl.BlockSpec((None, tc, LANES), lambda g, j: (g, time_block(g, j), 0)), state],
        out_shape=[jax.ShapeDtypeStruct((2 * gd, L, nv, LANES), f32),
                   jax.ShapeDtypeStruct((2 * gd, L, LANES), f32),
                   jax.ShapeDtypeStruct((2 * gd, nv, RWKV_HEAD, LANES), f32)],
        compiler_params=pltpu.CompilerParams(
            dimension_semantics=("arbitrary", "arbitrary"),
            vmem_limit_bytes=2 * 7 * block_bytes + 4 * nv * RWKV_HEAD * LANES * 4 + (8 << 20),
        ),
        name="rwkv_scan",
    )(r, v, k, zw, za, k_k, k_a, r_k, s0)


def rwkv_scan_bidir(r, v, k, zw, za, k_k, k_a, r_k, s_f0, s_b0):
    B, L, H, N = r.shape
    S = B * H
    gd = S // LANES

    def rows(x):
        T = x.shape[-3]
        lead = x.shape[:-4]
        x = x.reshape(-1, B, T, H, N).transpose(0, 2, 4, 1, 3).reshape(-1, T, N, gd, LANES)
        return x.transpose(0, 3, 1, 2, 4).reshape(-1, T, N, LANES) if lead else x[0].transpose(2, 0, 1, 3)

    def rows_inv(y):
        return y.transpose(1, 2, 0, 3).reshape(y.shape[1], N, B, H)

    def state_in(s):
        return rows(s.transpose(0, 3, 1, 2)).transpose(0, 2, 1, 3)

    def state_out(s):
        return rows_inv(s.transpose(0, 2, 1, 3)).transpose(2, 3, 1, 0)

    def const_rows(c):
        return rows(jnp.broadcast_to(c[None, None], (B, 1, H, N)))

    out, bonus, s_fin = rwkv_scan_lanes(rows(r), rows(v), rows(k), rows(zw), rows(za),
                                        const_rows(k_k), const_rows(k_a), const_rows(r_k),
                                        jnp.concatenate([state_in(s_f0), state_in(s_b0)], axis=0))
    o = rows_inv(out[:gd]) + rows_inv(out[gd:])
    bonus = (bonus[:gd] + bonus[gd:]).transpose(1, 0, 2).reshape(L, B, H).transpose(1, 0, 2)
    return o.transpose(2, 0, 3, 1), bonus, state_out(s_fin[:gd]), state_out(s_fin[gd:])


RWKV_SUB = 64
HEAD_PAIRS = D_MODEL // LANES


def _rwkv_rowsum_kernel(rf, vf, af, wf, kf, bf_, rb, vb, ab, wb, kb, bb, s0, of, ob, s, acc, *, tc):
    j = pl.program_id(1)

    @pl.when(j == 0)
    def _():
        s[...] = s0[...]

    row_head = lax.broadcasted_iota(jnp.int32, (2 * LANES, 2 * LANES), 0) // RWKV_HEAD
    col_head = lax.broadcasted_iota(jnp.int32, (2 * LANES, 2 * LANES), 1) // RWKV_HEAD
    ones_bd = (row_head == col_head).astype(bf16)
    sub = lax.broadcasted_iota(jnp.int32, (RWKV_HEAD, LANES), 0)
    lane_in_head = lax.broadcasted_iota(jnp.int32, (RWKV_HEAD, LANES), 1) % RWKV_HEAD
    eye = (sub == lane_in_head).astype(bf16)
    dirs = ((rf, vf, af, wf, kf, bf_), (rb, vb, ab, wb, kb, bb))

    def row_sums(x):
        return jnp.dot(x, ones_bd, preferred_element_type=f32)

    def row(ref, t, hp, reps):
        return jnp.broadcast_to(ref[t, pl.ds(hp, 1), :], (reps, LANES))

    def sub_chunk(sc, carry):
        def step(tt, carry):
            t_f = sc * RWKV_SUB + tt
            pos = ((t_f, jnp.maximum(t_f - 1, 0), tt - 1),
                   (tc - 1 - t_f, jnp.minimum(tc - t_f, tc - 1), RWKV_SUB - tt))
            for d in range(2):
                r_ref, v_ref, a_ref, w_ref, k_ref, b_ref = dirs[d]
                t_now, t_prev, out_lane = pos[d]
                lhs, tiles = [], []
                for hp2 in range(0, HEAD_PAIRS, 2):
                    v_diag = []
                    for hp in (hp2, hp2 + 1):
                        s_t = s[d, :, hp * LANES:(hp + 1) * LANES]
                        lhs.append(jnp.concatenate([(s_t * row(a_ref, t_now, hp, RWKV_HEAD)).astype(bf16),
                                                    (s_t * row(r_ref, t_prev, hp, RWKV_HEAD)).astype(bf16)], axis=1))
                        v_diag.append(eye * jnp.concatenate([row(v_ref, t_now, hp, 16).astype(bf16)] * 4, axis=0))
                        tiles.append(s_t)
                    lhs.append(jnp.concatenate(v_diag, axis=1))
                res = row_sums(jnp.concatenate(lhs, axis=0))
                for hp, s_t in enumerate(tiles):
                    ls = slice(hp * LANES, (hp + 1) * LANES)
                    base = (hp // 2) * 3 * RWKV_HEAD
                    own = res[base + (hp % 2) * RWKV_HEAD:base + (hp % 2 + 1) * RWKV_HEAD]
                    sa, out_prev = own[:, :LANES], own[:, LANES:]
                    v_col = res[base + 2 * RWKV_HEAD:base + 3 * RWKV_HEAD, (hp % 2) * LANES:(hp % 2 + 1) * LANES]
                    s[d, :, ls] = (s_t * row(w_ref, t_now, hp, RWKV_HEAD) + sa * row(b_ref, t_now, hp, RWKV_HEAD)
                                   + v_col * row(k_ref, t_now, hp, RWKV_HEAD))
                    acc[d, :, ls] = jnp.where(lane_in_head == out_lane, out_prev, acc[d, :, ls])
            return carry

        lax.fori_loop(0, RWKV_SUB, step, 0, unroll=8)
        t_last = sc * RWKV_SUB + RWKV_SUB - 1
        last = ((t_last, RWKV_SUB - 1), (tc - 1 - t_last, 0))
        for d in range(2):
            r_ref = dirs[d][0]
            t_now, out_lane = last[d]
            out_mask = jnp.concatenate([lane_in_head == out_lane] * 2, axis=1)
            for hp in range(0, HEAD_PAIRS, 2):
                ls = slice(hp * LANES, (hp + 2) * LANES)
                r_t = jnp.concatenate([row(r_ref, t_now, hp, RWKV_HEAD), row(r_ref, t_now, hp + 1, RWKV_HEAD)], axis=1)
                out_t = row_sums((s[d, :, ls] * r_t).astype(bf16))
                acc[d, :, ls] = jnp.where(out_mask, out_t, acc[d, :, ls])
        of[sc] = acc[0]
        ob[tc // RWKV_SUB - 1 - sc] = acc[1]
        return carry

    lax.fori_loop(0, tc // RWKV_SUB, sub_chunk, 0)


def rwkv_scan_rowsum(r, v, nkk, fwd, bwd, s_f0, s_b0):
    B, L, H, N = r.shape
    D = H * N
    tc = min(L, 256)
    n = L // tc
    fwd_blk = pl.BlockSpec((None, tc, HEAD_PAIRS, LANES), lambda b, j: (b, j, 0, 0))
    bwd_blk = pl.BlockSpec((None, tc, HEAD_PAIRS, LANES), lambda b, j: (b, n - 1 - j, 0, 0))
    r, v, nkk, w_f, k_f, b_f, w_b, k_b, b_b = (t.reshape(B, L, HEAD_PAIRS, LANES) for t in (r, v, nkk, *fwd, *bwd))
    state = pl.BlockSpec((None, 2, RWKV_HEAD, D), lambda b, j: (b, 0, 0, 0))
    nsub = tc // RWKV_SUB
    block_bytes = tc * D * 4
    s0 = jnp.stack([s.transpose(0, 2, 1, 3).reshape(B, N, D) for s in (s_f0, s_b0)], axis=1)
    o_f, o_b, s_fin = pl.pallas_call(
        functools.partial(_rwkv_rowsum_kernel, tc=tc),
        grid=(B, n),
        in_specs=[fwd_blk] * 6 + [bwd_blk] * 6 + [state],
        out_specs=[pl.BlockSpec((None, nsub, RWKV_HEAD, D), lambda b, j: (b, j, 0, 0)),
                   pl.BlockSpec((None, nsub, RWKV_HEAD, D), lambda b, j: (b, n - 1 - j, 0, 0)),
                   state],
        out_shape=[jax.ShapeDtypeStruct((B, L // RWKV_SUB, RWKV_HEAD, D), f32),
                   jax.ShapeDtypeStruct((B, L // RWKV_SUB, RWKV_HEAD, D), f32),
                   jax.ShapeDtypeStruct((B, 2, RWKV_HEAD, D), f32)],
        scratch_shapes=[pltpu.VMEM((2, RWKV_HEAD, D), f32)],
        compiler_params=pltpu.CompilerParams(
            dimension_semantics=("arbitrary", "arbitrary"),
            vmem_limit_bytes=min(V7X_VMEM_BYTES, 2 * 14 * block_bytes + (8 << 20)),
        ),
        name="rwkv_scan_rowsum",
    )(r, v, nkk, w_f, k_f, b_f, r, v, nkk, w_b, k_b, b_b, s0)
    o = (o_f + o_b).reshape(B, L // RWKV_SUB, RWKV_HEAD, H, RWKV_SUB).transpose(0, 1, 4, 3, 2).reshape(B, L, H, N)
    s_fin = s_fin.reshape(B, 2, N, H, N).transpose(0, 1, 3, 2, 4)
    return o, s_fin[:, 0], s_fin[:, 1]


def rwkv_mix(h, mu, wr, wk, wv, wo, w0, w1, w2, a0, a1, a2, g1, g2, k_k, k_a, r_k, lnx_w, lnx_b, s_f0, s_b0):
    B, L, D = h.shape
    H, N = RWKV_HEADS, RWKV_HEAD
    xx = centred_shift(h) - h
    xr, xw, xk, xv, xa, xg = [h + xx * mu[j] for j in range(6)]
    r = (xr @ wr).reshape(B, L, H, N)
    k = (xk @ wk).reshape(B, L, H, N)
    v = (xv @ wv).reshape(B, L, H, N)
    g = jax.nn.sigmoid(xg @ g1) @ g2
    zw = [(w0[d] + jnp.tanh(xw @ w1[d]) @ w2[d]).reshape(B, L, H, N) for d in range(2)]
    za = [(a0[d] + (xa @ a1[d]) @ a2[d]).reshape(B, L, H, N) for d in range(2)]

    if B * H >= LANES:
        o, bonus, s_f, s_b = rwkv_scan_bidir(r, v, k, jnp.stack(zw), jnp.stack(za), k_k.reshape(H, N),
                                             k_a.reshape(H, N), r_k, s_f0, s_b0)
        bonus = bonus[..., None]
    else:
        kk = k * k_k.reshape(H, N)
        kk = kk * lax.rsqrt(jnp.sum(kk * kk, axis=-1, keepdims=True) + 1e-12)

        def direction(d):
            decay = jnp.exp(-jnp.exp(-jax.nn.softplus(-zw[d]) - 0.5))
            a = jax.nn.sigmoid(za[d])
            return decay, k * (1.0 + (a - 1.0) * k_a.reshape(H, N)), kk * a

        fwd, bwd = direction(0), direction(1)
        o, s_f, s_b = rwkv_scan_rowsum(r, v, -kk, fwd, bwd, s_f0, s_b0)
        bonus = jnp.sum(r * 0.5 * (fwd[1] + bwd[1]) * r_k, axis=-1, keepdims=True)
    m = jnp.mean(o, axis=-1, keepdims=True)
    var = jnp.mean(jnp.square(o - m), axis=-1, keepdims=True)
    o = (o - m) * lax.rsqrt(var + RWKV_LN_EPS) * lnx_w.reshape(H, N) + lnx_b.reshape(H, N)
    o = (o + bonus * v).reshape(B, L, D)
    return (o * g) @ wo, s_f, s_b


GROUP_HID = EXP_PER_GROUP * D_EXPERT


def _moe_group_kernel(tile_group_ref, n_tiles_ref, xs_ref, gates_ref, wg_ref, wu_ref, wd_ref, ys_ref,
                      wg_bf, wu_bf, wd_bf):
    i = pl.program_id(0)
    prev_group = tile_group_ref[jnp.maximum(i - 1, 0)]
    group_changed = jnp.logical_or(i == 0, tile_group_ref[i] != prev_group)

    @pl.when(group_changed)
    def _():
        for e in range(EXP_PER_GROUP):
            hs = slice(e * D_EXPERT, (e + 1) * D_EXPERT)
            wg_bf[:, hs] = wg_ref[e].astype(bf16)
            wu_bf[:, hs] = wu_ref[e].astype(bf16)
            wd_bf[hs, :] = wd_ref[e].astype(bf16)

    @pl.when(i < n_tiles_ref[0])
    def _():
        x = xs_ref[...].astype(bf16)
        g = jnp.dot(x, wg_bf[...], preferred_element_type=f32)
        u = jnp.dot(x, wu_bf[...], preferred_element_type=f32)
        gate = jnp.concatenate([jnp.broadcast_to(gates_ref[:, e:e + 1], (MOE_TILE_M, D_EXPERT))
                                for e in range(EXP_PER_GROUP)], axis=1)
        hid = (g * jax.nn.sigmoid(g)) * u * gate
        ys_ref[...] = jnp.dot(hid.astype(bf16), wd_bf[...], preferred_element_type=f32)

    @pl.when(i >= n_tiles_ref[0])
    def _():
        ys_ref[...] = jnp.zeros_like(ys_ref)


def moe_group_experts(xs, gates, tile_group, n_tiles, w_gate, w_up, w_down):
    P, D = xs.shape
    max_tiles = P // MOE_TILE_M
    weight_bytes = 3 * EXP_PER_GROUP * D * D_EXPERT * (4 + 2)
    tile_bytes = 2 * MOE_TILE_M * (2 * D * 4 + LANES * 4) + 4 * MOE_TILE_M * GROUP_HID * 4
    once = pl.Buffered(1)
    grid_spec = pltpu.PrefetchScalarGridSpec(
        num_scalar_prefetch=2,
        grid=(max_tiles,),
        in_specs=[
            pl.BlockSpec((MOE_TILE_M, D), lambda i, tg, nt: (i, 0)),
            pl.BlockSpec((MOE_TILE_M, EXP_PER_GROUP), lambda i, tg, nt: (i, 0)),
            pl.BlockSpec((EXP_PER_GROUP, D, D_EXPERT), lambda i, tg, nt: (tg[i], 0, 0), pipeline_mode=once),
            pl.BlockSpec((EXP_PER_GROUP, D, D_EXPERT), lambda i, tg, nt: (tg[i], 0, 0), pipeline_mode=once),
            pl.BlockSpec((EXP_PER_GROUP, D_EXPERT, D), lambda i, tg, nt: (tg[i], 0, 0), pipeline_mode=once),
        ],
        out_specs=pl.BlockSpec((MOE_TILE_M, D), lambda i, tg, nt: (i, 0)),
        scratch_shapes=[
            pltpu.VMEM((D, GROUP_HID), bf16),
            pltpu.VMEM((D, GROUP_HID), bf16),
            pltpu.VMEM((GROUP_HID, D), bf16),
        ],
    )
    return pl.pallas_call(
        _moe_group_kernel,
        grid_spec=grid_spec,
        out_shape=jax.ShapeDtypeStruct((P, D), f32),
        compiler_params=pltpu.CompilerParams(
            dimension_semantics=("arbitrary",),
            vmem_limit_bytes=min(V7X_VMEM_BYTES - (4 << 20), weight_bytes + tile_bytes + (8 << 20)),
        ),
        name="moe_group_experts",
    )(tile_group, n_tiles, xs, gates, w_gate, w_up, w_down)


def hier_moe(t, w_grp, b_grp, w_exp, b_exp, w_gate, w_up, w_down):
    T, D = t.shape
    logits = jnp.dot(t, jnp.concatenate([w_grp, w_exp], axis=1), precision=lax.Precision.HIGHEST)
    grp_logits = logits[:, :N_GROUPS] + b_grp
    grp_prob = jax.nn.softmax(grp_logits, axis=-1)
    g_top = jnp.argmax(grp_logits, axis=-1).astype(jnp.int32)
    in_group = g_top[:, None] == jnp.arange(N_GROUPS, dtype=jnp.int32)[None, :]
    p_g = jnp.sum(jnp.where(in_group, grp_prob, 0.0), axis=1, keepdims=True)
    exp_logits = (logits[:, N_GROUPS:] + b_exp).reshape(-1, N_GROUPS, EXP_PER_GROUP)
    sel = jnp.sum(jnp.where(in_group[:, :, None], exp_logits, 0.0), axis=1)
    top_v, top_i = lax.top_k(sel, TOP_K)
    wts = p_g * jax.nn.softmax(top_v, axis=-1)
    gates = jnp.sum(jax.nn.one_hot(top_i, EXP_PER_GROUP, dtype=f32) * wts[..., None], axis=1)

    order = jnp.argsort(g_top, stable=True).astype(jnp.int32)
    grp_i32 = in_group.astype(jnp.int32)
    counts = jnp.sum(grp_i32, axis=0)
    rank = jnp.sum(jnp.where(in_group, jnp.cumsum(grp_i32, axis=0), 0), axis=1) - 1
    tiles_per = (counts + MOE_TILE_M - 1) // MOE_TILE_M
    tile_end = jnp.cumsum(tiles_per)
    start_padded = (tile_end - tiles_per) * MOE_TILE_M
    start_sorted = jnp.cumsum(counts) - counts
    max_tiles = T // MOE_TILE_M + N_GROUPS
    n_tiles = tile_end[-1:].astype(jnp.int32)
    tile_ids = jnp.arange(max_tiles, dtype=jnp.int32)
    tile_group = jnp.minimum(jnp.sum((tile_ids[:, None] >= tile_end[None, :]).astype(jnp.int32), axis=1),
                             N_GROUPS - 1).astype(jnp.int32)
    tile_group = jnp.where(tile_ids < n_tiles[0], tile_group, tile_group[jnp.maximum(n_tiles[0] - 1, 0)])
    row_group = jnp.repeat(tile_group, MOE_TILE_M)
    row_in_group = jnp.arange(max_tiles * MOE_TILE_M, dtype=jnp.int32) - start_padded[row_group]
    row_valid = (row_in_group >= 0) & (row_in_group < counts[row_group])
    row_tok = jnp.where(row_valid, order[jnp.clip(start_sorted[row_group] + row_in_group, 0, T - 1)], 0)
    pos = start_padded[g_top] + rank

    xs = jnp.take(t, row_tok, axis=0)
    row_gates = jnp.where(row_valid[:, None], jnp.take(gates, row_tok, axis=0), 0.0)
    ys = moe_group_experts(xs, row_gates, tile_group, n_tiles, w_gate, w_up, w_down)
    return jnp.take(ys, pos, axis=0)


def kernel(x_prompt, x_sample, c, cache_attn_k, cache_attn_v, state_gla, state_rwkv, c_ctx, ada_w, ada_b, norm_mix, norm_ffn, norm_out, ev_w_in, ev_w_out, gla_dec_w, gla_dec_b, gla_norm, att_sink, rw_mu, rw_wr, rw_wk, rw_wv, rw_wo, rw_w0, rw_w1, rw_w2, rw_a0, rw_a1, rw_a2, rw_g1, rw_g2, rw_kk, rw_ka, rw_rk, rw_lnx_w, rw_lnx_b, moe_w_grp, moe_b_grp, moe_w_exp, moe_b_exp, moe_w_gate, moe_w_up, moe_w_down):
    n_lat = x_sample.shape[1]
    rows = n_lat // GRID_W
    row_pos = jnp.repeat(jnp.arange(rows), GRID_W)
    col_pos = jnp.tile(jnp.arange(GRID_W), rows)
    rope = rope_tables(row_pos, col_pos)
    Bc, Lc, D = x_prompt.shape
    Bl, Ll, _ = x_sample.shape

    xc, xl = x_prompt, x_sample
    cond_ctx = c_ctx[None, :]
    ks_out, vs_out, gla_out, rwkv_out = [], [], [], []
    for l in range(DEPTH):
        sc1, scl1, gc1, sc2, scl2, gc2 = ada_mod(cond_ctx, ada_w[l], ada_b[l])
        sl1, sll1, gl1, sl2, sll2, gl2 = ada_mod(c, ada_w[l], ada_b[l])
        hc = modulate(rmsnorm(xc, norm_mix[l]), sc1, scl1)
        hl = modulate(rmsnorm(xl, norm_mix[l]), sl1, sll1)
        i = l // 2
        if l % 2 == 0:
            oc, k_c, v_c, s_gla = even_mixer_ctx(hc, ev_w_in[i], ev_w_out[i], gla_dec_w[i], gla_dec_b[i],
                                                 gla_norm[i], att_sink[i])
            ol = even_mixer_lat(hl, ev_w_in[i], ev_w_out[i], gla_dec_w[i], gla_dec_b[i], gla_norm[i], att_sink[i],
                                rope, cache_attn_k[:, i], cache_attn_v[:, i], state_gla[:, i])
            ks_out.append(k_c)
            vs_out.append(v_c)
            gla_out.append(s_gla)
        else:
            rw = (rw_mu[i], rw_wr[i], rw_wk[i], rw_wv[i], rw_wo[i], rw_w0[i], rw_w1[i], rw_w2[i], rw_a0[i],
                  rw_a1[i], rw_a2[i], rw_g1[i], rw_g2[i], rw_kk[i], rw_ka[i], rw_rk[i], rw_lnx_w[i], rw_lnx_b[i])
            zero = jnp.zeros((Bc, RWKV_HEADS, RWKV_HEAD, RWKV_HEAD), f32)
            oc, s_f, s_b = rwkv_mix(hc, *rw, zero, zero)
            ol, _, _ = rwkv_mix(hl, *rw, state_rwkv[:, i, 0], state_rwkv[:, i, 1])
            rwkv_out.append(jnp.stack([s_f, s_b], axis=1))
        xc = xc + gc1 * oc
        xl = xl + gl1 * ol
        tc = modulate(rmsnorm(xc, norm_ffn[l]), sc2, scl2).reshape(Bc * Lc, D)
        tl = modulate(rmsnorm(xl, norm_ffn[l]), sl2, sll2).reshape(Bl * Ll, D)
        y = hier_moe(jnp.concatenate([tc, tl], axis=0), moe_w_grp[l], moe_b_grp[l], moe_w_exp[l], moe_b_exp[l],
                     moe_w_gate[l], moe_w_up[l], moe_w_down[l])
        xc = xc + gc2 * y[:Bc * Lc].reshape(Bc, Lc, D)
        xl = xl + gl2 * y[Bc * Lc:].reshape(Bl, Ll, D)

    y_prompt = rmsnorm(xc, norm_out)
    y_sample = rmsnorm(xl, norm_out)
    new_attn_k = jnp.stack(ks_out, axis=1)
    new_attn_v = jnp.stack(vs_out, axis=1)
    new_gla = jnp.stack(gla_out, axis=1)
    new_rwkv = jnp.stack(rwkv_out, axis=1)
    return (y_prompt, y_sample, new_attn_k, new_attn_v, new_gla, new_rwkv)
```

```python
import functools

import jax
import jax.numpy as jnp
import numpy as np
from jax import lax
from jax.experimental import pallas as pl
from jax.experimental.pallas import tpu as pltpu

D_MODEL = 1024
DEPTH = 4
GRID_W = 64
EPS = 1e-6
GLA_HEADS = 4
GLA_DK = 64
GLA_DV = 128
GLA_LOWRANK = 16
GLA_GATE_TEMP = 16.0
GLA_CHUNK = 64
ATT_HEADS = 8
ATT_KV_HEADS = 2
ATT_GROUP = ATT_HEADS // ATT_KV_HEADS
ATT_HD = 64
WINDOW = 128
ATT_BLOCK = 128
ROPE_BASE = 10000.0
ROPE_AXIS_DIMS = ATT_HD // 2
NEG = -1e30
IN_WIDTHS = (GLA_HEADS * GLA_DK, GLA_HEADS * GLA_DK, GLA_HEADS * GLA_DV, GLA_HEADS * GLA_DV,
             ATT_HEADS * ATT_HD, ATT_KV_HEADS * ATT_HD, ATT_KV_HEADS * ATT_HD, 2 * GLA_LOWRANK)
IN_SPLITS = tuple(int(v) for v in np.cumsum(IN_WIDTHS)[:-1])
RWKV_HEAD = 64
RWKV_HEADS = D_MODEL // RWKV_HEAD
RWKV_LN_EPS = 64e-5
N_GROUPS = 4
EXP_PER_GROUP = 4
N_EXPERTS = N_GROUPS * EXP_PER_GROUP
TOP_K = 2
D_EXPERT = 512

V7X_VMEM_BYTES = 64 * 1024 * 1024
MOE_TILE_M = 256

f32 = jnp.float32
bf16 = jnp.bfloat16


def rmsnorm(x, w):
    y = x * lax.rsqrt(jnp.mean(x * x, axis=-1, keepdims=True) + EPS)
    return y * w


def ada_mod(cond, w, b):
    mod = jax.nn.silu(cond) @ w + b
    return [m[:, None, :] for m in jnp.split(mod, 6, axis=-1)]


def modulate(x, shift, scale):
    return x * (1.0 + scale) + shift


NORM_ROWS = 256


def _resid_norm_mod_kernel(x_ref, o_ref, gate_ref, w_ref, shift_ref, scale_ref, xn_ref, t_ref):
    x = x_ref[...] + gate_ref[...] * o_ref[...]
    xn_ref[...] = x
    y = x * lax.rsqrt(jnp.mean(x * x, axis=-1, keepdims=True) + EPS) * w_ref[...]
    t_ref[...] = y * (1.0 + scale_ref[...]) + shift_ref[...]


def resid_norm_mod(x, o, gate, w, shift, scale):
    B, L, D = x.shape
    rows = pl.BlockSpec((None, NORM_ROWS, D), lambda b, j: (b, j, 0))

    def per_batch(a):
        return pl.BlockSpec((None, 1, D), (lambda b, j: (b, 0, 0)) if a.shape[0] > 1 else (lambda b, j: (0, 0, 0)))

    return pl.pallas_call(
        _resid_norm_mod_kernel,
        grid=(B, L // NORM_ROWS),
        in_specs=[rows, rows, per_batch(gate), pl.BlockSpec((1, D), lambda b, j: (0, 0)),
                  per_batch(shift), per_batch(scale)],
        out_specs=[rows, rows],
        out_shape=[jax.ShapeDtypeStruct((B, L, D), f32)] * 2,
        compiler_params=pltpu.CompilerParams(dimension_semantics=("arbitrary", "arbitrary")),
        name="resid_norm_mod",
    )(x, o, gate, w.reshape(1, D), shift, scale)


def rope_tables(row_pos, col_pos):
    freqs = ROPE_BASE ** (-jnp.arange(0, ROPE_AXIS_DIMS, 2, dtype=f32) / ROPE_AXIS_DIMS)
    ang_r = row_pos.astype(f32)[:, None] * freqs[None, :]
    ang_c = col_pos.astype(f32)[:, None] * freqs[None, :]
    return jnp.cos(ang_r), jnp.sin(ang_r), jnp.cos(ang_c), jnp.sin(ang_c)


def rotate(x, cos, sin):
    x1, x2 = jnp.split(x, 2, axis=-1)
    cos = cos[None, :, None, :]
    sin = sin[None, :, None, :]
    return jnp.concatenate([x1 * cos - x2 * sin, x1 * sin + x2 * cos], axis=-1)


def apply_axial_rope(x, tables):
    cr, sr, cc, sc = tables
    xr, xc = jnp.split(x, 2, axis=-1)
    return jnp.concatenate([rotate(xr, cr, sr), rotate(xc, cc, sc)], axis=-1)


GLA_QK = GLA_HEADS * GLA_DK
GLA_V = GLA_HEADS * GLA_DV


def _gla_kernel(q_ref, v_ref, gg_ref, ldf_ref, ldb_ref, kt_ref, ldft_ref, ldbt_ref, s0_ref, norm_ref,
                o_ref, s_ref, *, seq_len):
    n = seq_len // GLA_CHUNK
    C = GLA_CHUNK
    ti = lax.broadcasted_iota(jnp.int32, (C, C), 0)
    tj = lax.broadcasted_iota(jnp.int32, (C, C), 1)
    keep = (tj <= ti, tj >= ti)
    tri = tuple(m.astype(bf16) for m in keep)
    tri_t = (tri[1], tri[0])
    ones = jnp.ones((C, GLA_DV), bf16)
    s_ref[...] = s0_ref[...]
    o_ref[...] = jnp.zeros_like(o_ref)
    ld_refs = ((ldf_ref, ldft_ref), (ldb_ref, ldbt_ref))

    def split3(x):
        hi = x.astype(bf16)
        rest = x - hi.astype(f32)
        mid = rest.astype(bf16)
        return hi, mid, (rest - mid.astype(f32)).astype(bf16)

    def sums(a, b):
        if isinstance(b, tuple):
            return sum(jnp.dot(a, p, preferred_element_type=f32) for p in b)
        return sum(jnp.dot(p, b, preferred_element_type=f32) for p in a)

    def chunk(c, carry):
        pending = []
        for d in range(2):
            cc = c if d == 0 else n - 1 - c
            rows = pl.ds(pl.multiple_of(cc * C, C), C)
            ld_ref, ldt_ref = ld_refs[d]
            ld_t3 = split3(ldt_ref[cc])
            b_all = sums(tri[d], split3(ld_ref[rows, :]))
            bt_all = sums(ld_t3, tri_t[d])
            total_all = sums(ld_t3, ones)
            for h in range(GLA_HEADS):
                ks = slice(h * GLA_DK, (h + 1) * GLA_DK)
                vs = slice(h * GLA_DV, (h + 1) * GLA_DV)
                q = q_ref[rows, ks] * (GLA_DK ** -0.5)
                v = v_ref[rows, vs].astype(bf16)
                k_t = kt_ref[cc, ks, :]
                b, b_t, total = b_all[:, ks], bt_all[ks], total_all[ks]
                q_e = (q * jnp.exp(b)).astype(bf16)
                k_e = (k_t * jnp.exp(-b_t)).astype(bf16)
                k_s = (k_t * jnp.exp(total[:, :C] - b_t)).astype(bf16)
                att = jnp.where(keep[d], jnp.dot(q_e, k_e, preferred_element_type=f32), 0.0).astype(bf16)
                s = s_ref[d, h]
                o = (jnp.dot(q_e, s.astype(bf16), preferred_element_type=f32)
                     + jnp.dot(att, v, preferred_element_type=f32))
                s_new = s * jnp.exp(total) + jnp.dot(k_s, v, preferred_element_type=f32)
                pending.append((d, h, rows, vs, o_ref[rows, vs] + o, s_new))
        for d, h, rows, vs, o, s_new in pending:
            o_ref[rows, vs] = o
            s_ref[d, h] = s_new
        return carry

    lax.fori_loop(0, n, chunk, 0)
    for h in range(GLA_HEADS):
        vs = slice(h * GLA_DV, (h + 1) * GLA_DV)
        x = o_ref[:, vs]
        y = x * lax.rsqrt(jnp.mean(x * x, axis=-1, keepdims=True) + EPS) * norm_ref[...]
        g = gg_ref[:, vs]
        o_ref[:, vs] = y * (g * jax.nn.sigmoid(g))


def gla_bidir_gated(z, ld_f, ld_b, s0, gla_norm):
    B, L, _ = z.shape
    n = L // GLA_CHUNK
    assert n % 2 == 0

    def per_chunk_t(a):
        return a.reshape(B, n, GLA_CHUNK, GLA_QK).transpose(0, 1, 3, 2)

    state = pl.BlockSpec((None, 2, GLA_HEADS, GLA_DK, GLA_DV), lambda b: (b, 0, 0, 0, 0))
    ld = pl.BlockSpec((None, L, GLA_QK), lambda b: (b, 0, 0))
    ld_t = pl.BlockSpec((None, n, GLA_QK, GLA_CHUNK), lambda b: (b, 0, 0, 0))
    return pl.pallas_call(
        functools.partial(_gla_kernel, seq_len=L),
        grid=(B,),
        in_specs=[pl.BlockSpec((None, L, GLA_QK), lambda b: (b, 0, 0)),
                  pl.BlockSpec((None, L, GLA_V), lambda b: (b, 0, 1)),
                  pl.BlockSpec((None, L, GLA_V), lambda b: (b, 0, 2)),
                  ld, ld, ld_t, ld_t, ld_t, state,
                  pl.BlockSpec((1, GLA_DV), lambda b: (0, 0))],
        out_specs=[pl.BlockSpec((None, L, GLA_V), lambda b: (b, 0, 0)), state],
        out_shape=[jax.ShapeDtypeStruct((B, L, GLA_V), f32),
                   jax.ShapeDtypeStruct((B, 2, GLA_HEADS, GLA_DK, GLA_DV), f32)],
        compiler_params=pltpu.CompilerParams(
            dimension_semantics=("arbitrary",),
            vmem_limit_bytes=min(V7X_VMEM_BYTES, 2 * L * (9 * GLA_QK + 3 * GLA_V) * 4 + (16 << 20)),
        ),
        name="gla_bidir",
    )(z, z, z, ld_f, ld_b, per_chunk_t(z[..., GLA_QK:2 * GLA_QK]), per_chunk_t(ld_f), per_chunk_t(ld_b),
      s0, gla_norm.reshape(1, GLA_DV))


ATT_Q = ATT_HEADS * ATT_HD
ATT_KV = ATT_KV_HEADS * ATT_HD


def _window_attn_kernel(q_ref, k_ref, v_ref, kc_ref, vc_ref, sink_ref, o_ref, *, tq, seq_len):
    scale = ATT_HD ** -0.5
    nt = (((1,), (1,)), ((), ()))
    start = pl.multiple_of(pl.program_id(1) * tq, tq)
    k_own = k_ref[pl.ds(start, 3 * tq), :].astype(bf16)
    v_own = v_ref[pl.ds(start, 3 * tq), :].astype(bf16)
    qpos = start + lax.broadcasted_iota(jnp.int32, (tq, 3 * tq), 0)
    kpos = start - tq + lax.broadcasted_iota(jnp.int32, (tq, 3 * tq), 1)
    valid = (jnp.abs(qpos - kpos) <= WINDOW) & (kpos >= 0) & (kpos < seq_len)
    for h in range(ATT_HEADS):
        g = h // ATT_GROUP
        kv = slice(g * ATT_HD, (g + 1) * ATT_HD)
        q = q_ref[:, h * ATT_HD:(h + 1) * ATT_HD].astype(bf16)
        s_own = lax.dot_general(q, k_own[:, kv], nt, preferred_element_type=f32) * scale
        s_own = jnp.where(valid, s_own, NEG)
        s_ctx = lax.dot_general(q, kc_ref[:, kv].astype(bf16), nt, preferred_element_type=f32) * scale
        sink = sink_ref[0:1, h:h + 1]
        m = jnp.maximum(jnp.maximum(jnp.max(s_own, axis=-1, keepdims=True), jnp.max(s_ctx, axis=-1, keepdims=True)),
                        sink)
        p_own = jnp.exp(s_own - m)
        p_ctx = jnp.exp(s_ctx - m)
        denom = (jnp.sum(p_own, axis=-1, keepdims=True) + jnp.sum(p_ctx, axis=-1, keepdims=True)
                 + jnp.exp(sink - m))
        o = (jnp.dot(p_own.astype(bf16), v_own[:, kv], preferred_element_type=f32)
             + jnp.dot(p_ctx.astype(bf16), vc_ref[:, kv].astype(bf16), preferred_element_type=f32))
        o_ref[:, h * ATT_HD:(h + 1) * ATT_HD] = o / denom


def ctx_attn(q, k, v, sink):
    B, L = q.shape[:2]
    scale = ATT_HD ** -0.5
    qg = q.reshape(B, L, ATT_KV_HEADS, ATT_GROUP, ATT_HD)
    sink_b = sink.reshape(ATT_KV_HEADS, ATT_GROUP)[None, :, :, None, None]
    s = jnp.einsum('bqhgd,bkhd->bhgqk', qg, k) * scale
    s_sink = jnp.broadcast_to(sink_b, s.shape[:-1] + (1,))
    p = jax.nn.softmax(jnp.concatenate([s_sink, s], axis=-1), axis=-1)[..., 1:]
    return jnp.einsum('bhgqk,bkhd->bqhgd', p, v).reshape(B, L, ATT_HEADS * ATT_HD)


def window_attn_latent(q, k, v, k_ctx, v_ctx, sink):
    B, S, _ = q.shape
    Lc = k_ctx.shape[1]
    nb = S // ATT_BLOCK
    pad = ((0, 0), (ATT_BLOCK, ATT_BLOCK), (0, 0))
    whole = lambda n: pl.BlockSpec((None, n, ATT_KV), lambda b, i: (b, 0, 0))
    return pl.pallas_call(
        functools.partial(_window_attn_kernel, tq=ATT_BLOCK, seq_len=S),
        grid=(B, nb),
        in_specs=[pl.BlockSpec((None, ATT_BLOCK, ATT_Q), lambda b, i: (b, i, 0)),
                  whole(S + 2 * ATT_BLOCK), whole(S + 2 * ATT_BLOCK), whole(Lc), whole(Lc),
                  pl.BlockSpec((1, ATT_HEADS), lambda b, i: (0, 0))],
        out_specs=pl.BlockSpec((None, ATT_BLOCK, ATT_Q), lambda b, i: (b, i, 0)),
        out_shape=jax.ShapeDtypeStruct((B, S, ATT_Q), f32),
        compiler_params=pltpu.CompilerParams(dimension_semantics=("arbitrary", "arbitrary")),
        name="window_attn",
    )(q, jnp.pad(k, pad), jnp.pad(v, pad), k_ctx, v_ctx, sink.reshape(1, ATT_HEADS))


def even_projections(h, w_in, dec_w, dec_b):
    B, L, _ = h.shape
    z = h @ w_in
    aq, ak, av, lr = jnp.split(z[..., IN_SPLITS[3]:], [s - IN_SPLITS[3] for s in IN_SPLITS[4:]], axis=-1)
    lr_f, lr_b = jnp.split(lr, 2, axis=-1)

    def logdecay(lr_d, w, b):
        return jax.nn.log_sigmoid(lr_d @ w + b) / GLA_GATE_TEMP

    ld_f = logdecay(lr_f, dec_w[0], dec_b[0])
    ld_b = logdecay(lr_b, dec_w[1], dec_b[1])
    aq = aq.reshape(B, L, ATT_HEADS, ATT_HD)
    ak = ak.reshape(B, L, ATT_KV_HEADS, ATT_HD)
    av = av.reshape(B, L, ATT_KV_HEADS, ATT_HD)
    return z, ld_f, ld_b, aq, ak, av


def even_mixer_ctx(h, w_in, w_out, dec_w, dec_b, gla_norm, sink):
    z, ld_f, ld_b, aq, ak, av = even_projections(h, w_in, dec_w, dec_b)
    zero = jnp.zeros((h.shape[0], 2, GLA_HEADS, GLA_DK, GLA_DV), f32)
    o_gla, s_fin = gla_bidir_gated(z, ld_f, ld_b, zero, gla_norm)
    o_att = ctx_attn(aq, ak, av, sink)
    return jnp.concatenate([o_gla, o_att], axis=-1) @ w_out, ak, av, s_fin


def even_mixer_lat(h, w_in, w_out, dec_w, dec_b, gla_norm, sink, rope, k_ctx, v_ctx, s0):
    z, ld_f, ld_b, aq, ak, av = even_projections(h, w_in, dec_w, dec_b)
    o_gla, _ = gla_bidir_gated(z, ld_f, ld_b, s0, gla_norm)
    B, S = h.shape[:2]
    o_att = window_attn_latent(apply_axial_rope(aq, rope).reshape(B, S, ATT_Q),
                               apply_axial_rope(ak, rope).reshape(B, S, ATT_KV), av.reshape(B, S, ATT_KV),
                               k_ctx.reshape(B, -1, ATT_KV), v_ctx.reshape(B, -1, ATT_KV), sink)
    return jnp.concatenate([o_gla, o_att], axis=-1) @ w_out


def centred_shift(x):
    xp = jnp.pad(x, ((0, 0), (1, 1), (0, 0)))
    return 0.5 * (xp[:, :-2] + xp[:, 2:])


LANES = 128
SUBLANES = 8
RWKV_TC = 32


def _rwkv_scan_kernel(r_ref, v_ref, k_ref, zw_ref, za_ref, kk_ref, ka_ref, rk_ref, s0_ref, o_ref, bonus_ref, s_ref,
                      *, tc, nv, n_dir_groups):
    backward = pl.program_id(0) >= n_dir_groups

    @pl.when(pl.program_id(1) == 0)
    def _():
        s_ref[...] = s0_ref[...]

    def step(i, carry):
        t = jnp.where(backward, tc - 1 - i, i)
        k_t, r = k_ref[t], r_ref[t]
        kk = k_t * kk_ref[0]
        kk = kk * lax.rsqrt(jnp.sum(kk * kk, axis=0, keepdims=True) + 1e-12)
        a = jax.nn.sigmoid(za_ref[t])
        z = -zw_ref[t]
        softplus = jnp.maximum(z, 0.0) + jnp.log1p(jnp.exp(-jnp.abs(z)))
        w = jnp.exp(-jnp.exp(-softplus - 0.5))
        alpha, beta = -kk, kk * a
        kd = k_t * (1.0 + (a - 1.0) * ka_ref[0])
        bonus_ref[pl.ds(t, 1), :] = jnp.sum(r * 0.5 * kd * rk_ref[0], axis=0, keepdims=True)
        for vb in range(nv // SUBLANES):
            outs = []
            for v in range(vb * SUBLANES, (vb + 1) * SUBLANES):
                s = s_ref[v]
                sa = jnp.sum(s * alpha, axis=0, keepdims=True)
                s_new = s * w + sa * beta + v_ref[t, pl.ds(v, 1), :] * kd
                s_ref[v] = s_new
                outs.append(jnp.sum(s_new * r, axis=0, keepdims=True))
            o_ref[t, vb * SUBLANES:(vb + 1) * SUBLANES, :] = jnp.concatenate(outs, axis=0)
        return carry

    lax.fori_loop(0, tc, step, 0)


def rwkv_scan_lanes(r, v, k, zw, za, k_k, k_a, r_k, s0):
    gd, L, nv, _ = v.shape
    tc = RWKV_TC
    n = L // tc

    def time_block(g, j):
        return jnp.where(g >= gd, n - 1 - j, j)

    def blk(rows, shared):
        return pl.BlockSpec((None, tc, rows, LANES), lambda g, j: (g % gd if shared else g, time_block(g, j), 0, 0))

    const = pl.BlockSpec((None, 1, RWKV_HEAD, LANES), lambda g, j: (g % gd, 0, 0, 0))
    state = pl.BlockSpec((None, nv, RWKV_HEAD, LANES), lambda g, j: (g, 0, 0, 0))
    block_bytes = tc * RWKV_HEAD * LANES * 4
    return pl.pallas_call(
        functools.partial(_rwkv_scan_kernel, tc=tc, nv=nv, n_dir_groups=gd),
        grid=(2 * gd, n),
        in_specs=[blk(RWKV_HEAD, True), blk(nv, True), blk(RWKV_HEAD, True),
                  blk(RWKV_HEAD, False), blk(RWKV_HEAD, False), const, const, const, state],
        out_specs=[blk(nv, False), pl.BlockSpec((None, tc, LANES), lambda g, j: (g, time_block(g, j), 0)), state],
        out_shape=[jax.ShapeDtypeStruct((2 * gd, L, nv, LANES), f32),
                   jax.ShapeDtypeStruct((2 * gd, L, LANES), f32),
                   jax.ShapeDtypeStruct((2 * gd, nv, RWKV_HEAD, LANES), f32)],
        compiler_params=pltpu.CompilerParams(
            dimension_semantics=("arbitrary", "arbitrary"),
            vmem_limit_bytes=2 * 7 * block_bytes + 4 * nv * RWKV_HEAD * LANES * 4 + (8 << 20),
        ),
        name="rwkv_scan",
    )(r, v, k, zw, za, k_k, k_a, r_k, s0)


def rwkv_scan_bidir(r, v, k, zw, za, k_k, k_a, r_k, s_f0, s_b0):
    B, L, H, N = r.shape
    S = B * H
    gd = S // LANES

    def rows(x):
        T = x.shape[-3]
        lead = x.shape[:-4]
        x = x.reshape(-1, B, T, H, N).transpose(0, 2, 4, 1, 3).reshape(-1, T, N, gd, LANES)
        return x.transpose(0, 3, 1, 2, 4).reshape(-1, T, N, LANES) if lead else x[0].transpose(2, 0, 1, 3)

    def rows_inv(y):
        return y.transpose(1, 2, 0, 3).reshape(y.shape[1], N, B, H)

    def state_in(s):
        return rows(s.transpose(0, 3, 1, 2)).transpose(0, 2, 1, 3)

    def state_out(s):
        return rows_inv(s.transpose(0, 2, 1, 3)).transpose(2, 3, 1, 0)

    def const_rows(c):
        return rows(jnp.broadcast_to(c[None, None], (B, 1, H, N)))

    out, bonus, s_fin = rwkv_scan_lanes(rows(r), rows(v), rows(k), rows(zw), rows(za),
                                        const_rows(k_k), const_rows(k_a), const_rows(r_k),
                                        jnp.concatenate([state_in(s_f0), state_in(s_b0)], axis=0))
    o = rows_inv(out[:gd]) + rows_inv(out[gd:])
    bonus = (bonus[:gd] + bonus[gd:]).transpose(1, 0, 2).reshape(L, B, H).transpose(1, 0, 2)
    return o.transpose(2, 0, 3, 1), bonus, state_out(s_fin[:gd]), state_out(s_fin[gd:])


RWKV_SUB = 64
HEAD_PAIRS = D_MODEL // LANES


def _rwkv_rowsum_kernel(rf, vf, af, wf, kf, bf_, rb, vb, ab, wb, kb, bb, s0, of, ob, s, acc, *, tc):
    j = pl.program_id(1)

    @pl.when(j == 0)
    def _():
        s[...] = s0[...]

    row_head = lax.broadcasted_iota(jnp.int32, (2 * LANES, 2 * LANES), 0) // RWKV_HEAD
    col_head = lax.broadcasted_iota(jnp.int32, (2 * LANES, 2 * LANES), 1) // RWKV_HEAD
    ones_bd = (row_head == col_head).astype(bf16)
    sub = lax.broadcasted_iota(jnp.int32, (RWKV_HEAD, LANES), 0)
    lane_in_head = lax.broadcasted_iota(jnp.int32, (RWKV_HEAD, LANES), 1) % RWKV_HEAD
    eye = (sub == lane_in_head).astype(bf16)
    dirs = ((rf, vf, af, wf, kf, bf_), (rb, vb, ab, wb, kb, bb))

    def row_sums(x):
        return jnp.dot(x, ones_bd, preferred_element_type=f32)

    def row(ref, t, hp, reps):
        return jnp.broadcast_to(ref[t, pl.ds(hp, 1), :], (reps, LANES))

    def sub_chunk(sc, carry):
        def step(tt, carry):
            t_f = sc * RWKV_SUB + tt
            pos = ((t_f, jnp.maximum(t_f - 1, 0), tt - 1),
                   (tc - 1 - t_f, jnp.minimum(tc - t_f, tc - 1), RWKV_SUB - tt))
            for d in range(2):
                r_ref, v_ref, a_ref, w_ref, k_ref, b_ref = dirs[d]
                t_now, t_prev, out_lane = pos[d]
                lhs, tiles = [], []
                for hp2 in range(0, HEAD_PAIRS, 2):
                    v_diag = []
                    for hp in (hp2, hp2 + 1):
                        s_t = s[d, :, hp * LANES:(hp + 1) * LANES]
                        lhs.append(jnp.concatenate([(s_t * row(a_ref, t_now, hp, RWKV_HEAD)).astype(bf16),
                                                    (s_t * row(r_ref, t_prev, hp, RWKV_HEAD)).astype(bf16)], axis=1))
                        v_diag.append(eye * jnp.concatenate([row(v_ref, t_now, hp, 16).astype(bf16)] * 4, axis=0))
                        tiles.append(s_t)
                    lhs.append(jnp.concatenate(v_diag, axis=1))
                res = row_sums(jnp.concatenate(lhs, axis=0))
                for hp, s_t in enumerate(tiles):
                    ls = slice(hp * LANES, (hp + 1) * LANES)
                    base = (hp // 2) * 3 * RWKV_HEAD
                    own = res[base + (hp % 2) * RWKV_HEAD:base + (hp % 2 + 1) * RWKV_HEAD]
                    sa, out_prev = own[:, :LANES], own[:, LANES:]
                    v_col = res[base + 2 * RWKV_HEAD:base + 3 * RWKV_HEAD, (hp % 2) * LANES:(hp % 2 + 1) * LANES]
                    s[d, :, ls] = (s_t * row(w_ref, t_now, hp, RWKV_HEAD) + sa * row(b_ref, t_now, hp, RWKV_HEAD)
                                   + v_col * row(k_ref, t_now, hp, RWKV_HEAD))
                    acc[d, :, ls] = jnp.where(lane_in_head == out_lane, out_prev, acc[d, :, ls])
            return carry

        lax.fori_loop(0, RWKV_SUB, step, 0, unroll=8)
        t_last = sc * RWKV_SUB + RWKV_SUB - 1
        last = ((t_last, RWKV_SUB - 1), (tc - 1 - t_last, 0))
        for d in range(2):
            r_ref = dirs[d][0]
            t_now, out_lane = last[d]
            out_mask = jnp.concatenate([lane_in_head == out_lane] * 2, axis=1)
            for hp in range(0, HEAD_PAIRS, 2):
                ls = slice(hp * LANES, (hp + 2) * LANES)
                r_t = jnp.concatenate([row(r_ref, t_now, hp, RWKV_HEAD), row(r_ref, t_now, hp + 1, RWKV_HEAD)], axis=1)
                out_t = row_sums((s[d, :, ls] * r_t).astype(bf16))
                acc[d, :, ls] = jnp.where(out_mask, out_t, acc[d, :, ls])
        of[sc] = acc[0]
        ob[tc // RWKV_SUB - 1 - sc] = acc[1]
        return carry

    lax.fori_loop(0, tc // RWKV_SUB, sub_chunk, 0)


def rwkv_scan_rowsum(r, v, nkk, fwd, bwd, s_f0, s_b0):
    B, L, H, N = r.shape
    D = H * N
    tc = min(L, 256)
    n = L // tc
    fwd_blk = pl.BlockSpec((None, tc, HEAD_PAIRS, LANES), lambda b, j: (b, j, 0, 0))
    bwd_blk = pl.BlockSpec((None, tc, HEAD_PAIRS, LANES), lambda b, j: (b, n - 1 - j, 0, 0))
    r, v, nkk, w_f, k_f, b_f, w_b, k_b, b_b = (t.reshape(B, L, HEAD_PAIRS, LANES) for t in (r, v, nkk, *fwd, *bwd))
    state = pl.BlockSpec((None, 2, RWKV_HEAD, D), lambda b, j: (b, 0, 0, 0))
    nsub = tc // RWKV_SUB
    block_bytes = tc * D * 4
    s0 = jnp.stack([s.transpose(0, 2, 1, 3).reshape(B, N, D) for s in (s_f0, s_b0)], axis=1)
    o_f, o_b, s_fin = pl.pallas_call(
        functools.partial(_rwkv_rowsum_kernel, tc=tc),
        grid=(B, n),
        in_specs=[fwd_blk] * 6 + [bwd_blk] * 6 + [state],
        out_specs=[pl.BlockSpec((None, nsub, RWKV_HEAD, D), lambda b, j: (b, j, 0, 0)),
                   pl.BlockSpec((None, nsub, RWKV_HEAD, D), lambda b, j: (b, n - 1 - j, 0, 0)),
                   state],
        out_shape=[jax.ShapeDtypeStruct((B, L // RWKV_SUB, RWKV_HEAD, D), f32),
                   jax.ShapeDtypeStruct((B, L // RWKV_SUB, RWKV_HEAD, D), f32),
                   jax.ShapeDtypeStruct((B, 2, RWKV_HEAD, D), f32)],
        scratch_shapes=[pltpu.VMEM((2, RWKV_HEAD, D), f32)],
        compiler_params=pltpu.CompilerParams(
            dimension_semantics=("arbitrary", "arbitrary"),
            vmem_limit_bytes=min(V7X_VMEM_BYTES, 2 * 14 * block_bytes + (8 << 20)),
        ),
        name="rwkv_scan_rowsum",
    )(r, v, nkk, w_f, k_f, b_f, r, v, nkk, w_b, k_b, b_b, s0)
    o = (o_f + o_b).reshape(B, L // RWKV_SUB, RWKV_HEAD, H, RWKV_SUB).transpose(0, 1, 4, 3, 2).reshape(B, L, H, N)
    s_fin = s_fin.reshape(B, 2, N, H, N).transpose(0, 1, 3, 2, 4)
    return o, s_fin[:, 0], s_fin[:, 1]


def rwkv_mix(h, mu, wr, wk, wv, wo, w0, w1, w2, a0, a1, a2, g1, g2, k_k, k_a, r_k, lnx_w, lnx_b, s_f0, s_b0):
    B, L, D = h.shape
    H, N = RWKV_HEADS, RWKV_HEAD
    xx = centred_shift(h) - h
    xr, xw, xk, xv, xa, xg = [h + xx * mu[j] for j in range(6)]
    r = (xr @ wr).reshape(B, L, H, N)
    k = (xk @ wk).reshape(B, L, H, N)
    v = (xv @ wv).reshape(B, L, H, N)
    g = jax.nn.sigmoid(xg @ g1) @ g2
    zw = [(w0[d] + jnp.tanh(xw @ w1[d]) @ w2[d]).reshape(B, L, H, N) for d in range(2)]
    za = [(a0[d] + (xa @ a1[d]) @ a2[d]).reshape(B, L, H, N) for d in range(2)]

    if B * H >= LANES:
        o, bonus, s_f, s_b = rwkv_scan_bidir(r, v, k, jnp.stack(zw), jnp.stack(za), k_k.reshape(H, N),
                                             k_a.reshape(H, N), r_k, s_f0, s_b0)
        bonus = bonus[..., None]
    else:
        kk = k * k_k.reshape(H, N)
        kk = kk * lax.rsqrt(jnp.sum(kk * kk, axis=-1, keepdims=True) + 1e-12)

        def direction(d):
            decay = jnp.exp(-jnp.exp(-jax.nn.softplus(-zw[d]) - 0.5))
            a = jax.nn.sigmoid(za[d])
            return decay, k * (1.0 + (a - 1.0) * k_a.reshape(H, N)), kk * a

        fwd, bwd = direction(0), direction(1)
        o, s_f, s_b = rwkv_scan_rowsum(r, v, -kk, fwd, bwd, s_f0, s_b0)
        bonus = jnp.sum(r * 0.5 * (fwd[1] + bwd[1]) * r_k, axis=-1, keepdims=True)
    m = jnp.mean(o, axis=-1, keepdims=True)
    var = jnp.mean(jnp.square(o - m), axis=-1, keepdims=True)
    o = (o - m) * lax.rsqrt(var + RWKV_LN_EPS) * lnx_w.reshape(H, N) + lnx_b.reshape(H, N)
    o = (o + bonus * v).reshape(B, L, D)
    return (o * g) @ wo, s_f, s_b


GROUP_HID = EXP_PER_GROUP * D_EXPERT


def _moe_group_kernel(tile_group_ref, n_tiles_ref, xs_ref, gates_ref, wg_ref, wu_ref, wd_ref, ys_ref,
                      wg_bf, wu_bf, wd_bf):
    i = pl.program_id(0)
    prev_group = tile_group_ref[jnp.maximum(i - 1, 0)]
    group_changed = jnp.logical_or(i == 0, tile_group_ref[i] != prev_group)

    @pl.when(group_changed)
    def _():
        for e in range(EXP_PER_GROUP):
            hs = slice(e * D_EXPERT, (e + 1) * D_EXPERT)
            wg_bf[:, hs] = wg_ref[e].astype(bf16)
            wu_bf[:, hs] = wu_ref[e].astype(bf16)
            wd_bf[hs, :] = wd_ref[e].astype(bf16)

    @pl.when(i < n_tiles_ref[0])
    def _():
        x = xs_ref[...].astype(bf16)
        g = jnp.dot(x, wg_bf[...], preferred_element_type=f32)
        u = jnp.dot(x, wu_bf[...], preferred_element_type=f32)
        gate = jnp.concatenate([jnp.broadcast_to(gates_ref[:, e:e + 1], (MOE_TILE_M, D_EXPERT))
                                for e in range(EXP_PER_GROUP)], axis=1)
        hid = (g * jax.nn.sigmoid(g)) * u * gate
        ys_ref[...] = jnp.dot(hid.astype(bf16), wd_bf[...], preferred_element_type=f32)

    @pl.when(i >= n_tiles_ref[0])
    def _():
        ys_ref[...] = jnp.zeros_like(ys_ref)


def moe_group_experts(xs, gates, tile_group, n_tiles, w_gate, w_up, w_down):
    P, D = xs.shape
    max_tiles = P // MOE_TILE_M
    weight_bytes = 3 * EXP_PER_GROUP * D * D_EXPERT * (4 + 2)
    tile_bytes = 2 * MOE_TILE_M * (2 * D * 4 + LANES * 4) + 4 * MOE_TILE_M * GROUP_HID * 4
    once = pl.Buffered(1)
    grid_spec = pltpu.PrefetchScalarGridSpec(
        num_scalar_prefetch=2,
        grid=(max_tiles,),
        in_specs=[
            pl.BlockSpec((MOE_TILE_M, D), lambda i, tg, nt: (i, 0)),
            pl.BlockSpec((MOE_TILE_M, EXP_PER_GROUP), lambda i, tg, nt: (i, 0)),
            pl.BlockSpec((EXP_PER_GROUP, D, D_EXPERT), lambda i, tg, nt: (tg[i], 0, 0), pipeline_mode=once),
            pl.BlockSpec((EXP_PER_GROUP, D, D_EXPERT), lambda i, tg, nt: (tg[i], 0, 0), pipeline_mode=once),
            pl.BlockSpec((EXP_PER_GROUP, D_EXPERT, D), lambda i, tg, nt: (tg[i], 0, 0), pipeline_mode=once),
        ],
        out_specs=pl.BlockSpec((MOE_TILE_M, D), lambda i, tg, nt: (i, 0)),
        scratch_shapes=[
            pltpu.VMEM((D, GROUP_HID), bf16),
            pltpu.VMEM((D, GROUP_HID), bf16),
            pltpu.VMEM((GROUP_HID, D), bf16),
        ],
    )
    return pl.pallas_call(
        _moe_group_kernel,
        grid_spec=grid_spec,
        out_shape=jax.ShapeDtypeStruct((P, D), f32),
        compiler_params=pltpu.CompilerParams(
            dimension_semantics=("arbitrary",),
            vmem_limit_bytes=min(V7X_VMEM_BYTES - (4 << 20), weight_bytes + tile_bytes + (8 << 20)),
        ),
        name="moe_group_experts",
    )(tile_group, n_tiles, xs, gates, w_gate, w_up, w_down)


def hier_moe(t, w_grp, b_grp, w_exp, b_exp, w_gate, w_up, w_down):
    T, D = t.shape
    logits = jnp.dot(t, jnp.concatenate([w_grp, w_exp], axis=1), precision=lax.Precision.HIGHEST)
    grp_logits = logits[:, :N_GROUPS] + b_grp
    grp_prob = jax.nn.softmax(grp_logits, axis=-1)
    g_top = jnp.argmax(grp_logits, axis=-1).astype(jnp.int32)
    in_group = g_top[:, None] == jnp.arange(N_GROUPS, dtype=jnp.int32)[None, :]
    p_g = jnp.sum(jnp.where(in_group, grp_prob, 0.0), axis=1, keepdims=True)
    exp_logits = (logits[:, N_GROUPS:] + b_exp).reshape(-1, N_GROUPS, EXP_PER_GROUP)
    sel = jnp.sum(jnp.where(in_group[:, :, None], exp_logits, 0.0), axis=1)
    top_v, top_i = lax.top_k(sel, TOP_K)
    wts = p_g * jax.nn.softmax(top_v, axis=-1)
    gates = jnp.sum(jax.nn.one_hot(top_i, EXP_PER_GROUP, dtype=f32) * wts[..., None], axis=1)

    order = jnp.argsort(g_top, stable=True).astype(jnp.int32)
    grp_i32 = in_group.astype(jnp.int32)
    counts = jnp.sum(grp_i32, axis=0)
    rank = jnp.sum(jnp.where(in_group, jnp.cumsum(grp_i32, axis=0), 0), axis=1) - 1
    tiles_per = (counts + MOE_TILE_M - 1) // MOE_TILE_M
    tile_end = jnp.cumsum(tiles_per)
    start_padded = (tile_end - tiles_per) * MOE_TILE_M
    start_sorted = jnp.cumsum(counts) - counts
    max_tiles = T // MOE_TILE_M + N_GROUPS
    n_tiles = tile_end[-1:].astype(jnp.int32)
    tile_ids = jnp.arange(max_tiles, dtype=jnp.int32)
    tile_group = jnp.minimum(jnp.sum((tile_ids[:, None] >= tile_end[None, :]).astype(jnp.int32), axis=1),
                             N_GROUPS - 1).astype(jnp.int32)
    tile_group = jnp.where(tile_ids < n_tiles[0], tile_group, tile_group[jnp.maximum(n_tiles[0] - 1, 0)])
    row_group = jnp.repeat(tile_group, MOE_TILE_M)
    row_in_group = jnp.arange(max_tiles * MOE_TILE_M, dtype=jnp.int32) - start_padded[row_group]
    row_valid = (row_in_group >= 0) & (row_in_group < counts[row_group])
    row_tok = jnp.where(row_valid, order[jnp.clip(start_sorted[row_group] + row_in_group, 0, T - 1)], 0)
    pos = start_padded[g_top] + rank

    xs = jnp.take(t, row_tok, axis=0)
    row_gates = jnp.where(row_valid[:, None], jnp.take(gates, row_tok, axis=0), 0.0)
    ys = moe_group_experts(xs, row_gates, tile_group, n_tiles, w_gate, w_up, w_down)
    return jnp.take(ys, pos, axis=0)


def kernel(x_prompt, x_sample, c, cache_attn_k, cache_attn_v, state_gla, state_rwkv, c_ctx, ada_w, ada_b, norm_mix, norm_ffn, norm_out, ev_w_in, ev_w_out, gla_dec_w, gla_dec_b, gla_norm, att_sink, rw_mu, rw_wr, rw_wk, rw_wv, rw_wo, rw_w0, rw_w1, rw_w2, rw_a0, rw_a1, rw_a2, rw_g1, rw_g2, rw_kk, rw_ka, rw_rk, rw_lnx_w, rw_lnx_b, moe_w_grp, moe_b_grp, moe_w_exp, moe_b_exp, moe_w_gate, moe_w_up, moe_w_down):
    n_lat = x_sample.shape[1]
    rows = n_lat // GRID_W
    row_pos = jnp.repeat(jnp.arange(rows), GRID_W)
    col_pos = jnp.tile(jnp.arange(GRID_W), rows)
    rope = rope_tables(row_pos, col_pos)
    Bc, Lc, D = x_prompt.shape
    Bl, Ll, _ = x_sample.shape

    xc, xl = x_prompt, x_sample
    cond_ctx = c_ctx[None, :]
    mods_c = [ada_mod(cond_ctx, ada_w[l], ada_b[l]) for l in range(DEPTH)]
    mods_l = [ada_mod(c, ada_w[l], ada_b[l]) for l in range(DEPTH)]
    no_mod = jnp.zeros((1, 1, D), f32)
    hc = modulate(rmsnorm(xc, norm_mix[0]), mods_c[0][0], mods_c[0][1])
    hl = modulate(rmsnorm(xl, norm_mix[0]), mods_l[0][0], mods_l[0][1])
    ks_out, vs_out, gla_out, rwkv_out = [], [], [], []
    for l in range(DEPTH):
        _, _, gc1, sc2, scl2, gc2 = mods_c[l]
        _, _, gl1, sl2, sll2, gl2 = mods_l[l]
        i = l // 2
        if l % 2 == 0:
            oc, k_c, v_c, s_gla = even_mixer_ctx(hc, ev_w_in[i], ev_w_out[i], gla_dec_w[i], gla_dec_b[i],
                                                 gla_norm[i], att_sink[i])
            ol = even_mixer_lat(hl, ev_w_in[i], ev_w_out[i], gla_dec_w[i], gla_dec_b[i], gla_norm[i], att_sink[i],
                                rope, cache_attn_k[:, i], cache_attn_v[:, i], state_gla[:, i])
            ks_out.append(k_c)
            vs_out.append(v_c)
            gla_out.append(s_gla)
        else:
            rw = (rw_mu[i], rw_wr[i], rw_wk[i], rw_wv[i], rw_wo[i], rw_w0[i], rw_w1[i], rw_w2[i], rw_a0[i],
                  rw_a1[i], rw_a2[i], rw_g1[i], rw_g2[i], rw_kk[i], rw_ka[i], rw_rk[i], rw_lnx_w[i], rw_lnx_b[i])
            zero = jnp.zeros((Bc, RWKV_HEADS, RWKV_HEAD, RWKV_HEAD), f32)
            oc, s_f, s_b = rwkv_mix(hc, *rw, zero, zero)
            ol, _, _ = rwkv_mix(hl, *rw, state_rwkv[:, i, 0], state_rwkv[:, i, 1])
            rwkv_out.append(jnp.stack([s_f, s_b], axis=1))
        xc, tc = resid_norm_mod(xc, oc, gc1, norm_ffn[l], sc2, scl2)
        xl, tl = resid_norm_mod(xl, ol, gl1, norm_ffn[l], sl2, sll2)
        y = hier_moe(jnp.concatenate([tc.reshape(Bc * Lc, D), tl.reshape(Bl * Ll, D)], axis=0),
                     moe_w_grp[l], moe_b_grp[l], moe_w_exp[l], moe_b_exp[l],
                     moe_w_gate[l], moe_w_up[l], moe_w_down[l])
        yc, yl = y[:Bc * Lc].reshape(Bc, Lc, D), y[Bc * Lc:].reshape(Bl, Ll, D)
        if l + 1 < DEPTH:
            xc, hc = resid_norm_mod(xc, yc, gc2, norm_mix[l + 1], mods_c[l + 1][0], mods_c[l + 1][1])
            xl, hl = resid_norm_mod(xl, yl, gl2, norm_mix[l + 1], mods_l[l + 1][0], mods_l[l + 1][1])
        else:
            xc, y_prompt = resid_norm_mod(xc, yc, gc2, norm_out, no_mod, no_mod)
            xl, y_sample = resid_norm_mod(xl, yl, gl2, norm_out, no_mod, no_mod)

    new_attn_k = jnp.stack(ks_out, axis=1)
    new_attn_v = jnp.stack(vs_out, axis=1)
    new_gla = jnp.stack(gla_out, axis=1)
    new_rwkv = jnp.stack(rwkv_out, axis=1)
    return (y_prompt, y_sample, new_attn_k, new_attn_v, new_gla, new_rwkv)
```

```python
import functools

import jax
import jax.numpy as jnp
import numpy as np
from jax import lax
from jax.experimental import pallas as pl
from jax.experimental.pallas import tpu as pltpu

D_MODEL = 1024
DEPTH = 4
GRID_W = 64
EPS = 1e-6
GLA_HEADS = 4
GLA_DK = 64
GLA_DV = 128
GLA_LOWRANK = 16
GLA_GATE_TEMP = 16.0
GLA_CHUNK = 64
ATT_HEADS = 8
ATT_KV_HEADS = 2
ATT_GROUP = ATT_HEADS // ATT_KV_HEADS
ATT_HD = 64
WINDOW = 128
ATT_BLOCK = 128
ROPE_BASE = 10000.0
ROPE_AXIS_DIMS = ATT_HD // 2
NEG = -1e30
IN_WIDTHS = (GLA_HEADS * GLA_DK, GLA_HEADS * GLA_DK, GLA_HEADS * GLA_DV, GLA_HEADS * GLA_DV,
             ATT_HEADS * ATT_HD, ATT_KV_HEADS * ATT_HD, ATT_KV_HEADS * ATT_HD, 2 * GLA_LOWRANK)
IN_SPLITS = tuple(int(v) for v in np.cumsum(IN_WIDTHS)[:-1])
RWKV_HEAD = 64
RWKV_HEADS = D_MODEL // RWKV_HEAD
RWKV_LN_EPS = 64e-5
N_GROUPS = 4
EXP_PER_GROUP = 4
N_EXPERTS = N_GROUPS * EXP_PER_GROUP
TOP_K = 2
D_EXPERT = 512

V7X_VMEM_BYTES = 64 * 1024 * 1024
MOE_TILE_M = 256

f32 = jnp.float32
bf16 = jnp.bfloat16


def rmsnorm(x, w):
    y = x * lax.rsqrt(jnp.mean(x * x, axis=-1, keepdims=True) + EPS)
    return y * w


def ada_mod(cond, w, b):
    mod = jax.nn.silu(cond) @ w + b
    return [m[:, None, :] for m in jnp.split(mod, 6, axis=-1)]


def modulate(x, shift, scale):
    return x * (1.0 + scale) + shift


NORM_ROWS = 256


def _resid_norm_mod_kernel(*refs, n_ctx_blocks, split):
    if split:
        x_ref, oc_ref, ol_ref, gate_ref, w_ref, shift_ref, scale_ref, xn_ref, t_ref = refs
        o = jnp.where(pl.program_id(0) < n_ctx_blocks, oc_ref[...], ol_ref[...])
    else:
        x_ref, o_ref, gate_ref, w_ref, shift_ref, scale_ref, xn_ref, t_ref = refs
        o = o_ref[...]
    x = x_ref[...] + gate_ref[0] * o
    xn_ref[...] = x
    y = x * lax.rsqrt(jnp.mean(x * x, axis=-1, keepdims=True) + EPS) * w_ref[...]
    t_ref[...] = y * (1.0 + scale_ref[0]) + shift_ref[0]


def resid_norm_mod(x, o, n_ctx, lat_len, gate, w, shift, scale):
    T, D = x.shape
    ncb = n_ctx // NORM_ROWS
    per_lat = lat_len // NORM_ROWS
    rows = pl.BlockSpec((NORM_ROWS, D), lambda i: (i, 0))
    mod = pl.BlockSpec((1, 1, D), lambda i: (jnp.where(i < ncb, 0, 1 + (i - ncb) // per_lat), 0, 0))
    split = isinstance(o, tuple)
    if split:
        o_specs = [pl.BlockSpec((NORM_ROWS, D), lambda i: (jnp.minimum(i, ncb - 1), 0)),
                   pl.BlockSpec((NORM_ROWS, D), lambda i: (jnp.maximum(i - ncb, 0), 0))]
        o_args = list(o)
    else:
        o_specs, o_args = [rows], [o]
    return pl.pallas_call(
        functools.partial(_resid_norm_mod_kernel, n_ctx_blocks=ncb, split=split),
        grid=(T // NORM_ROWS,),
        in_specs=[rows, *o_specs, mod, pl.BlockSpec((1, D), lambda i: (0, 0)), mod, mod],
        out_specs=[rows, rows],
        out_shape=[jax.ShapeDtypeStruct((T, D), f32)] * 2,
        compiler_params=pltpu.CompilerParams(dimension_semantics=("arbitrary",)),
        name="resid_norm_mod",
    )(x, *o_args, gate, w.reshape(1, D), shift, scale)


def rope_tables(row_pos, col_pos):
    freqs = ROPE_BASE ** (-jnp.arange(0, ROPE_AXIS_DIMS, 2, dtype=f32) / ROPE_AXIS_DIMS)
    ang_r = row_pos.astype(f32)[:, None] * freqs[None, :]
    ang_c = col_pos.astype(f32)[:, None] * freqs[None, :]
    return jnp.cos(ang_r), jnp.sin(ang_r), jnp.cos(ang_c), jnp.sin(ang_c)


def rotate(x, cos, sin):
    x1, x2 = jnp.split(x, 2, axis=-1)
    cos = cos[None, :, None, :]
    sin = sin[None, :, None, :]
    return jnp.concatenate([x1 * cos - x2 * sin, x1 * sin + x2 * cos], axis=-1)


def apply_axial_rope(x, tables):
    cr, sr, cc, sc = tables
    xr, xc = jnp.split(x, 2, axis=-1)
    return jnp.concatenate([rotate(xr, cr, sr), rotate(xc, cc, sc)], axis=-1)


GLA_QK = GLA_HEADS * GLA_DK
GLA_V = GLA_HEADS * GLA_DV


def _gla_kernel(q_ref, v_ref, gg_ref, ldf_ref, ldb_ref, kt_ref, ldft_ref, ldbt_ref, s0_ref, norm_ref,
                o_ref, s_ref, *, seq_len):
    n = seq_len // GLA_CHUNK
    C = GLA_CHUNK
    ti = lax.broadcasted_iota(jnp.int32, (C, C), 0)
    tj = lax.broadcasted_iota(jnp.int32, (C, C), 1)
    keep = (tj <= ti, tj >= ti)
    tri = tuple(m.astype(bf16) for m in keep)
    tri_t = (tri[1], tri[0])
    ones = jnp.ones((C, GLA_DV), bf16)
    s_ref[...] = s0_ref[...]
    o_ref[...] = jnp.zeros_like(o_ref)
    ld_refs = ((ldf_ref, ldft_ref), (ldb_ref, ldbt_ref))

    def split3(x):
        hi = x.astype(bf16)
        rest = x - hi.astype(f32)
        mid = rest.astype(bf16)
        return hi, mid, (rest - mid.astype(f32)).astype(bf16)

    def sums(a, b):
        if isinstance(b, tuple):
            return sum(jnp.dot(a, p, preferred_element_type=f32) for p in b)
        return sum(jnp.dot(p, b, preferred_element_type=f32) for p in a)

    def chunk(c, carry):
        pending = []
        for d in range(2):
            cc = c if d == 0 else n - 1 - c
            rows = pl.ds(pl.multiple_of(cc * C, C), C)
            ld_ref, ldt_ref = ld_refs[d]
            ld_t3 = split3(ldt_ref[cc])
            b_all = sums(tri[d], split3(ld_ref[rows, :]))
            bt_all = sums(ld_t3, tri_t[d])
            total_all = sums(ld_t3, ones)
            for h in range(GLA_HEADS):
                ks = slice(h * GLA_DK, (h + 1) * GLA_DK)
                vs = slice(h * GLA_DV, (h + 1) * GLA_DV)
                q = q_ref[rows, ks] * (GLA_DK ** -0.5)
                v = v_ref[rows, vs].astype(bf16)
                k_t = kt_ref[cc, ks, :]
                b, b_t, total = b_all[:, ks], bt_all[ks], total_all[ks]
                q_e = (q * jnp.exp(b)).astype(bf16)
                k_e = (k_t * jnp.exp(-b_t)).astype(bf16)
                k_s = (k_t * jnp.exp(total[:, :C] - b_t)).astype(bf16)
                att = jnp.where(keep[d], jnp.dot(q_e, k_e, preferred_element_type=f32), 0.0).astype(bf16)
                s = s_ref[d, h]
                o = (jnp.dot(q_e, s.astype(bf16), preferred_element_type=f32)
                     + jnp.dot(att, v, preferred_element_type=f32))
                s_new = s * jnp.exp(total) + jnp.dot(k_s, v, preferred_element_type=f32)
                pending.append((d, h, rows, vs, o_ref[rows, vs] + o, s_new))
        for d, h, rows, vs, o, s_new in pending:
            o_ref[rows, vs] = o
            s_ref[d, h] = s_new
        return carry

    lax.fori_loop(0, n, chunk, 0)
    for h in range(GLA_HEADS):
        vs = slice(h * GLA_DV, (h + 1) * GLA_DV)
        x = o_ref[:, vs]
        y = x * lax.rsqrt(jnp.mean(x * x, axis=-1, keepdims=True) + EPS) * norm_ref[...]
        g = gg_ref[:, vs]
        o_ref[:, vs] = y * (g * jax.nn.sigmoid(g))


def gla_bidir_gated(z, ld_f, ld_b, s0, gla_norm):
    B, L, _ = z.shape
    n = L // GLA_CHUNK
    assert n % 2 == 0

    def per_chunk_t(a):
        return a.reshape(B, n, GLA_CHUNK, GLA_QK).transpose(0, 1, 3, 2)

    state = pl.BlockSpec((None, 2, GLA_HEADS, GLA_DK, GLA_DV), lambda b: (b, 0, 0, 0, 0))
    ld = pl.BlockSpec((None, L, GLA_QK), lambda b: (b, 0, 0))
    ld_t = pl.BlockSpec((None, n, GLA_QK, GLA_CHUNK), lambda b: (b, 0, 0, 0))
    return pl.pallas_call(
        functools.partial(_gla_kernel, seq_len=L),
        grid=(B,),
        in_specs=[pl.BlockSpec((None, L, GLA_QK), lambda b: (b, 0, 0)),
                  pl.BlockSpec((None, L, GLA_V), lambda b: (b, 0, 1)),
                  pl.BlockSpec((None, L, GLA_V), lambda b: (b, 0, 2)),
                  ld, ld, ld_t, ld_t, ld_t, state,
                  pl.BlockSpec((1, GLA_DV), lambda b: (0, 0))],
        out_specs=[pl.BlockSpec((None, L, GLA_V), lambda b: (b, 0, 0)), state],
        out_shape=[jax.ShapeDtypeStruct((B, L, GLA_V), f32),
                   jax.ShapeDtypeStruct((B, 2, GLA_HEADS, GLA_DK, GLA_DV), f32)],
        compiler_params=pltpu.CompilerParams(
            dimension_semantics=("arbitrary",),
            vmem_limit_bytes=min(V7X_VMEM_BYTES, 2 * L * (9 * GLA_QK + 3 * GLA_V) * 4 + (16 << 20)),
        ),
        name="gla_bidir",
    )(z, z, z, ld_f, ld_b, per_chunk_t(z[..., GLA_QK:2 * GLA_QK]), per_chunk_t(ld_f), per_chunk_t(ld_b),
      s0, gla_norm.reshape(1, GLA_DV))


ATT_Q = ATT_HEADS * ATT_HD
ATT_KV = ATT_KV_HEADS * ATT_HD


def _window_attn_kernel(q_ref, k_ref, v_ref, kc_ref, vc_ref, sink_ref, o_ref, *, tq, seq_len):
    scale = ATT_HD ** -0.5
    nt = (((1,), (1,)), ((), ()))
    start = pl.multiple_of(pl.program_id(1) * tq, tq)
    k_own = k_ref[pl.ds(start, 3 * tq), :].astype(bf16)
    v_own = v_ref[pl.ds(start, 3 * tq), :].astype(bf16)
    qpos = start + lax.broadcasted_iota(jnp.int32, (tq, 3 * tq), 0)
    kpos = start - tq + lax.broadcasted_iota(jnp.int32, (tq, 3 * tq), 1)
    valid = (jnp.abs(qpos - kpos) <= WINDOW) & (kpos >= 0) & (kpos < seq_len)
    for h in range(ATT_HEADS):
        g = h // ATT_GROUP
        kv = slice(g * ATT_HD, (g + 1) * ATT_HD)
        q = q_ref[:, h * ATT_HD:(h + 1) * ATT_HD].astype(bf16)
        s_own = lax.dot_general(q, k_own[:, kv], nt, preferred_element_type=f32) * scale
        s_own = jnp.where(valid, s_own, NEG)
        s_ctx = lax.dot_general(q, kc_ref[:, kv].astype(bf16), nt, preferred_element_type=f32) * scale
        sink = sink_ref[0:1, h:h + 1]
        m = jnp.maximum(jnp.maximum(jnp.max(s_own, axis=-1, keepdims=True), jnp.max(s_ctx, axis=-1, keepdims=True)),
                        sink)
        p_own = jnp.exp(s_own - m)
        p_ctx = jnp.exp(s_ctx - m)
        denom = (jnp.sum(p_own, axis=-1, keepdims=True) + jnp.sum(p_ctx, axis=-1, keepdims=True)
                 + jnp.exp(sink - m))
        o = (jnp.dot(p_own.astype(bf16), v_own[:, kv], preferred_element_type=f32)
             + jnp.dot(p_ctx.astype(bf16), vc_ref[:, kv].astype(bf16), preferred_element_type=f32))
        o_ref[:, h * ATT_HD:(h + 1) * ATT_HD] = o / denom


def ctx_attn(q, k, v, sink):
    B, L = q.shape[:2]
    scale = ATT_HD ** -0.5
    qg = q.reshape(B, L, ATT_KV_HEADS, ATT_GROUP, ATT_HD)
    sink_b = sink.reshape(ATT_KV_HEADS, ATT_GROUP)[None, :, :, None, None]
    s = jnp.einsum('bqhgd,bkhd->bhgqk', qg, k) * scale
    s_sink = jnp.broadcast_to(sink_b, s.shape[:-1] + (1,))
    p = jax.nn.softmax(jnp.concatenate([s_sink, s], axis=-1), axis=-1)[..., 1:]
    return jnp.einsum('bhgqk,bkhd->bqhgd', p, v).reshape(B, L, ATT_HEADS * ATT_HD)


def window_attn_latent(q, k, v, k_ctx, v_ctx, sink):
    B, S, _ = q.shape
    Lc = k_ctx.shape[1]
    nb = S // ATT_BLOCK
    pad = ((0, 0), (ATT_BLOCK, ATT_BLOCK), (0, 0))
    whole = lambda n: pl.BlockSpec((None, n, ATT_KV), lambda b, i: (b, 0, 0))
    return pl.pallas_call(
        functools.partial(_window_attn_kernel, tq=ATT_BLOCK, seq_len=S),
        grid=(B, nb),
        in_specs=[pl.BlockSpec((None, ATT_BLOCK, ATT_Q), lambda b, i: (b, i, 0)),
                  whole(S + 2 * ATT_BLOCK), whole(S + 2 * ATT_BLOCK), whole(Lc), whole(Lc),
                  pl.BlockSpec((1, ATT_HEADS), lambda b, i: (0, 0))],
        out_specs=pl.BlockSpec((None, ATT_BLOCK, ATT_Q), lambda b, i: (b, i, 0)),
        out_shape=jax.ShapeDtypeStruct((B, S, ATT_Q), f32),
        compiler_params=pltpu.CompilerParams(dimension_semantics=("arbitrary", "arbitrary")),
        name="window_attn",
    )(q, jnp.pad(k, pad), jnp.pad(v, pad), k_ctx, v_ctx, sink.reshape(1, ATT_HEADS))


def even_projections(h, w_in, dec_w, dec_b):
    B, L, _ = h.shape
    z = h @ w_in
    aq, ak, av, lr = jnp.split(z[..., IN_SPLITS[3]:], [s - IN_SPLITS[3] for s in IN_SPLITS[4:]], axis=-1)
    lr_f, lr_b = jnp.split(lr, 2, axis=-1)

    def logdecay(lr_d, w, b):
        return jax.nn.log_sigmoid(lr_d @ w + b) / GLA_GATE_TEMP

    ld_f = logdecay(lr_f, dec_w[0], dec_b[0])
    ld_b = logdecay(lr_b, dec_w[1], dec_b[1])
    aq = aq.reshape(B, L, ATT_HEADS, ATT_HD)
    ak = ak.reshape(B, L, ATT_KV_HEADS, ATT_HD)
    av = av.reshape(B, L, ATT_KV_HEADS, ATT_HD)
    return z, ld_f, ld_b, aq, ak, av


def even_mixer_ctx(h, w_in, w_out, dec_w, dec_b, gla_norm, sink):
    z, ld_f, ld_b, aq, ak, av = even_projections(h, w_in, dec_w, dec_b)
    zero = jnp.zeros((h.shape[0], 2, GLA_HEADS, GLA_DK, GLA_DV), f32)
    o_gla, s_fin = gla_bidir_gated(z, ld_f, ld_b, zero, gla_norm)
    o_att = ctx_attn(aq, ak, av, sink)
    return jnp.concatenate([o_gla, o_att], axis=-1) @ w_out, ak, av, s_fin


def even_mixer_lat(h, w_in, w_out, dec_w, dec_b, gla_norm, sink, rope, k_ctx, v_ctx, s0):
    z, ld_f, ld_b, aq, ak, av = even_projections(h, w_in, dec_w, dec_b)
    o_gla, _ = gla_bidir_gated(z, ld_f, ld_b, s0, gla_norm)
    B, S = h.shape[:2]
    o_att = window_attn_latent(apply_axial_rope(aq, rope).reshape(B, S, ATT_Q),
                               apply_axial_rope(ak, rope).reshape(B, S, ATT_KV), av.reshape(B, S, ATT_KV),
                               k_ctx.reshape(B, -1, ATT_KV), v_ctx.reshape(B, -1, ATT_KV), sink)
    return jnp.concatenate([o_gla, o_att], axis=-1) @ w_out


def centred_shift(x):
    xp = jnp.pad(x, ((0, 0), (1, 1), (0, 0)))
    return 0.5 * (xp[:, :-2] + xp[:, 2:])


LANES = 128
SUBLANES = 8
RWKV_TC = 32


def _rwkv_scan_kernel(r_ref, v_ref, k_ref, zw_ref, za_ref, kk_ref, ka_ref, rk_ref, s0_ref, o_ref, bonus_ref, s_ref,
                      *, tc, nv, n_dir_groups):
    backward = pl.program_id(0) >= n_dir_groups

    @pl.when(pl.program_id(1) == 0)
    def _():
        s_ref[...] = s0_ref[...]

    def step(i, carry):
        t = jnp.where(backward, tc - 1 - i, i)
        k_t, r = k_ref[t], r_ref[t]
        kk = k_t * kk_ref[0]
        kk = kk * lax.rsqrt(jnp.sum(kk * kk, axis=0, keepdims=True) + 1e-12)
        a = jax.nn.sigmoid(za_ref[t])
        z = -zw_ref[t]
        softplus = jnp.maximum(z, 0.0) + jnp.log1p(jnp.exp(-jnp.abs(z)))
        w = jnp.exp(-jnp.exp(-softplus - 0.5))
        alpha, beta = -kk, kk * a
        kd = k_t * (1.0 + (a - 1.0) * ka_ref[0])
        bonus_ref[pl.ds(t, 1), :] = jnp.sum(r * 0.5 * kd * rk_ref[0], axis=0, keepdims=True)
        for vb in range(nv // SUBLANES):
            outs = []
            for v in range(vb * SUBLANES, (vb + 1) * SUBLANES):
                s = s_ref[v]
                sa = jnp.sum(s * alpha, axis=0, keepdims=True)
                s_new = s * w + sa * beta + v_ref[t, pl.ds(v, 1), :] * kd
                s_ref[v] = s_new
                outs.append(jnp.sum(s_new * r, axis=0, keepdims=True))
            o_ref[t, vb * SUBLANES:(vb + 1) * SUBLANES, :] = jnp.concatenate(outs, axis=0)
        return carry

    lax.fori_loop(0, tc, step, 0)


def rwkv_scan_lanes(r, v, k, zw, za, k_k, k_a, r_k, s0):
    gd, L, nv, _ = v.shape
    tc = RWKV_TC
    n = L // tc

    def time_block(g, j):
        return jnp.where(g >= gd, n - 1 - j, j)

    def blk(rows, shared):
        return pl.BlockSpec((None, tc, rows, LANES), lambda g, j: (g % gd if shared else g, time_block(g, j), 0, 0))

    const = pl.BlockSpec((None, 1, RWKV_HEAD, LANES), lambda g, j: (g % gd, 0, 0, 0))
    state = pl.BlockSpec((None, nv, RWKV_HEAD, LANES), lambda g, j: (g, 0, 0, 0))
    block_bytes = tc * RWKV_HEAD * LANES * 4
    return pl.pallas_call(
        functools.partial(_rwkv_scan_kernel, tc=tc, nv=nv, n_dir_groups=gd),
        grid=(2 * gd, n),
        in_specs=[blk(RWKV_HEAD, True), blk(nv, True), blk(RWKV_HEAD, True),
                  blk(RWKV_HEAD, False), blk(RWKV_HEAD, False), const, const, const, state],
        out_specs=[blk(nv, False), pl.BlockSpec((None, tc, LANES), lambda g, j: (g, time_block(g, j), 0)), state],
        out_shape=[jax.ShapeDtypeStruct((2 * gd, L, nv, LANES), f32),
                   jax.ShapeDtypeStruct((2 * gd, L, LANES), f32),
                   jax.ShapeDtypeStruct((2 * gd, nv, RWKV_HEAD, LANES), f32)],
        compiler_params=pltpu.CompilerParams(
            dimension_semantics=("arbitrary", "arbitrary"),
            vmem_limit_bytes=2 * 7 * block_bytes + 4 * nv * RWKV_HEAD * LANES * 4 + (8 << 20),
        ),
        name="rwkv_scan",
    )(r, v, k, zw, za, k_k, k_a, r_k, s0)


def rwkv_scan_bidir(r, v, k, zw, za, k_k, k_a, r_k, s_f0, s_b0):
    B, L, H, N = r.shape
    S = B * H
    gd = S // LANES

    def rows(x):
        T = x.shape[-3]
        lead = x.shape[:-4]
        x = x.reshape(-1, B, T, H, N).transpose(0, 2, 4, 1, 3).reshape(-1, T, N, gd, LANES)
        return x.transpose(0, 3, 1, 2, 4).reshape(-1, T, N, LANES) if lead else x[0].transpose(2, 0, 1, 3)

    def rows_inv(y):
        return y.transpose(1, 2, 0, 3).reshape(y.shape[1], N, B, H)

    def state_in(s):
        return rows(s.transpose(0, 3, 1, 2)).transpose(0, 2, 1, 3)

    def state_out(s):
        return rows_inv(s.transpose(0, 2, 1, 3)).transpose(2, 3, 1, 0)

    def const_rows(c):
        return rows(jnp.broadcast_to(c[None, None], (B, 1, H, N)))

    out, bonus, s_fin = rwkv_scan_lanes(rows(r), rows(v), rows(k), rows(zw), rows(za),
                                        const_rows(k_k), const_rows(k_a), const_rows(r_k),
                                        jnp.concatenate([state_in(s_f0), state_in(s_b0)], axis=0))
    o = rows_inv(out[:gd]) + rows_inv(out[gd:])
    bonus = (bonus[:gd] + bonus[gd:]).transpose(1, 0, 2).reshape(L, B, H).transpose(1, 0, 2)
    return o.transpose(2, 0, 3, 1), bonus, state_out(s_fin[:gd]), state_out(s_fin[gd:])


RWKV_SUB = 64
HEAD_PAIRS = D_MODEL // LANES


def _rwkv_rowsum_kernel(rf, vf, af, wf, kf, bf_, rb, vb, ab, wb, kb, bb, s0, of, ob, s, acc, *, tc):
    j = pl.program_id(1)

    @pl.when(j == 0)
    def _():
        s[...] = s0[...]

    row_head = lax.broadcasted_iota(jnp.int32, (2 * LANES, 2 * LANES), 0) // RWKV_HEAD
    col_head = lax.broadcasted_iota(jnp.int32, (2 * LANES, 2 * LANES), 1) // RWKV_HEAD
    ones_bd = (row_head == col_head).astype(bf16)
    sub = lax.broadcasted_iota(jnp.int32, (RWKV_HEAD, LANES), 0)
    lane_in_head = lax.broadcasted_iota(jnp.int32, (RWKV_HEAD, LANES), 1) % RWKV_HEAD
    eye = (sub == lane_in_head).astype(bf16)
    dirs = ((rf, vf, af, wf, kf, bf_), (rb, vb, ab, wb, kb, bb))

    def row_sums(x):
        return jnp.dot(x, ones_bd, preferred_element_type=f32)

    def row(ref, t, hp, reps):
        return jnp.broadcast_to(ref[t, pl.ds(hp, 1), :], (reps, LANES))

    def sub_chunk(sc, carry):
        def step(tt, carry):
            t_f = sc * RWKV_SUB + tt
            pos = ((t_f, jnp.maximum(t_f - 1, 0), tt - 1),
                   (tc - 1 - t_f, jnp.minimum(tc - t_f, tc - 1), RWKV_SUB - tt))
            for d in range(2):
                r_ref, v_ref, a_ref, w_ref, k_ref, b_ref = dirs[d]
                t_now, t_prev, out_lane = pos[d]
                lhs, tiles = [], []
                for hp2 in range(0, HEAD_PAIRS, 2):
                    v_diag = []
                    for hp in (hp2, hp2 + 1):
                        s_t = s[d, :, hp * LANES:(hp + 1) * LANES]
                        lhs.append(jnp.concatenate([(s_t * row(a_ref, t_now, hp, RWKV_HEAD)).astype(bf16),
                                                    (s_t * row(r_ref, t_prev, hp, RWKV_HEAD)).astype(bf16)], axis=1))
                        v_diag.append(eye * jnp.concatenate([row(v_ref, t_now, hp, 16).astype(bf16)] * 4, axis=0))
                        tiles.append(s_t)
                    lhs.append(jnp.concatenate(v_diag, axis=1))
                res = row_sums(jnp.concatenate(lhs, axis=0))
                for hp, s_t in enumerate(tiles):
                    ls = slice(hp * LANES, (hp + 1) * LANES)
                    base = (hp // 2) * 3 * RWKV_HEAD
                    own = res[base + (hp % 2) * RWKV_HEAD:base + (hp % 2 + 1) * RWKV_HEAD]
                    sa, out_prev = own[:, :LANES], own[:, LANES:]
                    v_col = res[base + 2 * RWKV_HEAD:base + 3 * RWKV_HEAD, (hp % 2) * LANES:(hp % 2 + 1) * LANES]
                    s[d, :, ls] = (s_t * row(w_ref, t_now, hp, RWKV_HEAD) + sa * row(b_ref, t_now, hp, RWKV_HEAD)
                                   + v_col * row(k_ref, t_now, hp, RWKV_HEAD))
                    acc[d, :, ls] = jnp.where(lane_in_head == out_lane, out_prev, acc[d, :, ls])
            return carry

        lax.fori_loop(0, RWKV_SUB, step, 0, unroll=8)
        t_last = sc * RWKV_SUB + RWKV_SUB - 1
        last = ((t_last, RWKV_SUB - 1), (tc - 1 - t_last, 0))
        for d in range(2):
            r_ref = dirs[d][0]
            t_now, out_lane = last[d]
            out_mask = jnp.concatenate([lane_in_head == out_lane] * 2, axis=1)
            for hp in range(0, HEAD_PAIRS, 2):
                ls = slice(hp * LANES, (hp + 2) * LANES)
                r_t = jnp.concatenate([row(r_ref, t_now, hp, RWKV_HEAD), row(r_ref, t_now, hp + 1, RWKV_HEAD)], axis=1)
                out_t = row_sums((s[d, :, ls] * r_t).astype(bf16))
                acc[d, :, ls] = jnp.where(out_mask, out_t, acc[d, :, ls])
        of[sc] = acc[0]
        ob[tc // RWKV_SUB - 1 - sc] = acc[1]
        return carry

    lax.fori_loop(0, tc // RWKV_SUB, sub_chunk, 0)


def rwkv_scan_rowsum(r, v, nkk, fwd, bwd, s_f0, s_b0):
    B, L, H, N = r.shape
    D = H * N
    tc = min(L, 256)
    n = L // tc
    fwd_blk = pl.BlockSpec((None, tc, HEAD_PAIRS, LANES), lambda b, j: (b, j, 0, 0))
    bwd_blk = pl.BlockSpec((None, tc, HEAD_PAIRS, LANES), lambda b, j: (b, n - 1 - j, 0, 0))
    r, v, nkk, w_f, k_f, b_f, w_b, k_b, b_b = (t.reshape(B, L, HEAD_PAIRS, LANES) for t in (r, v, nkk, *fwd, *bwd))
    state = pl.BlockSpec((None, 2, RWKV_HEAD, D), lambda b, j: (b, 0, 0, 0))
    nsub = tc // RWKV_SUB
    block_bytes = tc * D * 4
    s0 = jnp.stack([s.transpose(0, 2, 1, 3).reshape(B, N, D) for s in (s_f0, s_b0)], axis=1)
    o_f, o_b, s_fin = pl.pallas_call(
        functools.partial(_rwkv_rowsum_kernel, tc=tc),
        grid=(B, n),
        in_specs=[fwd_blk] * 6 + [bwd_blk] * 6 + [state],
        out_specs=[pl.BlockSpec((None, nsub, RWKV_HEAD, D), lambda b, j: (b, j, 0, 0)),
                   pl.BlockSpec((None, nsub, RWKV_HEAD, D), lambda b, j: (b, n - 1 - j, 0, 0)),
                   state],
        out_shape=[jax.ShapeDtypeStruct((B, L // RWKV_SUB, RWKV_HEAD, D), f32),
                   jax.ShapeDtypeStruct((B, L // RWKV_SUB, RWKV_HEAD, D), f32),
                   jax.ShapeDtypeStruct((B, 2, RWKV_HEAD, D), f32)],
        scratch_shapes=[pltpu.VMEM((2, RWKV_HEAD, D), f32)],
        compiler_params=pltpu.CompilerParams(
            dimension_semantics=("arbitrary", "arbitrary"),
            vmem_limit_bytes=min(V7X_VMEM_BYTES, 2 * 14 * block_bytes + (8 << 20)),
        ),
        name="rwkv_scan_rowsum",
    )(r, v, nkk, w_f, k_f, b_f, r, v, nkk, w_b, k_b, b_b, s0)
    o = (o_f + o_b).reshape(B, L // RWKV_SUB, RWKV_HEAD, H, RWKV_SUB).transpose(0, 1, 4, 3, 2).reshape(B, L, H, N)
    s_fin = s_fin.reshape(B, 2, N, H, N).transpose(0, 1, 3, 2, 4)
    return o, s_fin[:, 0], s_fin[:, 1]


def rwkv_mix(h, mu, wr, wk, wv, wo, w0, w1, w2, a0, a1, a2, g1, g2, k_k, k_a, r_k, lnx_w, lnx_b, s_f0, s_b0):
    B, L, D = h.shape
    H, N = RWKV_HEADS, RWKV_HEAD
    xx = centred_shift(h) - h
    xr, xw, xk, xv, xa, xg = [h + xx * mu[j] for j in range(6)]
    r = (xr @ wr).reshape(B, L, H, N)
    k = (xk @ wk).reshape(B, L, H, N)
    v = (xv @ wv).reshape(B, L, H, N)
    g = jax.nn.sigmoid(xg @ g1) @ g2
    zw = [(w0[d] + jnp.tanh(xw @ w1[d]) @ w2[d]).reshape(B, L, H, N) for d in range(2)]
    za = [(a0[d] + (xa @ a1[d]) @ a2[d]).reshape(B, L, H, N) for d in range(2)]

    if B * H >= LANES:
        o, bonus, s_f, s_b = rwkv_scan_bidir(r, v, k, jnp.stack(zw), jnp.stack(za), k_k.reshape(H, N),
                                             k_a.reshape(H, N), r_k, s_f0, s_b0)
        bonus = bonus[..., None]
    else:
        kk = k * k_k.reshape(H, N)
        kk = kk * lax.rsqrt(jnp.sum(kk * kk, axis=-1, keepdims=True) + 1e-12)

        def direction(d):
            decay = jnp.exp(-jnp.exp(-jax.nn.softplus(-zw[d]) - 0.5))
            a = jax.nn.sigmoid(za[d])
            return decay, k * (1.0 + (a - 1.0) * k_a.reshape(H, N)), kk * a

        fwd, bwd = direction(0), direction(1)
        o, s_f, s_b = rwkv_scan_rowsum(r, v, -kk, fwd, bwd, s_f0, s_b0)
        bonus = jnp.sum(r * 0.5 * (fwd[1] + bwd[1]) * r_k, axis=-1, keepdims=True)
    m = jnp.mean(o, axis=-1, keepdims=True)
    var = jnp.mean(jnp.square(o - m), axis=-1, keepdims=True)
    o = (o - m) * lax.rsqrt(var + RWKV_LN_EPS) * lnx_w.reshape(H, N) + lnx_b.reshape(H, N)
    o = (o + bonus * v).reshape(B, L, D)
    return (o * g) @ wo, s_f, s_b


GROUP_HID = EXP_PER_GROUP * D_EXPERT


def _moe_group_kernel(tile_group_ref, n_tiles_ref, xs_ref, gates_ref, wg_ref, wu_ref, wd_ref, ys_ref,
                      wg_bf, wu_bf, wd_bf):
    i = pl.program_id(0)
    prev_group = tile_group_ref[jnp.maximum(i - 1, 0)]
    group_changed = jnp.logical_or(i == 0, tile_group_ref[i] != prev_group)

    @pl.when(group_changed)
    def _():
        for e in range(EXP_PER_GROUP):
            hs = slice(e * D_EXPERT, (e + 1) * D_EXPERT)
            wg_bf[:, hs] = wg_ref[e].astype(bf16)
            wu_bf[:, hs] = wu_ref[e].astype(bf16)
            wd_bf[hs, :] = wd_ref[e].astype(bf16)

    @pl.when(i < n_tiles_ref[0])
    def _():
        x = xs_ref[...].astype(bf16)
        g = jnp.dot(x, wg_bf[...], preferred_element_type=f32)
        u = jnp.dot(x, wu_bf[...], preferred_element_type=f32)
        gate = jnp.concatenate([jnp.broadcast_to(gates_ref[:, e:e + 1], (MOE_TILE_M, D_EXPERT))
                                for e in range(EXP_PER_GROUP)], axis=1)
        hid = (g * jax.nn.sigmoid(g)) * u * gate
        ys_ref[...] = jnp.dot(hid.astype(bf16), wd_bf[...], preferred_element_type=f32)

    @pl.when(i >= n_tiles_ref[0])
    def _():
        ys_ref[...] = jnp.zeros_like(ys_ref)


def moe_group_experts(xs, gates, tile_group, n_tiles, w_gate, w_up, w_down, layer):
    P, D = xs.shape
    max_tiles = P // MOE_TILE_M
    weight_bytes = 3 * EXP_PER_GROUP * D * D_EXPERT * (4 + 2)
    tile_bytes = 2 * MOE_TILE_M * (2 * D * 4 + LANES * 4) + 4 * MOE_TILE_M * GROUP_HID * 4
    once = pl.Buffered(1)
    grid_spec = pltpu.PrefetchScalarGridSpec(
        num_scalar_prefetch=2,
        grid=(max_tiles,),
        in_specs=[
            pl.BlockSpec((MOE_TILE_M, D), lambda i, tg, nt: (i, 0)),
            pl.BlockSpec((MOE_TILE_M, EXP_PER_GROUP), lambda i, tg, nt: (i, 0)),
            pl.BlockSpec((None, EXP_PER_GROUP, D, D_EXPERT), lambda i, tg, nt: (layer, tg[i], 0, 0),
                         pipeline_mode=once),
            pl.BlockSpec((None, EXP_PER_GROUP, D, D_EXPERT), lambda i, tg, nt: (layer, tg[i], 0, 0),
                         pipeline_mode=once),
            pl.BlockSpec((None, EXP_PER_GROUP, D_EXPERT, D), lambda i, tg, nt: (layer, tg[i], 0, 0),
                         pipeline_mode=once),
        ],
        out_specs=pl.BlockSpec((MOE_TILE_M, D), lambda i, tg, nt: (i, 0)),
        scratch_shapes=[
            pltpu.VMEM((D, GROUP_HID), bf16),
            pltpu.VMEM((D, GROUP_HID), bf16),
            pltpu.VMEM((GROUP_HID, D), bf16),
        ],
    )
    return pl.pallas_call(
        _moe_group_kernel,
        grid_spec=grid_spec,
        out_shape=jax.ShapeDtypeStruct((P, D), f32),
        compiler_params=pltpu.CompilerParams(
            dimension_semantics=("arbitrary",),
            vmem_limit_bytes=min(V7X_VMEM_BYTES - (4 << 20), weight_bytes + tile_bytes + (8 << 20)),
        ),
        name="moe_group_experts",
    )(tile_group, n_tiles, xs, gates, w_gate, w_up, w_down)


def hier_moe(t, w_grp, b_grp, w_exp, b_exp, w_gate, w_up, w_down, layer):
    T, D = t.shape
    logits = jnp.dot(t, jnp.concatenate([w_grp, w_exp], axis=1), precision=lax.Precision.HIGHEST)
    grp_logits = logits[:, :N_GROUPS] + b_grp
    grp_prob = jax.nn.softmax(grp_logits, axis=-1)
    g_top = jnp.argmax(grp_logits, axis=-1).astype(jnp.int32)
    in_group = g_top[:, None] == jnp.arange(N_GROUPS, dtype=jnp.int32)[None, :]
    p_g = jnp.sum(jnp.where(in_group, grp_prob, 0.0), axis=1, keepdims=True)
    exp_logits = (logits[:, N_GROUPS:] + b_exp).reshape(-1, N_GROUPS, EXP_PER_GROUP)
    sel = jnp.sum(jnp.where(in_group[:, :, None], exp_logits, 0.0), axis=1)
    top_v, top_i = lax.top_k(sel, TOP_K)
    wts = p_g * jax.nn.softmax(top_v, axis=-1)
    gates = jnp.sum(jax.nn.one_hot(top_i, EXP_PER_GROUP, dtype=f32) * wts[..., None], axis=1)

    order = jnp.argsort(g_top, stable=True).astype(jnp.int32)
    grp_i32 = in_group.astype(jnp.int32)
    counts = jnp.sum(grp_i32, axis=0)
    rank = jnp.sum(jnp.where(in_group, jnp.cumsum(grp_i32, axis=0), 0), axis=1) - 1
    tiles_per = (counts + MOE_TILE_M - 1) // MOE_TILE_M
    tile_end = jnp.cumsum(tiles_per)
    start_padded = (tile_end - tiles_per) * MOE_TILE_M
    start_sorted = jnp.cumsum(counts) - counts
    max_tiles = T // MOE_TILE_M + N_GROUPS
    n_tiles = tile_end[-1:].astype(jnp.int32)
    tile_ids = jnp.arange(max_tiles, dtype=jnp.int32)
    tile_group = jnp.minimum(jnp.sum((tile_ids[:, None] >= tile_end[None, :]).astype(jnp.int32), axis=1),
                             N_GROUPS - 1).astype(jnp.int32)
    tile_group = jnp.where(tile_ids < n_tiles[0], tile_group, tile_group[jnp.maximum(n_tiles[0] - 1, 0)])
    row_group = jnp.repeat(tile_group, MOE_TILE_M)
    row_in_group = jnp.arange(max_tiles * MOE_TILE_M, dtype=jnp.int32) - start_padded[row_group]
    row_valid = (row_in_group >= 0) & (row_in_group < counts[row_group])
    row_tok = jnp.where(row_valid, order[jnp.clip(start_sorted[row_group] + row_in_group, 0, T - 1)], 0)
    pos = start_padded[g_top] + rank

    xs = jnp.take(t, row_tok, axis=0)
    row_gates = jnp.where(row_valid[:, None], jnp.take(gates, row_tok, axis=0), 0.0)
    ys = moe_group_experts(xs, row_gates, tile_group, n_tiles, w_gate, w_up, w_down, layer)
    return jnp.take(ys, pos, axis=0)


def kernel(x_prompt, x_sample, c, cache_attn_k, cache_attn_v, state_gla, state_rwkv, c_ctx, ada_w, ada_b, norm_mix, norm_ffn, norm_out, ev_w_in, ev_w_out, gla_dec_w, gla_dec_b, gla_norm, att_sink, rw_mu, rw_wr, rw_wk, rw_wv, rw_wo, rw_w0, rw_w1, rw_w2, rw_a0, rw_a1, rw_a2, rw_g1, rw_g2, rw_kk, rw_ka, rw_rk, rw_lnx_w, rw_lnx_b, moe_w_grp, moe_b_grp, moe_w_exp, moe_b_exp, moe_w_gate, moe_w_up, moe_w_down):
    n_lat = x_sample.shape[1]
    rows = n_lat // GRID_W
    row_pos = jnp.repeat(jnp.arange(rows), GRID_W)
    col_pos = jnp.tile(jnp.arange(GRID_W), rows)
    rope = rope_tables(row_pos, col_pos)
    Bc, Lc, D = x_prompt.shape
    Bl, Ll, _ = x_sample.shape

    n_ctx = Bc * Lc
    x_all = jnp.concatenate([x_prompt.reshape(n_ctx, D), x_sample.reshape(Bl * Ll, D)], axis=0)
    cond_all = jnp.concatenate([c_ctx[None, :], c], axis=0)
    mods = [ada_mod(cond_all, ada_w[l], ada_b[l]) for l in range(DEPTH)]
    no_mod = jnp.zeros((1 + Bl, 1, D), f32)

    def split(t):
        return t[:n_ctx].reshape(Bc, Lc, D), t[n_ctx:].reshape(Bl, Ll, D)

    hc = modulate(rmsnorm(x_prompt, norm_mix[0]), mods[0][0][:1], mods[0][1][:1])
    hl = modulate(rmsnorm(x_sample, norm_mix[0]), mods[0][0][1:], mods[0][1][1:])
    ks_out, vs_out, gla_out, rwkv_out = [], [], [], []
    for l in range(DEPTH):
        _, _, gate1, shift2, scale2, gate2 = mods[l]
        i = l // 2
        if l % 2 == 0:
            oc, k_c, v_c, s_gla = even_mixer_ctx(hc, ev_w_in[i], ev_w_out[i], gla_dec_w[i], gla_dec_b[i],
                                                 gla_norm[i], att_sink[i])
            ol = even_mixer_lat(hl, ev_w_in[i], ev_w_out[i], gla_dec_w[i], gla_dec_b[i], gla_norm[i], att_sink[i],
                                rope, cache_attn_k[:, i], cache_attn_v[:, i], state_gla[:, i])
            ks_out.append(k_c)
            vs_out.append(v_c)
            gla_out.append(s_gla)
        else:
            rw = (rw_mu[i], rw_wr[i], rw_wk[i], rw_wv[i], rw_wo[i], rw_w0[i], rw_w1[i], rw_w2[i], rw_a0[i],
                  rw_a1[i], rw_a2[i], rw_g1[i], rw_g2[i], rw_kk[i], rw_ka[i], rw_rk[i], rw_lnx_w[i], rw_lnx_b[i])
            zero = jnp.zeros((Bc, RWKV_HEADS, RWKV_HEAD, RWKV_HEAD), f32)
            oc, s_f, s_b = rwkv_mix(hc, *rw, zero, zero)
            ol, _, _ = rwkv_mix(hl, *rw, state_rwkv[:, i, 0], state_rwkv[:, i, 1])
            rwkv_out.append(jnp.stack([s_f, s_b], axis=1))
        x_all, t = resid_norm_mod(x_all, (oc.reshape(n_ctx, D), ol.reshape(Bl * Ll, D)), n_ctx, Ll,
                                  gate1, norm_ffn[l], shift2, scale2)
        y = hier_moe(t, moe_w_grp[l], moe_b_grp[l], moe_w_exp[l], moe_b_exp[l], moe_w_gate, moe_w_up, moe_w_down, l)
        if l + 1 < DEPTH:
            x_all, h_all = resid_norm_mod(x_all, y, n_ctx, Ll, gate2, norm_mix[l + 1], mods[l + 1][0], mods[l + 1][1])
            hc, hl = split(h_all)
        else:
            x_all, y_all = resid_norm_mod(x_all, y, n_ctx, Ll, gate2, norm_out, no_mod, no_mod)
            y_prompt, y_sample = split(y_all)

    new_attn_k = jnp.stack(ks_out, axis=1)
    new_attn_v = jnp.stack(vs_out, axis=1)
    new_gla = jnp.stack(gla_out, axis=1)
    new_rwkv = jnp.stack(rwkv_out, axis=1)
    return (y_prompt, y_sample, new_attn_k, new_attn_v, new_gla, new_rwkv)
```

```python
import functools

import jax
import jax.numpy as jnp
import numpy as np
from jax import lax
from jax.experimental import pallas as pl
from jax.experimental.pallas import tpu as pltpu

D_MODEL = 1024
DEPTH = 4
GRID_W = 64
EPS = 1e-6
GLA_HEADS = 4
GLA_DK = 64
GLA_DV = 128
GLA_LOWRANK = 16
GLA_GATE_TEMP = 16.0
GLA_CHUNK = 64
ATT_HEADS = 8
ATT_KV_HEADS = 2
ATT_GROUP = ATT_HEADS // ATT_KV_HEADS
ATT_HD = 64
WINDOW = 128
ATT_BLOCK = 128
ROPE_BASE = 10000.0
ROPE_AXIS_DIMS = ATT_HD // 2
NEG = -1e30
IN_WIDTHS = (GLA_HEADS * GLA_DK, GLA_HEADS * GLA_DK, GLA_HEADS * GLA_DV, GLA_HEADS * GLA_DV,
             ATT_HEADS * ATT_HD, ATT_KV_HEADS * ATT_HD, ATT_KV_HEADS * ATT_HD, 2 * GLA_LOWRANK)
IN_SPLITS = tuple(int(v) for v in np.cumsum(IN_WIDTHS)[:-1])
RWKV_HEAD = 64
RWKV_HEADS = D_MODEL // RWKV_HEAD
RWKV_LN_EPS = 64e-5
N_GROUPS = 4
EXP_PER_GROUP = 4
N_EXPERTS = N_GROUPS * EXP_PER_GROUP
TOP_K = 2
D_EXPERT = 512

V7X_VMEM_BYTES = 64 * 1024 * 1024
MOE_TILE_M = 256

f32 = jnp.float32
bf16 = jnp.bfloat16


def rmsnorm(x, w):
    y = x * lax.rsqrt(jnp.mean(x * x, axis=-1, keepdims=True) + EPS)
    return y * w


def ada_mod(cond, w, b):
    mod = jax.nn.silu(cond) @ w + b
    return [m[:, None, :] for m in jnp.split(mod, 6, axis=-1)]


def modulate(x, shift, scale):
    return x * (1.0 + scale) + shift


NORM_ROWS = 256


def _resid_norm_mod_kernel(*refs, n_ctx_blocks, split):
    if split:
        x_ref, oc_ref, ol_ref, gate_ref, w_ref, shift_ref, scale_ref, router_ref, xn_ref, t_ref, logit_ref = refs
        o = jnp.where(pl.program_id(0) < n_ctx_blocks, oc_ref[...], ol_ref[...])
    else:
        x_ref, o_ref, gate_ref, w_ref, shift_ref, scale_ref, xn_ref, t_ref = refs
        o = o_ref[...]
    x = x_ref[...] + gate_ref[0] * o
    xn_ref[...] = x
    y = x * lax.rsqrt(jnp.mean(x * x, axis=-1, keepdims=True) + EPS) * w_ref[...]
    t = y * (1.0 + scale_ref[0]) + shift_ref[0]
    t_ref[...] = t
    if split:
        logit_ref[...] = jnp.dot(t, router_ref[...], precision=lax.Precision.HIGHEST, preferred_element_type=f32)


def resid_norm_mod(x, o, n_ctx, lat_len, gate, w, shift, scale, router_w=None):
    T, D = x.shape
    ncb = n_ctx // NORM_ROWS
    per_lat = lat_len // NORM_ROWS
    rows = pl.BlockSpec((NORM_ROWS, D), lambda i: (i, 0))
    mod = pl.BlockSpec((1, 1, D), lambda i: (jnp.where(i < ncb, 0, 1 + (i - ncb) // per_lat), 0, 0))
    split = isinstance(o, tuple)
    assert split == (router_w is not None)
    if split:
        o_specs = [pl.BlockSpec((NORM_ROWS, D), lambda i: (jnp.minimum(i, ncb - 1), 0)),
                   pl.BlockSpec((NORM_ROWS, D), lambda i: (jnp.maximum(i - ncb, 0), 0))]
        o_args = list(o)
        extra_in, extra_args = [pl.BlockSpec((D, LANES), lambda i: (0, 0))], [router_w]
        extra_out = [pl.BlockSpec((NORM_ROWS, LANES), lambda i: (i, 0))]
        extra_shape = [jax.ShapeDtypeStruct((T, LANES), f32)]
    else:
        o_specs, o_args = [rows], [o]
        extra_in, extra_args, extra_out, extra_shape = [], [], [], []
    return pl.pallas_call(
        functools.partial(_resid_norm_mod_kernel, n_ctx_blocks=ncb, split=split),
        grid=(T // NORM_ROWS,),
        in_specs=[rows, *o_specs, mod, pl.BlockSpec((1, D), lambda i: (0, 0)), mod, mod, *extra_in],
        out_specs=[rows, rows, *extra_out],
        out_shape=[jax.ShapeDtypeStruct((T, D), f32)] * 2 + extra_shape,
        compiler_params=pltpu.CompilerParams(dimension_semantics=("arbitrary",)),
        name="resid_norm_mod",
    )(x, *o_args, gate, w.reshape(1, D), shift, scale, *extra_args)


def rope_tables(row_pos, col_pos):
    freqs = ROPE_BASE ** (-jnp.arange(0, ROPE_AXIS_DIMS, 2, dtype=f32) / ROPE_AXIS_DIMS)
    ang_r = row_pos.astype(f32)[:, None] * freqs[None, :]
    ang_c = col_pos.astype(f32)[:, None] * freqs[None, :]
    return jnp.cos(ang_r), jnp.sin(ang_r), jnp.cos(ang_c), jnp.sin(ang_c)


def rotate(x, cos, sin):
    x1, x2 = jnp.split(x, 2, axis=-1)
    cos = cos[None, :, None, :]
    sin = sin[None, :, None, :]
    return jnp.concatenate([x1 * cos - x2 * sin, x1 * sin + x2 * cos], axis=-1)


def apply_axial_rope(x, tables):
    cr, sr, cc, sc = tables
    xr, xc = jnp.split(x, 2, axis=-1)
    return jnp.concatenate([rotate(xr, cr, sr), rotate(xc, cc, sc)], axis=-1)


GLA_QK = GLA_HEADS * GLA_DK
GLA_V = GLA_HEADS * GLA_DV


def _gla_kernel(q_ref, v_ref, gg_ref, ldf_ref, ldb_ref, kt_ref, ldft_ref, ldbt_ref, s0_ref, norm_ref,
                o_ref, s_ref, *, seq_len):
    n = seq_len // GLA_CHUNK
    C = GLA_CHUNK
    ti = lax.broadcasted_iota(jnp.int32, (C, C), 0)
    tj = lax.broadcasted_iota(jnp.int32, (C, C), 1)
    keep = (tj <= ti, tj >= ti)
    tri = tuple(m.astype(bf16) for m in keep)
    tri_t = (tri[1], tri[0])
    ones = jnp.ones((C, GLA_DV), bf16)
    s_ref[...] = s0_ref[...]
    o_ref[...] = jnp.zeros_like(o_ref)
    ld_refs = ((ldf_ref, ldft_ref), (ldb_ref, ldbt_ref))

    def split3(x):
        hi = x.astype(bf16)
        rest = x - hi.astype(f32)
        mid = rest.astype(bf16)
        return hi, mid, (rest - mid.astype(f32)).astype(bf16)

    def sums(a, b):
        if isinstance(b, tuple):
            return sum(jnp.dot(a, p, preferred_element_type=f32) for p in b)
        return sum(jnp.dot(p, b, preferred_element_type=f32) for p in a)

    def chunk(c, carry):
        pending = []
        for d in range(2):
            cc = c if d == 0 else n - 1 - c
            rows = pl.ds(pl.multiple_of(cc * C, C), C)
            ld_ref, ldt_ref = ld_refs[d]
            ld_t3 = split3(ldt_ref[cc])
            b_all = sums(tri[d], split3(ld_ref[rows, :]))
            bt_all = sums(ld_t3, tri_t[d])
            total_all = sums(ld_t3, ones)
            for h in range(GLA_HEADS):
                ks = slice(h * GLA_DK, (h + 1) * GLA_DK)
                vs = slice(h * GLA_DV, (h + 1) * GLA_DV)
                q = q_ref[rows, ks] * (GLA_DK ** -0.5)
                v = v_ref[rows, vs].astype(bf16)
                k_t = kt_ref[cc, ks, :]
                b, b_t, total = b_all[:, ks], bt_all[ks], total_all[ks]
                q_e = (q * jnp.exp(b)).astype(bf16)
                k_e = (k_t * jnp.exp(-b_t)).astype(bf16)
                k_s = (k_t * jnp.exp(total[:, :C] - b_t)).astype(bf16)
                att = jnp.where(keep[d], jnp.dot(q_e, k_e, preferred_element_type=f32), 0.0).astype(bf16)
                s = s_ref[d, h]
                o = (jnp.dot(q_e, s.astype(bf16), preferred_element_type=f32)
                     + jnp.dot(att, v, preferred_element_type=f32))
                s_new = s * jnp.exp(total) + jnp.dot(k_s, v, preferred_element_type=f32)
                pending.append((d, h, rows, vs, o_ref[rows, vs] + o, s_new))
        for d, h, rows, vs, o, s_new in pending:
            o_ref[rows, vs] = o
            s_ref[d, h] = s_new
        return carry

    lax.fori_loop(0, n, chunk, 0)
    for h in range(GLA_HEADS):
        vs = slice(h * GLA_DV, (h + 1) * GLA_DV)
        x = o_ref[:, vs]
        y = x * lax.rsqrt(jnp.mean(x * x, axis=-1, keepdims=True) + EPS) * norm_ref[...]
        g = gg_ref[:, vs]
        o_ref[:, vs] = y * (g * jax.nn.sigmoid(g))


def gla_bidir_gated(z, ld_f, ld_b, s0, gla_norm):
    B, L, _ = z.shape
    n = L // GLA_CHUNK
    assert n % 2 == 0

    def per_chunk_t(a):
        return a.reshape(B, n, GLA_CHUNK, GLA_QK).transpose(0, 1, 3, 2)

    state = pl.BlockSpec((None, 2, GLA_HEADS, GLA_DK, GLA_DV), lambda b: (b, 0, 0, 0, 0))
    ld = pl.BlockSpec((None, L, GLA_QK), lambda b: (b, 0, 0))
    ld_t = pl.BlockSpec((None, n, GLA_QK, GLA_CHUNK), lambda b: (b, 0, 0, 0))
    return pl.pallas_call(
        functools.partial(_gla_kernel, seq_len=L),
        grid=(B,),
        in_specs=[pl.BlockSpec((None, L, GLA_QK), lambda b: (b, 0, 0)),
                  pl.BlockSpec((None, L, GLA_V), lambda b: (b, 0, 1)),
                  pl.BlockSpec((None, L, GLA_V), lambda b: (b, 0, 2)),
                  ld, ld, ld_t, ld_t, ld_t, state,
                  pl.BlockSpec((1, GLA_DV), lambda b: (0, 0))],
        out_specs=[pl.BlockSpec((None, L, GLA_V), lambda b: (b, 0, 0)), state],
        out_shape=[jax.ShapeDtypeStruct((B, L, GLA_V), f32),
                   jax.ShapeDtypeStruct((B, 2, GLA_HEADS, GLA_DK, GLA_DV), f32)],
        compiler_params=pltpu.CompilerParams(
            dimension_semantics=("arbitrary",),
            vmem_limit_bytes=min(V7X_VMEM_BYTES, 2 * L * (9 * GLA_QK + 3 * GLA_V) * 4 + (16 << 20)),
        ),
        name="gla_bidir",
    )(z, z, z, ld_f, ld_b, per_chunk_t(z[..., GLA_QK:2 * GLA_QK]), per_chunk_t(ld_f), per_chunk_t(ld_b),
      s0, gla_norm.reshape(1, GLA_DV))


ATT_Q = ATT_HEADS * ATT_HD
ATT_KV = ATT_KV_HEADS * ATT_HD


def _window_attn_kernel(q_ref, k_ref, v_ref, kc_ref, vc_ref, sink_ref, o_ref, *, tq, seq_len):
    scale = ATT_HD ** -0.5
    nt = (((1,), (1,)), ((), ()))
    start = pl.multiple_of(pl.program_id(1) * tq, tq)
    k_own = k_ref[pl.ds(start, 3 * tq), :].astype(bf16)
    v_own = v_ref[pl.ds(start, 3 * tq), :].astype(bf16)
    qpos = start + lax.broadcasted_iota(jnp.int32, (tq, 3 * tq), 0)
    kpos = start - tq + lax.broadcasted_iota(jnp.int32, (tq, 3 * tq), 1)
    valid = (jnp.abs(qpos - kpos) <= WINDOW) & (kpos >= 0) & (kpos < seq_len)
    for h in range(ATT_HEADS):
        g = h // ATT_GROUP
        kv = slice(g * ATT_HD, (g + 1) * ATT_HD)
        q = q_ref[:, h * ATT_HD:(h + 1) * ATT_HD].astype(bf16)
        s_own = lax.dot_general(q, k_own[:, kv], nt, preferred_element_type=f32) * scale
        s_own = jnp.where(valid, s_own, NEG)
        s_ctx = lax.dot_general(q, kc_ref[:, kv].astype(bf16), nt, preferred_element_type=f32) * scale
        sink = sink_ref[0:1, h:h + 1]
        m = jnp.maximum(jnp.maximum(jnp.max(s_own, axis=-1, keepdims=True), jnp.max(s_ctx, axis=-1, keepdims=True)),
                        sink)
        p_own = jnp.exp(s_own - m)
        p_ctx = jnp.exp(s_ctx - m)
        denom = (jnp.sum(p_own, axis=-1, keepdims=True) + jnp.sum(p_ctx, axis=-1, keepdims=True)
                 + jnp.exp(sink - m))
        o = (jnp.dot(p_own.astype(bf16), v_own[:, kv], preferred_element_type=f32)
             + jnp.dot(p_ctx.astype(bf16), vc_ref[:, kv].astype(bf16), preferred_element_type=f32))
        o_ref[:, h * ATT_HD:(h + 1) * ATT_HD] = o / denom


def ctx_attn(q, k, v, sink):
    B, L = q.shape[:2]
    scale = ATT_HD ** -0.5
    qg = q.reshape(B, L, ATT_KV_HEADS, ATT_GROUP, ATT_HD)
    sink_b = sink.reshape(ATT_KV_HEADS, ATT_GROUP)[None, :, :, None, None]
    s = jnp.einsum('bqhgd,bkhd->bhgqk', qg, k) * scale
    s_sink = jnp.broadcast_to(sink_b, s.shape[:-1] + (1,))
    p = jax.nn.softmax(jnp.concatenate([s_sink, s], axis=-1), axis=-1)[..., 1:]
    return jnp.einsum('bhgqk,bkhd->bqhgd', p, v).reshape(B, L, ATT_HEADS * ATT_HD)


def window_attn_latent(q, k, v, k_ctx, v_ctx, sink):
    B, S, _ = q.shape
    Lc = k_ctx.shape[1]
    nb = S // ATT_BLOCK
    pad = ((0, 0), (ATT_BLOCK, ATT_BLOCK), (0, 0))
    whole = lambda n: pl.BlockSpec((None, n, ATT_KV), lambda b, i: (b, 0, 0))
    return pl.pallas_call(
        functools.partial(_window_attn_kernel, tq=ATT_BLOCK, seq_len=S),
        grid=(B, nb),
        in_specs=[pl.BlockSpec((None, ATT_BLOCK, ATT_Q), lambda b, i: (b, i, 0)),
                  whole(S + 2 * ATT_BLOCK), whole(S + 2 * ATT_BLOCK), whole(Lc), whole(Lc),
                  pl.BlockSpec((1, ATT_HEADS), lambda b, i: (0, 0))],
        out_specs=pl.BlockSpec((None, ATT_BLOCK, ATT_Q), lambda b, i: (b, i, 0)),
        out_shape=jax.ShapeDtypeStruct((B, S, ATT_Q), f32),
        compiler_params=pltpu.CompilerParams(dimension_semantics=("arbitrary", "arbitrary")),
        name="window_attn",
    )(q, jnp.pad(k, pad), jnp.pad(v, pad), k_ctx, v_ctx, sink.reshape(1, ATT_HEADS))


def even_projections(h, w_in, dec_w, dec_b):
    B, L, _ = h.shape
    z = h @ w_in
    aq, ak, av, lr = jnp.split(z[..., IN_SPLITS[3]:], [s - IN_SPLITS[3] for s in IN_SPLITS[4:]], axis=-1)
    lr_f, lr_b = jnp.split(lr, 2, axis=-1)

    def logdecay(lr_d, w, b):
        return jax.nn.log_sigmoid(lr_d @ w + b) / GLA_GATE_TEMP

    ld_f = logdecay(lr_f, dec_w[0], dec_b[0])
    ld_b = logdecay(lr_b, dec_w[1], dec_b[1])
    aq = aq.reshape(B, L, ATT_HEADS, ATT_HD)
    ak = ak.reshape(B, L, ATT_KV_HEADS, ATT_HD)
    av = av.reshape(B, L, ATT_KV_HEADS, ATT_HD)
    return z, ld_f, ld_b, aq, ak, av


def even_mixer_ctx(h, w_in, w_out, dec_w, dec_b, gla_norm, sink):
    z, ld_f, ld_b, aq, ak, av = even_projections(h, w_in, dec_w, dec_b)
    zero = jnp.zeros((h.shape[0], 2, GLA_HEADS, GLA_DK, GLA_DV), f32)
    o_gla, s_fin = gla_bidir_gated(z, ld_f, ld_b, zero, gla_norm)
    o_att = ctx_attn(aq, ak, av, sink)
    return jnp.concatenate([o_gla, o_att], axis=-1) @ w_out, ak, av, s_fin


def even_mixer_lat(h, w_in, w_out, dec_w, dec_b, gla_norm, sink, rope, k_ctx, v_ctx, s0):
    z, ld_f, ld_b, aq, ak, av = even_projections(h, w_in, dec_w, dec_b)
    o_gla, _ = gla_bidir_gated(z, ld_f, ld_b, s0, gla_norm)
    B, S = h.shape[:2]
    o_att = window_attn_latent(apply_axial_rope(aq, rope).reshape(B, S, ATT_Q),
                               apply_axial_rope(ak, rope).reshape(B, S, ATT_KV), av.reshape(B, S, ATT_KV),
                               k_ctx.reshape(B, -1, ATT_KV), v_ctx.reshape(B, -1, ATT_KV), sink)
    return jnp.concatenate([o_gla, o_att], axis=-1) @ w_out


def centred_shift(x):
    xp = jnp.pad(x, ((0, 0), (1, 1), (0, 0)))
    return 0.5 * (xp[:, :-2] + xp[:, 2:])


LANES = 128
SUBLANES = 8
RWKV_TC = 32


def _rwkv_scan_kernel(r_ref, v_ref, k_ref, zw_ref, za_ref, kk_ref, ka_ref, rk_ref, s0_ref, o_ref, bonus_ref, s_ref,
                      *, tc, nv, n_dir_groups):
    backward = pl.program_id(0) >= n_dir_groups

    @pl.when(pl.program_id(1) == 0)
    def _():
        s_ref[...] = s0_ref[...]

    def step(i, carry):
        t = jnp.where(backward, tc - 1 - i, i)
        k_t, r = k_ref[t], r_ref[t]
        kk = k_t * kk_ref[0]
        kk = kk * lax.rsqrt(jnp.sum(kk * kk, axis=0, keepdims=True) + 1e-12)
        a = jax.nn.sigmoid(za_ref[t])
        z = -zw_ref[t]
        softplus = jnp.maximum(z, 0.0) + jnp.log1p(jnp.exp(-jnp.abs(z)))
        w = jnp.exp(-jnp.exp(-softplus - 0.5))
        alpha, beta = -kk, kk * a
        kd = k_t * (1.0 + (a - 1.0) * ka_ref[0])
        bonus_ref[pl.ds(t, 1), :] = jnp.sum(r * 0.5 * kd * rk_ref[0], axis=0, keepdims=True)
        for vb in range(nv // SUBLANES):
            outs = []
            for v in range(vb * SUBLANES, (vb + 1) * SUBLANES):
                s = s_ref[v]
                sa = jnp.sum(s * alpha, axis=0, keepdims=True)
                s_new = s * w + sa * beta + v_ref[t, pl.ds(v, 1), :] * kd
                s_ref[v] = s_new
                outs.append(jnp.sum(s_new * r, axis=0, keepdims=True))
            o_ref[t, vb * SUBLANES:(vb + 1) * SUBLANES, :] = jnp.concatenate(outs, axis=0)
        return carry

    lax.fori_loop(0, tc, step, 0)


def rwkv_scan_lanes(r, v, k, zw, za, k_k, k_a, r_k, s0):
    gd, L, nv, _ = v.shape
    tc = RWKV_TC
    n = L // tc

    def time_block(g, j):
        return jnp.where(g >= gd, n - 1 - j, j)

    def blk(rows, shared):
        return pl.BlockSpec((None, tc, rows, LANES), lambda g, j: (g % gd if shared else g, time_block(g, j), 0, 0))

    const = pl.BlockSpec((None, 1, RWKV_HEAD, LANES), lambda g, j: (g % gd, 0, 0, 0))
    state = pl.BlockSpec((None, nv, RWKV_HEAD, LANES), lambda g, j: (g, 0, 0, 0))
    block_bytes = tc * RWKV_HEAD * LANES * 4
    return pl.pallas_call(
        functools.partial(_rwkv_scan_kernel, tc=tc, nv=nv, n_dir_groups=gd),
        grid=(2 * gd, n),
        in_specs=[blk(RWKV_HEAD, True), blk(nv, True), blk(RWKV_HEAD, True),
                  blk(RWKV_HEAD, False), blk(RWKV_HEAD, False), const, const, const, state],
        out_specs=[blk(nv, False), pl.BlockSpec((None, tc, LANES), lambda g, j: (g, time_block(g, j), 0)), state],
        out_shape=[jax.ShapeDtypeStruct((2 * gd, L, nv, LANES), f32),
                   jax.ShapeDtypeStruct((2 * gd, L, LANES), f32),
                   jax.ShapeDtypeStruct((2 * gd, nv, RWKV_HEAD, LANES), f32)],
        compiler_params=pltpu.CompilerParams(
            dimension_semantics=("arbitrary", "arbitrary"),
            vmem_limit_bytes=2 * 7 * block_bytes + 4 * nv * RWKV_HEAD * LANES * 4 + (8 << 20),
        ),
        name="rwkv_scan",
    )(r, v, k, zw, za, k_k, k_a, r_k, s0)


def rwkv_scan_bidir(r, v, k, zw, za, k_k, k_a, r_k, s_f0, s_b0):
    B, L, H, N = r.shape
    S = B * H
    gd = S // LANES

    def rows(x):
        T = x.shape[-3]
        lead = x.shape[:-4]
        x = x.reshape(-1, B, T, H, N).transpose(0, 2, 4, 1, 3).reshape(-1, T, N, gd, LANES)
        return x.transpose(0, 3, 1, 2, 4).reshape(-1, T, N, LANES) if lead else x[0].transpose(2, 0, 1, 3)

    def rows_inv(y):
        return y.transpose(1, 2, 0, 3).reshape(y.shape[1], N, B, H)

    def state_in(s):
        return rows(s.transpose(0, 3, 1, 2)).transpose(0, 2, 1, 3)

    def state_out(s):
        return rows_inv(s.transpose(0, 2, 1, 3)).transpose(2, 3, 1, 0)

    def const_rows(c):
        return rows(jnp.broadcast_to(c[None, None], (B, 1, H, N)))

    out, bonus, s_fin = rwkv_scan_lanes(rows(r), rows(v), rows(k), rows(zw), rows(za),
                                        const_rows(k_k), const_rows(k_a), const_rows(r_k),
                                        jnp.concatenate([state_in(s_f0), state_in(s_b0)], axis=0))
    o = rows_inv(out[:gd]) + rows_inv(out[gd:])
    bonus = (bonus[:gd] + bonus[gd:]).transpose(1, 0, 2).reshape(L, B, H).transpose(1, 0, 2)
    return o.transpose(2, 0, 3, 1), bonus, state_out(s_fin[:gd]), state_out(s_fin[gd:])


RWKV_SUB = 64
HEAD_PAIRS = D_MODEL // LANES


def _rwkv_rowsum_kernel(rf, vf, af, wf, kf, bf_, rb, vb, ab, wb, kb, bb, s0, of, ob, s, acc, *, tc):
    j = pl.program_id(1)

    @pl.when(j == 0)
    def _():
        s[...] = s0[...]

    row_head = lax.broadcasted_iota(jnp.int32, (2 * LANES, 2 * LANES), 0) // RWKV_HEAD
    col_head = lax.broadcasted_iota(jnp.int32, (2 * LANES, 2 * LANES), 1) // RWKV_HEAD
    ones_bd = (row_head == col_head).astype(bf16)
    sub = lax.broadcasted_iota(jnp.int32, (RWKV_HEAD, LANES), 0)
    lane_in_head = lax.broadcasted_iota(jnp.int32, (RWKV_HEAD, LANES), 1) % RWKV_HEAD
    eye = (sub == lane_in_head).astype(bf16)
    dirs = ((rf, vf, af, wf, kf, bf_), (rb, vb, ab, wb, kb, bb))

    def row_sums(x):
        return jnp.dot(x, ones_bd, preferred_element_type=f32)

    def row(ref, t, hp, reps):
        return jnp.broadcast_to(ref[t, pl.ds(hp, 1), :], (reps, LANES))

    def sub_chunk(sc, carry):
        def step(tt, carry):
            t_f = sc * RWKV_SUB + tt
            pos = ((t_f, jnp.maximum(t_f - 1, 0), tt - 1),
                   (tc - 1 - t_f, jnp.minimum(tc - t_f, tc - 1), RWKV_SUB - tt))
            for d in range(2):
                r_ref, v_ref, a_ref, w_ref, k_ref, b_ref = dirs[d]
                t_now, t_prev, out_lane = pos[d]
                lhs, tiles = [], []
                for hp2 in range(0, HEAD_PAIRS, 2):
                    v_diag = []
                    for hp in (hp2, hp2 + 1):
                        s_t = s[d, :, hp * LANES:(hp + 1) * LANES]
                        lhs.append(jnp.concatenate([(s_t * row(a_ref, t_now, hp, RWKV_HEAD)).astype(bf16),
                                                    (s_t * row(r_ref, t_prev, hp, RWKV_HEAD)).astype(bf16)], axis=1))
                        v_diag.append(eye * jnp.concatenate([row(v_ref, t_now, hp, 16).astype(bf16)] * 4, axis=0))
                        tiles.append(s_t)
                    lhs.append(jnp.concatenate(v_diag, axis=1))
                res = row_sums(jnp.concatenate(lhs, axis=0))
                for hp, s_t in enumerate(tiles):
                    ls = slice(hp * LANES, (hp + 1) * LANES)
                    base = (hp // 2) * 3 * RWKV_HEAD
                    own = res[base + (hp % 2) * RWKV_HEAD:base + (hp % 2 + 1) * RWKV_HEAD]
                    sa, out_prev = own[:, :LANES], own[:, LANES:]
                    v_col = res[base + 2 * RWKV_HEAD:base + 3 * RWKV_HEAD, (hp % 2) * LANES:(hp % 2 + 1) * LANES]
                    s[d, :, ls] = (s_t * row(w_ref, t_now, hp, RWKV_HEAD) + sa * row(b_ref, t_now, hp, RWKV_HEAD)
                                   + v_col * row(k_ref, t_now, hp, RWKV_HEAD))
                    acc[d, :, ls] = jnp.where(lane_in_head == out_lane, out_prev, acc[d, :, ls])
            return carry

        lax.fori_loop(0, RWKV_SUB, step, 0, unroll=8)
        t_last = sc * RWKV_SUB + RWKV_SUB - 1
        last = ((t_last, RWKV_SUB - 1), (tc - 1 - t_last, 0))
        for d in range(2):
            r_ref = dirs[d][0]
            t_now, out_lane = last[d]
            out_mask = jnp.concatenate([lane_in_head == out_lane] * 2, axis=1)
            for hp in range(0, HEAD_PAIRS, 2):
                ls = slice(hp * LANES, (hp + 2) * LANES)
                r_t = jnp.concatenate([row(r_ref, t_now, hp, RWKV_HEAD), row(r_ref, t_now, hp + 1, RWKV_HEAD)], axis=1)
                out_t = row_sums((s[d, :, ls] * r_t).astype(bf16))
                acc[d, :, ls] = jnp.where(out_mask, out_t, acc[d, :, ls])
        of[sc] = acc[0]
        ob[tc // RWKV_SUB - 1 - sc] = acc[1]
        return carry

    lax.fori_loop(0, tc // RWKV_SUB, sub_chunk, 0)


def rwkv_scan_rowsum(r, v, nkk, fwd, bwd, s_f0, s_b0):
    B, L, H, N = r.shape
    D = H * N
    tc = min(L, 256)
    n = L // tc
    fwd_blk = pl.BlockSpec((None, tc, HEAD_PAIRS, LANES), lambda b, j: (b, j, 0, 0))
    bwd_blk = pl.BlockSpec((None, tc, HEAD_PAIRS, LANES), lambda b, j: (b, n - 1 - j, 0, 0))
    r, v, nkk, w_f, k_f, b_f, w_b, k_b, b_b = (t.reshape(B, L, HEAD_PAIRS, LANES) for t in (r, v, nkk, *fwd, *bwd))
    state = pl.BlockSpec((None, 2, RWKV_HEAD, D), lambda b, j: (b, 0, 0, 0))
    nsub = tc // RWKV_SUB
    block_bytes = tc * D * 4
    s0 = jnp.stack([s.transpose(0, 2, 1, 3).reshape(B, N, D) for s in (s_f0, s_b0)], axis=1)
    o_f, o_b, s_fin = pl.pallas_call(
        functools.partial(_rwkv_rowsum_kernel, tc=tc),
        grid=(B, n),
        in_specs=[fwd_blk] * 6 + [bwd_blk] * 6 + [state],
        out_specs=[pl.BlockSpec((None, nsub, RWKV_HEAD, D), lambda b, j: (b, j, 0, 0)),
                   pl.BlockSpec((None, nsub, RWKV_HEAD, D), lambda b, j: (b, n - 1 - j, 0, 0)),
                   state],
        out_shape=[jax.ShapeDtypeStruct((B, L // RWKV_SUB, RWKV_HEAD, D), f32),
                   jax.ShapeDtypeStruct((B, L // RWKV_SUB, RWKV_HEAD, D), f32),
                   jax.ShapeDtypeStruct((B, 2, RWKV_HEAD, D), f32)],
        scratch_shapes=[pltpu.VMEM((2, RWKV_HEAD, D), f32)],
        compiler_params=pltpu.CompilerParams(
            dimension_semantics=("arbitrary", "arbitrary"),
            vmem_limit_bytes=min(V7X_VMEM_BYTES, 2 * 14 * block_bytes + (8 << 20)),
        ),
        name="rwkv_scan_rowsum",
    )(r, v, nkk, w_f, k_f, b_f, r, v, nkk, w_b, k_b, b_b, s0)
    o = (o_f + o_b).reshape(B, L // RWKV_SUB, RWKV_HEAD, H, RWKV_SUB).transpose(0, 1, 4, 3, 2).reshape(B, L, H, N)
    s_fin = s_fin.reshape(B, 2, N, H, N).transpose(0, 1, 3, 2, 4)
    return o, s_fin[:, 0], s_fin[:, 1]


def rwkv_mix(h, mu, wr, wk, wv, wo, w0, w1, w2, a0, a1, a2, g1, g2, k_k, k_a, r_k, lnx_w, lnx_b, s_f0, s_b0):
    B, L, D = h.shape
    H, N = RWKV_HEADS, RWKV_HEAD
    xx = centred_shift(h) - h
    xr, xw, xk, xv, xa, xg = [h + xx * mu[j] for j in range(6)]
    r = (xr @ wr).reshape(B, L, H, N)
    k = (xk @ wk).reshape(B, L, H, N)
    v = (xv @ wv).reshape(B, L, H, N)
    g = jax.nn.sigmoid(xg @ g1) @ g2
    zw = [(w0[d] + jnp.tanh(xw @ w1[d]) @ w2[d]).reshape(B, L, H, N) for d in range(2)]
    za = [(a0[d] + (xa @ a1[d]) @ a2[d]).reshape(B, L, H, N) for d in range(2)]

    if B * H >= LANES:
        o, bonus, s_f, s_b = rwkv_scan_bidir(r, v, k, jnp.stack(zw), jnp.stack(za), k_k.reshape(H, N),
                                             k_a.reshape(H, N), r_k, s_f0, s_b0)
        bonus = bonus[..., None]
    else:
        kk = k * k_k.reshape(H, N)
        kk = kk * lax.rsqrt(jnp.sum(kk * kk, axis=-1, keepdims=True) + 1e-12)

        def direction(d):
            decay = jnp.exp(-jnp.exp(-jax.nn.softplus(-zw[d]) - 0.5))
            a = jax.nn.sigmoid(za[d])
            return decay, k * (1.0 + (a - 1.0) * k_a.reshape(H, N)), kk * a

        fwd, bwd = direction(0), direction(1)
        o, s_f, s_b = rwkv_scan_rowsum(r, v, -kk, fwd, bwd, s_f0, s_b0)
        bonus = jnp.sum(r * 0.5 * (fwd[1] + bwd[1]) * r_k, axis=-1, keepdims=True)
    m = jnp.mean(o, axis=-1, keepdims=True)
    var = jnp.mean(jnp.square(o - m), axis=-1, keepdims=True)
    o = (o - m) * lax.rsqrt(var + RWKV_LN_EPS) * lnx_w.reshape(H, N) + lnx_b.reshape(H, N)
    o = (o + bonus * v).reshape(B, L, D)
    return (o * g) @ wo, s_f, s_b


GROUP_HID = EXP_PER_GROUP * D_EXPERT


def _moe_group_kernel(tile_group_ref, n_tiles_ref, xs_ref, gates_ref, wg_ref, wu_ref, wd_ref, ys_ref,
                      wg_bf, wu_bf, wd_bf):
    i = pl.program_id(0)
    prev_group = tile_group_ref[jnp.maximum(i - 1, 0)]
    group_changed = jnp.logical_or(i == 0, tile_group_ref[i] != prev_group)

    @pl.when(group_changed)
    def _():
        for e in range(EXP_PER_GROUP):
            hs = slice(e * D_EXPERT, (e + 1) * D_EXPERT)
            wg_bf[:, hs] = wg_ref[e].astype(bf16)
            wu_bf[:, hs] = wu_ref[e].astype(bf16)
            wd_bf[hs, :] = wd_ref[e].astype(bf16)

    @pl.when(i < n_tiles_ref[0])
    def _():
        x = xs_ref[...].astype(bf16)
        g = jnp.dot(x, wg_bf[...], preferred_element_type=f32)
        u = jnp.dot(x, wu_bf[...], preferred_element_type=f32)
        gate = jnp.concatenate([jnp.broadcast_to(gates_ref[:, e:e + 1], (MOE_TILE_M, D_EXPERT))
                                for e in range(EXP_PER_GROUP)], axis=1)
        hid = (g * jax.nn.sigmoid(g)) * u * gate
        ys_ref[...] = jnp.dot(hid.astype(bf16), wd_bf[...], preferred_element_type=f32)

    @pl.when(i >= n_tiles_ref[0])
    def _():
        ys_ref[...] = jnp.zeros_like(ys_ref)


def moe_group_experts(xs, gates, tile_group, n_tiles, w_gate, w_up, w_down, layer):
    P, D = xs.shape
    max_tiles = P // MOE_TILE_M
    weight_bytes = 3 * EXP_PER_GROUP * D * D_EXPERT * (4 + 2)
    tile_bytes = 2 * MOE_TILE_M * (2 * D * 4 + LANES * 4) + 4 * MOE_TILE_M * GROUP_HID * 4
    once = pl.Buffered(1)
    grid_spec = pltpu.PrefetchScalarGridSpec(
        num_scalar_prefetch=2,
        grid=(max_tiles,),
        in_specs=[
            pl.BlockSpec((MOE_TILE_M, D), lambda i, tg, nt: (i, 0)),
            pl.BlockSpec((MOE_TILE_M, EXP_PER_GROUP), lambda i, tg, nt: (i, 0)),
            pl.BlockSpec((None, EXP_PER_GROUP, D, D_EXPERT), lambda i, tg, nt: (layer, tg[i], 0, 0),
                         pipeline_mode=once),
            pl.BlockSpec((None, EXP_PER_GROUP, D, D_EXPERT), lambda i, tg, nt: (layer, tg[i], 0, 0),
                         pipeline_mode=once),
            pl.BlockSpec((None, EXP_PER_GROUP, D_EXPERT, D), lambda i, tg, nt: (layer, tg[i], 0, 0),
                         pipeline_mode=once),
        ],
        out_specs=pl.BlockSpec((MOE_TILE_M, D), lambda i, tg, nt: (i, 0)),
        scratch_shapes=[
            pltpu.VMEM((D, GROUP_HID), bf16),
            pltpu.VMEM((D, GROUP_HID), bf16),
            pltpu.VMEM((GROUP_HID, D), bf16),
        ],
    )
    return pl.pallas_call(
        _moe_group_kernel,
        grid_spec=grid_spec,
        out_shape=jax.ShapeDtypeStruct((P, D), f32),
        compiler_params=pltpu.CompilerParams(
            dimension_semantics=("arbitrary",),
            vmem_limit_bytes=min(V7X_VMEM_BYTES - (4 << 20), weight_bytes + tile_bytes + (8 << 20)),
        ),
        name="moe_group_experts",
    )(tile_group, n_tiles, xs, gates, w_gate, w_up, w_down)


def router_weights(w_grp, w_exp):
    w = jnp.concatenate([w_grp, w_exp], axis=1)
    return jnp.pad(w, ((0, 0), (0, LANES - w.shape[1])))


def hier_moe(t, logits, b_grp, b_exp, w_gate, w_up, w_down, layer):
    T, D = t.shape
    grp_logits = logits[:, :N_GROUPS] + b_grp
    grp_prob = jax.nn.softmax(grp_logits, axis=-1)
    g_top = jnp.argmax(grp_logits, axis=-1).astype(jnp.int32)
    in_group = g_top[:, None] == jnp.arange(N_GROUPS, dtype=jnp.int32)[None, :]
    p_g = jnp.sum(jnp.where(in_group, grp_prob, 0.0), axis=1, keepdims=True)
    exp_logits = (logits[:, N_GROUPS:N_GROUPS + N_EXPERTS] + b_exp).reshape(-1, N_GROUPS, EXP_PER_GROUP)
    sel = jnp.sum(jnp.where(in_group[:, :, None], exp_logits, 0.0), axis=1)
    top_v, top_i = lax.top_k(sel, TOP_K)
    wts = p_g * jax.nn.softmax(top_v, axis=-1)
    gates = jnp.sum(jax.nn.one_hot(top_i, EXP_PER_GROUP, dtype=f32) * wts[..., None], axis=1)

    order = jnp.argsort(g_top, stable=True).astype(jnp.int32)
    grp_i32 = in_group.astype(jnp.int32)
    counts = jnp.sum(grp_i32, axis=0)
    rank = jnp.sum(jnp.where(in_group, jnp.cumsum(grp_i32, axis=0), 0), axis=1) - 1
    tiles_per = (counts + MOE_TILE_M - 1) // MOE_TILE_M
    tile_end = jnp.cumsum(tiles_per)
    start_padded = (tile_end - tiles_per) * MOE_TILE_M
    start_sorted = jnp.cumsum(counts) - counts
    max_tiles = T // MOE_TILE_M + N_GROUPS
    n_tiles = tile_end[-1:].astype(jnp.int32)
    tile_ids = jnp.arange(max_tiles, dtype=jnp.int32)
    tile_group = jnp.minimum(jnp.sum((tile_ids[:, None] >= tile_end[None, :]).astype(jnp.int32), axis=1),
                             N_GROUPS - 1).astype(jnp.int32)
    tile_group = jnp.where(tile_ids < n_tiles[0], tile_group, tile_group[jnp.maximum(n_tiles[0] - 1, 0)])
    row_group = jnp.repeat(tile_group, MOE_TILE_M)
    row_in_group = jnp.arange(max_tiles * MOE_TILE_M, dtype=jnp.int32) - start_padded[row_group]
    row_valid = (row_in_group >= 0) & (row_in_group < counts[row_group])
    row_tok = jnp.where(row_valid, order[jnp.clip(start_sorted[row_group] + row_in_group, 0, T - 1)], 0)
    pos = start_padded[g_top] + rank

    xs = jnp.take(t, row_tok, axis=0)
    row_gates = jnp.where(row_valid[:, None], jnp.take(gates, row_tok, axis=0), 0.0)
    ys = moe_group_experts(xs, row_gates, tile_group, n_tiles, w_gate, w_up, w_down, layer)
    return jnp.take(ys, pos, axis=0)


def kernel(x_prompt, x_sample, c, cache_attn_k, cache_attn_v, state_gla, state_rwkv, c_ctx, ada_w, ada_b, norm_mix, norm_ffn, norm_out, ev_w_in, ev_w_out, gla_dec_w, gla_dec_b, gla_norm, att_sink, rw_mu, rw_wr, rw_wk, rw_wv, rw_wo, rw_w0, rw_w1, rw_w2, rw_a0, rw_a1, rw_a2, rw_g1, rw_g2, rw_kk, rw_ka, rw_rk, rw_lnx_w, rw_lnx_b, moe_w_grp, moe_b_grp, moe_w_exp, moe_b_exp, moe_w_gate, moe_w_up, moe_w_down):
    n_lat = x_sample.shape[1]
    rows = n_lat // GRID_W
    row_pos = jnp.repeat(jnp.arange(rows), GRID_W)
    col_pos = jnp.tile(jnp.arange(GRID_W), rows)
    rope = rope_tables(row_pos, col_pos)
    Bc, Lc, D = x_prompt.shape
    Bl, Ll, _ = x_sample.shape

    n_ctx = Bc * Lc
    x_all = jnp.concatenate([x_prompt.reshape(n_ctx, D), x_sample.reshape(Bl * Ll, D)], axis=0)
    cond_all = jnp.concatenate([c_ctx[None, :], c], axis=0)
    mods = [ada_mod(cond_all, ada_w[l], ada_b[l]) for l in range(DEPTH)]
    no_mod = jnp.zeros((1 + Bl, 1, D), f32)

    def split(t):
        return t[:n_ctx].reshape(Bc, Lc, D), t[n_ctx:].reshape(Bl, Ll, D)

    hc = modulate(rmsnorm(x_prompt, norm_mix[0]), mods[0][0][:1], mods[0][1][:1])
    hl = modulate(rmsnorm(x_sample, norm_mix[0]), mods[0][0][1:], mods[0][1][1:])
    ks_out, vs_out, gla_out, rwkv_out = [], [], [], []
    for l in range(DEPTH):
        _, _, gate1, shift2, scale2, gate2 = mods[l]
        i = l // 2
        if l % 2 == 0:
            oc, k_c, v_c, s_gla = even_mixer_ctx(hc, ev_w_in[i], ev_w_out[i], gla_dec_w[i], gla_dec_b[i],
                                                 gla_norm[i], att_sink[i])
            ol = even_mixer_lat(hl, ev_w_in[i], ev_w_out[i], gla_dec_w[i], gla_dec_b[i], gla_norm[i], att_sink[i],
                                rope, cache_attn_k[:, i], cache_attn_v[:, i], state_gla[:, i])
            ks_out.append(k_c)
            vs_out.append(v_c)
            gla_out.append(s_gla)
        else:
            rw = (rw_mu[i], rw_wr[i], rw_wk[i], rw_wv[i], rw_wo[i], rw_w0[i], rw_w1[i], rw_w2[i], rw_a0[i],
                  rw_a1[i], rw_a2[i], rw_g1[i], rw_g2[i], rw_kk[i], rw_ka[i], rw_rk[i], rw_lnx_w[i], rw_lnx_b[i])
            zero = jnp.zeros((Bc, RWKV_HEADS, RWKV_HEAD, RWKV_HEAD), f32)
            oc, s_f, s_b = rwkv_mix(hc, *rw, zero, zero)
            ol, _, _ = rwkv_mix(hl, *rw, state_rwkv[:, i, 0], state_rwkv[:, i, 1])
            rwkv_out.append(jnp.stack([s_f, s_b], axis=1))
        x_all, t, logits = resid_norm_mod(x_all, (oc.reshape(n_ctx, D), ol.reshape(Bl * Ll, D)), n_ctx, Ll,
                                          gate1, norm_ffn[l], shift2, scale2,
                                          router_weights(moe_w_grp[l], moe_w_exp[l]))
        y = hier_moe(t, logits, moe_b_grp[l], moe_b_exp[l], moe_w_gate, moe_w_up, moe_w_down, l)
        if l + 1 < DEPTH:
            x_all, h_all = resid_norm_mod(x_all, y, n_ctx, Ll, gate2, norm_mix[l + 1], mods[l + 1][0], mods[l + 1][1])
            hc, hl = split(h_all)
        else:
            x_all, y_all = resid_norm_mod(x_all, y, n_ctx, Ll, gate2, norm_out, no_mod, no_mod)
            y_prompt, y_sample = split(y_all)

    new_attn_k = jnp.stack(ks_out, axis=1)
    new_attn_v = jnp.stack(vs_out, axis=1)
    new_gla = jnp.stack(gla_out, axis=1)
    new_rwkv = jnp.stack(rwkv_out, axis=1)
    return (y_prompt, y_sample, new_attn_k, new_attn_v, new_gla, new_rwkv)
```

```python
import functools

import jax
import jax.numpy as jnp
import numpy as np
from jax import lax
from jax.experimental import pallas as pl
from jax.experimental.pallas import tpu as pltpu

D_MODEL = 1024
DEPTH = 4
GRID_W = 64
EPS = 1e-6
GLA_HEADS = 4
GLA_DK = 64
GLA_DV = 128
GLA_LOWRANK = 16
GLA_GATE_TEMP = 16.0
GLA_CHUNK = 64
ATT_HEADS = 8
ATT_KV_HEADS = 2
ATT_GROUP = ATT_HEADS // ATT_KV_HEADS
ATT_HD = 64
WINDOW = 128
ATT_BLOCK = 128
ROPE_BASE = 10000.0
ROPE_AXIS_DIMS = ATT_HD // 2
NEG = -1e30
IN_WIDTHS = (GLA_HEADS * GLA_DK, GLA_HEADS * GLA_DK, GLA_HEADS * GLA_DV, GLA_HEADS * GLA_DV,
             ATT_HEADS * ATT_HD, ATT_KV_HEADS * ATT_HD, ATT_KV_HEADS * ATT_HD, 2 * GLA_LOWRANK)
IN_SPLITS = tuple(int(v) for v in np.cumsum(IN_WIDTHS)[:-1])
RWKV_HEAD = 64
RWKV_HEADS = D_MODEL // RWKV_HEAD
RWKV_LN_EPS = 64e-5
N_GROUPS = 4
EXP_PER_GROUP = 4
N_EXPERTS = N_GROUPS * EXP_PER_GROUP
TOP_K = 2
D_EXPERT = 512

V7X_VMEM_BYTES = 64 * 1024 * 1024
MOE_TILE_M = 256

f32 = jnp.float32
bf16 = jnp.bfloat16


def rmsnorm(x, w):
    y = x * lax.rsqrt(jnp.mean(x * x, axis=-1, keepdims=True) + EPS)
    return y * w


def ada_mod(cond, w, b):
    mod = jax.nn.silu(cond) @ w + b
    return [m[:, None, :] for m in jnp.split(mod, 6, axis=-1)]


def modulate(x, shift, scale):
    return x * (1.0 + scale) + shift


NORM_ROWS = 256


def _resid_norm_mod_kernel(*refs, n_ctx_blocks, split):
    is_ctx = pl.program_id(0) < n_ctx_blocks
    if split:
        x_ref, oc_ref, ol_ref, gate_ref, w_ref, shift_ref, scale_ref, xn_ref, t_ref = refs
        o = jnp.where(is_ctx, oc_ref[...], ol_ref[...])
    else:
        x_ref, o_ref, gate_ref, w_ref, shift_ref, scale_ref, xn_ref, tc_ref, tl_ref = refs
        o = o_ref[...]
    x = x_ref[...] + gate_ref[0] * o
    xn_ref[...] = x
    y = x * lax.rsqrt(jnp.mean(x * x, axis=-1, keepdims=True) + EPS) * w_ref[...]
    t = y * (1.0 + scale_ref[0]) + shift_ref[0]
    if split:
        t_ref[...] = t
    else:
        @pl.when(is_ctx)
        def _():
            tc_ref[...] = t

        @pl.when(jnp.logical_not(is_ctx))
        def _():
            tl_ref[...] = t


def resid_norm_mod(x, o, n_ctx, lat_len, gate, w, shift, scale):
    T, D = x.shape
    ncb = n_ctx // NORM_ROWS
    per_lat = lat_len // NORM_ROWS
    rows = pl.BlockSpec((NORM_ROWS, D), lambda i: (i, 0))
    mod = pl.BlockSpec((1, 1, D), lambda i: (jnp.where(i < ncb, 0, 1 + (i - ncb) // per_lat), 0, 0))
    pair = [pl.BlockSpec((NORM_ROWS, D), lambda i: (jnp.minimum(i, ncb - 1), 0)),
            pl.BlockSpec((NORM_ROWS, D), lambda i: (jnp.maximum(i - ncb, 0), 0))]
    split = isinstance(o, tuple)
    if split:
        o_specs, o_args, t_specs = pair, list(o), [rows]
        t_shapes = [jax.ShapeDtypeStruct((T, D), f32)]
    else:
        o_specs, o_args, t_specs = [rows], [o], pair
        t_shapes = [jax.ShapeDtypeStruct((n_ctx, D), f32), jax.ShapeDtypeStruct((T - n_ctx, D), f32)]
    out = pl.pallas_call(
        functools.partial(_resid_norm_mod_kernel, n_ctx_blocks=ncb, split=split),
        grid=(T // NORM_ROWS,),
        in_specs=[rows, *o_specs, mod, pl.BlockSpec((1, D), lambda i: (0, 0)), mod, mod],
        out_specs=[rows, *t_specs],
        out_shape=[jax.ShapeDtypeStruct((T, D), f32), *t_shapes],
        compiler_params=pltpu.CompilerParams(dimension_semantics=("arbitrary",)),
        name="resid_norm_mod",
    )(x, *o_args, gate, w.reshape(1, D), shift, scale)
    return (out[0], out[1]) if split else (out[0], (out[1], out[2]))


def rope_tables(row_pos, col_pos):
    freqs = ROPE_BASE ** (-jnp.arange(0, ROPE_AXIS_DIMS, 2, dtype=f32) / ROPE_AXIS_DIMS)
    ang_r = row_pos.astype(f32)[:, None] * freqs[None, :]
    ang_c = col_pos.astype(f32)[:, None] * freqs[None, :]
    return jnp.cos(ang_r), jnp.sin(ang_r), jnp.cos(ang_c), jnp.sin(ang_c)


def rotate(x, cos, sin):
    x1, x2 = jnp.split(x, 2, axis=-1)
    cos = cos[None, :, None, :]
    sin = sin[None, :, None, :]
    return jnp.concatenate([x1 * cos - x2 * sin, x1 * sin + x2 * cos], axis=-1)


def apply_axial_rope(x, tables):
    cr, sr, cc, sc = tables
    xr, xc = jnp.split(x, 2, axis=-1)
    return jnp.concatenate([rotate(xr, cr, sr), rotate(xc, cc, sc)], axis=-1)


GLA_QK = GLA_HEADS * GLA_DK
GLA_V = GLA_HEADS * GLA_DV


def _gla_kernel(q_ref, v_ref, gg_ref, ldf_ref, ldb_ref, kt_ref, ldft_ref, ldbt_ref, s0_ref, norm_ref,
                o_ref, s_ref, *, seq_len):
    n = seq_len // GLA_CHUNK
    C = GLA_CHUNK
    ti = lax.broadcasted_iota(jnp.int32, (C, C), 0)
    tj = lax.broadcasted_iota(jnp.int32, (C, C), 1)
    keep = (tj <= ti, tj >= ti)
    tri = tuple(m.astype(bf16) for m in keep)
    tri_t = (tri[1], tri[0])
    ones = jnp.ones((C, GLA_DV), bf16)
    s_ref[...] = s0_ref[...]
    o_ref[...] = jnp.zeros_like(o_ref)
    ld_refs = ((ldf_ref, ldft_ref), (ldb_ref, ldbt_ref))

    def split3(x):
        hi = x.astype(bf16)
        rest = x - hi.astype(f32)
        mid = rest.astype(bf16)
        return hi, mid, (rest - mid.astype(f32)).astype(bf16)

    def sums(a, b):
        if isinstance(b, tuple):
            return sum(jnp.dot(a, p, preferred_element_type=f32) for p in b)
        return sum(jnp.dot(p, b, preferred_element_type=f32) for p in a)

    def chunk(c, carry):
        pending = []
        for d in range(2):
            cc = c if d == 0 else n - 1 - c
            rows = pl.ds(pl.multiple_of(cc * C, C), C)
            ld_ref, ldt_ref = ld_refs[d]
            ld_t3 = split3(ldt_ref[cc])
            b_all = sums(tri[d], split3(ld_ref[rows, :]))
            bt_all = sums(ld_t3, tri_t[d])
            total_all = sums(ld_t3, ones)
            for h in range(GLA_HEADS):
                ks = slice(h * GLA_DK, (h + 1) * GLA_DK)
                vs = slice(h * GLA_DV, (h + 1) * GLA_DV)
                q = q_ref[rows, ks] * (GLA_DK ** -0.5)
                v = v_ref[rows, vs].astype(bf16)
                k_t = kt_ref[cc, ks, :]
                b, b_t, total = b_all[:, ks], bt_all[ks], total_all[ks]
                q_e = (q * jnp.exp(b)).astype(bf16)
                k_e = (k_t * jnp.exp(-b_t)).astype(bf16)
                k_s = (k_t * jnp.exp(total[:, :C] - b_t)).astype(bf16)
                att = jnp.where(keep[d], jnp.dot(q_e, k_e, preferred_element_type=f32), 0.0).astype(bf16)
                s = s_ref[d, h]
                o = (jnp.dot(q_e, s.astype(bf16), preferred_element_type=f32)
                     + jnp.dot(att, v, preferred_element_type=f32))
                s_new = s * jnp.exp(total) + jnp.dot(k_s, v, preferred_element_type=f32)
                pending.append((d, h, rows, vs, o_ref[rows, vs] + o, s_new))
        for d, h, rows, vs, o, s_new in pending:
            o_ref[rows, vs] = o
            s_ref[d, h] = s_new
        return carry

    lax.fori_loop(0, n, chunk, 0)
    for h in range(GLA_HEADS):
        vs = slice(h * GLA_DV, (h + 1) * GLA_DV)
        x = o_ref[:, vs]
        y = x * lax.rsqrt(jnp.mean(x * x, axis=-1, keepdims=True) + EPS) * norm_ref[...]
        g = gg_ref[:, vs]
        o_ref[:, vs] = y * (g * jax.nn.sigmoid(g))


def gla_bidir_gated(z, ld_f, ld_b, s0, gla_norm):
    B, L, _ = z.shape
    n = L // GLA_CHUNK
    assert n % 2 == 0

    def per_chunk_t(a):
        return a.reshape(B, n, GLA_CHUNK, GLA_QK).transpose(0, 1, 3, 2)

    state = pl.BlockSpec((None, 2, GLA_HEADS, GLA_DK, GLA_DV), lambda b: (b, 0, 0, 0, 0))
    ld = pl.BlockSpec((None, L, GLA_QK), lambda b: (b, 0, 0))
    ld_t = pl.BlockSpec((None, n, GLA_QK, GLA_CHUNK), lambda b: (b, 0, 0, 0))
    return pl.pallas_call(
        functools.partial(_gla_kernel, seq_len=L),
        grid=(B,),
        in_specs=[pl.BlockSpec((None, L, GLA_QK), lambda b: (b, 0, 0)),
                  pl.BlockSpec((None, L, GLA_V), lambda b: (b, 0, 1)),
                  pl.BlockSpec((None, L, GLA_V), lambda b: (b, 0, 2)),
                  ld, ld, ld_t, ld_t, ld_t, state,
                  pl.BlockSpec((1, GLA_DV), lambda b: (0, 0))],
        out_specs=[pl.BlockSpec((None, L, GLA_V), lambda b: (b, 0, 0)), state],
        out_shape=[jax.ShapeDtypeStruct((B, L, GLA_V), f32),
                   jax.ShapeDtypeStruct((B, 2, GLA_HEADS, GLA_DK, GLA_DV), f32)],
        compiler_params=pltpu.CompilerParams(
            dimension_semantics=("arbitrary",),
            vmem_limit_bytes=min(V7X_VMEM_BYTES, 2 * L * (9 * GLA_QK + 3 * GLA_V) * 4 + (16 << 20)),
        ),
        name="gla_bidir",
    )(z, z, z, ld_f, ld_b, per_chunk_t(z[..., GLA_QK:2 * GLA_QK]), per_chunk_t(ld_f), per_chunk_t(ld_b),
      s0, gla_norm.reshape(1, GLA_DV))


ATT_Q = ATT_HEADS * ATT_HD
ATT_KV = ATT_KV_HEADS * ATT_HD


def _window_attn_kernel(q_ref, k_ref, v_ref, kc_ref, vc_ref, sink_ref, o_ref, *, tq, seq_len):
    scale = ATT_HD ** -0.5
    nt = (((1,), (1,)), ((), ()))
    start = pl.multiple_of(pl.program_id(1) * tq, tq)
    k_own = k_ref[pl.ds(start, 3 * tq), :].astype(bf16)
    v_own = v_ref[pl.ds(start, 3 * tq), :].astype(bf16)
    qpos = start + lax.broadcasted_iota(jnp.int32, (tq, 3 * tq), 0)
    kpos = start - tq + lax.broadcasted_iota(jnp.int32, (tq, 3 * tq), 1)
    valid = (jnp.abs(qpos - kpos) <= WINDOW) & (kpos >= 0) & (kpos < seq_len)
    for h in range(ATT_HEADS):
        g = h // ATT_GROUP
        kv = slice(g * ATT_HD, (g + 1) * ATT_HD)
        q = q_ref[:, h * ATT_HD:(h + 1) * ATT_HD].astype(bf16)
        s_own = lax.dot_general(q, k_own[:, kv], nt, preferred_element_type=f32) * scale
        s_own = jnp.where(valid, s_own, NEG)
        s_ctx = lax.dot_general(q, kc_ref[:, kv].astype(bf16), nt, preferred_element_type=f32) * scale
        sink = sink_ref[0:1, h:h + 1]
        m = jnp.maximum(jnp.maximum(jnp.max(s_own, axis=-1, keepdims=True), jnp.max(s_ctx, axis=-1, keepdims=True)),
                        sink)
        p_own = jnp.exp(s_own - m)
        p_ctx = jnp.exp(s_ctx - m)
        denom = (jnp.sum(p_own, axis=-1, keepdims=True) + jnp.sum(p_ctx, axis=-1, keepdims=True)
                 + jnp.exp(sink - m))
        o = (jnp.dot(p_own.astype(bf16), v_own[:, kv], preferred_element_type=f32)
             + jnp.dot(p_ctx.astype(bf16), vc_ref[:, kv].astype(bf16), preferred_element_type=f32))
        o_ref[:, h * ATT_HD:(h + 1) * ATT_HD] = o / denom


def ctx_attn(q, k, v, sink):
    B, L = q.shape[:2]
    scale = ATT_HD ** -0.5
    qg = q.reshape(B, L, ATT_KV_HEADS, ATT_GROUP, ATT_HD)
    sink_b = sink.reshape(ATT_KV_HEADS, ATT_GROUP)[None, :, :, None, None]
    s = jnp.einsum('bqhgd,bkhd->bhgqk', qg, k) * scale
    s_sink = jnp.broadcast_to(sink_b, s.shape[:-1] + (1,))
    p = jax.nn.softmax(jnp.concatenate([s_sink, s], axis=-1), axis=-1)[..., 1:]
    return jnp.einsum('bhgqk,bkhd->bqhgd', p, v).reshape(B, L, ATT_HEADS * ATT_HD)


def window_attn_latent(q, k, v, k_ctx, v_ctx, sink):
    B, S, _ = q.shape
    Lc = k_ctx.shape[1]
    nb = S // ATT_BLOCK
    pad = ((0, 0), (ATT_BLOCK, ATT_BLOCK), (0, 0))
    whole = lambda n: pl.BlockSpec((None, n, ATT_KV), lambda b, i: (b, 0, 0))
    return pl.pallas_call(
        functools.partial(_window_attn_kernel, tq=ATT_BLOCK, seq_len=S),
        grid=(B, nb),
        in_specs=[pl.BlockSpec((None, ATT_BLOCK, ATT_Q), lambda b, i: (b, i, 0)),
                  whole(S + 2 * ATT_BLOCK), whole(S + 2 * ATT_BLOCK), whole(Lc), whole(Lc),
                  pl.BlockSpec((1, ATT_HEADS), lambda b, i: (0, 0))],
        out_specs=pl.BlockSpec((None, ATT_BLOCK, ATT_Q), lambda b, i: (b, i, 0)),
        out_shape=jax.ShapeDtypeStruct((B, S, ATT_Q), f32),
        compiler_params=pltpu.CompilerParams(dimension_semantics=("arbitrary", "arbitrary")),
        name="window_attn",
    )(q, jnp.pad(k, pad), jnp.pad(v, pad), k_ctx, v_ctx, sink.reshape(1, ATT_HEADS))


def even_projections(h, w_in, dec_w, dec_b):
    B, L, _ = h.shape
    z = h @ w_in
    aq, ak, av, lr = jnp.split(z[..., IN_SPLITS[3]:], [s - IN_SPLITS[3] for s in IN_SPLITS[4:]], axis=-1)
    lr_f, lr_b = jnp.split(lr, 2, axis=-1)

    def logdecay(lr_d, w, b):
        return jax.nn.log_sigmoid(lr_d @ w + b) / GLA_GATE_TEMP

    ld_f = logdecay(lr_f, dec_w[0], dec_b[0])
    ld_b = logdecay(lr_b, dec_w[1], dec_b[1])
    aq = aq.reshape(B, L, ATT_HEADS, ATT_HD)
    ak = ak.reshape(B, L, ATT_KV_HEADS, ATT_HD)
    av = av.reshape(B, L, ATT_KV_HEADS, ATT_HD)
    return z, ld_f, ld_b, aq, ak, av


def even_mixer_ctx(h, w_in, w_out, dec_w, dec_b, gla_norm, sink):
    z, ld_f, ld_b, aq, ak, av = even_projections(h, w_in, dec_w, dec_b)
    zero = jnp.zeros((h.shape[0], 2, GLA_HEADS, GLA_DK, GLA_DV), f32)
    o_gla, s_fin = gla_bidir_gated(z, ld_f, ld_b, zero, gla_norm)
    o_att = ctx_attn(aq, ak, av, sink)
    return jnp.concatenate([o_gla, o_att], axis=-1) @ w_out, ak, av, s_fin


def even_mixer_lat(h, w_in, w_out, dec_w, dec_b, gla_norm, sink, rope, k_ctx, v_ctx, s0):
    z, ld_f, ld_b, aq, ak, av = even_projections(h, w_in, dec_w, dec_b)
    o_gla, _ = gla_bidir_gated(z, ld_f, ld_b, s0, gla_norm)
    B, S = h.shape[:2]
    o_att = window_attn_latent(apply_axial_rope(aq, rope).reshape(B, S, ATT_Q),
                               apply_axial_rope(ak, rope).reshape(B, S, ATT_KV), av.reshape(B, S, ATT_KV),
                               k_ctx.reshape(B, -1, ATT_KV), v_ctx.reshape(B, -1, ATT_KV), sink)
    return jnp.concatenate([o_gla, o_att], axis=-1) @ w_out


def centred_shift(x):
    xp = jnp.pad(x, ((0, 0), (1, 1), (0, 0)))
    return 0.5 * (xp[:, :-2] + xp[:, 2:])


LANES = 128
SUBLANES = 8
RWKV_TC = 32


def _rwkv_scan_kernel(r_ref, v_ref, k_ref, zw_ref, za_ref, kk_ref, ka_ref, rk_ref, s0_ref, o_ref, bonus_ref, s_ref,
                      *, tc, nv, n_dir_groups):
    backward = pl.program_id(0) >= n_dir_groups

    @pl.when(pl.program_id(1) == 0)
    def _():
        s_ref[...] = s0_ref[...]

    def step(i, carry):
        t = jnp.where(backward, tc - 1 - i, i)
        k_t, r = k_ref[t], r_ref[t]
        kk = k_t * kk_ref[0]
        kk = kk * lax.rsqrt(jnp.sum(kk * kk, axis=0, keepdims=True) + 1e-12)
        a = jax.nn.sigmoid(za_ref[t])
        z = -zw_ref[t]
        softplus = jnp.maximum(z, 0.0) + jnp.log1p(jnp.exp(-jnp.abs(z)))
        w = jnp.exp(-jnp.exp(-softplus - 0.5))
        alpha, beta = -kk, kk * a
        kd = k_t * (1.0 + (a - 1.0) * ka_ref[0])
        bonus_ref[pl.ds(t, 1), :] = jnp.sum(r * 0.5 * kd * rk_ref[0], axis=0, keepdims=True)
        for vb in range(nv // SUBLANES):
            outs = []
            for v in range(vb * SUBLANES, (vb + 1) * SUBLANES):
                s = s_ref[v]
                sa = jnp.sum(s * alpha, axis=0, keepdims=True)
                s_new = s * w + sa * beta + v_ref[t, pl.ds(v, 1), :] * kd
                s_ref[v] = s_new
                outs.append(jnp.sum(s_new * r, axis=0, keepdims=True))
            o_ref[t, vb * SUBLANES:(vb + 1) * SUBLANES, :] = jnp.concatenate(outs, axis=0)
        return carry

    lax.fori_loop(0, tc, step, 0)


def rwkv_scan_lanes(r, v, k, zw, za, k_k, k_a, r_k, s0):
    gd, L, nv, _ = v.shape
    tc = RWKV_TC
    n = L // tc

    def time_block(g, j):
        return jnp.where(g >= gd, n - 1 - j, j)

    def blk(rows, shared):
        return pl.BlockSpec((None, tc, rows, LANES), lambda g, j: (g % gd if shared else g, time_block(g, j), 0, 0))

    const = pl.BlockSpec((None, 1, RWKV_HEAD, LANES), lambda g, j: (g % gd, 0, 0, 0))
    state = pl.BlockSpec((None, nv, RWKV_HEAD, LANES), lambda g, j: (g, 0, 0, 0))
    block_bytes = tc * RWKV_HEAD * LANES * 4
    return pl.pallas_call(
        functools.partial(_rwkv_scan_kernel, tc=tc, nv=nv, n_dir_groups=gd),
        grid=(2 * gd, n),
        in_specs=[blk(RWKV_HEAD, True), blk(nv, True), blk(RWKV_HEAD, True),
                  blk(RWKV_HEAD, False), blk(RWKV_HEAD, False), const, const, const, state],
        out_specs=[blk(nv, False), pl.BlockSpec((None, tc, LANES), lambda g, j: (g, time_block(g, j), 0)), state],
        out_shape=[jax.ShapeDtypeStruct((2 * gd, L, nv, LANES), f32),
                   jax.ShapeDtypeStruct((2 * gd, L, LANES), f32),
                   jax.ShapeDtypeStruct((2 * gd, nv, RWKV_HEAD, LANES), f32)],
        compiler_params=pltpu.CompilerParams(
            dimension_semantics=("arbitrary", "arbitrary"),
            vmem_limit_bytes=2 * 7 * block_bytes + 4 * nv * RWKV_HEAD * LANES * 4 + (8 << 20),
        ),
        name="rwkv_scan",
    )(r, v, k, zw, za, k_k, k_a, r_k, s0)


def rwkv_scan_bidir(r, v, k, zw, za, k_k, k_a, r_k, s_f0, s_b0):
    B, L, H, N = r.shape
    S = B * H
    gd = S // LANES

    def rows(x):
        T = x.shape[-3]
        lead = x.shape[:-4]
        x = x.reshape(-1, B, T, H, N).transpose(0, 2, 4, 1, 3).reshape(-1, T, N, gd, LANES)
        return x.transpose(0, 3, 1, 2, 4).reshape(-1, T, N, LANES) if lead else x[0].transpose(2, 0, 1, 3)

    def rows_inv(y):
        return y.transpose(1, 2, 0, 3).reshape(y.shape[1], N, B, H)

    def state_in(s):
        return rows(s.transpose(0, 3, 1, 2)).transpose(0, 2, 1, 3)

    def state_out(s):
        return rows_inv(s.transpose(0, 2, 1, 3)).transpose(2, 3, 1, 0)

    def const_rows(c):
        return rows(jnp.broadcast_to(c[None, None], (B, 1, H, N)))

    out, bonus, s_fin = rwkv_scan_lanes(rows(r), rows(v), rows(k), rows(zw), rows(za),
                                        const_rows(k_k), const_rows(k_a), const_rows(r_k),
                                        jnp.concatenate([state_in(s_f0), state_in(s_b0)], axis=0))
    o = rows_inv(out[:gd]) + rows_inv(out[gd:])
    bonus = (bonus[:gd] + bonus[gd:]).transpose(1, 0, 2).reshape(L, B, H).transpose(1, 0, 2)
    return o.transpose(2, 0, 3, 1), bonus, state_out(s_fin[:gd]), state_out(s_fin[gd:])


RWKV_SUB = 64
HEAD_PAIRS = D_MODEL // LANES


def _rwkv_rowsum_kernel(rf, vf, af, wf, kf, bf_, rb, vb, ab, wb, kb, bb, s0, of, ob, s, acc, *, tc):
    j = pl.program_id(1)

    @pl.when(j == 0)
    def _():
        s[...] = s0[...]

    row_head = lax.broadcasted_iota(jnp.int32, (2 * LANES, 2 * LANES), 0) // RWKV_HEAD
    col_head = lax.broadcasted_iota(jnp.int32, (2 * LANES, 2 * LANES), 1) // RWKV_HEAD
    ones_bd = (row_head == col_head).astype(bf16)
    sub = lax.broadcasted_iota(jnp.int32, (RWKV_HEAD, LANES), 0)
    lane_in_head = lax.broadcasted_iota(jnp.int32, (RWKV_HEAD, LANES), 1) % RWKV_HEAD
    eye = (sub == lane_in_head).astype(bf16)
    dirs = ((rf, vf, af, wf, kf, bf_), (rb, vb, ab, wb, kb, bb))

    def row_sums(x):
        return jnp.dot(x, ones_bd, preferred_element_type=f32)

    def row(ref, t, hp, reps):
        return jnp.broadcast_to(ref[t, pl.ds(hp, 1), :], (reps, LANES))

    def sub_chunk(sc, carry):
        def step(tt, carry):
            t_f = sc * RWKV_SUB + tt
            pos = ((t_f, jnp.maximum(t_f - 1, 0), tt - 1),
                   (tc - 1 - t_f, jnp.minimum(tc - t_f, tc - 1), RWKV_SUB - tt))
            for d in range(2):
                r_ref, v_ref, a_ref, w_ref, k_ref, b_ref = dirs[d]
                t_now, t_prev, out_lane = pos[d]
                lhs, tiles = [], []
                for hp2 in range(0, HEAD_PAIRS, 2):
                    v_diag = []
                    for hp in (hp2, hp2 + 1):
                        s_t = s[d, :, hp * LANES:(hp + 1) * LANES]
                        lhs.append(jnp.concatenate([(s_t * row(a_ref, t_now, hp, RWKV_HEAD)).astype(bf16),
                                                    (s_t * row(r_ref, t_prev, hp, RWKV_HEAD)).astype(bf16)], axis=1))
                        v_diag.append(eye * jnp.concatenate([row(v_ref, t_now, hp, 16).astype(bf16)] * 4, axis=0))
                        tiles.append(s_t)
                    lhs.append(jnp.concatenate(v_diag, axis=1))
                res = row_sums(jnp.concatenate(lhs, axis=0))
                for hp, s_t in enumerate(tiles):
                    ls = slice(hp * LANES, (hp + 1) * LANES)
                    base = (hp // 2) * 3 * RWKV_HEAD
                    own = res[base + (hp % 2) * RWKV_HEAD:base + (hp % 2 + 1) * RWKV_HEAD]
                    sa, out_prev = own[:, :LANES], own[:, LANES:]
                    v_col = res[base + 2 * RWKV_HEAD:base + 3 * RWKV_HEAD, (hp % 2) * LANES:(hp % 2 + 1) * LANES]
                    s[d, :, ls] = (s_t * row(w_ref, t_now, hp, RWKV_HEAD) + sa * row(b_ref, t_now, hp, RWKV_HEAD)
                                   + v_col * row(k_ref, t_now, hp, RWKV_HEAD))
                    acc[d, :, ls] = jnp.where(lane_in_head == out_lane, out_prev, acc[d, :, ls])
            return carry

        lax.fori_loop(0, RWKV_SUB, step, 0, unroll=8)
        t_last = sc * RWKV_SUB + RWKV_SUB - 1
        last = ((t_last, RWKV_SUB - 1), (tc - 1 - t_last, 0))
        for d in range(2):
            r_ref = dirs[d][0]
            t_now, out_lane = last[d]
            out_mask = jnp.concatenate([lane_in_head == out_lane] * 2, axis=1)
            for hp in range(0, HEAD_PAIRS, 2):
                ls = slice(hp * LANES, (hp + 2) * LANES)
                r_t = jnp.concatenate([row(r_ref, t_now, hp, RWKV_HEAD), row(r_ref, t_now, hp + 1, RWKV_HEAD)], axis=1)
                out_t = row_sums((s[d, :, ls] * r_t).astype(bf16))
                acc[d, :, ls] = jnp.where(out_mask, out_t, acc[d, :, ls])
        of[sc] = acc[0]
        ob[tc // RWKV_SUB - 1 - sc] = acc[1]
        return carry

    lax.fori_loop(0, tc // RWKV_SUB, sub_chunk, 0)


def rwkv_scan_rowsum(r, v, nkk, fwd, bwd, s_f0, s_b0):
    B, L, H, N = r.shape
    D = H * N
    tc = min(L, 256)
    n = L // tc
    fwd_blk = pl.BlockSpec((None, tc, HEAD_PAIRS, LANES), lambda b, j: (b, j, 0, 0))
    bwd_blk = pl.BlockSpec((None, tc, HEAD_PAIRS, LANES), lambda b, j: (b, n - 1 - j, 0, 0))
    r, v, nkk, w_f, k_f, b_f, w_b, k_b, b_b = (t.reshape(B, L, HEAD_PAIRS, LANES) for t in (r, v, nkk, *fwd, *bwd))
    state = pl.BlockSpec((None, 2, RWKV_HEAD, D), lambda b, j: (b, 0, 0, 0))
    nsub = tc // RWKV_SUB
    block_bytes = tc * D * 4
    s0 = jnp.stack([s.transpose(0, 2, 1, 3).reshape(B, N, D) for s in (s_f0, s_b0)], axis=1)
    o_f, o_b, s_fin = pl.pallas_call(
        functools.partial(_rwkv_rowsum_kernel, tc=tc),
        grid=(B, n),
        in_specs=[fwd_blk] * 6 + [bwd_blk] * 6 + [state],
        out_specs=[pl.BlockSpec((None, nsub, RWKV_HEAD, D), lambda b, j: (b, j, 0, 0)),
                   pl.BlockSpec((None, nsub, RWKV_HEAD, D), lambda b, j: (b, n - 1 - j, 0, 0)),
                   state],
        out_shape=[jax.ShapeDtypeStruct((B, L // RWKV_SUB, RWKV_HEAD, D), f32),
                   jax.ShapeDtypeStruct((B, L // RWKV_SUB, RWKV_HEAD, D), f32),
                   jax.ShapeDtypeStruct((B, 2, RWKV_HEAD, D), f32)],
        scratch_shapes=[pltpu.VMEM((2, RWKV_HEAD, D), f32)],
        compiler_params=pltpu.CompilerParams(
            dimension_semantics=("arbitrary", "arbitrary"),
            vmem_limit_bytes=min(V7X_VMEM_BYTES, 2 * 14 * block_bytes + (8 << 20)),
        ),
        name="rwkv_scan_rowsum",
    )(r, v, nkk, w_f, k_f, b_f, r, v, nkk, w_b, k_b, b_b, s0)
    o = (o_f + o_b).reshape(B, L // RWKV_SUB, RWKV_HEAD, H, RWKV_SUB).transpose(0, 1, 4, 3, 2).reshape(B, L, H, N)
    s_fin = s_fin.reshape(B, 2, N, H, N).transpose(0, 1, 3, 2, 4)
    return o, s_fin[:, 0], s_fin[:, 1]


def rwkv_mix(h, mu, wr, wk, wv, wo, w0, w1, w2, a0, a1, a2, g1, g2, k_k, k_a, r_k, lnx_w, lnx_b, s_f0, s_b0):
    B, L, D = h.shape
    H, N = RWKV_HEADS, RWKV_HEAD
    xx = centred_shift(h) - h
    xr, xw, xk, xv, xa, xg = [h + xx * mu[j] for j in range(6)]
    r = (xr @ wr).reshape(B, L, H, N)
    k = (xk @ wk).reshape(B, L, H, N)
    v = (xv @ wv).reshape(B, L, H, N)
    g = jax.nn.sigmoid(xg @ g1) @ g2
    zw = [(w0[d] + jnp.tanh(xw @ w1[d]) @ w2[d]).reshape(B, L, H, N) for d in range(2)]
    za = [(a0[d] + (xa @ a1[d]) @ a2[d]).reshape(B, L, H, N) for d in range(2)]

    if B * H >= LANES:
        o, bonus, s_f, s_b = rwkv_scan_bidir(r, v, k, jnp.stack(zw), jnp.stack(za), k_k.reshape(H, N),
                                             k_a.reshape(H, N), r_k, s_f0, s_b0)
        bonus = bonus[..., None]
    else:
        kk = k * k_k.reshape(H, N)
        kk = kk * lax.rsqrt(jnp.sum(kk * kk, axis=-1, keepdims=True) + 1e-12)

        def direction(d):
            decay = jnp.exp(-jnp.exp(-jax.nn.softplus(-zw[d]) - 0.5))
            a = jax.nn.sigmoid(za[d])
            return decay, k * (1.0 + (a - 1.0) * k_a.reshape(H, N)), kk * a

        fwd, bwd = direction(0), direction(1)
        o, s_f, s_b = rwkv_scan_rowsum(r, v, -kk, fwd, bwd, s_f0, s_b0)
        bonus = jnp.sum(r * 0.5 * (fwd[1] + bwd[1]) * r_k, axis=-1, keepdims=True)
    m = jnp.mean(o, axis=-1, keepdims=True)
    var = jnp.mean(jnp.square(o - m), axis=-1, keepdims=True)
    o = (o - m) * lax.rsqrt(var + RWKV_LN_EPS) * lnx_w.reshape(H, N) + lnx_b.reshape(H, N)
    o = (o + bonus * v).reshape(B, L, D)
    return (o * g) @ wo, s_f, s_b


GROUP_HID = EXP_PER_GROUP * D_EXPERT


def _moe_group_kernel(tile_group_ref, n_tiles_ref, xs_ref, gates_ref, wg_ref, wu_ref, wd_ref, ys_ref,
                      wg_bf, wu_bf, wd_bf):
    i = pl.program_id(0)
    prev_group = tile_group_ref[jnp.maximum(i - 1, 0)]
    group_changed = jnp.logical_or(i == 0, tile_group_ref[i] != prev_group)

    @pl.when(group_changed)
    def _():
        for e in range(EXP_PER_GROUP):
            hs = slice(e * D_EXPERT, (e + 1) * D_EXPERT)
            wg_bf[:, hs] = wg_ref[e].astype(bf16)
            wu_bf[:, hs] = wu_ref[e].astype(bf16)
            wd_bf[hs, :] = wd_ref[e].astype(bf16)

    @pl.when(i < n_tiles_ref[0])
    def _():
        x = xs_ref[...].astype(bf16)
        g = jnp.dot(x, wg_bf[...], preferred_element_type=f32)
        u = jnp.dot(x, wu_bf[...], preferred_element_type=f32)
        gate = jnp.concatenate([jnp.broadcast_to(gates_ref[:, e:e + 1], (MOE_TILE_M, D_EXPERT))
                                for e in range(EXP_PER_GROUP)], axis=1)
        hid = (g * jax.nn.sigmoid(g)) * u * gate
        ys_ref[...] = jnp.dot(hid.astype(bf16), wd_bf[...], preferred_element_type=f32)

    @pl.when(i >= n_tiles_ref[0])
    def _():
        ys_ref[...] = jnp.zeros_like(ys_ref)


def moe_group_experts(xs, gates, tile_group, n_tiles, w_gate, w_up, w_down, layer):
    P, D = xs.shape
    max_tiles = P // MOE_TILE_M
    weight_bytes = 3 * EXP_PER_GROUP * D * D_EXPERT * (4 + 2)
    tile_bytes = 2 * MOE_TILE_M * (2 * D * 4 + LANES * 4) + 4 * MOE_TILE_M * GROUP_HID * 4
    once = pl.Buffered(1)
    grid_spec = pltpu.PrefetchScalarGridSpec(
        num_scalar_prefetch=2,
        grid=(max_tiles,),
        in_specs=[
            pl.BlockSpec((MOE_TILE_M, D), lambda i, tg, nt: (i, 0)),
            pl.BlockSpec((MOE_TILE_M, EXP_PER_GROUP), lambda i, tg, nt: (i, 0)),
            pl.BlockSpec((None, EXP_PER_GROUP, D, D_EXPERT), lambda i, tg, nt: (layer, tg[i], 0, 0),
                         pipeline_mode=once),
            pl.BlockSpec((None, EXP_PER_GROUP, D, D_EXPERT), lambda i, tg, nt: (layer, tg[i], 0, 0),
                         pipeline_mode=once),
            pl.BlockSpec((None, EXP_PER_GROUP, D_EXPERT, D), lambda i, tg, nt: (layer, tg[i], 0, 0),
                         pipeline_mode=once),
        ],
        out_specs=pl.BlockSpec((MOE_TILE_M, D), lambda i, tg, nt: (i, 0)),
        scratch_shapes=[
            pltpu.VMEM((D, GROUP_HID), bf16),
            pltpu.VMEM((D, GROUP_HID), bf16),
            pltpu.VMEM((GROUP_HID, D), bf16),
        ],
    )
    return pl.pallas_call(
        _moe_group_kernel,
        grid_spec=grid_spec,
        out_shape=jax.ShapeDtypeStruct((P, D), f32),
        compiler_params=pltpu.CompilerParams(
            dimension_semantics=("arbitrary",),
            vmem_limit_bytes=min(V7X_VMEM_BYTES - (4 << 20), weight_bytes + tile_bytes + (8 << 20)),
        ),
        name="moe_group_experts",
    )(tile_group, n_tiles, xs, gates, w_gate, w_up, w_down)


def hier_moe(t, w_grp, b_grp, w_exp, b_exp, w_gate, w_up, w_down, layer):
    T, D = t.shape
    logits = jnp.dot(t, jnp.concatenate([w_grp, w_exp], axis=1), precision=lax.Precision.HIGHEST)
    grp_logits = logits[:, :N_GROUPS] + b_grp
    grp_prob = jax.nn.softmax(grp_logits, axis=-1)
    g_top = jnp.argmax(grp_logits, axis=-1).astype(jnp.int32)
    in_group = g_top[:, None] == jnp.arange(N_GROUPS, dtype=jnp.int32)[None, :]
    p_g = jnp.sum(jnp.where(in_group, grp_prob, 0.0), axis=1, keepdims=True)
    exp_logits = (logits[:, N_GROUPS:] + b_exp).reshape(-1, N_GROUPS, EXP_PER_GROUP)
    sel = jnp.sum(jnp.where(in_group[:, :, None], exp_logits, 0.0), axis=1)
    top_v, top_i = lax.top_k(sel, TOP_K)
    wts = p_g * jax.nn.softmax(top_v, axis=-1)
    gates = jnp.sum(jax.nn.one_hot(top_i, EXP_PER_GROUP, dtype=f32) * wts[..., None], axis=1)

    order = jnp.argsort(g_top, stable=True).astype(jnp.int32)
    grp_i32 = in_group.astype(jnp.int32)
    counts = jnp.sum(grp_i32, axis=0)
    rank = jnp.sum(jnp.where(in_group, jnp.cumsum(grp_i32, axis=0), 0), axis=1) - 1
    tiles_per = (counts + MOE_TILE_M - 1) // MOE_TILE_M
    tile_end = jnp.cumsum(tiles_per)
    start_padded = (tile_end - tiles_per) * MOE_TILE_M
    start_sorted = jnp.cumsum(counts) - counts
    max_tiles = T // MOE_TILE_M + N_GROUPS
    n_tiles = tile_end[-1:].astype(jnp.int32)
    tile_ids = jnp.arange(max_tiles, dtype=jnp.int32)
    tile_group = jnp.minimum(jnp.sum((tile_ids[:, None] >= tile_end[None, :]).astype(jnp.int32), axis=1),
                             N_GROUPS - 1).astype(jnp.int32)
    tile_group = jnp.where(tile_ids < n_tiles[0], tile_group, tile_group[jnp.maximum(n_tiles[0] - 1, 0)])
    row_group = jnp.repeat(tile_group, MOE_TILE_M)
    row_in_group = jnp.arange(max_tiles * MOE_TILE_M, dtype=jnp.int32) - start_padded[row_group]
    row_valid = (row_in_group >= 0) & (row_in_group < counts[row_group])
    row_tok = jnp.where(row_valid, order[jnp.clip(start_sorted[row_group] + row_in_group, 0, T - 1)], 0)
    pos = start_padded[g_top] + rank

    xs = jnp.take(t, row_tok, axis=0)
    row_gates = jnp.where(row_valid[:, None], jnp.take(gates, row_tok, axis=0), 0.0)
    ys = moe_group_experts(xs, row_gates, tile_group, n_tiles, w_gate, w_up, w_down, layer)
    return jnp.take(ys, pos, axis=0)


def kernel(x_prompt, x_sample, c, cache_attn_k, cache_attn_v, state_gla, state_rwkv, c_ctx, ada_w, ada_b, norm_mix, norm_ffn, norm_out, ev_w_in, ev_w_out, gla_dec_w, gla_dec_b, gla_norm, att_sink, rw_mu, rw_wr, rw_wk, rw_wv, rw_wo, rw_w0, rw_w1, rw_w2, rw_a0, rw_a1, rw_a2, rw_g1, rw_g2, rw_kk, rw_ka, rw_rk, rw_lnx_w, rw_lnx_b, moe_w_grp, moe_b_grp, moe_w_exp, moe_b_exp, moe_w_gate, moe_w_up, moe_w_down):
    n_lat = x_sample.shape[1]
    rows = n_lat // GRID_W
    row_pos = jnp.repeat(jnp.arange(rows), GRID_W)
    col_pos = jnp.tile(jnp.arange(GRID_W), rows)
    rope = rope_tables(row_pos, col_pos)
    Bc, Lc, D = x_prompt.shape
    Bl, Ll, _ = x_sample.shape

    n_ctx = Bc * Lc
    x_all = jnp.concatenate([x_prompt.reshape(n_ctx, D), x_sample.reshape(Bl * Ll, D)], axis=0)
    cond_all = jnp.concatenate([c_ctx[None, :], c], axis=0)
    mods = [ada_mod(cond_all, ada_w[l], ada_b[l]) for l in range(DEPTH)]
    no_mod = jnp.zeros((1 + Bl, 1, D), f32)

    hc = modulate(rmsnorm(x_prompt, norm_mix[0]), mods[0][0][:1], mods[0][1][:1])
    hl = modulate(rmsnorm(x_sample, norm_mix[0]), mods[0][0][1:], mods[0][1][1:])
    ks_out, vs_out, gla_out, rwkv_out = [], [], [], []
    for l in range(DEPTH):
        _, _, gate1, shift2, scale2, gate2 = mods[l]
        i = l // 2
        if l % 2 == 0:
            oc, k_c, v_c, s_gla = even_mixer_ctx(hc, ev_w_in[i], ev_w_out[i], gla_dec_w[i], gla_dec_b[i],
                                                 gla_norm[i], att_sink[i])
            ol = even_mixer_lat(hl, ev_w_in[i], ev_w_out[i], gla_dec_w[i], gla_dec_b[i], gla_norm[i], att_sink[i],
                                rope, cache_attn_k[:, i], cache_attn_v[:, i], state_gla[:, i])
            ks_out.append(k_c)
            vs_out.append(v_c)
            gla_out.append(s_gla)
        else:
            rw = (rw_mu[i], rw_wr[i], rw_wk[i], rw_wv[i], rw_wo[i], rw_w0[i], rw_w1[i], rw_w2[i], rw_a0[i],
                  rw_a1[i], rw_a2[i], rw_g1[i], rw_g2[i], rw_kk[i], rw_ka[i], rw_rk[i], rw_lnx_w[i], rw_lnx_b[i])
            zero = jnp.zeros((Bc, RWKV_HEADS, RWKV_HEAD, RWKV_HEAD), f32)
            oc, s_f, s_b = rwkv_mix(hc, *rw, zero, zero)
            ol, _, _ = rwkv_mix(hl, *rw, state_rwkv[:, i, 0], state_rwkv[:, i, 1])
            rwkv_out.append(jnp.stack([s_f, s_b], axis=1))
        x_all, t = resid_norm_mod(x_all, (oc.reshape(n_ctx, D), ol.reshape(Bl * Ll, D)), n_ctx, Ll,
                                  gate1, norm_ffn[l], shift2, scale2)
        y = hier_moe(t, moe_w_grp[l], moe_b_grp[l], moe_w_exp[l], moe_b_exp[l], moe_w_gate, moe_w_up, moe_w_down, l)
        if l + 1 < DEPTH:
            x_all, (hc, hl) = resid_norm_mod(x_all, y, n_ctx, Ll, gate2, norm_mix[l + 1], mods[l + 1][0],
                                             mods[l + 1][1])
        else:
            x_all, (hc, hl) = resid_norm_mod(x_all, y, n_ctx, Ll, gate2, norm_out, no_mod, no_mod)
        hc, hl = hc.reshape(Bc, Lc, D), hl.reshape(Bl, Ll, D)
    y_prompt, y_sample = hc, hl

    new_attn_k = jnp.stack(ks_out, axis=1)
    new_attn_v = jnp.stack(vs_out, axis=1)
    new_gla = jnp.stack(gla_out, axis=1)
    new_rwkv = jnp.stack(rwkv_out, axis=1)
    return (y_prompt, y_sample, new_attn_k, new_attn_v, new_gla, new_rwkv)
```

```python
import functools

import jax
import jax.numpy as jnp
import numpy as np
from jax import lax
from jax.experimental import pallas as pl
from jax.experimental.pallas import tpu as pltpu

D_MODEL = 1024
DEPTH = 4
GRID_W = 64
EPS = 1e-6
GLA_HEADS = 4
GLA_DK = 64
GLA_DV = 128
GLA_LOWRANK = 16
GLA_GATE_TEMP = 16.0
GLA_CHUNK = 64
ATT_HEADS = 8
ATT_KV_HEADS = 2
ATT_GROUP = ATT_HEADS // ATT_KV_HEADS
ATT_HD = 64
WINDOW = 128
ATT_BLOCK = 128
ROPE_BASE = 10000.0
ROPE_AXIS_DIMS = ATT_HD // 2
NEG = -1e30
IN_WIDTHS = (GLA_HEADS * GLA_DK, GLA_HEADS * GLA_DK, GLA_HEADS * GLA_DV, GLA_HEADS * GLA_DV,
             ATT_HEADS * ATT_HD, ATT_KV_HEADS * ATT_HD, ATT_KV_HEADS * ATT_HD, 2 * GLA_LOWRANK)
IN_SPLITS = tuple(int(v) for v in np.cumsum(IN_WIDTHS)[:-1])
RWKV_HEAD = 64
RWKV_HEADS = D_MODEL // RWKV_HEAD
RWKV_LN_EPS = 64e-5
N_GROUPS = 4
EXP_PER_GROUP = 4
N_EXPERTS = N_GROUPS * EXP_PER_GROUP
TOP_K = 2
D_EXPERT = 512

V7X_VMEM_BYTES = 64 * 1024 * 1024
MOE_TILE_M = 256

f32 = jnp.float32
bf16 = jnp.bfloat16


def rmsnorm(x, w):
    y = x * lax.rsqrt(jnp.mean(x * x, axis=-1, keepdims=True) + EPS)
    return y * w


def ada_mod(cond, w, b):
    mod = jax.nn.silu(cond) @ w + b
    return [m[:, None, :] for m in jnp.split(mod, 6, axis=-1)]


def modulate(x, shift, scale):
    return x * (1.0 + scale) + shift


NORM_ROWS = 256


def _resid_norm_mod_kernel(*refs, n_ctx_blocks, split):
    is_ctx = pl.program_id(0) < n_ctx_blocks
    if split:
        x_ref, oc_ref, ol_ref, gate_ref, w_ref, shift_ref, scale_ref, rw_ref, rb_ref, xn_ref, t_ref = refs
        o = jnp.where(is_ctx, oc_ref[...], ol_ref[...])
    else:
        x_ref, o_ref, gate_ref, w_ref, shift_ref, scale_ref, xn_ref, tc_ref, tl_ref = refs
        o = o_ref[...]
    x = x_ref[...] + gate_ref[0] * o
    xn_ref[...] = x
    y = x * lax.rsqrt(jnp.mean(x * x, axis=-1, keepdims=True) + EPS) * w_ref[...]
    t = y * (1.0 + scale_ref[0]) + shift_ref[0]
    if split:
        d = t.shape[1]
        t_ref[:, :d] = t
        t_ref[:, d:] = jnp.dot(t, rw_ref[...], precision=lax.Precision.HIGHEST,
                               preferred_element_type=f32) + rb_ref[...]
    else:
        @pl.when(is_ctx)
        def _():
            tc_ref[...] = t

        @pl.when(jnp.logical_not(is_ctx))
        def _():
            tl_ref[...] = t


def resid_norm_mod(x, o, n_ctx, lat_len, gate, w, shift, scale, router=None):
    T, D = x.shape
    ncb = n_ctx // NORM_ROWS
    per_lat = lat_len // NORM_ROWS
    rows = pl.BlockSpec((NORM_ROWS, D), lambda i: (i, 0))
    mod = pl.BlockSpec((1, 1, D), lambda i: (jnp.where(i < ncb, 0, 1 + (i - ncb) // per_lat), 0, 0))
    pair = [pl.BlockSpec((NORM_ROWS, D), lambda i: (jnp.minimum(i, ncb - 1), 0)),
            pl.BlockSpec((NORM_ROWS, D), lambda i: (jnp.maximum(i - ncb, 0), 0))]
    split = isinstance(o, tuple)
    assert split == (router is not None)
    if split:
        o_specs, o_args = pair, list(o)
        t_specs = [pl.BlockSpec((NORM_ROWS, D + LANES), lambda i: (i, 0))]
        t_shapes = [jax.ShapeDtypeStruct((T, D + LANES), f32)]
        r_specs = [pl.BlockSpec((D, LANES), lambda i: (0, 0)), pl.BlockSpec((1, LANES), lambda i: (0, 0))]
        r_args = list(router)
    else:
        o_specs, o_args, t_specs = [rows], [o], pair
        t_shapes = [jax.ShapeDtypeStruct((n_ctx, D), f32), jax.ShapeDtypeStruct((T - n_ctx, D), f32)]
        r_specs, r_args = [], []
    out = pl.pallas_call(
        functools.partial(_resid_norm_mod_kernel, n_ctx_blocks=ncb, split=split),
        grid=(T // NORM_ROWS,),
        in_specs=[rows, *o_specs, mod, pl.BlockSpec((1, D), lambda i: (0, 0)), mod, mod, *r_specs],
        out_specs=[rows, *t_specs],
        out_shape=[jax.ShapeDtypeStruct((T, D), f32), *t_shapes],
        compiler_params=pltpu.CompilerParams(dimension_semantics=("arbitrary",)),
        name="resid_norm_mod",
    )(x, *o_args, gate, w.reshape(1, D), shift, scale, *r_args)
    return (out[0], out[1]) if split else (out[0], (out[1], out[2]))


def rope_tables(row_pos, col_pos):
    freqs = ROPE_BASE ** (-jnp.arange(0, ROPE_AXIS_DIMS, 2, dtype=f32) / ROPE_AXIS_DIMS)
    ang_r = row_pos.astype(f32)[:, None] * freqs[None, :]
    ang_c = col_pos.astype(f32)[:, None] * freqs[None, :]
    return jnp.cos(ang_r), jnp.sin(ang_r), jnp.cos(ang_c), jnp.sin(ang_c)


def rotate(x, cos, sin):
    x1, x2 = jnp.split(x, 2, axis=-1)
    cos = cos[None, :, None, :]
    sin = sin[None, :, None, :]
    return jnp.concatenate([x1 * cos - x2 * sin, x1 * sin + x2 * cos], axis=-1)


def apply_axial_rope(x, tables):
    cr, sr, cc, sc = tables
    xr, xc = jnp.split(x, 2, axis=-1)
    return jnp.concatenate([rotate(xr, cr, sr), rotate(xc, cc, sc)], axis=-1)


GLA_QK = GLA_HEADS * GLA_DK
GLA_V = GLA_HEADS * GLA_DV


def _gla_kernel(q_ref, v_ref, gg_ref, ldf_ref, ldb_ref, kt_ref, ldft_ref, ldbt_ref, s0_ref, norm_ref,
                o_ref, s_ref, *, seq_len):
    n = seq_len // GLA_CHUNK
    C = GLA_CHUNK
    ti = lax.broadcasted_iota(jnp.int32, (C, C), 0)
    tj = lax.broadcasted_iota(jnp.int32, (C, C), 1)
    keep = (tj <= ti, tj >= ti)
    tri = tuple(m.astype(bf16) for m in keep)
    tri_t = (tri[1], tri[0])
    ones = jnp.ones((C, GLA_DV), bf16)
    s_ref[...] = s0_ref[...]
    o_ref[...] = jnp.zeros_like(o_ref)
    ld_refs = ((ldf_ref, ldft_ref), (ldb_ref, ldbt_ref))

    def split3(x):
        hi = x.astype(bf16)
        rest = x - hi.astype(f32)
        mid = rest.astype(bf16)
        return hi, mid, (rest - mid.astype(f32)).astype(bf16)

    def sums(a, b):
        if isinstance(b, tuple):
            return sum(jnp.dot(a, p, preferred_element_type=f32) for p in b)
        return sum(jnp.dot(p, b, preferred_element_type=f32) for p in a)

    def chunk(c, carry):
        pending = []
        for d in range(2):
            cc = c if d == 0 else n - 1 - c
            rows = pl.ds(pl.multiple_of(cc * C, C), C)
            ld_ref, ldt_ref = ld_refs[d]
            ld_t3 = split3(ldt_ref[cc])
            b_all = sums(tri[d], split3(ld_ref[rows, :]))
            bt_all = sums(ld_t3, tri_t[d])
            total_all = sums(ld_t3, ones)
            for h in range(GLA_HEADS):
                ks = slice(h * GLA_DK, (h + 1) * GLA_DK)
                vs = slice(h * GLA_DV, (h + 1) * GLA_DV)
                q = q_ref[rows, ks] * (GLA_DK ** -0.5)
                v = v_ref[rows, vs].astype(bf16)
                k_t = kt_ref[cc, ks, :]
                b, b_t, total = b_all[:, ks], bt_all[ks], total_all[ks]
                q_e = (q * jnp.exp(b)).astype(bf16)
                k_e = (k_t * jnp.exp(-b_t)).astype(bf16)
                k_s = (k_t * jnp.exp(total[:, :C] - b_t)).astype(bf16)
                att = jnp.where(keep[d], jnp.dot(q_e, k_e, preferred_element_type=f32), 0.0).astype(bf16)
                s = s_ref[d, h]
                o = (jnp.dot(q_e, s.astype(bf16), preferred_element_type=f32)
                     + jnp.dot(att, v, preferred_element_type=f32))
                s_new = s * jnp.exp(total) + jnp.dot(k_s, v, preferred_element_type=f32)
                pending.append((d, h, rows, vs, o_ref[rows, vs] + o, s_new))
        for d, h, rows, vs, o, s_new in pending:
            o_ref[rows, vs] = o
            s_ref[d, h] = s_new
        return carry

    lax.fori_loop(0, n, chunk, 0)
    for h in range(GLA_HEADS):
        vs = slice(h * GLA_DV, (h + 1) * GLA_DV)
        x = o_ref[:, vs]
        y = x * lax.rsqrt(jnp.mean(x * x, axis=-1, keepdims=True) + EPS) * norm_ref[...]
        g = gg_ref[:, vs]
        o_ref[:, vs] = y * (g * jax.nn.sigmoid(g))


def gla_bidir_gated(z, ld_f, ld_b, s0, gla_norm):
    B, L, _ = z.shape
    n = L // GLA_CHUNK
    assert n % 2 == 0

    def per_chunk_t(a):
        return a.reshape(B, n, GLA_CHUNK, GLA_QK).transpose(0, 1, 3, 2)

    state = pl.BlockSpec((None, 2, GLA_HEADS, GLA_DK, GLA_DV), lambda b: (b, 0, 0, 0, 0))
    ld = pl.BlockSpec((None, L, GLA_QK), lambda b: (b, 0, 0))
    ld_t = pl.BlockSpec((None, n, GLA_QK, GLA_CHUNK), lambda b: (b, 0, 0, 0))
    return pl.pallas_call(
        functools.partial(_gla_kernel, seq_len=L),
        grid=(B,),
        in_specs=[pl.BlockSpec((None, L, GLA_QK), lambda b: (b, 0, 0)),
                  pl.BlockSpec((None, L, GLA_V), lambda b: (b, 0, 1)),
                  pl.BlockSpec((None, L, GLA_V), lambda b: (b, 0, 2)),
                  ld, ld, ld_t, ld_t, ld_t, state,
                  pl.BlockSpec((1, GLA_DV), lambda b: (0, 0))],
        out_specs=[pl.BlockSpec((None, L, GLA_V), lambda b: (b, 0, 0)), state],
        out_shape=[jax.ShapeDtypeStruct((B, L, GLA_V), f32),
                   jax.ShapeDtypeStruct((B, 2, GLA_HEADS, GLA_DK, GLA_DV), f32)],
        compiler_params=pltpu.CompilerParams(
            dimension_semantics=("arbitrary",),
            vmem_limit_bytes=min(V7X_VMEM_BYTES, 2 * L * (9 * GLA_QK + 3 * GLA_V) * 4 + (16 << 20)),
        ),
        name="gla_bidir",
    )(z, z, z, ld_f, ld_b, per_chunk_t(z[..., GLA_QK:2 * GLA_QK]), per_chunk_t(ld_f), per_chunk_t(ld_b),
      s0, gla_norm.reshape(1, GLA_DV))


ATT_Q = ATT_HEADS * ATT_HD
ATT_KV = ATT_KV_HEADS * ATT_HD


def _window_attn_kernel(q_ref, k_ref, v_ref, kc_ref, vc_ref, sink_ref, o_ref, *, tq, seq_len):
    scale = ATT_HD ** -0.5
    nt = (((1,), (1,)), ((), ()))
    start = pl.multiple_of(pl.program_id(1) * tq, tq)
    k_own = k_ref[pl.ds(start, 3 * tq), :].astype(bf16)
    v_own = v_ref[pl.ds(start, 3 * tq), :].astype(bf16)
    qpos = start + lax.broadcasted_iota(jnp.int32, (tq, 3 * tq), 0)
    kpos = start - tq + lax.broadcasted_iota(jnp.int32, (tq, 3 * tq), 1)
    valid = (jnp.abs(qpos - kpos) <= WINDOW) & (kpos >= 0) & (kpos < seq_len)
    for h in range(ATT_HEADS):
        g = h // ATT_GROUP
        kv = slice(g * ATT_HD, (g + 1) * ATT_HD)
        q = q_ref[:, h * ATT_HD:(h + 1) * ATT_HD].astype(bf16)
        s_own = lax.dot_general(q, k_own[:, kv], nt, preferred_element_type=f32) * scale
        s_own = jnp.where(valid, s_own, NEG)
        s_ctx = lax.dot_general(q, kc_ref[:, kv].astype(bf16), nt, preferred_element_type=f32) * scale
        sink = sink_ref[0:1, h:h + 1]
        m = jnp.maximum(jnp.maximum(jnp.max(s_own, axis=-1, keepdims=True), jnp.max(s_ctx, axis=-1, keepdims=True)),
                        sink)
        p_own = jnp.exp(s_own - m)
        p_ctx = jnp.exp(s_ctx - m)
        denom = (jnp.sum(p_own, axis=-1, keepdims=True) + jnp.sum(p_ctx, axis=-1, keepdims=True)
                 + jnp.exp(sink - m))
        o = (jnp.dot(p_own.astype(bf16), v_own[:, kv], preferred_element_type=f32)
             + jnp.dot(p_ctx.astype(bf16), vc_ref[:, kv].astype(bf16), preferred_element_type=f32))
        o_ref[:, h * ATT_HD:(h + 1) * ATT_HD] = o / denom


def ctx_attn(q, k, v, sink):
    B, L = q.shape[:2]
    scale = ATT_HD ** -0.5
    qg = q.reshape(B, L, ATT_KV_HEADS, ATT_GROUP, ATT_HD)
    sink_b = sink.reshape(ATT_KV_HEADS, ATT_GROUP)[None, :, :, None, None]
    s = jnp.einsum('bqhgd,bkhd->bhgqk', qg, k) * scale
    s_sink = jnp.broadcast_to(sink_b, s.shape[:-1] + (1,))
    p = jax.nn.softmax(jnp.concatenate([s_sink, s], axis=-1), axis=-1)[..., 1:]
    return jnp.einsum('bhgqk,bkhd->bqhgd', p, v).reshape(B, L, ATT_HEADS * ATT_HD)


def window_attn_latent(q, k, v, k_ctx, v_ctx, sink):
    B, S, _ = q.shape
    Lc = k_ctx.shape[1]
    nb = S // ATT_BLOCK
    pad = ((0, 0), (ATT_BLOCK, ATT_BLOCK), (0, 0))
    whole = lambda n: pl.BlockSpec((None, n, ATT_KV), lambda b, i: (b, 0, 0))
    return pl.pallas_call(
        functools.partial(_window_attn_kernel, tq=ATT_BLOCK, seq_len=S),
        grid=(B, nb),
        in_specs=[pl.BlockSpec((None, ATT_BLOCK, ATT_Q), lambda b, i: (b, i, 0)),
                  whole(S + 2 * ATT_BLOCK), whole(S + 2 * ATT_BLOCK), whole(Lc), whole(Lc),
                  pl.BlockSpec((1, ATT_HEADS), lambda b, i: (0, 0))],
        out_specs=pl.BlockSpec((None, ATT_BLOCK, ATT_Q), lambda b, i: (b, i, 0)),
        out_shape=jax.ShapeDtypeStruct((B, S, ATT_Q), f32),
        compiler_params=pltpu.CompilerParams(dimension_semantics=("arbitrary", "arbitrary")),
        name="window_attn",
    )(q, jnp.pad(k, pad), jnp.pad(v, pad), k_ctx, v_ctx, sink.reshape(1, ATT_HEADS))


def even_projections(h, w_in, dec_w, dec_b):
    B, L, _ = h.shape
    z = h @ w_in
    aq, ak, av, lr = jnp.split(z[..., IN_SPLITS[3]:], [s - IN_SPLITS[3] for s in IN_SPLITS[4:]], axis=-1)
    lr_f, lr_b = jnp.split(lr, 2, axis=-1)

    def logdecay(lr_d, w, b):
        return jax.nn.log_sigmoid(lr_d @ w + b) / GLA_GATE_TEMP

    ld_f = logdecay(lr_f, dec_w[0], dec_b[0])
    ld_b = logdecay(lr_b, dec_w[1], dec_b[1])
    aq = aq.reshape(B, L, ATT_HEADS, ATT_HD)
    ak = ak.reshape(B, L, ATT_KV_HEADS, ATT_HD)
    av = av.reshape(B, L, ATT_KV_HEADS, ATT_HD)
    return z, ld_f, ld_b, aq, ak, av


def even_mixer_ctx(h, w_in, w_out, dec_w, dec_b, gla_norm, sink):
    z, ld_f, ld_b, aq, ak, av = even_projections(h, w_in, dec_w, dec_b)
    zero = jnp.zeros((h.shape[0], 2, GLA_HEADS, GLA_DK, GLA_DV), f32)
    o_gla, s_fin = gla_bidir_gated(z, ld_f, ld_b, zero, gla_norm)
    o_att = ctx_attn(aq, ak, av, sink)
    return jnp.concatenate([o_gla, o_att], axis=-1) @ w_out, ak, av, s_fin


def even_mixer_lat(h, w_in, w_out, dec_w, dec_b, gla_norm, sink, rope, k_ctx, v_ctx, s0):
    z, ld_f, ld_b, aq, ak, av = even_projections(h, w_in, dec_w, dec_b)
    o_gla, _ = gla_bidir_gated(z, ld_f, ld_b, s0, gla_norm)
    B, S = h.shape[:2]
    o_att = window_attn_latent(apply_axial_rope(aq, rope).reshape(B, S, ATT_Q),
                               apply_axial_rope(ak, rope).reshape(B, S, ATT_KV), av.reshape(B, S, ATT_KV),
                               k_ctx.reshape(B, -1, ATT_KV), v_ctx.reshape(B, -1, ATT_KV), sink)
    return jnp.concatenate([o_gla, o_att], axis=-1) @ w_out


def centred_shift(x):
    xp = jnp.pad(x, ((0, 0), (1, 1), (0, 0)))
    return 0.5 * (xp[:, :-2] + xp[:, 2:])


LANES = 128
SUBLANES = 8
RWKV_TC = 32


def _rwkv_scan_kernel(r_ref, v_ref, k_ref, zw_ref, za_ref, kk_ref, ka_ref, rk_ref, s0_ref, o_ref, bonus_ref, s_ref,
                      *, tc, nv, n_dir_groups):
    backward = pl.program_id(0) >= n_dir_groups

    @pl.when(pl.program_id(1) == 0)
    def _():
        s_ref[...] = s0_ref[...]

    def step(i, carry):
        t = jnp.where(backward, tc - 1 - i, i)
        k_t, r = k_ref[t], r_ref[t]
        kk = k_t * kk_ref[0]
        kk = kk * lax.rsqrt(jnp.sum(kk * kk, axis=0, keepdims=True) + 1e-12)
        a = jax.nn.sigmoid(za_ref[t])
        z = -zw_ref[t]
        softplus = jnp.maximum(z, 0.0) + jnp.log1p(jnp.exp(-jnp.abs(z)))
        w = jnp.exp(-jnp.exp(-softplus - 0.5))
        alpha, beta = -kk, kk * a
        kd = k_t * (1.0 + (a - 1.0) * ka_ref[0])
        bonus_ref[pl.ds(t, 1), :] = jnp.sum(r * 0.5 * kd * rk_ref[0], axis=0, keepdims=True)
        for vb in range(nv // SUBLANES):
            outs = []
            for v in range(vb * SUBLANES, (vb + 1) * SUBLANES):
                s = s_ref[v]
                sa = jnp.sum(s * alpha, axis=0, keepdims=True)
                s_new = s * w + sa * beta + v_ref[t, pl.ds(v, 1), :] * kd
                s_ref[v] = s_new
                outs.append(jnp.sum(s_new * r, axis=0, keepdims=True))
            o_ref[t, vb * SUBLANES:(vb + 1) * SUBLANES, :] = jnp.concatenate(outs, axis=0)
        return carry

    lax.fori_loop(0, tc, step, 0)


def rwkv_scan_lanes(r, v, k, zw, za, k_k, k_a, r_k, s0):
    gd, L, nv, _ = v.shape
    tc = RWKV_TC
    n = L // tc

    def time_block(g, j):
        return jnp.where(g >= gd, n - 1 - j, j)

    def blk(rows, shared):
        return pl.BlockSpec((None, tc, rows, LANES), lambda g, j: (g % gd if shared else g, time_block(g, j), 0, 0))

    const = pl.BlockSpec((None, 1, RWKV_HEAD, LANES), lambda g, j: (g % gd, 0, 0, 0))
    state = pl.BlockSpec((None, nv, RWKV_HEAD, LANES), lambda g, j: (g, 0, 0, 0))
    block_bytes = tc * RWKV_HEAD * LANES * 4
    return pl.pallas_call(
        functools.partial(_rwkv_scan_kernel, tc=tc, nv=nv, n_dir_groups=gd),
        grid=(2 * gd, n),
        in_specs=[blk(RWKV_HEAD, True), blk(nv, True), blk(RWKV_HEAD, True),
                  blk(RWKV_HEAD, False), blk(RWKV_HEAD, False), const, const, const, state],
        out_specs=[blk(nv, False), pl.BlockSpec((None, tc, LANES), lambda g, j: (g, time_block(g, j), 0)), state],
        out_shape=[jax.ShapeDtypeStruct((2 * gd, L, nv, LANES), f32),
                   jax.ShapeDtypeStruct((2 * gd, L, LANES), f32),
                   jax.ShapeDtypeStruct((2 * gd, nv, RWKV_HEAD, LANES), f32)],
        compiler_params=pltpu.CompilerParams(
            dimension_semantics=("arbitrary", "arbitrary"),
            vmem_limit_bytes=2 * 7 * block_bytes + 4 * nv * RWKV_HEAD * LANES * 4 + (8 << 20),
        ),
        name="rwkv_scan",
    )(r, v, k, zw, za, k_k, k_a, r_k, s0)


def rwkv_scan_bidir(r, v, k, zw, za, k_k, k_a, r_k, s_f0, s_b0):
    B, L, H, N = r.shape
    S = B * H
    gd = S // LANES

    def rows(x):
        T = x.shape[-3]
        lead = x.shape[:-4]
        x = x.reshape(-1, B, T, H, N).transpose(0, 2, 4, 1, 3).reshape(-1, T, N, gd, LANES)
        return x.transpose(0, 3, 1, 2, 4).reshape(-1, T, N, LANES) if lead else x[0].transpose(2, 0, 1, 3)

    def rows_inv(y):
        return y.transpose(1, 2, 0, 3).reshape(y.shape[1], N, B, H)

    def state_in(s):
        return rows(s.transpose(0, 3, 1, 2)).transpose(0, 2, 1, 3)

    def state_out(s):
        return rows_inv(s.transpose(0, 2, 1, 3)).transpose(2, 3, 1, 0)

    def const_rows(c):
        return rows(jnp.broadcast_to(c[None, None], (B, 1, H, N)))

    out, bonus, s_fin = rwkv_scan_lanes(rows(r), rows(v), rows(k), rows(zw), rows(za),
                                        const_rows(k_k), const_rows(k_a), const_rows(r_k),
                                        jnp.concatenate([state_in(s_f0), state_in(s_b0)], axis=0))
    o = rows_inv(out[:gd]) + rows_inv(out[gd:])
    bonus = (bonus[:gd] + bonus[gd:]).transpose(1, 0, 2).reshape(L, B, H).transpose(1, 0, 2)
    return o.transpose(2, 0, 3, 1), bonus, state_out(s_fin[:gd]), state_out(s_fin[gd:])


RWKV_SUB = 64
HEAD_PAIRS = D_MODEL // LANES


def _rwkv_rowsum_kernel(rf, vf, af, wf, kf, bf_, rb, vb, ab, wb, kb, bb, s0, of, ob, s, acc, *, tc):
    j = pl.program_id(1)

    @pl.when(j == 0)
    def _():
        s[...] = s0[...]

    row_head = lax.broadcasted_iota(jnp.int32, (2 * LANES, 2 * LANES), 0) // RWKV_HEAD
    col_head = lax.broadcasted_iota(jnp.int32, (2 * LANES, 2 * LANES), 1) // RWKV_HEAD
    ones_bd = (row_head == col_head).astype(bf16)
    sub = lax.broadcasted_iota(jnp.int32, (RWKV_HEAD, LANES), 0)
    lane_in_head = lax.broadcasted_iota(jnp.int32, (RWKV_HEAD, LANES), 1) % RWKV_HEAD
    eye = (sub == lane_in_head).astype(bf16)
    dirs = ((rf, vf, af, wf, kf, bf_), (rb, vb, ab, wb, kb, bb))

    def row_sums(x):
        return jnp.dot(x, ones_bd, preferred_element_type=f32)

    def row(ref, t, hp, reps):
        return jnp.broadcast_to(ref[t, pl.ds(hp, 1), :], (reps, LANES))

    def sub_chunk(sc, carry):
        def step(tt, carry):
            t_f = sc * RWKV_SUB + tt
            pos = ((t_f, jnp.maximum(t_f - 1, 0), tt - 1),
                   (tc - 1 - t_f, jnp.minimum(tc - t_f, tc - 1), RWKV_SUB - tt))
            for d in range(2):
                r_ref, v_ref, a_ref, w_ref, k_ref, b_ref = dirs[d]
                t_now, t_prev, out_lane = pos[d]
                lhs, tiles = [], []
                for hp2 in range(0, HEAD_PAIRS, 2):
                    v_diag = []
                    for hp in (hp2, hp2 + 1):
                        s_t = s[d, :, hp * LANES:(hp + 1) * LANES]
                        lhs.append(jnp.concatenate([(s_t * row(a_ref, t_now, hp, RWKV_HEAD)).astype(bf16),
                                                    (s_t * row(r_ref, t_prev, hp, RWKV_HEAD)).astype(bf16)], axis=1))
                        v_diag.append(eye * jnp.concatenate([row(v_ref, t_now, hp, 16).astype(bf16)] * 4, axis=0))
                        tiles.append(s_t)
                    lhs.append(jnp.concatenate(v_diag, axis=1))
                res = row_sums(jnp.concatenate(lhs, axis=0))
                for hp, s_t in enumerate(tiles):
                    ls = slice(hp * LANES, (hp + 1) * LANES)
                    base = (hp // 2) * 3 * RWKV_HEAD
                    own = res[base + (hp % 2) * RWKV_HEAD:base + (hp % 2 + 1) * RWKV_HEAD]
                    sa, out_prev = own[:, :LANES], own[:, LANES:]
                    v_col = res[base + 2 * RWKV_HEAD:base + 3 * RWKV_HEAD, (hp % 2) * LANES:(hp % 2 + 1) * LANES]
                    s[d, :, ls] = (s_t * row(w_ref, t_now, hp, RWKV_HEAD) + sa * row(b_ref, t_now, hp, RWKV_HEAD)
                                   + v_col * row(k_ref, t_now, hp, RWKV_HEAD))
                    acc[d, :, ls] = jnp.where(lane_in_head == out_lane, out_prev, acc[d, :, ls])
            return carry

        lax.fori_loop(0, RWKV_SUB, step, 0, unroll=8)
        t_last = sc * RWKV_SUB + RWKV_SUB - 1
        last = ((t_last, RWKV_SUB - 1), (tc - 1 - t_last, 0))
        for d in range(2):
            r_ref = dirs[d][0]
            t_now, out_lane = last[d]
            out_mask = jnp.concatenate([lane_in_head == out_lane] * 2, axis=1)
            for hp in range(0, HEAD_PAIRS, 2):
                ls = slice(hp * LANES, (hp + 2) * LANES)
                r_t = jnp.concatenate([row(r_ref, t_now, hp, RWKV_HEAD), row(r_ref, t_now, hp + 1, RWKV_HEAD)], axis=1)
                out_t = row_sums((s[d, :, ls] * r_t).astype(bf16))
                acc[d, :, ls] = jnp.where(out_mask, out_t, acc[d, :, ls])
        of[sc] = acc[0]
        ob[tc // RWKV_SUB - 1 - sc] = acc[1]
        return carry

    lax.fori_loop(0, tc // RWKV_SUB, sub_chunk, 0)


def rwkv_scan_rowsum(r, v, nkk, fwd, bwd, s_f0, s_b0):
    B, L, H, N = r.shape
    D = H * N
    tc = min(L, 256)
    n = L // tc
    fwd_blk = pl.BlockSpec((None, tc, HEAD_PAIRS, LANES), lambda b, j: (b, j, 0, 0))
    bwd_blk = pl.BlockSpec((None, tc, HEAD_PAIRS, LANES), lambda b, j: (b, n - 1 - j, 0, 0))
    r, v, nkk, w_f, k_f, b_f, w_b, k_b, b_b = (t.reshape(B, L, HEAD_PAIRS, LANES) for t in (r, v, nkk, *fwd, *bwd))
    state = pl.BlockSpec((None, 2, RWKV_HEAD, D), lambda b, j: (b, 0, 0, 0))
    nsub = tc // RWKV_SUB
    block_bytes = tc * D * 4
    s0 = jnp.stack([s.transpose(0, 2, 1, 3).reshape(B, N, D) for s in (s_f0, s_b0)], axis=1)
    o_f, o_b, s_fin = pl.pallas_call(
        functools.partial(_rwkv_rowsum_kernel, tc=tc),
        grid=(B, n),
        in_specs=[fwd_blk] * 6 + [bwd_blk] * 6 + [state],
        out_specs=[pl.BlockSpec((None, nsub, RWKV_HEAD, D), lambda b, j: (b, j, 0, 0)),
                   pl.BlockSpec((None, nsub, RWKV_HEAD, D), lambda b, j: (b, n - 1 - j, 0, 0)),
                   state],
        out_shape=[jax.ShapeDtypeStruct((B, L // RWKV_SUB, RWKV_HEAD, D), f32),
                   jax.ShapeDtypeStruct((B, L // RWKV_SUB, RWKV_HEAD, D), f32),
                   jax.ShapeDtypeStruct((B, 2, RWKV_HEAD, D), f32)],
        scratch_shapes=[pltpu.VMEM((2, RWKV_HEAD, D), f32)],
        compiler_params=pltpu.CompilerParams(
            dimension_semantics=("arbitrary", "arbitrary"),
            vmem_limit_bytes=min(V7X_VMEM_BYTES, 2 * 14 * block_bytes + (8 << 20)),
        ),
        name="rwkv_scan_rowsum",
    )(r, v, nkk, w_f, k_f, b_f, r, v, nkk, w_b, k_b, b_b, s0)
    o = (o_f + o_b).reshape(B, L // RWKV_SUB, RWKV_HEAD, H, RWKV_SUB).transpose(0, 1, 4, 3, 2).reshape(B, L, H, N)
    s_fin = s_fin.reshape(B, 2, N, H, N).transpose(0, 1, 3, 2, 4)
    return o, s_fin[:, 0], s_fin[:, 1]


def rwkv_mix(h, mu, wr, wk, wv, wo, w0, w1, w2, a0, a1, a2, g1, g2, k_k, k_a, r_k, lnx_w, lnx_b, s_f0, s_b0):
    B, L, D = h.shape
    H, N = RWKV_HEADS, RWKV_HEAD
    xx = centred_shift(h) - h
    xr, xw, xk, xv, xa, xg = [h + xx * mu[j] for j in range(6)]
    r = (xr @ wr).reshape(B, L, H, N)
    k = (xk @ wk).reshape(B, L, H, N)
    v = (xv @ wv).reshape(B, L, H, N)
    g = jax.nn.sigmoid(xg @ g1) @ g2
    zw = [(w0[d] + jnp.tanh(xw @ w1[d]) @ w2[d]).reshape(B, L, H, N) for d in range(2)]
    za = [(a0[d] + (xa @ a1[d]) @ a2[d]).reshape(B, L, H, N) for d in range(2)]

    if B * H >= LANES:
        o, bonus, s_f, s_b = rwkv_scan_bidir(r, v, k, jnp.stack(zw), jnp.stack(za), k_k.reshape(H, N),
                                             k_a.reshape(H, N), r_k, s_f0, s_b0)
        bonus = bonus[..., None]
    else:
        kk = k * k_k.reshape(H, N)
        kk = kk * lax.rsqrt(jnp.sum(kk * kk, axis=-1, keepdims=True) + 1e-12)

        def direction(d):
            decay = jnp.exp(-jnp.exp(-jax.nn.softplus(-zw[d]) - 0.5))
            a = jax.nn.sigmoid(za[d])
            return decay, k * (1.0 + (a - 1.0) * k_a.reshape(H, N)), kk * a

        fwd, bwd = direction(0), direction(1)
        o, s_f, s_b = rwkv_scan_rowsum(r, v, -kk, fwd, bwd, s_f0, s_b0)
        bonus = jnp.sum(r * 0.5 * (fwd[1] + bwd[1]) * r_k, axis=-1, keepdims=True)
    m = jnp.mean(o, axis=-1, keepdims=True)
    var = jnp.mean(jnp.square(o - m), axis=-1, keepdims=True)
    o = (o - m) * lax.rsqrt(var + RWKV_LN_EPS) * lnx_w.reshape(H, N) + lnx_b.reshape(H, N)
    o = (o + bonus * v).reshape(B, L, D)
    return (o * g) @ wo, s_f, s_b


GROUP_HID = EXP_PER_GROUP * D_EXPERT


def _moe_group_kernel(tile_group_ref, n_tiles_ref, xs_ref, wg_ref, wu_ref, wd_ref, ys_ref,
                      wg_bf, wu_bf, wd_bf):
    i = pl.program_id(0)
    group = tile_group_ref[i]
    prev_group = tile_group_ref[jnp.maximum(i - 1, 0)]
    group_changed = jnp.logical_or(i == 0, group != prev_group)

    @pl.when(group_changed)
    def _():
        for e in range(EXP_PER_GROUP):
            hs = slice(e * D_EXPERT, (e + 1) * D_EXPERT)
            wg_bf[:, hs] = wg_ref[e].astype(bf16)
            wu_bf[:, hs] = wu_ref[e].astype(bf16)
            wd_bf[hs, :] = wd_ref[e].astype(bf16)

    @pl.when(i < n_tiles_ref[0])
    def _():
        D = wg_bf.shape[0]
        logits = xs_ref[:, D:]
        lane = lax.broadcasted_iota(jnp.int32, logits.shape, 1)
        grp = jnp.where(lane < N_GROUPS, logits, -jnp.inf)
        grp = jnp.exp(grp - jnp.max(grp, axis=-1, keepdims=True))
        p_group = (jnp.sum(jnp.where(lane == group, grp, 0.0), axis=-1, keepdims=True)
                   / jnp.sum(grp, axis=-1, keepdims=True))
        first = N_GROUPS + group * EXP_PER_GROUP
        le = [jnp.sum(jnp.where(lane == first + e, logits, 0.0), axis=-1, keepdims=True)
              for e in range(EXP_PER_GROUP)]
        chosen = []
        for e in range(EXP_PER_GROUP):
            ahead = sum(((le[k] > le[e]) | ((le[k] == le[e]) & (k < e))).astype(jnp.int32)
                        for k in range(EXP_PER_GROUP) if k != e)
            chosen.append(ahead < TOP_K)
        top = functools.reduce(jnp.maximum, [jnp.where(c, v, -jnp.inf) for c, v in zip(chosen, le)])
        ex = [jnp.where(c, jnp.exp(v - top), 0.0) for c, v in zip(chosen, le)]
        scale = p_group / sum(ex)
        x = xs_ref[:, :D].astype(bf16)
        g = jnp.dot(x, wg_bf[...], preferred_element_type=f32)
        u = jnp.dot(x, wu_bf[...], preferred_element_type=f32)
        gate = jnp.concatenate([jnp.broadcast_to(ex[e] * scale, (MOE_TILE_M, D_EXPERT))
                                for e in range(EXP_PER_GROUP)], axis=1)
        hid = (g * jax.nn.sigmoid(g)) * u * gate
        ys_ref[...] = jnp.dot(hid.astype(bf16), wd_bf[...], preferred_element_type=f32)

    @pl.when(i >= n_tiles_ref[0])
    def _():
        ys_ref[...] = jnp.zeros_like(ys_ref)


def moe_group_experts(xs, tile_group, n_tiles, w_gate, w_up, w_down, layer):
    P = xs.shape[0]
    D = xs.shape[1] - LANES
    max_tiles = P // MOE_TILE_M
    weight_bytes = 3 * EXP_PER_GROUP * D * D_EXPERT * (4 + 2)
    tile_bytes = 2 * MOE_TILE_M * (2 * D * 4 + LANES * 4) + 4 * MOE_TILE_M * GROUP_HID * 4
    once = pl.Buffered(1)
    grid_spec = pltpu.PrefetchScalarGridSpec(
        num_scalar_prefetch=2,
        grid=(max_tiles,),
        in_specs=[
            pl.BlockSpec((MOE_TILE_M, D + LANES), lambda i, tg, nt: (i, 0)),
            pl.BlockSpec((None, EXP_PER_GROUP, D, D_EXPERT), lambda i, tg, nt: (layer, tg[i], 0, 0),
                         pipeline_mode=once),
            pl.BlockSpec((None, EXP_PER_GROUP, D, D_EXPERT), lambda i, tg, nt: (layer, tg[i], 0, 0),
                         pipeline_mode=once),
            pl.BlockSpec((None, EXP_PER_GROUP, D_EXPERT, D), lambda i, tg, nt: (layer, tg[i], 0, 0),
                         pipeline_mode=once),
        ],
        out_specs=pl.BlockSpec((MOE_TILE_M, D), lambda i, tg, nt: (i, 0)),
        scratch_shapes=[
            pltpu.VMEM((D, GROUP_HID), bf16),
            pltpu.VMEM((D, GROUP_HID), bf16),
            pltpu.VMEM((GROUP_HID, D), bf16),
        ],
    )
    return pl.pallas_call(
        _moe_group_kernel,
        grid_spec=grid_spec,
        out_shape=jax.ShapeDtypeStruct((P, D), f32),
        compiler_params=pltpu.CompilerParams(
            dimension_semantics=("arbitrary",),
            vmem_limit_bytes=min(V7X_VMEM_BYTES - (4 << 20), weight_bytes + tile_bytes + (8 << 20)),
        ),
        name="moe_group_experts",
    )(tile_group, n_tiles, xs, w_gate, w_up, w_down)


def router_params(w_grp, b_grp, w_exp, b_exp):
    w = jnp.concatenate([w_grp, w_exp], axis=1)
    b = jnp.concatenate([b_grp, b_exp], axis=0)[None, :]
    return jnp.pad(w, ((0, 0), (0, LANES - w.shape[1]))), jnp.pad(b, ((0, 0), (0, LANES - b.shape[1])))


def hier_moe(t_aug, w_gate, w_up, w_down, layer):
    T = t_aug.shape[0]
    D = t_aug.shape[1] - LANES
    g_top = jnp.argmax(t_aug[:, D:D + N_GROUPS], axis=-1).astype(jnp.int32)
    in_group = g_top[:, None] == jnp.arange(N_GROUPS, dtype=jnp.int32)[None, :]

    order = jnp.argsort(g_top, stable=True).astype(jnp.int32)
    grp_i32 = in_group.astype(jnp.int32)
    counts = jnp.sum(grp_i32, axis=0)
    rank = jnp.sum(jnp.where(in_group, jnp.cumsum(grp_i32, axis=0), 0), axis=1) - 1
    tiles_per = (counts + MOE_TILE_M - 1) // MOE_TILE_M
    tile_end = jnp.cumsum(tiles_per)
    start_padded = (tile_end - tiles_per) * MOE_TILE_M
    start_sorted = jnp.cumsum(counts) - counts
    max_tiles = T // MOE_TILE_M + N_GROUPS
    n_tiles = tile_end[-1:].astype(jnp.int32)
    tile_ids = jnp.arange(max_tiles, dtype=jnp.int32)
    tile_group = jnp.minimum(jnp.sum((tile_ids[:, None] >= tile_end[None, :]).astype(jnp.int32), axis=1),
                             N_GROUPS - 1).astype(jnp.int32)
    tile_group = jnp.where(tile_ids < n_tiles[0], tile_group, tile_group[jnp.maximum(n_tiles[0] - 1, 0)])
    row_group = jnp.repeat(tile_group, MOE_TILE_M)
    row_in_group = jnp.arange(max_tiles * MOE_TILE_M, dtype=jnp.int32) - start_padded[row_group]
    row_valid = (row_in_group >= 0) & (row_in_group < counts[row_group])
    row_tok = jnp.where(row_valid, order[jnp.clip(start_sorted[row_group] + row_in_group, 0, T - 1)], 0)
    pos = start_padded[g_top] + rank

    xs = jnp.take(t_aug, row_tok, axis=0)
    ys = moe_group_experts(xs, tile_group, n_tiles, w_gate, w_up, w_down, layer)
    return jnp.take(ys, pos, axis=0)


def kernel(x_prompt, x_sample, c, cache_attn_k, cache_attn_v, state_gla, state_rwkv, c_ctx, ada_w, ada_b, norm_mix, norm_ffn, norm_out, ev_w_in, ev_w_out, gla_dec_w, gla_dec_b, gla_norm, att_sink, rw_mu, rw_wr, rw_wk, rw_wv, rw_wo, rw_w0, rw_w1, rw_w2, rw_a0, rw_a1, rw_a2, rw_g1, rw_g2, rw_kk, rw_ka, rw_rk, rw_lnx_w, rw_lnx_b, moe_w_grp, moe_b_grp, moe_w_exp, moe_b_exp, moe_w_gate, moe_w_up, moe_w_down):
    n_lat = x_sample.shape[1]
    rows = n_lat // GRID_W
    row_pos = jnp.repeat(jnp.arange(rows), GRID_W)
    col_pos = jnp.tile(jnp.arange(GRID_W), rows)
    rope = rope_tables(row_pos, col_pos)
    Bc, Lc, D = x_prompt.shape
    Bl, Ll, _ = x_sample.shape

    n_ctx = Bc * Lc
    x_all = jnp.concatenate([x_prompt.reshape(n_ctx, D), x_sample.reshape(Bl * Ll, D)], axis=0)
    cond_all = jnp.concatenate([c_ctx[None, :], c], axis=0)
    mods = [ada_mod(cond_all, ada_w[l], ada_b[l]) for l in range(DEPTH)]
    no_mod = jnp.zeros((1 + Bl, 1, D), f32)

    hc = modulate(rmsnorm(x_prompt, norm_mix[0]), mods[0][0][:1], mods[0][1][:1])
    hl = modulate(rmsnorm(x_sample, norm_mix[0]), mods[0][0][1:], mods[0][1][1:])
    ks_out, vs_out, gla_out, rwkv_out = [], [], [], []
    for l in range(DEPTH):
        _, _, gate1, shift2, scale2, gate2 = mods[l]
        i = l // 2
        if l % 2 == 0:
            oc, k_c, v_c, s_gla = even_mixer_ctx(hc, ev_w_in[i], ev_w_out[i], gla_dec_w[i], gla_dec_b[i],
                                                 gla_norm[i], att_sink[i])
            ol = even_mixer_lat(hl, ev_w_in[i], ev_w_out[i], gla_dec_w[i], gla_dec_b[i], gla_norm[i], att_sink[i],
                                rope, cache_attn_k[:, i], cache_attn_v[:, i], state_gla[:, i])
            ks_out.append(k_c)
            vs_out.append(v_c)
            gla_out.append(s_gla)
        else:
            rw = (rw_mu[i], rw_wr[i], rw_wk[i], rw_wv[i], rw_wo[i], rw_w0[i], rw_w1[i], rw_w2[i], rw_a0[i],
                  rw_a1[i], rw_a2[i], rw_g1[i], rw_g2[i], rw_kk[i], rw_ka[i], rw_rk[i], rw_lnx_w[i], rw_lnx_b[i])
            zero = jnp.zeros((Bc, RWKV_HEADS, RWKV_HEAD, RWKV_HEAD), f32)
            oc, s_f, s_b = rwkv_mix(hc, *rw, zero, zero)
            ol, _, _ = rwkv_mix(hl, *rw, state_rwkv[:, i, 0], state_rwkv[:, i, 1])
            rwkv_out.append(jnp.stack([s_f, s_b], axis=1))
        x_all, t_aug = resid_norm_mod(x_all, (oc.reshape(n_ctx, D), ol.reshape(Bl * Ll, D)), n_ctx, Ll,
                                      gate1, norm_ffn[l], shift2, scale2,
                                      router_params(moe_w_grp[l], moe_b_grp[l], moe_w_exp[l], moe_b_exp[l]))
        y = hier_moe(t_aug, moe_w_gate, moe_w_up, moe_w_down, l)
        if l + 1 < DEPTH:
            x_all, (hc, hl) = resid_norm_mod(x_all, y, n_ctx, Ll, gate2, norm_mix[l + 1], mods[l + 1][0],
                                             mods[l + 1][1])
        else:
            x_all, (hc, hl) = resid_norm_mod(x_all, y, n_ctx, Ll, gate2, norm_out, no_mod, no_mod)
        hc, hl = hc.reshape(Bc, Lc, D), hl.reshape(Bl, Ll, D)
    y_prompt, y_sample = hc, hl

    new_attn_k = jnp.stack(ks_out, axis=1)
    new_attn_v = jnp.stack(vs_out, axis=1)
    new_gla = jnp.stack(gla_out, axis=1)
    new_rwkv = jnp.stack(rwkv_out, axis=1)
    return (y_prompt, y_sample, new_attn_k, new_attn_v, new_gla, new_rwkv)
```

```python
import functools

import jax
import jax.numpy as jnp
import numpy as np
from jax import lax
from jax.experimental import pallas as pl
from jax.experimental.pallas import tpu as pltpu

D_MODEL = 1024
DEPTH = 4
GRID_W = 64
EPS = 1e-6
GLA_HEADS = 4
GLA_DK = 64
GLA_DV = 128
GLA_LOWRANK = 16
GLA_GATE_TEMP = 16.0
GLA_CHUNK = 64
ATT_HEADS = 8
ATT_KV_HEADS = 2
ATT_GROUP = ATT_HEADS // ATT_KV_HEADS
ATT_HD = 64
WINDOW = 128
ATT_BLOCK = 128
ROPE_BASE = 10000.0
ROPE_AXIS_DIMS = ATT_HD // 2
NEG = -1e30
IN_WIDTHS = (GLA_HEADS * GLA_DK, GLA_HEADS * GLA_DK, GLA_HEADS * GLA_DV, GLA_HEADS * GLA_DV,
             ATT_HEADS * ATT_HD, ATT_KV_HEADS * ATT_HD, ATT_KV_HEADS * ATT_HD, 2 * GLA_LOWRANK)
IN_SPLITS = tuple(int(v) for v in np.cumsum(IN_WIDTHS)[:-1])
RWKV_HEAD = 64
RWKV_HEADS = D_MODEL // RWKV_HEAD
RWKV_LN_EPS = 64e-5
N_GROUPS = 4
EXP_PER_GROUP = 4
N_EXPERTS = N_GROUPS * EXP_PER_GROUP
TOP_K = 2
D_EXPERT = 512

V7X_VMEM_BYTES = 64 * 1024 * 1024
MOE_TILE_M = 256

f32 = jnp.float32
bf16 = jnp.bfloat16


def rmsnorm(x, w):
    y = x * lax.rsqrt(jnp.mean(x * x, axis=-1, keepdims=True) + EPS)
    return y * w


def ada_mod(cond, w, b):
    mod = jax.nn.silu(cond) @ w + b
    return [m[:, None, :] for m in jnp.split(mod, 6, axis=-1)]


def modulate(x, shift, scale):
    return x * (1.0 + scale) + shift


NORM_ROWS = 256


def _resid_norm_mod_kernel(*refs, n_ctx_blocks, split):
    is_ctx = pl.program_id(0) < n_ctx_blocks
    if split:
        x_ref, oc_ref, ol_ref, gate_ref, w_ref, shift_ref, scale_ref, rw_ref, rb_ref, xn_ref, t_ref = refs
        o = jnp.where(is_ctx, oc_ref[...], ol_ref[...])
    else:
        x_ref, o_ref, gate_ref, w_ref, shift_ref, scale_ref, xn_ref, tc_ref, tl_ref = refs
        o = o_ref[...]
    x = x_ref[...] + gate_ref[0] * o
    xn_ref[...] = x
    y = x * lax.rsqrt(jnp.mean(x * x, axis=-1, keepdims=True) + EPS) * w_ref[...]
    t = y * (1.0 + scale_ref[0]) + shift_ref[0]
    if split:
        half = t.shape[1] // 2
        logits = jnp.dot(t, rw_ref[...], precision=lax.Precision.HIGHEST, preferred_element_type=f32) + rb_ref[...]
        bits = lax.bitcast_convert_type(t.astype(bf16).astype(f32), jnp.uint32)
        t_ref[:, :half] = (bits[:, :half] >> 16) | (bits[:, half:] & jnp.uint32(0xFFFF0000))
        t_ref[:, half:] = lax.bitcast_convert_type(logits, jnp.uint32)
    else:
        @pl.when(is_ctx)
        def _():
            tc_ref[...] = t

        @pl.when(jnp.logical_not(is_ctx))
        def _():
            tl_ref[...] = t


def resid_norm_mod(x, o, n_ctx, lat_len, gate, w, shift, scale, router=None):
    T, D = x.shape
    ncb = n_ctx // NORM_ROWS
    per_lat = lat_len // NORM_ROWS
    rows = pl.BlockSpec((NORM_ROWS, D), lambda i: (i, 0))
    mod = pl.BlockSpec((1, 1, D), lambda i: (jnp.where(i < ncb, 0, 1 + (i - ncb) // per_lat), 0, 0))
    pair = [pl.BlockSpec((NORM_ROWS, D), lambda i: (jnp.minimum(i, ncb - 1), 0)),
            pl.BlockSpec((NORM_ROWS, D), lambda i: (jnp.maximum(i - ncb, 0), 0))]
    split = isinstance(o, tuple)
    assert split == (router is not None)
    if split:
        o_specs, o_args = pair, list(o)
        t_specs = [pl.BlockSpec((NORM_ROWS, D // 2 + LANES), lambda i: (i, 0))]
        t_shapes = [jax.ShapeDtypeStruct((T, D // 2 + LANES), jnp.uint32)]
        r_specs = [pl.BlockSpec((D, LANES), lambda i: (0, 0)), pl.BlockSpec((1, LANES), lambda i: (0, 0))]
        r_args = list(router)
    else:
        o_specs, o_args, t_specs = [rows], [o], pair
        t_shapes = [jax.ShapeDtypeStruct((n_ctx, D), f32), jax.ShapeDtypeStruct((T - n_ctx, D), f32)]
        r_specs, r_args = [], []
    out = pl.pallas_call(
        functools.partial(_resid_norm_mod_kernel, n_ctx_blocks=ncb, split=split),
        grid=(T // NORM_ROWS,),
        in_specs=[rows, *o_specs, mod, pl.BlockSpec((1, D), lambda i: (0, 0)), mod, mod, *r_specs],
        out_specs=[rows, *t_specs],
        out_shape=[jax.ShapeDtypeStruct((T, D), f32), *t_shapes],
        compiler_params=pltpu.CompilerParams(dimension_semantics=("arbitrary",)),
        name="resid_norm_mod",
    )(x, *o_args, gate, w.reshape(1, D), shift, scale, *r_args)
    return (out[0], out[1]) if split else (out[0], (out[1], out[2]))


def rope_tables(row_pos, col_pos):
    freqs = ROPE_BASE ** (-jnp.arange(0, ROPE_AXIS_DIMS, 2, dtype=f32) / ROPE_AXIS_DIMS)
    ang_r = row_pos.astype(f32)[:, None] * freqs[None, :]
    ang_c = col_pos.astype(f32)[:, None] * freqs[None, :]
    return jnp.cos(ang_r), jnp.sin(ang_r), jnp.cos(ang_c), jnp.sin(ang_c)


def rotate(x, cos, sin):
    x1, x2 = jnp.split(x, 2, axis=-1)
    cos = cos[None, :, None, :]
    sin = sin[None, :, None, :]
    return jnp.concatenate([x1 * cos - x2 * sin, x1 * sin + x2 * cos], axis=-1)


def apply_axial_rope(x, tables):
    cr, sr, cc, sc = tables
    xr, xc = jnp.split(x, 2, axis=-1)
    return jnp.concatenate([rotate(xr, cr, sr), rotate(xc, cc, sc)], axis=-1)


GLA_QK = GLA_HEADS * GLA_DK
GLA_V = GLA_HEADS * GLA_DV


def _gla_kernel(q_ref, v_ref, gg_ref, ldf_ref, ldb_ref, kt_ref, ldft_ref, ldbt_ref, s0_ref, norm_ref,
                o_ref, s_ref, *, seq_len):
    n = seq_len // GLA_CHUNK
    C = GLA_CHUNK
    ti = lax.broadcasted_iota(jnp.int32, (C, C), 0)
    tj = lax.broadcasted_iota(jnp.int32, (C, C), 1)
    keep = (tj <= ti, tj >= ti)
    tri = tuple(m.astype(bf16) for m in keep)
    tri_t = (tri[1], tri[0])
    ones = jnp.ones((C, GLA_DV), bf16)
    s_ref[...] = s0_ref[...]
    o_ref[...] = jnp.zeros_like(o_ref)
    ld_refs = ((ldf_ref, ldft_ref), (ldb_ref, ldbt_ref))

    def split3(x):
        hi = x.astype(bf16)
        rest = x - hi.astype(f32)
        mid = rest.astype(bf16)
        return hi, mid, (rest - mid.astype(f32)).astype(bf16)

    def sums(a, b):
        if isinstance(b, tuple):
            return sum(jnp.dot(a, p, preferred_element_type=f32) for p in b)
        return sum(jnp.dot(p, b, preferred_element_type=f32) for p in a)

    def chunk(c, carry):
        pending = []
        for d in range(2):
            cc = c if d == 0 else n - 1 - c
            rows = pl.ds(pl.multiple_of(cc * C, C), C)
            ld_ref, ldt_ref = ld_refs[d]
            ld_t3 = split3(ldt_ref[cc])
            b_all = sums(tri[d], split3(ld_ref[rows, :]))
            bt_all = sums(ld_t3, tri_t[d])
            total_all = sums(ld_t3, ones)
            for h in range(GLA_HEADS):
                ks = slice(h * GLA_DK, (h + 1) * GLA_DK)
                vs = slice(h * GLA_DV, (h + 1) * GLA_DV)
                q = q_ref[rows, ks] * (GLA_DK ** -0.5)
                v = v_ref[rows, vs].astype(bf16)
                k_t = kt_ref[cc, ks, :]
                b, b_t, total = b_all[:, ks], bt_all[ks], total_all[ks]
                q_e = (q * jnp.exp(b)).astype(bf16)
                k_e = (k_t * jnp.exp(-b_t)).astype(bf16)
                k_s = (k_t * jnp.exp(total[:, :C] - b_t)).astype(bf16)
                att = jnp.where(keep[d], jnp.dot(q_e, k_e, preferred_element_type=f32), 0.0).astype(bf16)
                s = s_ref[d, h]
                o = (jnp.dot(q_e, s.astype(bf16), preferred_element_type=f32)
                     + jnp.dot(att, v, preferred_element_type=f32))
                s_new = s * jnp.exp(total) + jnp.dot(k_s, v, preferred_element_type=f32)
                pending.append((d, h, rows, vs, o_ref[rows, vs] + o, s_new))
        for d, h, rows, vs, o, s_new in pending:
            o_ref[rows, vs] = o
            s_ref[d, h] = s_new
        return carry

    lax.fori_loop(0, n, chunk, 0)
    for h in range(GLA_HEADS):
        vs = slice(h * GLA_DV, (h + 1) * GLA_DV)
        x = o_ref[:, vs]
        y = x * lax.rsqrt(jnp.mean(x * x, axis=-1, keepdims=True) + EPS) * norm_ref[...]
        g = gg_ref[:, vs]
        o_ref[:, vs] = y * (g * jax.nn.sigmoid(g))


def gla_bidir_gated(z, ld_f, ld_b, s0, gla_norm):
    B, L, _ = z.shape
    n = L // GLA_CHUNK
    assert n % 2 == 0

    def per_chunk_t(a):
        return a.reshape(B, n, GLA_CHUNK, GLA_QK).transpose(0, 1, 3, 2)

    state = pl.BlockSpec((None, 2, GLA_HEADS, GLA_DK, GLA_DV), lambda b: (b, 0, 0, 0, 0))
    ld = pl.BlockSpec((None, L, GLA_QK), lambda b: (b, 0, 0))
    ld_t = pl.BlockSpec((None, n, GLA_QK, GLA_CHUNK), lambda b: (b, 0, 0, 0))
    return pl.pallas_call(
        functools.partial(_gla_kernel, seq_len=L),
        grid=(B,),
        in_specs=[pl.BlockSpec((None, L, GLA_QK), lambda b: (b, 0, 0)),
                  pl.BlockSpec((None, L, GLA_V), lambda b: (b, 0, 1)),
                  pl.BlockSpec((None, L, GLA_V), lambda b: (b, 0, 2)),
                  ld, ld, ld_t, ld_t, ld_t, state,
                  pl.BlockSpec((1, GLA_DV), lambda b: (0, 0))],
        out_specs=[pl.BlockSpec((None, L, GLA_V), lambda b: (b, 0, 0)), state],
        out_shape=[jax.ShapeDtypeStruct((B, L, GLA_V), f32),
                   jax.ShapeDtypeStruct((B, 2, GLA_HEADS, GLA_DK, GLA_DV), f32)],
        compiler_params=pltpu.CompilerParams(
            dimension_semantics=("arbitrary",),
            vmem_limit_bytes=min(V7X_VMEM_BYTES, 2 * L * (9 * GLA_QK + 3 * GLA_V) * 4 + (16 << 20)),
        ),
        name="gla_bidir",
    )(z, z, z, ld_f, ld_b, per_chunk_t(z[..., GLA_QK:2 * GLA_QK]), per_chunk_t(ld_f), per_chunk_t(ld_b),
      s0, gla_norm.reshape(1, GLA_DV))


ATT_Q = ATT_HEADS * ATT_HD
ATT_KV = ATT_KV_HEADS * ATT_HD


def _window_attn_kernel(q_ref, k_ref, v_ref, kc_ref, vc_ref, sink_ref, o_ref, *, tq, seq_len):
    scale = ATT_HD ** -0.5
    nt = (((1,), (1,)), ((), ()))
    start = pl.multiple_of(pl.program_id(1) * tq, tq)
    k_own = k_ref[pl.ds(start, 3 * tq), :].astype(bf16)
    v_own = v_ref[pl.ds(start, 3 * tq), :].astype(bf16)
    qpos = start + lax.broadcasted_iota(jnp.int32, (tq, 3 * tq), 0)
    kpos = start - tq + lax.broadcasted_iota(jnp.int32, (tq, 3 * tq), 1)
    valid = (jnp.abs(qpos - kpos) <= WINDOW) & (kpos >= 0) & (kpos < seq_len)
    for h in range(ATT_HEADS):
        g = h // ATT_GROUP
        kv = slice(g * ATT_HD, (g + 1) * ATT_HD)
        q = q_ref[:, h * ATT_HD:(h + 1) * ATT_HD].astype(bf16)
        s_own = lax.dot_general(q, k_own[:, kv], nt, preferred_element_type=f32) * scale
        s_own = jnp.where(valid, s_own, NEG)
        s_ctx = lax.dot_general(q, kc_ref[:, kv].astype(bf16), nt, preferred_element_type=f32) * scale
        sink = sink_ref[0:1, h:h + 1]
        m = jnp.maximum(jnp.maximum(jnp.max(s_own, axis=-1, keepdims=True), jnp.max(s_ctx, axis=-1, keepdims=True)),
                        sink)
        p_own = jnp.exp(s_own - m)
        p_ctx = jnp.exp(s_ctx - m)
        denom = (jnp.sum(p_own, axis=-1, keepdims=True) + jnp.sum(p_ctx, axis=-1, keepdims=True)
                 + jnp.exp(sink - m))
        o = (jnp.dot(p_own.astype(bf16), v_own[:, kv], preferred_element_type=f32)
             + jnp.dot(p_ctx.astype(bf16), vc_ref[:, kv].astype(bf16), preferred_element_type=f32))
        o_ref[:, h * ATT_HD:(h + 1) * ATT_HD] = o / denom


def ctx_attn(q, k, v, sink):
    B, L = q.shape[:2]
    scale = ATT_HD ** -0.5
    qg = q.reshape(B, L, ATT_KV_HEADS, ATT_GROUP, ATT_HD)
    sink_b = sink.reshape(ATT_KV_HEADS, ATT_GROUP)[None, :, :, None, None]
    s = jnp.einsum('bqhgd,bkhd->bhgqk', qg, k) * scale
    s_sink = jnp.broadcast_to(sink_b, s.shape[:-1] + (1,))
    p = jax.nn.softmax(jnp.concatenate([s_sink, s], axis=-1), axis=-1)[..., 1:]
    return jnp.einsum('bhgqk,bkhd->bqhgd', p, v).reshape(B, L, ATT_HEADS * ATT_HD)


def window_attn_latent(q, k, v, k_ctx, v_ctx, sink):
    B, S, _ = q.shape
    Lc = k_ctx.shape[1]
    nb = S // ATT_BLOCK
    pad = ((0, 0), (ATT_BLOCK, ATT_BLOCK), (0, 0))
    whole = lambda n: pl.BlockSpec((None, n, ATT_KV), lambda b, i: (b, 0, 0))
    return pl.pallas_call(
        functools.partial(_window_attn_kernel, tq=ATT_BLOCK, seq_len=S),
        grid=(B, nb),
        in_specs=[pl.BlockSpec((None, ATT_BLOCK, ATT_Q), lambda b, i: (b, i, 0)),
                  whole(S + 2 * ATT_BLOCK), whole(S + 2 * ATT_BLOCK), whole(Lc), whole(Lc),
                  pl.BlockSpec((1, ATT_HEADS), lambda b, i: (0, 0))],
        out_specs=pl.BlockSpec((None, ATT_BLOCK, ATT_Q), lambda b, i: (b, i, 0)),
        out_shape=jax.ShapeDtypeStruct((B, S, ATT_Q), f32),
        compiler_params=pltpu.CompilerParams(dimension_semantics=("arbitrary", "arbitrary")),
        name="window_attn",
    )(q, jnp.pad(k, pad), jnp.pad(v, pad), k_ctx, v_ctx, sink.reshape(1, ATT_HEADS))


def even_projections(h, w_in, dec_w, dec_b):
    B, L, _ = h.shape
    z = h @ w_in
    aq, ak, av, lr = jnp.split(z[..., IN_SPLITS[3]:], [s - IN_SPLITS[3] for s in IN_SPLITS[4:]], axis=-1)
    lr_f, lr_b = jnp.split(lr, 2, axis=-1)

    def logdecay(lr_d, w, b):
        return jax.nn.log_sigmoid(lr_d @ w + b) / GLA_GATE_TEMP

    ld_f = logdecay(lr_f, dec_w[0], dec_b[0])
    ld_b = logdecay(lr_b, dec_w[1], dec_b[1])
    aq = aq.reshape(B, L, ATT_HEADS, ATT_HD)
    ak = ak.reshape(B, L, ATT_KV_HEADS, ATT_HD)
    av = av.reshape(B, L, ATT_KV_HEADS, ATT_HD)
    return z, ld_f, ld_b, aq, ak, av


def even_mixer_ctx(h, w_in, w_out, dec_w, dec_b, gla_norm, sink):
    z, ld_f, ld_b, aq, ak, av = even_projections(h, w_in, dec_w, dec_b)
    zero = jnp.zeros((h.shape[0], 2, GLA_HEADS, GLA_DK, GLA_DV), f32)
    o_gla, s_fin = gla_bidir_gated(z, ld_f, ld_b, zero, gla_norm)
    o_att = ctx_attn(aq, ak, av, sink)
    return jnp.concatenate([o_gla, o_att], axis=-1) @ w_out, ak, av, s_fin


def even_mixer_lat(h, w_in, w_out, dec_w, dec_b, gla_norm, sink, rope, k_ctx, v_ctx, s0):
    z, ld_f, ld_b, aq, ak, av = even_projections(h, w_in, dec_w, dec_b)
    o_gla, _ = gla_bidir_gated(z, ld_f, ld_b, s0, gla_norm)
    B, S = h.shape[:2]
    o_att = window_attn_latent(apply_axial_rope(aq, rope).reshape(B, S, ATT_Q),
                               apply_axial_rope(ak, rope).reshape(B, S, ATT_KV), av.reshape(B, S, ATT_KV),
                               k_ctx.reshape(B, -1, ATT_KV), v_ctx.reshape(B, -1, ATT_KV), sink)
    return jnp.concatenate([o_gla, o_att], axis=-1) @ w_out


def centred_shift(x):
    xp = jnp.pad(x, ((0, 0), (1, 1), (0, 0)))
    return 0.5 * (xp[:, :-2] + xp[:, 2:])


LANES = 128
SUBLANES = 8
RWKV_TC = 32


def _rwkv_scan_kernel(r_ref, v_ref, k_ref, zw_ref, za_ref, kk_ref, ka_ref, rk_ref, s0_ref, o_ref, bonus_ref, s_ref,
                      *, tc, nv, n_dir_groups):
    backward = pl.program_id(0) >= n_dir_groups

    @pl.when(pl.program_id(1) == 0)
    def _():
        s_ref[...] = s0_ref[...]

    def step(i, carry):
        t = jnp.where(backward, tc - 1 - i, i)
        k_t, r = k_ref[t], r_ref[t]
        kk = k_t * kk_ref[0]
        kk = kk * lax.rsqrt(jnp.sum(kk * kk, axis=0, keepdims=True) + 1e-12)
        a = jax.nn.sigmoid(za_ref[t])
        z = -zw_ref[t]
        softplus = jnp.maximum(z, 0.0) + jnp.log1p(jnp.exp(-jnp.abs(z)))
        w = jnp.exp(-jnp.exp(-softplus - 0.5))
        alpha, beta = -kk, kk * a
        kd = k_t * (1.0 + (a - 1.0) * ka_ref[0])
        bonus_ref[pl.ds(t, 1), :] = jnp.sum(r * 0.5 * kd * rk_ref[0], axis=0, keepdims=True)
        for vb in range(nv // SUBLANES):
            outs = []
            for v in range(vb * SUBLANES, (vb + 1) * SUBLANES):
                s = s_ref[v]
                sa = jnp.sum(s * alpha, axis=0, keepdims=True)
                s_new = s * w + sa * beta + v_ref[t, pl.ds(v, 1), :] * kd
                s_ref[v] = s_new
                outs.append(jnp.sum(s_new * r, axis=0, keepdims=True))
            o_ref[t, vb * SUBLANES:(vb + 1) * SUBLANES, :] = jnp.concatenate(outs, axis=0)
        return carry

    lax.fori_loop(0, tc, step, 0)


def rwkv_scan_lanes(r, v, k, zw, za, k_k, k_a, r_k, s0):
    gd, L, nv, _ = v.shape
    tc = RWKV_TC
    n = L // tc

    def time_block(g, j):
        return jnp.where(g >= gd, n - 1 - j, j)

    def blk(rows, shared):
        return pl.BlockSpec((None, tc, rows, LANES), lambda g, j: (g % gd if shared else g, time_block(g, j), 0, 0))

    const = pl.BlockSpec((None, 1, RWKV_HEAD, LANES), lambda g, j: (g % gd, 0, 0, 0))
    state = pl.BlockSpec((None, nv, RWKV_HEAD, LANES), lambda g, j: (g, 0, 0, 0))
    block_bytes = tc * RWKV_HEAD * LANES * 4
    return pl.pallas_call(
        functools.partial(_rwkv_scan_kernel, tc=tc, nv=nv, n_dir_groups=gd),
        grid=(2 * gd, n),
        in_specs=[blk(RWKV_HEAD, True), blk(nv, True), blk(RWKV_HEAD, True),
                  blk(RWKV_HEAD, False), blk(RWKV_HEAD, False), const, const, const, state],
        out_specs=[blk(nv, False), pl.BlockSpec((None, tc, LANES), lambda g, j: (g, time_block(g, j), 0)), state],
        out_shape=[jax.ShapeDtypeStruct((2 * gd, L, nv, LANES), f32),
                   jax.ShapeDtypeStruct((2 * gd, L, LANES), f32),
                   jax.ShapeDtypeStruct((2 * gd, nv, RWKV_HEAD, LANES), f32)],
        compiler_params=pltpu.CompilerParams(
            dimension_semantics=("arbitrary", "arbitrary"),
            vmem_limit_bytes=2 * 7 * block_bytes + 4 * nv * RWKV_HEAD * LANES * 4 + (8 << 20),
        ),
        name="rwkv_scan",
    )(r, v, k, zw, za, k_k, k_a, r_k, s0)


def rwkv_scan_bidir(r, v, k, zw, za, k_k, k_a, r_k, s_f0, s_b0):
    B, L, H, N = r.shape
    S = B * H
    gd = S // LANES

    def rows(x):
        T = x.shape[-3]
        lead = x.shape[:-4]
        x = x.reshape(-1, B, T, H, N).transpose(0, 2, 4, 1, 3).reshape(-1, T, N, gd, LANES)
        return x.transpose(0, 3, 1, 2, 4).reshape(-1, T, N, LANES) if lead else x[0].transpose(2, 0, 1, 3)

    def rows_inv(y):
        return y.transpose(1, 2, 0, 3).reshape(y.shape[1], N, B, H)

    def state_in(s):
        return rows(s.transpose(0, 3, 1, 2)).transpose(0, 2, 1, 3)

    def state_out(s):
        return rows_inv(s.transpose(0, 2, 1, 3)).transpose(2, 3, 1, 0)

    def const_rows(c):
        return rows(jnp.broadcast_to(c[None, None], (B, 1, H, N)))

    out, bonus, s_fin = rwkv_scan_lanes(rows(r), rows(v), rows(k), rows(zw), rows(za),
                                        const_rows(k_k), const_rows(k_a), const_rows(r_k),
                                        jnp.concatenate([state_in(s_f0), state_in(s_b0)], axis=0))
    o = rows_inv(out[:gd]) + rows_inv(out[gd:])
    bonus = (bonus[:gd] + bonus[gd:]).transpose(1, 0, 2).reshape(L, B, H).transpose(1, 0, 2)
    return o.transpose(2, 0, 3, 1), bonus, state_out(s_fin[:gd]), state_out(s_fin[gd:])


RWKV_SUB = 64
HEAD_PAIRS = D_MODEL // LANES


def _rwkv_rowsum_kernel(rf, vf, af, wf, kf, bf_, rb, vb, ab, wb, kb, bb, s0, of, ob, s, acc, *, tc):
    j = pl.program_id(1)

    @pl.when(j == 0)
    def _():
        s[...] = s0[...]

    row_head = lax.broadcasted_iota(jnp.int32, (2 * LANES, 2 * LANES), 0) // RWKV_HEAD
    col_head = lax.broadcasted_iota(jnp.int32, (2 * LANES, 2 * LANES), 1) // RWKV_HEAD
    ones_bd = (row_head == col_head).astype(bf16)
    sub = lax.broadcasted_iota(jnp.int32, (RWKV_HEAD, LANES), 0)
    lane_in_head = lax.broadcasted_iota(jnp.int32, (RWKV_HEAD, LANES), 1) % RWKV_HEAD
    eye = (sub == lane_in_head).astype(bf16)
    dirs = ((rf, vf, af, wf, kf, bf_), (rb, vb, ab, wb, kb, bb))

    def row_sums(x):
        return jnp.dot(x, ones_bd, preferred_element_type=f32)

    def row(ref, t, hp, reps):
        return jnp.broadcast_to(ref[t, pl.ds(hp, 1), :], (reps, LANES))

    def sub_chunk(sc, carry):
        def step(tt, carry):
            t_f = sc * RWKV_SUB + tt
            pos = ((t_f, jnp.maximum(t_f - 1, 0), tt - 1),
                   (tc - 1 - t_f, jnp.minimum(tc - t_f, tc - 1), RWKV_SUB - tt))
            for d in range(2):
                r_ref, v_ref, a_ref, w_ref, k_ref, b_ref = dirs[d]
                t_now, t_prev, out_lane = pos[d]
                lhs, tiles = [], []
                for hp2 in range(0, HEAD_PAIRS, 2):
                    v_diag = []
                    for hp in (hp2, hp2 + 1):
                        s_t = s[d, :, hp * LANES:(hp + 1) * LANES]
                        lhs.append(jnp.concatenate([(s_t * row(a_ref, t_now, hp, RWKV_HEAD)).astype(bf16),
                                                    (s_t * row(r_ref, t_prev, hp, RWKV_HEAD)).astype(bf16)], axis=1))
                        v_diag.append(eye * jnp.concatenate([row(v_ref, t_now, hp, 16).astype(bf16)] * 4, axis=0))
                        tiles.append(s_t)
                    lhs.append(jnp.concatenate(v_diag, axis=1))
                res = row_sums(jnp.concatenate(lhs, axis=0))
                for hp, s_t in enumerate(tiles):
                    ls = slice(hp * LANES, (hp + 1) * LANES)
                    base = (hp // 2) * 3 * RWKV_HEAD
                    own = res[base + (hp % 2) * RWKV_HEAD:base + (hp % 2 + 1) * RWKV_HEAD]
                    sa, out_prev = own[:, :LANES], own[:, LANES:]
                    v_col = res[base + 2 * RWKV_HEAD:base + 3 * RWKV_HEAD, (hp % 2) * LANES:(hp % 2 + 1) * LANES]
                    s[d, :, ls] = (s_t * row(w_ref, t_now, hp, RWKV_HEAD) + sa * row(b_ref, t_now, hp, RWKV_HEAD)
                                   + v_col * row(k_ref, t_now, hp, RWKV_HEAD))
                    acc[d, :, ls] = jnp.where(lane_in_head == out_lane, out_prev, acc[d, :, ls])
            return carry

        lax.fori_loop(0, RWKV_SUB, step, 0, unroll=8)
        t_last = sc * RWKV_SUB + RWKV_SUB - 1
        last = ((t_last, RWKV_SUB - 1), (tc - 1 - t_last, 0))
        for d in range(2):
            r_ref = dirs[d][0]
            t_now, out_lane = last[d]
            out_mask = jnp.concatenate([lane_in_head == out_lane] * 2, axis=1)
            for hp in range(0, HEAD_PAIRS, 2):
                ls = slice(hp * LANES, (hp + 2) * LANES)
                r_t = jnp.concatenate([row(r_ref, t_now, hp, RWKV_HEAD), row(r_ref, t_now, hp + 1, RWKV_HEAD)], axis=1)
                out_t = row_sums((s[d, :, ls] * r_t).astype(bf16))
                acc[d, :, ls] = jnp.where(out_mask, out_t, acc[d, :, ls])
        of[sc] = acc[0]
        ob[tc // RWKV_SUB - 1 - sc] = acc[1]
        return carry

    lax.fori_loop(0, tc // RWKV_SUB, sub_chunk, 0)


def rwkv_scan_rowsum(r, v, nkk, fwd, bwd, s_f0, s_b0):
    B, L, H, N = r.shape
    D = H * N
    tc = min(L, 256)
    n = L // tc
    fwd_blk = pl.BlockSpec((None, tc, HEAD_PAIRS, LANES), lambda b, j: (b, j, 0, 0))
    bwd_blk = pl.BlockSpec((None, tc, HEAD_PAIRS, LANES), lambda b, j: (b, n - 1 - j, 0, 0))
    r, v, nkk, w_f, k_f, b_f, w_b, k_b, b_b = (t.reshape(B, L, HEAD_PAIRS, LANES) for t in (r, v, nkk, *fwd, *bwd))
    state = pl.BlockSpec((None, 2, RWKV_HEAD, D), lambda b, j: (b, 0, 0, 0))
    nsub = tc // RWKV_SUB
    block_bytes = tc * D * 4
    s0 = jnp.stack([s.transpose(0, 2, 1, 3).reshape(B, N, D) for s in (s_f0, s_b0)], axis=1)
    o_f, o_b, s_fin = pl.pallas_call(
        functools.partial(_rwkv_rowsum_kernel, tc=tc),
        grid=(B, n),
        in_specs=[fwd_blk] * 6 + [bwd_blk] * 6 + [state],
        out_specs=[pl.BlockSpec((None, nsub, RWKV_HEAD, D), lambda b, j: (b, j, 0, 0)),
                   pl.BlockSpec((None, nsub, RWKV_HEAD, D), lambda b, j: (b, n - 1 - j, 0, 0)),
                   state],
        out_shape=[jax.ShapeDtypeStruct((B, L // RWKV_SUB, RWKV_HEAD, D), f32),
                   jax.ShapeDtypeStruct((B, L // RWKV_SUB, RWKV_HEAD, D), f32),
                   jax.ShapeDtypeStruct((B, 2, RWKV_HEAD, D), f32)],
        scratch_shapes=[pltpu.VMEM((2, RWKV_HEAD, D), f32)],
        compiler_params=pltpu.CompilerParams(
            dimension_semantics=("arbitrary", "arbitrary"),
            vmem_limit_bytes=min(V7X_VMEM_BYTES, 2 * 14 * block_bytes + (8 << 20)),
        ),
        name="rwkv_scan_rowsum",
    )(r, v, nkk, w_f, k_f, b_f, r, v, nkk, w_b, k_b, b_b, s0)
    o = (o_f + o_b).reshape(B, L // RWKV_SUB, RWKV_HEAD, H, RWKV_SUB).transpose(0, 1, 4, 3, 2).reshape(B, L, H, N)
    s_fin = s_fin.reshape(B, 2, N, H, N).transpose(0, 1, 3, 2, 4)
    return o, s_fin[:, 0], s_fin[:, 1]


def rwkv_mix(h, mu, wr, wk, wv, wo, w0, w1, w2, a0, a1, a2, g1, g2, k_k, k_a, r_k, lnx_w, lnx_b, s_f0, s_b0):
    B, L, D = h.shape
    H, N = RWKV_HEADS, RWKV_HEAD
    xx = centred_shift(h) - h
    xr, xw, xk, xv, xa, xg = [h + xx * mu[j] for j in range(6)]
    r = (xr @ wr).reshape(B, L, H, N)
    k = (xk @ wk).reshape(B, L, H, N)
    v = (xv @ wv).reshape(B, L, H, N)
    g = jax.nn.sigmoid(xg @ g1) @ g2
    zw = [(w0[d] + jnp.tanh(xw @ w1[d]) @ w2[d]).reshape(B, L, H, N) for d in range(2)]
    za = [(a0[d] + (xa @ a1[d]) @ a2[d]).reshape(B, L, H, N) for d in range(2)]

    if B * H >= LANES:
        o, bonus, s_f, s_b = rwkv_scan_bidir(r, v, k, jnp.stack(zw), jnp.stack(za), k_k.reshape(H, N),
                                             k_a.reshape(H, N), r_k, s_f0, s_b0)
        bonus = bonus[..., None]
    else:
        kk = k * k_k.reshape(H, N)
        kk = kk * lax.rsqrt(jnp.sum(kk * kk, axis=-1, keepdims=True) + 1e-12)

        def direction(d):
            decay = jnp.exp(-jnp.exp(-jax.nn.softplus(-zw[d]) - 0.5))
            a = jax.nn.sigmoid(za[d])
            return decay, k * (1.0 + (a - 1.0) * k_a.reshape(H, N)), kk * a

        fwd, bwd = direction(0), direction(1)
        o, s_f, s_b = rwkv_scan_rowsum(r, v, -kk, fwd, bwd, s_f0, s_b0)
        bonus = jnp.sum(r * 0.5 * (fwd[1] + bwd[1]) * r_k, axis=-1, keepdims=True)
    m = jnp.mean(o, axis=-1, keepdims=True)
    var = jnp.mean(jnp.square(o - m), axis=-1, keepdims=True)
    o = (o - m) * lax.rsqrt(var + RWKV_LN_EPS) * lnx_w.reshape(H, N) + lnx_b.reshape(H, N)
    o = (o + bonus * v).reshape(B, L, D)
    return (o * g) @ wo, s_f, s_b


GROUP_HID = EXP_PER_GROUP * D_EXPERT


def _moe_group_kernel(tile_group_ref, n_tiles_ref, xs_ref, wg_ref, wu_ref, wd_ref, ys_ref,
                      wg_bf, wu_bf, wd_bf):
    i = pl.program_id(0)
    group = tile_group_ref[i]
    prev_group = tile_group_ref[jnp.maximum(i - 1, 0)]
    group_changed = jnp.logical_or(i == 0, group != prev_group)

    @pl.when(group_changed)
    def _():
        for e in range(EXP_PER_GROUP):
            hs = slice(e * D_EXPERT, (e + 1) * D_EXPERT)
            wg_bf[:, hs] = wg_ref[e].astype(bf16)
            wu_bf[:, hs] = wu_ref[e].astype(bf16)
            wd_bf[hs, :] = wd_ref[e].astype(bf16)

    @pl.when(i < n_tiles_ref[0])
    def _():
        half = wg_bf.shape[0] // 2
        logits = lax.bitcast_convert_type(xs_ref[:, half:], f32)
        lane = lax.broadcasted_iota(jnp.int32, logits.shape, 1)
        grp = jnp.where(lane < N_GROUPS, logits, -jnp.inf)
        grp = jnp.exp(grp - jnp.max(grp, axis=-1, keepdims=True))
        p_group = (jnp.sum(jnp.where(lane == group, grp, 0.0), axis=-1, keepdims=True)
                   / jnp.sum(grp, axis=-1, keepdims=True))
        first = N_GROUPS + group * EXP_PER_GROUP
        le = [jnp.sum(jnp.where(lane == first + e, logits, 0.0), axis=-1, keepdims=True)
              for e in range(EXP_PER_GROUP)]
        chosen = []
        for e in range(EXP_PER_GROUP):
            ahead = sum(((le[k] > le[e]) | ((le[k] == le[e]) & (k < e))).astype(jnp.int32)
                        for k in range(EXP_PER_GROUP) if k != e)
            chosen.append(ahead < TOP_K)
        top = functools.reduce(jnp.maximum, [jnp.where(c, v, -jnp.inf) for c, v in zip(chosen, le)])
        ex = [jnp.where(c, jnp.exp(v - top), 0.0) for c, v in zip(chosen, le)]
        scale = p_group / sum(ex)
        packed = xs_ref[:, :half]
        x = jnp.concatenate([lax.bitcast_convert_type(packed << 16, f32),
                             lax.bitcast_convert_type(packed & jnp.uint32(0xFFFF0000), f32)], axis=1).astype(bf16)
        g = jnp.dot(x, wg_bf[...], preferred_element_type=f32)
        u = jnp.dot(x, wu_bf[...], preferred_element_type=f32)
        gate = jnp.concatenate([jnp.broadcast_to(ex[e] * scale, (MOE_TILE_M, D_EXPERT))
                                for e in range(EXP_PER_GROUP)], axis=1)
        hid = (g * jax.nn.sigmoid(g)) * u * gate
        ys_ref[...] = jnp.dot(hid.astype(bf16), wd_bf[...], preferred_element_type=f32)

    @pl.when(i >= n_tiles_ref[0])
    def _():
        ys_ref[...] = jnp.zeros_like(ys_ref)


def moe_group_experts(xs, tile_group, n_tiles, w_gate, w_up, w_down, layer):
    P = xs.shape[0]
    D = 2 * (xs.shape[1] - LANES)
    max_tiles = P // MOE_TILE_M
    weight_bytes = 3 * EXP_PER_GROUP * D * D_EXPERT * (4 + 2)
    tile_bytes = 2 * MOE_TILE_M * (2 * D * 4 + LANES * 4) + 4 * MOE_TILE_M * GROUP_HID * 4
    once = pl.Buffered(1)
    grid_spec = pltpu.PrefetchScalarGridSpec(
        num_scalar_prefetch=2,
        grid=(max_tiles,),
        in_specs=[
            pl.BlockSpec((MOE_TILE_M, D // 2 + LANES), lambda i, tg, nt: (i, 0)),
            pl.BlockSpec((None, EXP_PER_GROUP, D, D_EXPERT), lambda i, tg, nt: (layer, tg[i], 0, 0),
                         pipeline_mode=once),
            pl.BlockSpec((None, EXP_PER_GROUP, D, D_EXPERT), lambda i, tg, nt: (layer, tg[i], 0, 0),
                         pipeline_mode=once),
            pl.BlockSpec((None, EXP_PER_GROUP, D_EXPERT, D), lambda i, tg, nt: (layer, tg[i], 0, 0),
                         pipeline_mode=once),
        ],
        out_specs=pl.BlockSpec((MOE_TILE_M, D), lambda i, tg, nt: (i, 0)),
        scratch_shapes=[
            pltpu.VMEM((D, GROUP_HID), bf16),
            pltpu.VMEM((D, GROUP_HID), bf16),
            pltpu.VMEM((GROUP_HID, D), bf16),
        ],
    )
    return pl.pallas_call(
        _moe_group_kernel,
        grid_spec=grid_spec,
        out_shape=jax.ShapeDtypeStruct((P, D), f32),
        compiler_params=pltpu.CompilerParams(
            dimension_semantics=("arbitrary",),
            vmem_limit_bytes=min(V7X_VMEM_BYTES - (4 << 20), weight_bytes + tile_bytes + (8 << 20)),
        ),
        name="moe_group_experts",
    )(tile_group, n_tiles, xs, w_gate, w_up, w_down)


def router_params(w_grp, b_grp, w_exp, b_exp):
    w = jnp.concatenate([w_grp, w_exp], axis=1)
    b = jnp.concatenate([b_grp, b_exp], axis=0)[None, :]
    return jnp.pad(w, ((0, 0), (0, LANES - w.shape[1]))), jnp.pad(b, ((0, 0), (0, LANES - b.shape[1])))


def hier_moe(t_aug, w_gate, w_up, w_down, layer):
    T = t_aug.shape[0]
    half = t_aug.shape[1] - LANES
    g_top = jnp.argmax(lax.bitcast_convert_type(t_aug[:, half:half + N_GROUPS], f32), axis=-1).astype(jnp.int32)
    in_group = g_top[:, None] == jnp.arange(N_GROUPS, dtype=jnp.int32)[None, :]

    order = jnp.argsort(g_top, stable=True).astype(jnp.int32)
    grp_i32 = in_group.astype(jnp.int32)
    counts = jnp.sum(grp_i32, axis=0)
    rank = jnp.sum(jnp.where(in_group, jnp.cumsum(grp_i32, axis=0), 0), axis=1) - 1
    tiles_per = (counts + MOE_TILE_M - 1) // MOE_TILE_M
    tile_end = jnp.cumsum(tiles_per)
    start_padded = (tile_end - tiles_per) * MOE_TILE_M
    start_sorted = jnp.cumsum(counts) - counts
    max_tiles = T // MOE_TILE_M + N_GROUPS
    n_tiles = tile_end[-1:].astype(jnp.int32)
    tile_ids = jnp.arange(max_tiles, dtype=jnp.int32)
    tile_group = jnp.minimum(jnp.sum((tile_ids[:, None] >= tile_end[None, :]).astype(jnp.int32), axis=1),
                             N_GROUPS - 1).astype(jnp.int32)
    tile_group = jnp.where(tile_ids < n_tiles[0], tile_group, tile_group[jnp.maximum(n_tiles[0] - 1, 0)])
    row_group = jnp.repeat(tile_group, MOE_TILE_M)
    row_in_group = jnp.arange(max_tiles * MOE_TILE_M, dtype=jnp.int32) - start_padded[row_group]
    row_valid = (row_in_group >= 0) & (row_in_group < counts[row_group])
    row_tok = jnp.where(row_valid, order[jnp.clip(start_sorted[row_group] + row_in_group, 0, T - 1)], 0)
    pos = start_padded[g_top] + rank

    xs = jnp.take(t_aug, row_tok, axis=0)
    ys = moe_group_experts(xs, tile_group, n_tiles, w_gate, w_up, w_down, layer)
    return jnp.take(ys, pos, axis=0)


def kernel(x_prompt, x_sample, c, cache_attn_k, cache_attn_v, state_gla, state_rwkv, c_ctx, ada_w, ada_b, norm_mix, norm_ffn, norm_out, ev_w_in, ev_w_out, gla_dec_w, gla_dec_b, gla_norm, att_sink, rw_mu, rw_wr, rw_wk, rw_wv, rw_wo, rw_w0, rw_w1, rw_w2, rw_a0, rw_a1, rw_a2, rw_g1, rw_g2, rw_kk, rw_ka, rw_rk, rw_lnx_w, rw_lnx_b, moe_w_grp, moe_b_grp, moe_w_exp, moe_b_exp, moe_w_gate, moe_w_up, moe_w_down):
    n_lat = x_sample.shape[1]
    rows = n_lat // GRID_W
    row_pos = jnp.repeat(jnp.arange(rows), GRID_W)
    col_pos = jnp.tile(jnp.arange(GRID_W), rows)
    rope = rope_tables(row_pos, col_pos)
    Bc, Lc, D = x_prompt.shape
    Bl, Ll, _ = x_sample.shape

    n_ctx = Bc * Lc
    x_all = jnp.concatenate([x_prompt.reshape(n_ctx, D), x_sample.reshape(Bl * Ll, D)], axis=0)
    cond_all = jnp.concatenate([c_ctx[None, :], c], axis=0)
    mods = [ada_mod(cond_all, ada_w[l], ada_b[l]) for l in range(DEPTH)]
    no_mod = jnp.zeros((1 + Bl, 1, D), f32)

    hc = modulate(rmsnorm(x_prompt, norm_mix[0]), mods[0][0][:1], mods[0][1][:1])
    hl = modulate(rmsnorm(x_sample, norm_mix[0]), mods[0][0][1:], mods[0][1][1:])
    ks_out, vs_out, gla_out, rwkv_out = [], [], [], []
    for l in range(DEPTH):
        _, _, gate1, shift2, scale2, gate2 = mods[l]
        i = l // 2
        if l % 2 == 0:
            oc, k_c, v_c, s_gla = even_mixer_ctx(hc, ev_w_in[i], ev_w_out[i], gla_dec_w[i], gla_dec_b[i],
                                                 gla_norm[i], att_sink[i])
            ol = even_mixer_lat(hl, ev_w_in[i], ev_w_out[i], gla_dec_w[i], gla_dec_b[i], gla_norm[i], att_sink[i],
                                rope, cache_attn_k[:, i], cache_attn_v[:, i], state_gla[:, i])
            ks_out.append(k_c)
            vs_out.append(v_c)
            gla_out.append(s_gla)
        else:
            rw = (rw_mu[i], rw_wr[i], rw_wk[i], rw_wv[i], rw_wo[i], rw_w0[i], rw_w1[i], rw_w2[i], rw_a0[i],
                  rw_a1[i], rw_a2[i], rw_g1[i], rw_g2[i], rw_kk[i], rw_ka[i], rw_rk[i], rw_lnx_w[i], rw_lnx_b[i])
            zero = jnp.zeros((Bc, RWKV_HEADS, RWKV_HEAD, RWKV_HEAD), f32)
            oc, s_f, s_b = rwkv_mix(hc, *rw, zero, zero)
            ol, _, _ = rwkv_mix(hl, *rw, state_rwkv[:, i, 0], state_rwkv[:, i, 1])
            rwkv_out.append(jnp.stack([s_f, s_b], axis=1))
        x_all, t_aug = resid_norm_mod(x_all, (oc.reshape(n_ctx, D), ol.reshape(Bl * Ll, D)), n_ctx, Ll,
                                      gate1, norm_ffn[l], shift2, scale2,
                                      router_params(moe_w_grp[l], moe_b_grp[l], moe_w_exp[l], moe_b_exp[l]))
        y = hier_moe(t_aug, moe_w_gate, moe_w_up, moe_w_down, l)
        if l + 1 < DEPTH:
            x_all, (hc, hl) = resid_norm_mod(x_all, y, n_ctx, Ll, gate2, norm_mix[l + 1], mods[l + 1][0],
                                             mods[l + 1][1])
        else:
            x_all, (hc, hl) = resid_norm_mod(x_all, y, n_ctx, Ll, gate2, norm_out, no_mod, no_mod)
        hc, hl = hc.reshape(Bc, Lc, D), hl.reshape(Bl, Ll, D)
    y_prompt, y_sample = hc, hl

    new_attn_k = jnp.stack(ks_out, axis=1)
    new_attn_v = jnp.stack(vs_out, axis=1)
    new_gla = jnp.stack(gla_out, axis=1)
    new_rwkv = jnp.stack(rwkv_out, axis=1)
    return (y_prompt, y_sample, new_attn_k, new_attn_v, new_gla, new_rwkv)
```

```python
import functools

import jax
import jax.numpy as jnp
import numpy as np
from jax import lax
from jax.experimental import pallas as pl
from jax.experimental.pallas import tpu as pltpu

D_MODEL = 1024
DEPTH = 4
GRID_W = 64
EPS = 1e-6
GLA_HEADS = 4
GLA_DK = 64
GLA_DV = 128
GLA_LOWRANK = 16
GLA_GATE_TEMP = 16.0
GLA_CHUNK = 64
ATT_HEADS = 8
ATT_KV_HEADS = 2
ATT_GROUP = ATT_HEADS // ATT_KV_HEADS
ATT_HD = 64
WINDOW = 128
ATT_BLOCK = 128
ROPE_BASE = 10000.0
ROPE_AXIS_DIMS = ATT_HD // 2
NEG = -1e30
IN_WIDTHS = (GLA_HEADS * GLA_DK, GLA_HEADS * GLA_DK, GLA_HEADS * GLA_DV, GLA_HEADS * GLA_DV,
             ATT_HEADS * ATT_HD, ATT_KV_HEADS * ATT_HD, ATT_KV_HEADS * ATT_HD, 2 * GLA_LOWRANK)
IN_SPLITS = tuple(int(v) for v in np.cumsum(IN_WIDTHS)[:-1])
RWKV_HEAD = 64
RWKV_HEADS = D_MODEL // RWKV_HEAD
RWKV_LN_EPS = 64e-5
N_GROUPS = 4
EXP_PER_GROUP = 4
N_EXPERTS = N_GROUPS * EXP_PER_GROUP
TOP_K = 2
D_EXPERT = 512

V7X_VMEM_BYTES = 64 * 1024 * 1024
MOE_TILE_M = 256

f32 = jnp.float32
bf16 = jnp.bfloat16


def rmsnorm(x, w):
    y = x * lax.rsqrt(jnp.mean(x * x, axis=-1, keepdims=True) + EPS)
    return y * w


def ada_mod(cond, w, b):
    mod = jax.nn.silu(cond) @ w + b
    return [m[:, None, :] for m in jnp.split(mod, 6, axis=-1)]


def modulate(x, shift, scale):
    return x * (1.0 + scale) + shift


NORM_ROWS = 512


def _resid_norm_mod_kernel(*refs, n_ctx_blocks, split):
    is_ctx = pl.program_id(0) < n_ctx_blocks
    if split:
        x_ref, oc_ref, ol_ref, gate_ref, w_ref, shift_ref, scale_ref, rw_ref, rb_ref, xn_ref, t_ref = refs
        o = jnp.where(is_ctx, oc_ref[...], ol_ref[...])
    else:
        x_ref, o_ref, gate_ref, w_ref, shift_ref, scale_ref, xn_ref, tc_ref, tl_ref = refs
        o = o_ref[...]
    x = x_ref[...] + gate_ref[0] * o
    xn_ref[...] = x
    y = x * lax.rsqrt(jnp.mean(x * x, axis=-1, keepdims=True) + EPS) * w_ref[...]
    t = y * (1.0 + scale_ref[0]) + shift_ref[0]
    if split:
        half = t.shape[1] // 2
        logits = jnp.dot(t, rw_ref[...], precision=lax.Precision.HIGHEST, preferred_element_type=f32) + rb_ref[...]
        bits = lax.bitcast_convert_type(t.astype(bf16).astype(f32), jnp.uint32)
        t_ref[:, :half] = (bits[:, :half] >> 16) | (bits[:, half:] & jnp.uint32(0xFFFF0000))
        t_ref[:, half:] = lax.bitcast_convert_type(logits, jnp.uint32)
    else:
        @pl.when(is_ctx)
        def _():
            tc_ref[...] = t

        @pl.when(jnp.logical_not(is_ctx))
        def _():
            tl_ref[...] = t


def resid_norm_mod(x, o, n_ctx, lat_len, gate, w, shift, scale, router=None):
    T, D = x.shape
    ncb = n_ctx // NORM_ROWS
    per_lat = lat_len // NORM_ROWS
    rows = pl.BlockSpec((NORM_ROWS, D), lambda i: (i, 0))
    mod = pl.BlockSpec((1, 1, D), lambda i: (jnp.where(i < ncb, 0, 1 + (i - ncb) // per_lat), 0, 0))
    pair = [pl.BlockSpec((NORM_ROWS, D), lambda i: (jnp.minimum(i, ncb - 1), 0)),
            pl.BlockSpec((NORM_ROWS, D), lambda i: (jnp.maximum(i - ncb, 0), 0))]
    split = isinstance(o, tuple)
    assert split == (router is not None)
    if split:
        o_specs, o_args = pair, list(o)
        t_specs = [pl.BlockSpec((NORM_ROWS, D // 2 + LANES), lambda i: (i, 0))]
        t_shapes = [jax.ShapeDtypeStruct((T, D // 2 + LANES), jnp.uint32)]
        r_specs = [pl.BlockSpec((D, LANES), lambda i: (0, 0)), pl.BlockSpec((1, LANES), lambda i: (0, 0))]
        r_args = list(router)
    else:
        o_specs, o_args, t_specs = [rows], [o], pair
        t_shapes = [jax.ShapeDtypeStruct((n_ctx, D), f32), jax.ShapeDtypeStruct((T - n_ctx, D), f32)]
        r_specs, r_args = [], []
    out = pl.pallas_call(
        functools.partial(_resid_norm_mod_kernel, n_ctx_blocks=ncb, split=split),
        grid=(T // NORM_ROWS,),
        in_specs=[rows, *o_specs, mod, pl.BlockSpec((1, D), lambda i: (0, 0)), mod, mod, *r_specs],
        out_specs=[rows, *t_specs],
        out_shape=[jax.ShapeDtypeStruct((T, D), f32), *t_shapes],
        compiler_params=pltpu.CompilerParams(dimension_semantics=("arbitrary",)),
        name="resid_norm_mod",
    )(x, *o_args, gate, w.reshape(1, D), shift, scale, *r_args)
    return (out[0], out[1]) if split else (out[0], (out[1], out[2]))


def rope_tables(row_pos, col_pos):
    freqs = ROPE_BASE ** (-jnp.arange(0, ROPE_AXIS_DIMS, 2, dtype=f32) / ROPE_AXIS_DIMS)
    ang_r = row_pos.astype(f32)[:, None] * freqs[None, :]
    ang_c = col_pos.astype(f32)[:, None] * freqs[None, :]
    return jnp.cos(ang_r), jnp.sin(ang_r), jnp.cos(ang_c), jnp.sin(ang_c)


def rotate(x, cos, sin):
    x1, x2 = jnp.split(x, 2, axis=-1)
    cos = cos[None, :, None, :]
    sin = sin[None, :, None, :]
    return jnp.concatenate([x1 * cos - x2 * sin, x1 * sin + x2 * cos], axis=-1)


def apply_axial_rope(x, tables):
    cr, sr, cc, sc = tables
    xr, xc = jnp.split(x, 2, axis=-1)
    return jnp.concatenate([rotate(xr, cr, sr), rotate(xc, cc, sc)], axis=-1)


GLA_QK = GLA_HEADS * GLA_DK
GLA_V = GLA_HEADS * GLA_DV


def _gla_kernel(q_ref, v_ref, gg_ref, ldf_ref, ldb_ref, kt_ref, ldft_ref, ldbt_ref, s0_ref, norm_ref,
                o_ref, s_ref, *, seq_len):
    n = seq_len // GLA_CHUNK
    C = GLA_CHUNK
    ti = lax.broadcasted_iota(jnp.int32, (C, C), 0)
    tj = lax.broadcasted_iota(jnp.int32, (C, C), 1)
    keep = (tj <= ti, tj >= ti)
    tri = tuple(m.astype(bf16) for m in keep)
    tri_t = (tri[1], tri[0])
    ones = jnp.ones((C, GLA_DV), bf16)
    s_ref[...] = s0_ref[...]
    o_ref[...] = jnp.zeros_like(o_ref)
    ld_refs = ((ldf_ref, ldft_ref), (ldb_ref, ldbt_ref))

    def split3(x):
        hi = x.astype(bf16)
        rest = x - hi.astype(f32)
        mid = rest.astype(bf16)
        return hi, mid, (rest - mid.astype(f32)).astype(bf16)

    def sums(a, b):
        if isinstance(b, tuple):
            return sum(jnp.dot(a, p, preferred_element_type=f32) for p in b)
        return sum(jnp.dot(p, b, preferred_element_type=f32) for p in a)

    def chunk(c, carry):
        pending = []
        for d in range(2):
            cc = c if d == 0 else n - 1 - c
            rows = pl.ds(pl.multiple_of(cc * C, C), C)
            ld_ref, ldt_ref = ld_refs[d]
            ld_t3 = split3(ldt_ref[cc])
            b_all = sums(tri[d], split3(ld_ref[rows, :]))
            bt_all = sums(ld_t3, tri_t[d])
            total_all = sums(ld_t3, ones)
            for h in range(GLA_HEADS):
                ks = slice(h * GLA_DK, (h + 1) * GLA_DK)
                vs = slice(h * GLA_DV, (h + 1) * GLA_DV)
                q = q_ref[rows, ks] * (GLA_DK ** -0.5)
                v = v_ref[rows, vs].astype(bf16)
                k_t = kt_ref[cc, ks, :]
                b, b_t, total = b_all[:, ks], bt_all[ks], total_all[ks]
                q_e = (q * jnp.exp(b)).astype(bf16)
                k_e = (k_t * jnp.exp(-b_t)).astype(bf16)
                k_s = (k_t * jnp.exp(total[:, :C] - b_t)).astype(bf16)
                att = jnp.where(keep[d], jnp.dot(q_e, k_e, preferred_element_type=f32), 0.0).astype(bf16)
                s = s_ref[d, h]
                o = (jnp.dot(q_e, s.astype(bf16), preferred_element_type=f32)
                     + jnp.dot(att, v, preferred_element_type=f32))
                s_new = s * jnp.exp(total) + jnp.dot(k_s, v, preferred_element_type=f32)
                pending.append((d, h, rows, vs, o_ref[rows, vs] + o, s_new))
        for d, h, rows, vs, o, s_new in pending:
            o_ref[rows, vs] = o
            s_ref[d, h] = s_new
        return carry

    lax.fori_loop(0, n, chunk, 0)
    for h in range(GLA_HEADS):
        vs = slice(h * GLA_DV, (h + 1) * GLA_DV)
        x = o_ref[:, vs]
        y = x * lax.rsqrt(jnp.mean(x * x, axis=-1, keepdims=True) + EPS) * norm_ref[...]
        g = gg_ref[:, vs]
        o_ref[:, vs] = y * (g * jax.nn.sigmoid(g))


def gla_bidir_gated(z, ld_f, ld_b, s0, gla_norm):
    B, L, _ = z.shape
    n = L // GLA_CHUNK
    assert n % 2 == 0

    def per_chunk_t(a):
        return a.reshape(B, n, GLA_CHUNK, GLA_QK).transpose(0, 1, 3, 2)

    state = pl.BlockSpec((None, 2, GLA_HEADS, GLA_DK, GLA_DV), lambda b: (b, 0, 0, 0, 0))
    ld = pl.BlockSpec((None, L, GLA_QK), lambda b: (b, 0, 0))
    ld_t = pl.BlockSpec((None, n, GLA_QK, GLA_CHUNK), lambda b: (b, 0, 0, 0))
    return pl.pallas_call(
        functools.partial(_gla_kernel, seq_len=L),
        grid=(B,),
        in_specs=[pl.BlockSpec((None, L, GLA_QK), lambda b: (b, 0, 0)),
                  pl.BlockSpec((None, L, GLA_V), lambda b: (b, 0, 1)),
                  pl.BlockSpec((None, L, GLA_V), lambda b: (b, 0, 2)),
                  ld, ld, ld_t, ld_t, ld_t, state,
                  pl.BlockSpec((1, GLA_DV), lambda b: (0, 0))],
        out_specs=[pl.BlockSpec((None, L, GLA_V), lambda b: (b, 0, 0)), state],
        out_shape=[jax.ShapeDtypeStruct((B, L, GLA_V), f32),
                   jax.ShapeDtypeStruct((B, 2, GLA_HEADS, GLA_DK, GLA_DV), f32)],
        compiler_params=pltpu.CompilerParams(
            dimension_semantics=("arbitrary",),
            vmem_limit_bytes=min(V7X_VMEM_BYTES, 2 * L * (9 * GLA_QK + 3 * GLA_V) * 4 + (16 << 20)),
        ),
        name="gla_bidir",
    )(z, z, z, ld_f, ld_b, per_chunk_t(z[..., GLA_QK:2 * GLA_QK]), per_chunk_t(ld_f), per_chunk_t(ld_b),
      s0, gla_norm.reshape(1, GLA_DV))


ATT_Q = ATT_HEADS * ATT_HD
ATT_KV = ATT_KV_HEADS * ATT_HD


def _window_attn_kernel(q_ref, k_ref, v_ref, kc_ref, vc_ref, sink_ref, o_ref, *, tq, seq_len):
    scale = ATT_HD ** -0.5
    nt = (((1,), (1,)), ((), ()))
    start = pl.multiple_of(pl.program_id(1) * tq, tq)
    k_own = k_ref[pl.ds(start, 3 * tq), :].astype(bf16)
    v_own = v_ref[pl.ds(start, 3 * tq), :].astype(bf16)
    qpos = start + lax.broadcasted_iota(jnp.int32, (tq, 3 * tq), 0)
    kpos = start - tq + lax.broadcasted_iota(jnp.int32, (tq, 3 * tq), 1)
    valid = (jnp.abs(qpos - kpos) <= WINDOW) & (kpos >= 0) & (kpos < seq_len)
    for h in range(ATT_HEADS):
        g = h // ATT_GROUP
        kv = slice(g * ATT_HD, (g + 1) * ATT_HD)
        q = q_ref[:, h * ATT_HD:(h + 1) * ATT_HD].astype(bf16)
        s_own = lax.dot_general(q, k_own[:, kv], nt, preferred_element_type=f32) * scale
        s_own = jnp.where(valid, s_own, NEG)
        s_ctx = lax.dot_general(q, kc_ref[:, kv].astype(bf16), nt, preferred_element_type=f32) * scale
        sink = sink_ref[0:1, h:h + 1]
        m = jnp.maximum(jnp.maximum(jnp.max(s_own, axis=-1, keepdims=True), jnp.max(s_ctx, axis=-1, keepdims=True)),
                        sink)
        p_own = jnp.exp(s_own - m)
        p_ctx = jnp.exp(s_ctx - m)
        denom = (jnp.sum(p_own, axis=-1, keepdims=True) + jnp.sum(p_ctx, axis=-1, keepdims=True)
                 + jnp.exp(sink - m))
        o = (jnp.dot(p_own.astype(bf16), v_own[:, kv], preferred_element_type=f32)
             + jnp.dot(p_ctx.astype(bf16), vc_ref[:, kv].astype(bf16), preferred_element_type=f32))
        o_ref[:, h * ATT_HD:(h + 1) * ATT_HD] = o / denom


def ctx_attn(q, k, v, sink):
    B, L = q.shape[:2]
    scale = ATT_HD ** -0.5
    qg = q.reshape(B, L, ATT_KV_HEADS, ATT_GROUP, ATT_HD)
    sink_b = sink.reshape(ATT_KV_HEADS, ATT_GROUP)[None, :, :, None, None]
    s = jnp.einsum('bqhgd,bkhd->bhgqk', qg, k) * scale
    s_sink = jnp.broadcast_to(sink_b, s.shape[:-1] + (1,))
    p = jax.nn.softmax(jnp.concatenate([s_sink, s], axis=-1), axis=-1)[..., 1:]
    return jnp.einsum('bhgqk,bkhd->bqhgd', p, v).reshape(B, L, ATT_HEADS * ATT_HD)


def window_attn_latent(q, k, v, k_ctx, v_ctx, sink):
    B, S, _ = q.shape
    Lc = k_ctx.shape[1]
    nb = S // ATT_BLOCK
    pad = ((0, 0), (ATT_BLOCK, ATT_BLOCK), (0, 0))
    whole = lambda n: pl.BlockSpec((None, n, ATT_KV), lambda b, i: (b, 0, 0))
    return pl.pallas_call(
        functools.partial(_window_attn_kernel, tq=ATT_BLOCK, seq_len=S),
        grid=(B, nb),
        in_specs=[pl.BlockSpec((None, ATT_BLOCK, ATT_Q), lambda b, i: (b, i, 0)),
                  whole(S + 2 * ATT_BLOCK), whole(S + 2 * ATT_BLOCK), whole(Lc), whole(Lc),
                  pl.BlockSpec((1, ATT_HEADS), lambda b, i: (0, 0))],
        out_specs=pl.BlockSpec((None, ATT_BLOCK, ATT_Q), lambda b, i: (b, i, 0)),
        out_shape=jax.ShapeDtypeStruct((B, S, ATT_Q), f32),
        compiler_params=pltpu.CompilerParams(dimension_semantics=("arbitrary", "arbitrary")),
        name="window_attn",
    )(q, jnp.pad(k, pad), jnp.pad(v, pad), k_ctx, v_ctx, sink.reshape(1, ATT_HEADS))


def even_projections(h, w_in, dec_w, dec_b):
    B, L, _ = h.shape
    z = h @ w_in
    aq, ak, av, lr = jnp.split(z[..., IN_SPLITS[3]:], [s - IN_SPLITS[3] for s in IN_SPLITS[4:]], axis=-1)
    lr_f, lr_b = jnp.split(lr, 2, axis=-1)

    def logdecay(lr_d, w, b):
        return jax.nn.log_sigmoid(lr_d @ w + b) / GLA_GATE_TEMP

    ld_f = logdecay(lr_f, dec_w[0], dec_b[0])
    ld_b = logdecay(lr_b, dec_w[1], dec_b[1])
    aq = aq.reshape(B, L, ATT_HEADS, ATT_HD)
    ak = ak.reshape(B, L, ATT_KV_HEADS, ATT_HD)
    av = av.reshape(B, L, ATT_KV_HEADS, ATT_HD)
    return z, ld_f, ld_b, aq, ak, av


def even_mixer_ctx(h, w_in, w_out, dec_w, dec_b, gla_norm, sink):
    z, ld_f, ld_b, aq, ak, av = even_projections(h, w_in, dec_w, dec_b)
    zero = jnp.zeros((h.shape[0], 2, GLA_HEADS, GLA_DK, GLA_DV), f32)
    o_gla, s_fin = gla_bidir_gated(z, ld_f, ld_b, zero, gla_norm)
    o_att = ctx_attn(aq, ak, av, sink)
    return jnp.concatenate([o_gla, o_att], axis=-1) @ w_out, ak, av, s_fin


def even_mixer_lat(h, w_in, w_out, dec_w, dec_b, gla_norm, sink, rope, k_ctx, v_ctx, s0):
    z, ld_f, ld_b, aq, ak, av = even_projections(h, w_in, dec_w, dec_b)
    o_gla, _ = gla_bidir_gated(z, ld_f, ld_b, s0, gla_norm)
    B, S = h.shape[:2]
    o_att = window_attn_latent(apply_axial_rope(aq, rope).reshape(B, S, ATT_Q),
                               apply_axial_rope(ak, rope).reshape(B, S, ATT_KV), av.reshape(B, S, ATT_KV),
                               k_ctx.reshape(B, -1, ATT_KV), v_ctx.reshape(B, -1, ATT_KV), sink)
    return jnp.concatenate([o_gla, o_att], axis=-1) @ w_out


def centred_shift(x):
    xp = jnp.pad(x, ((0, 0), (1, 1), (0, 0)))
    return 0.5 * (xp[:, :-2] + xp[:, 2:])


LANES = 128
SUBLANES = 8
RWKV_TC = 32


def _rwkv_scan_kernel(r_ref, v_ref, k_ref, zw_ref, za_ref, kk_ref, ka_ref, rk_ref, s0_ref, o_ref, bonus_ref, s_ref,
                      *, tc, nv, n_dir_groups):
    backward = pl.program_id(0) >= n_dir_groups

    @pl.when(pl.program_id(1) == 0)
    def _():
        s_ref[...] = s0_ref[...]

    def step(i, carry):
        t = jnp.where(backward, tc - 1 - i, i)
        k_t, r = k_ref[t], r_ref[t]
        kk = k_t * kk_ref[0]
        kk = kk * lax.rsqrt(jnp.sum(kk * kk, axis=0, keepdims=True) + 1e-12)
        a = jax.nn.sigmoid(za_ref[t])
        z = -zw_ref[t]
        softplus = jnp.maximum(z, 0.0) + jnp.log1p(jnp.exp(-jnp.abs(z)))
        w = jnp.exp(-jnp.exp(-softplus - 0.5))
        alpha, beta = -kk, kk * a
        kd = k_t * (1.0 + (a - 1.0) * ka_ref[0])
        bonus_ref[pl.ds(t, 1), :] = jnp.sum(r * 0.5 * kd * rk_ref[0], axis=0, keepdims=True)
        for vb in range(nv // SUBLANES):
            outs = []
            for v in range(vb * SUBLANES, (vb + 1) * SUBLANES):
                s = s_ref[v]
                sa = jnp.sum(s * alpha, axis=0, keepdims=True)
                s_new = s * w + sa * beta + v_ref[t, pl.ds(v, 1), :] * kd
                s_ref[v] = s_new
                outs.append(jnp.sum(s_new * r, axis=0, keepdims=True))
            o_ref[t, vb * SUBLANES:(vb + 1) * SUBLANES, :] = jnp.concatenate(outs, axis=0)
        return carry

    lax.fori_loop(0, tc, step, 0)


def rwkv_scan_lanes(r, v, k, zw, za, k_k, k_a, r_k, s0):
    gd, L, nv, _ = v.shape
    tc = RWKV_TC
    n = L // tc

    def time_block(g, j):
        return jnp.where(g >= gd, n - 1 - j, j)

    def blk(rows, shared):
        return pl.BlockSpec((None, tc, rows, LANES), lambda g, j: (g % gd if shared else g, time_block(g, j), 0, 0))

    const = pl.BlockSpec((None, 1, RWKV_HEAD, LANES), lambda g, j: (g % gd, 0, 0, 0))
    state = pl.BlockSpec((None, nv, RWKV_HEAD, LANES), lambda g, j: (g, 0, 0, 0))
    block_bytes = tc * RWKV_HEAD * LANES * 4
    return pl.pallas_call(
        functools.partial(_rwkv_scan_kernel, tc=tc, nv=nv, n_dir_groups=gd),
        grid=(2 * gd, n),
        in_specs=[blk(RWKV_HEAD, True), blk(nv, True), blk(RWKV_HEAD, True),
                  blk(RWKV_HEAD, False), blk(RWKV_HEAD, False), const, const, const, state],
        out_specs=[blk(nv, False), pl.BlockSpec((None, tc, LANES), lambda g, j: (g, time_block(g, j), 0)), state],
        out_shape=[jax.ShapeDtypeStruct((2 * gd, L, nv, LANES), f32),
                   jax.ShapeDtypeStruct((2 * gd, L, LANES), f32),
                   jax.ShapeDtypeStruct((2 * gd, nv, RWKV_HEAD, LANES), f32)],
        compiler_params=pltpu.CompilerParams(
            dimension_semantics=("arbitrary", "arbitrary"),
            vmem_limit_bytes=2 * 7 * block_bytes + 4 * nv * RWKV_HEAD * LANES * 4 + (8 << 20),
        ),
        name="rwkv_scan",
    )(r, v, k, zw, za, k_k, k_a, r_k, s0)


def rwkv_scan_bidir(r, v, k, zw, za, k_k, k_a, r_k, s_f0, s_b0):
    B, L, H, N = r.shape
    S = B * H
    gd = S // LANES

    def rows(x):
        T = x.shape[-3]
        lead = x.shape[:-4]
        x = x.reshape(-1, B, T, H, N).transpose(0, 2, 4, 1, 3).reshape(-1, T, N, gd, LANES)
        return x.transpose(0, 3, 1, 2, 4).reshape(-1, T, N, LANES) if lead else x[0].transpose(2, 0, 1, 3)

    def rows_inv(y):
        return y.transpose(1, 2, 0, 3).reshape(y.shape[1], N, B, H)

    def state_in(s):
        return rows(s.transpose(0, 3, 1, 2)).transpose(0, 2, 1, 3)

    def state_out(s):
        return rows_inv(s.transpose(0, 2, 1, 3)).transpose(2, 3, 1, 0)

    def const_rows(c):
        return rows(jnp.broadcast_to(c[None, None], (B, 1, H, N)))

    out, bonus, s_fin = rwkv_scan_lanes(rows(r), rows(v), rows(k), rows(zw), rows(za),
                                        const_rows(k_k), const_rows(k_a), const_rows(r_k),
                                        jnp.concatenate([state_in(s_f0), state_in(s_b0)], axis=0))
    o = rows_inv(out[:gd]) + rows_inv(out[gd:])
    bonus = (bonus[:gd] + bonus[gd:]).transpose(1, 0, 2).reshape(L, B, H).transpose(1, 0, 2)
    return o.transpose(2, 0, 3, 1), bonus, state_out(s_fin[:gd]), state_out(s_fin[gd:])


RWKV_SUB = 64
HEAD_PAIRS = D_MODEL // LANES


def _rwkv_rowsum_kernel(rf, vf, af, wf, kf, bf_, rb, vb, ab, wb, kb, bb, s0, of, ob, s, acc, *, tc):
    j = pl.program_id(1)

    @pl.when(j == 0)
    def _():
        s[...] = s0[...]

    row_head = lax.broadcasted_iota(jnp.int32, (2 * LANES, 2 * LANES), 0) // RWKV_HEAD
    col_head = lax.broadcasted_iota(jnp.int32, (2 * LANES, 2 * LANES), 1) // RWKV_HEAD
    ones_bd = (row_head == col_head).astype(bf16)
    sub = lax.broadcasted_iota(jnp.int32, (RWKV_HEAD, LANES), 0)
    lane_in_head = lax.broadcasted_iota(jnp.int32, (RWKV_HEAD, LANES), 1) % RWKV_HEAD
    eye = (sub == lane_in_head).astype(bf16)
    dirs = ((rf, vf, af, wf, kf, bf_), (rb, vb, ab, wb, kb, bb))

    def row_sums(x):
        return jnp.dot(x, ones_bd, preferred_element_type=f32)

    def row(ref, t, hp, reps):
        return jnp.broadcast_to(ref[t, pl.ds(hp, 1), :], (reps, LANES))

    def sub_chunk(sc, carry):
        def step(tt, carry):
            t_f = sc * RWKV_SUB + tt
            pos = ((t_f, jnp.maximum(t_f - 1, 0), tt - 1),
                   (tc - 1 - t_f, jnp.minimum(tc - t_f, tc - 1), RWKV_SUB - tt))
            for d in range(2):
                r_ref, v_ref, a_ref, w_ref, k_ref, b_ref = dirs[d]
                t_now, t_prev, out_lane = pos[d]
                lhs, tiles = [], []
                for hp2 in range(0, HEAD_PAIRS, 2):
                    v_diag = []
                    for hp in (hp2, hp2 + 1):
                        s_t = s[d, :, hp * LANES:(hp + 1) * LANES]
                        lhs.append(jnp.concatenate([(s_t * row(a_ref, t_now, hp, RWKV_HEAD)).astype(bf16),
                                                    (s_t * row(r_ref, t_prev, hp, RWKV_HEAD)).astype(bf16)], axis=1))
                        v_diag.append(eye * jnp.concatenate([row(v_ref, t_now, hp, 16).astype(bf16)] * 4, axis=0))
                        tiles.append(s_t)
                    lhs.append(jnp.concatenate(v_diag, axis=1))
                res = row_sums(jnp.concatenate(lhs, axis=0))
                for hp, s_t in enumerate(tiles):
                    ls = slice(hp * LANES, (hp + 1) * LANES)
                    base = (hp // 2) * 3 * RWKV_HEAD
                    own = res[base + (hp % 2) * RWKV_HEAD:base + (hp % 2 + 1) * RWKV_HEAD]
                    sa, out_prev = own[:, :LANES], own[:, LANES:]
                    v_col = res[base + 2 * RWKV_HEAD:base + 3 * RWKV_HEAD, (hp % 2) * LANES:(hp % 2 + 1) * LANES]
                    s[d, :, ls] = (s_t * row(w_ref, t_now, hp, RWKV_HEAD) + sa * row(b_ref, t_now, hp, RWKV_HEAD)
                                   + v_col * row(k_ref, t_now, hp, RWKV_HEAD))
                    acc[d, :, ls] = jnp.where(lane_in_head == out_lane, out_prev, acc[d, :, ls])
            return carry

        lax.fori_loop(0, RWKV_SUB, step, 0, unroll=8)
        t_last = sc * RWKV_SUB + RWKV_SUB - 1
        last = ((t_last, RWKV_SUB - 1), (tc - 1 - t_last, 0))
        for d in range(2):
            r_ref = dirs[d][0]
            t_now, out_lane = last[d]
            out_mask = jnp.concatenate([lane_in_head == out_lane] * 2, axis=1)
            for hp in range(0, HEAD_PAIRS, 2):
                ls = slice(hp * LANES, (hp + 2) * LANES)
                r_t = jnp.concatenate([row(r_ref, t_now, hp, RWKV_HEAD), row(r_ref, t_now, hp + 1, RWKV_HEAD)], axis=1)
                out_t = row_sums((s[d, :, ls] * r_t).astype(bf16))
                acc[d, :, ls] = jnp.where(out_mask, out_t, acc[d, :, ls])
        of[sc] = acc[0]
        ob[tc // RWKV_SUB - 1 - sc] = acc[1]
        return carry

    lax.fori_loop(0, tc // RWKV_SUB, sub_chunk, 0)


def rwkv_scan_rowsum(r, v, nkk, fwd, bwd, s_f0, s_b0):
    B, L, H, N = r.shape
    D = H * N
    tc = min(L, 256)
    n = L // tc
    fwd_blk = pl.BlockSpec((None, tc, HEAD_PAIRS, LANES), lambda b, j: (b, j, 0, 0))
    bwd_blk = pl.BlockSpec((None, tc, HEAD_PAIRS, LANES), lambda b, j: (b, n - 1 - j, 0, 0))
    r, v, nkk, w_f, k_f, b_f, w_b, k_b, b_b = (t.reshape(B, L, HEAD_PAIRS, LANES) for t in (r, v, nkk, *fwd, *bwd))
    state = pl.BlockSpec((None, 2, RWKV_HEAD, D), lambda b, j: (b, 0, 0, 0))
    nsub = tc // RWKV_SUB
    block_bytes = tc * D * 4
    s0 = jnp.stack([s.transpose(0, 2, 1, 3).reshape(B, N, D) for s in (s_f0, s_b0)], axis=1)
    o_f, o_b, s_fin = pl.pallas_call(
        functools.partial(_rwkv_rowsum_kernel, tc=tc),
        grid=(B, n),
        in_specs=[fwd_blk] * 6 + [bwd_blk] * 6 + [state],
        out_specs=[pl.BlockSpec((None, nsub, RWKV_HEAD, D), lambda b, j: (b, j, 0, 0)),
                   pl.BlockSpec((None, nsub, RWKV_HEAD, D), lambda b, j: (b, n - 1 - j, 0, 0)),
                   state],
        out_shape=[jax.ShapeDtypeStruct((B, L // RWKV_SUB, RWKV_HEAD, D), f32),
                   jax.ShapeDtypeStruct((B, L // RWKV_SUB, RWKV_HEAD, D), f32),
                   jax.ShapeDtypeStruct((B, 2, RWKV_HEAD, D), f32)],
        scratch_shapes=[pltpu.VMEM((2, RWKV_HEAD, D), f32)],
        compiler_params=pltpu.CompilerParams(
            dimension_semantics=("arbitrary", "arbitrary"),
            vmem_limit_bytes=min(V7X_VMEM_BYTES, 2 * 14 * block_bytes + (8 << 20)),
        ),
        name="rwkv_scan_rowsum",
    )(r, v, nkk, w_f, k_f, b_f, r, v, nkk, w_b, k_b, b_b, s0)
    o = (o_f + o_b).reshape(B, L // RWKV_SUB, RWKV_HEAD, H, RWKV_SUB).transpose(0, 1, 4, 3, 2).reshape(B, L, H, N)
    s_fin = s_fin.reshape(B, 2, N, H, N).transpose(0, 1, 3, 2, 4)
    return o, s_fin[:, 0], s_fin[:, 1]


def rwkv_mix(h, mu, wr, wk, wv, wo, w0, w1, w2, a0, a1, a2, g1, g2, k_k, k_a, r_k, lnx_w, lnx_b, s_f0, s_b0):
    B, L, D = h.shape
    H, N = RWKV_HEADS, RWKV_HEAD
    xx = centred_shift(h) - h
    xr, xw, xk, xv, xa, xg = [h + xx * mu[j] for j in range(6)]
    r = (xr @ wr).reshape(B, L, H, N)
    k = (xk @ wk).reshape(B, L, H, N)
    v = (xv @ wv).reshape(B, L, H, N)
    g = jax.nn.sigmoid(xg @ g1) @ g2
    zw = [(w0[d] + jnp.tanh(xw @ w1[d]) @ w2[d]).reshape(B, L, H, N) for d in range(2)]
    za = [(a0[d] + (xa @ a1[d]) @ a2[d]).reshape(B, L, H, N) for d in range(2)]

    if B * H >= LANES:
        o, bonus, s_f, s_b = rwkv_scan_bidir(r, v, k, jnp.stack(zw), jnp.stack(za), k_k.reshape(H, N),
                                             k_a.reshape(H, N), r_k, s_f0, s_b0)
        bonus = bonus[..., None]
    else:
        kk = k * k_k.reshape(H, N)
        kk = kk * lax.rsqrt(jnp.sum(kk * kk, axis=-1, keepdims=True) + 1e-12)

        def direction(d):
            decay = jnp.exp(-jnp.exp(-jax.nn.softplus(-zw[d]) - 0.5))
            a = jax.nn.sigmoid(za[d])
            return decay, k * (1.0 + (a - 1.0) * k_a.reshape(H, N)), kk * a

        fwd, bwd = direction(0), direction(1)
        o, s_f, s_b = rwkv_scan_rowsum(r, v, -kk, fwd, bwd, s_f0, s_b0)
        bonus = jnp.sum(r * 0.5 * (fwd[1] + bwd[1]) * r_k, axis=-1, keepdims=True)
    m = jnp.mean(o, axis=-1, keepdims=True)
    var = jnp.mean(jnp.square(o - m), axis=-1, keepdims=True)
    o = (o - m) * lax.rsqrt(var + RWKV_LN_EPS) * lnx_w.reshape(H, N) + lnx_b.reshape(H, N)
    o = (o + bonus * v).reshape(B, L, D)
    return (o * g) @ wo, s_f, s_b


GROUP_HID = EXP_PER_GROUP * D_EXPERT


def _moe_group_kernel(tile_group_ref, n_tiles_ref, xs_ref, wg_ref, wu_ref, wd_ref, ys_ref,
                      wg_bf, wu_bf, wd_bf):
    i = pl.program_id(0)
    group = tile_group_ref[i]
    prev_group = tile_group_ref[jnp.maximum(i - 1, 0)]
    group_changed = jnp.logical_or(i == 0, group != prev_group)

    @pl.when(group_changed)
    def _():
        for e in range(EXP_PER_GROUP):
            hs = slice(e * D_EXPERT, (e + 1) * D_EXPERT)
            wg_bf[:, hs] = wg_ref[e].astype(bf16)
            wu_bf[:, hs] = wu_ref[e].astype(bf16)
            wd_bf[hs, :] = wd_ref[e].astype(bf16)

    @pl.when(i < n_tiles_ref[0])
    def _():
        half = wg_bf.shape[0] // 2
        logits = lax.bitcast_convert_type(xs_ref[:, half:], f32)
        lane = lax.broadcasted_iota(jnp.int32, logits.shape, 1)
        grp = jnp.where(lane < N_GROUPS, logits, -jnp.inf)
        grp = jnp.exp(grp - jnp.max(grp, axis=-1, keepdims=True))
        p_group = (jnp.sum(jnp.where(lane == group, grp, 0.0), axis=-1, keepdims=True)
                   / jnp.sum(grp, axis=-1, keepdims=True))
        first = N_GROUPS + group * EXP_PER_GROUP
        le = [jnp.sum(jnp.where(lane == first + e, logits, 0.0), axis=-1, keepdims=True)
              for e in range(EXP_PER_GROUP)]
        chosen = []
        for e in range(EXP_PER_GROUP):
            ahead = sum(((le[k] > le[e]) | ((le[k] == le[e]) & (k < e))).astype(jnp.int32)
                        for k in range(EXP_PER_GROUP) if k != e)
            chosen.append(ahead < TOP_K)
        top = functools.reduce(jnp.maximum, [jnp.where(c, v, -jnp.inf) for c, v in zip(chosen, le)])
        ex = [jnp.where(c, jnp.exp(v - top), 0.0) for c, v in zip(chosen, le)]
        scale = p_group / sum(ex)
        packed = xs_ref[:, :half]
        x = jnp.concatenate([lax.bitcast_convert_type(packed << 16, f32),
                             lax.bitcast_convert_type(packed & jnp.uint32(0xFFFF0000), f32)], axis=1).astype(bf16)
        g = jnp.dot(x, wg_bf[...], preferred_element_type=f32)
        u = jnp.dot(x, wu_bf[...], preferred_element_type=f32)
        gate = jnp.concatenate([jnp.broadcast_to(ex[e] * scale, (MOE_TILE_M, D_EXPERT))
                                for e in range(EXP_PER_GROUP)], axis=1)
        hid = (g * jax.nn.sigmoid(g)) * u * gate
        ys_ref[...] = jnp.dot(hid.astype(bf16), wd_bf[...], preferred_element_type=f32)

    @pl.when(i >= n_tiles_ref[0])
    def _():
        ys_ref[...] = jnp.zeros_like(ys_ref)


def moe_group_experts(xs, tile_group, n_tiles, w_gate, w_up, w_down, layer):
    P = xs.shape[0]
    D = 2 * (xs.shape[1] - LANES)
    max_tiles = P // MOE_TILE_M
    weight_bytes = 3 * EXP_PER_GROUP * D * D_EXPERT * (4 + 2)
    tile_bytes = 2 * MOE_TILE_M * (2 * D * 4 + LANES * 4) + 4 * MOE_TILE_M * GROUP_HID * 4
    once = pl.Buffered(1)
    grid_spec = pltpu.PrefetchScalarGridSpec(
        num_scalar_prefetch=2,
        grid=(max_tiles,),
        in_specs=[
            pl.BlockSpec((MOE_TILE_M, D // 2 + LANES), lambda i, tg, nt: (i, 0)),
            pl.BlockSpec((None, EXP_PER_GROUP, D, D_EXPERT), lambda i, tg, nt: (layer, tg[i], 0, 0),
                         pipeline_mode=once),
            pl.BlockSpec((None, EXP_PER_GROUP, D, D_EXPERT), lambda i, tg, nt: (layer, tg[i], 0, 0),
                         pipeline_mode=once),
            pl.BlockSpec((None, EXP_PER_GROUP, D_EXPERT, D), lambda i, tg, nt: (layer, tg[i], 0, 0),
                         pipeline_mode=once),
        ],
        out_specs=pl.BlockSpec((MOE_TILE_M, D), lambda i, tg, nt: (i, 0)),
        scratch_shapes=[
            pltpu.VMEM((D, GROUP_HID), bf16),
            pltpu.VMEM((D, GROUP_HID), bf16),
            pltpu.VMEM((GROUP_HID, D), bf16),
        ],
    )
    return pl.pallas_call(
        _moe_group_kernel,
        grid_spec=grid_spec,
        out_shape=jax.ShapeDtypeStruct((P, D), f32),
        compiler_params=pltpu.CompilerParams(
            dimension_semantics=("arbitrary",),
            vmem_limit_bytes=min(V7X_VMEM_BYTES - (4 << 20), weight_bytes + tile_bytes + (8 << 20)),
        ),
        name="moe_group_experts",
    )(tile_group, n_tiles, xs, w_gate, w_up, w_down)


def router_params(w_grp, b_grp, w_exp, b_exp):
    w = jnp.concatenate([w_grp, w_exp], axis=1)
    b = jnp.concatenate([b_grp, b_exp], axis=0)[None, :]
    return jnp.pad(w, ((0, 0), (0, LANES - w.shape[1]))), jnp.pad(b, ((0, 0), (0, LANES - b.shape[1])))


def hier_moe(t_aug, w_gate, w_up, w_down, layer):
    T = t_aug.shape[0]
    half = t_aug.shape[1] - LANES
    g_top = jnp.argmax(lax.bitcast_convert_type(t_aug[:, half:half + N_GROUPS], f32), axis=-1).astype(jnp.int32)
    in_group = g_top[:, None] == jnp.arange(N_GROUPS, dtype=jnp.int32)[None, :]

    order = jnp.argsort(g_top, stable=True).astype(jnp.int32)
    grp_i32 = in_group.astype(jnp.int32)
    counts = jnp.sum(grp_i32, axis=0)
    rank = jnp.sum(jnp.where(in_group, jnp.cumsum(grp_i32, axis=0), 0), axis=1) - 1
    tiles_per = (counts + MOE_TILE_M - 1) // MOE_TILE_M
    tile_end = jnp.cumsum(tiles_per)
    start_padded = (tile_end - tiles_per) * MOE_TILE_M
    start_sorted = jnp.cumsum(counts) - counts
    max_tiles = T // MOE_TILE_M + N_GROUPS
    n_tiles = tile_end[-1:].astype(jnp.int32)
    tile_ids = jnp.arange(max_tiles, dtype=jnp.int32)
    tile_group = jnp.minimum(jnp.sum((tile_ids[:, None] >= tile_end[None, :]).astype(jnp.int32), axis=1),
                             N_GROUPS - 1).astype(jnp.int32)
    tile_group = jnp.where(tile_ids < n_tiles[0], tile_group, tile_group[jnp.maximum(n_tiles[0] - 1, 0)])
    row_group = jnp.repeat(tile_group, MOE_TILE_M)
    row_in_group = jnp.arange(max_tiles * MOE_TILE_M, dtype=jnp.int32) - start_padded[row_group]
    row_valid = (row_in_group >= 0) & (row_in_group < counts[row_group])
    row_tok = jnp.where(row_valid, order[jnp.clip(start_sorted[row_group] + row_in_group, 0, T - 1)], 0)
    pos = start_padded[g_top] + rank

    xs = jnp.take(t_aug, row_tok, axis=0)
    ys = moe_group_experts(xs, tile_group, n_tiles, w_gate, w_up, w_down, layer)
    return jnp.take(ys, pos, axis=0)


def kernel(x_prompt, x_sample, c, cache_attn_k, cache_attn_v, state_gla, state_rwkv, c_ctx, ada_w, ada_b, norm_mix, norm_ffn, norm_out, ev_w_in, ev_w_out, gla_dec_w, gla_dec_b, gla_norm, att_sink, rw_mu, rw_wr, rw_wk, rw_wv, rw_wo, rw_w0, rw_w1, rw_w2, rw_a0, rw_a1, rw_a2, rw_g1, rw_g2, rw_kk, rw_ka, rw_rk, rw_lnx_w, rw_lnx_b, moe_w_grp, moe_b_grp, moe_w_exp, moe_b_exp, moe_w_gate, moe_w_up, moe_w_down):
    n_lat = x_sample.shape[1]
    rows = n_lat // GRID_W
    row_pos = jnp.repeat(jnp.arange(rows), GRID_W)
    col_pos = jnp.tile(jnp.arange(GRID_W), rows)
    rope = rope_tables(row_pos, col_pos)
    Bc, Lc, D = x_prompt.shape
    Bl, Ll, _ = x_sample.shape

    n_ctx = Bc * Lc
    x_all = jnp.concatenate([x_prompt.reshape(n_ctx, D), x_sample.reshape(Bl * Ll, D)], axis=0)
    cond_all = jnp.concatenate([c_ctx[None, :], c], axis=0)
    mods = [ada_mod(cond_all, ada_w[l], ada_b[l]) for l in range(DEPTH)]
    no_mod = jnp.zeros((1 + Bl, 1, D), f32)

    hc = modulate(rmsnorm(x_prompt, norm_mix[0]), mods[0][0][:1], mods[0][1][:1])
    hl = modulate(rmsnorm(x_sample, norm_mix[0]), mods[0][0][1:], mods[0][1][1:])
    ks_out, vs_out, gla_out, rwkv_out = [], [], [], []
    for l in range(DEPTH):
        _, _, gate1, shift2, scale2, gate2 = mods[l]
        i = l // 2
        if l % 2 == 0:
            oc, k_c, v_c, s_gla = even_mixer_ctx(hc, ev_w_in[i], ev_w_out[i], gla_dec_w[i], gla_dec_b[i],
                                                 gla_norm[i], att_sink[i])
            ol = even_mixer_lat(hl, ev_w_in[i], ev_w_out[i], gla_dec_w[i], gla_dec_b[i], gla_norm[i], att_sink[i],
                                rope, cache_attn_k[:, i], cache_attn_v[:, i], state_gla[:, i])
            ks_out.append(k_c)
            vs_out.append(v_c)
            gla_out.append(s_gla)
        else:
            rw = (rw_mu[i], rw_wr[i], rw_wk[i], rw_wv[i], rw_wo[i], rw_w0[i], rw_w1[i], rw_w2[i], rw_a0[i],
                  rw_a1[i], rw_a2[i], rw_g1[i], rw_g2[i], rw_kk[i], rw_ka[i], rw_rk[i], rw_lnx_w[i], rw_lnx_b[i])
            zero = jnp.zeros((Bc, RWKV_HEADS, RWKV_HEAD, RWKV_HEAD), f32)
            oc, s_f, s_b = rwkv_mix(hc, *rw, zero, zero)
            ol, _, _ = rwkv_mix(hl, *rw, state_rwkv[:, i, 0], state_rwkv[:, i, 1])
            rwkv_out.append(jnp.stack([s_f, s_b], axis=1))
        x_all, t_aug = resid_norm_mod(x_all, (oc.reshape(n_ctx, D), ol.reshape(Bl * Ll, D)), n_ctx, Ll,
                                      gate1, norm_ffn[l], shift2, scale2,
                                      router_params(moe_w_grp[l], moe_b_grp[l], moe_w_exp[l], moe_b_exp[l]))
        y = hier_moe(t_aug, moe_w_gate, moe_w_up, moe_w_down, l)
        if l + 1 < DEPTH:
            x_all, (hc, hl) = resid_norm_mod(x_all, y, n_ctx, Ll, gate2, norm_mix[l + 1], mods[l + 1][0],
                                             mods[l + 1][1])
        else:
            x_all, (hc, hl) = resid_norm_mod(x_all, y, n_ctx, Ll, gate2, norm_out, no_mod, no_mod)
        hc, hl = hc.reshape(Bc, Lc, D), hl.reshape(Bl, Ll, D)
    y_prompt, y_sample = hc, hl

    new_attn_k = jnp.stack(ks_out, axis=1)
    new_attn_v = jnp.stack(vs_out, axis=1)
    new_gla = jnp.stack(gla_out, axis=1)
    new_rwkv = jnp.stack(rwkv_out, axis=1)
    return (y_prompt, y_sample, new_attn_k, new_attn_v, new_gla, new_rwkv)
```

```python
import functools

import jax
import jax.numpy as jnp
import numpy as np
from jax import lax
from jax.experimental import pallas as pl
from jax.experimental.pallas import tpu as pltpu

D_MODEL = 1024
DEPTH = 4
GRID_W = 64
EPS = 1e-6
GLA_HEADS = 4
GLA_DK = 64
GLA_DV = 128
GLA_LOWRANK = 16
GLA_GATE_TEMP = 16.0
GLA_CHUNK = 64
ATT_HEADS = 8
ATT_KV_HEADS = 2
ATT_GROUP = ATT_HEADS // ATT_KV_HEADS
ATT_HD = 64
WINDOW = 128
ATT_BLOCK = 128
ROPE_BASE = 10000.0
ROPE_AXIS_DIMS = ATT_HD // 2
NEG = -1e30
IN_WIDTHS = (GLA_HEADS * GLA_DK, GLA_HEADS * GLA_DK, GLA_HEADS * GLA_DV, GLA_HEADS * GLA_DV,
             ATT_HEADS * ATT_HD, ATT_KV_HEADS * ATT_HD, ATT_KV_HEADS * ATT_HD, 2 * GLA_LOWRANK)
IN_SPLITS = tuple(int(v) for v in np.cumsum(IN_WIDTHS)[:-1])
RWKV_HEAD = 64
RWKV_HEADS = D_MODEL // RWKV_HEAD
RWKV_LN_EPS = 64e-5
N_GROUPS = 4
EXP_PER_GROUP = 4
N_EXPERTS = N_GROUPS * EXP_PER_GROUP
TOP_K = 2
D_EXPERT = 512

V7X_VMEM_BYTES = 64 * 1024 * 1024
MOE_TILE_M = 256

f32 = jnp.float32
bf16 = jnp.bfloat16


def rmsnorm(x, w):
    y = x * lax.rsqrt(jnp.mean(x * x, axis=-1, keepdims=True) + EPS)
    return y * w


def ada_mod(cond, w, b):
    mod = jax.nn.silu(cond) @ w + b
    return [m[:, None, :] for m in jnp.split(mod, 6, axis=-1)]


def modulate(x, shift, scale):
    return x * (1.0 + scale) + shift


NORM_ROWS = 512


def _resid_norm_mod_kernel(*refs, n_ctx_blocks, split):
    is_ctx = pl.program_id(0) < n_ctx_blocks
    if split:
        x_ref, oc_ref, ol_ref, gate_ref, w_ref, shift_ref, scale_ref, rw_ref, rb_ref, xn_ref, t_ref = refs
        o = jnp.where(is_ctx, oc_ref[...], ol_ref[...])
    else:
        x_ref, o_ref, gate_ref, w_ref, shift_ref, scale_ref, xn_ref, tc_ref, tl_ref = refs
        o = o_ref[...]
    x = x_ref[...] + gate_ref[0] * o
    xn_ref[...] = x
    y = x * lax.rsqrt(jnp.mean(x * x, axis=-1, keepdims=True) + EPS) * w_ref[...]
    t = y * (1.0 + scale_ref[0]) + shift_ref[0]
    if split:
        half = t.shape[1] // 2
        logits = jnp.dot(t, rw_ref[...], precision=lax.Precision.HIGHEST, preferred_element_type=f32) + rb_ref[...]
        bits = lax.bitcast_convert_type(t.astype(bf16).astype(f32), jnp.uint32)
        t_ref[:, :half] = (bits[:, :half] >> 16) | (bits[:, half:] & jnp.uint32(0xFFFF0000))
        t_ref[:, half:] = lax.bitcast_convert_type(logits, jnp.uint32)
    else:
        @pl.when(is_ctx)
        def _():
            tc_ref[...] = t

        @pl.when(jnp.logical_not(is_ctx))
        def _():
            tl_ref[...] = t


def resid_norm_mod(x, o, n_ctx, lat_len, gate, w, shift, scale, router=None):
    T, D = x.shape
    ncb = n_ctx // NORM_ROWS
    per_lat = lat_len // NORM_ROWS
    rows = pl.BlockSpec((NORM_ROWS, D), lambda i: (i, 0))
    mod = pl.BlockSpec((1, 1, D), lambda i: (jnp.where(i < ncb, 0, 1 + (i - ncb) // per_lat), 0, 0))
    pair = [pl.BlockSpec((NORM_ROWS, D), lambda i: (jnp.minimum(i, ncb - 1), 0)),
            pl.BlockSpec((NORM_ROWS, D), lambda i: (jnp.maximum(i - ncb, 0), 0))]
    split = isinstance(o, tuple)
    assert split == (router is not None)
    if split:
        o_specs, o_args = pair, list(o)
        t_specs = [pl.BlockSpec((NORM_ROWS, D // 2 + LANES), lambda i: (i, 0))]
        t_shapes = [jax.ShapeDtypeStruct((T, D // 2 + LANES), jnp.uint32)]
        r_specs = [pl.BlockSpec((D, LANES), lambda i: (0, 0)), pl.BlockSpec((1, LANES), lambda i: (0, 0))]
        r_args = list(router)
    else:
        o_specs, o_args, t_specs = [rows], [o], pair
        t_shapes = [jax.ShapeDtypeStruct((n_ctx, D), f32), jax.ShapeDtypeStruct((T - n_ctx, D), f32)]
        r_specs, r_args = [], []
    out = pl.pallas_call(
        functools.partial(_resid_norm_mod_kernel, n_ctx_blocks=ncb, split=split),
        grid=(T // NORM_ROWS,),
        in_specs=[rows, *o_specs, mod, pl.BlockSpec((1, D), lambda i: (0, 0)), mod, mod, *r_specs],
        out_specs=[rows, *t_specs],
        out_shape=[jax.ShapeDtypeStruct((T, D), f32), *t_shapes],
        compiler_params=pltpu.CompilerParams(dimension_semantics=("arbitrary",)),
        name="resid_norm_mod",
    )(x, *o_args, gate, w.reshape(1, D), shift, scale, *r_args)
    return (out[0], out[1]) if split else (out[0], (out[1], out[2]))


def rope_tables(row_pos, col_pos):
    freqs = ROPE_BASE ** (-jnp.arange(0, ROPE_AXIS_DIMS, 2, dtype=f32) / ROPE_AXIS_DIMS)
    ang_r = row_pos.astype(f32)[:, None] * freqs[None, :]
    ang_c = col_pos.astype(f32)[:, None] * freqs[None, :]
    return jnp.cos(ang_r), jnp.sin(ang_r), jnp.cos(ang_c), jnp.sin(ang_c)


def rotate(x, cos, sin):
    x1, x2 = jnp.split(x, 2, axis=-1)
    cos = cos[None, :, None, :]
    sin = sin[None, :, None, :]
    return jnp.concatenate([x1 * cos - x2 * sin, x1 * sin + x2 * cos], axis=-1)


def apply_axial_rope(x, tables):
    cr, sr, cc, sc = tables
    xr, xc = jnp.split(x, 2, axis=-1)
    return jnp.concatenate([rotate(xr, cr, sr), rotate(xc, cc, sc)], axis=-1)


GLA_QK = GLA_HEADS * GLA_DK
GLA_V = GLA_HEADS * GLA_DV


def _gla_kernel(q_ref, v_ref, gg_ref, ldf_ref, ldb_ref, kt_ref, ldft_ref, ldbt_ref, s0_ref, norm_ref,
                o_ref, s_ref, *, seq_len):
    n = seq_len // GLA_CHUNK
    C = GLA_CHUNK
    ti = lax.broadcasted_iota(jnp.int32, (C, C), 0)
    tj = lax.broadcasted_iota(jnp.int32, (C, C), 1)
    keep = (tj <= ti, tj >= ti)
    tri = tuple(m.astype(bf16) for m in keep)
    tri_t = (tri[1], tri[0])
    ones = jnp.ones((C, GLA_DV), bf16)
    s_ref[...] = s0_ref[...]
    o_ref[...] = jnp.zeros_like(o_ref)
    ld_refs = ((ldf_ref, ldft_ref), (ldb_ref, ldbt_ref))

    def split3(x):
        hi = x.astype(bf16)
        rest = x - hi.astype(f32)
        mid = rest.astype(bf16)
        return hi, mid, (rest - mid.astype(f32)).astype(bf16)

    def sums(a, b):
        if isinstance(b, tuple):
            return sum(jnp.dot(a, p, preferred_element_type=f32) for p in b)
        return sum(jnp.dot(p, b, preferred_element_type=f32) for p in a)

    def chunk(c, carry):
        pending = []
        for d in range(2):
            cc = c if d == 0 else n - 1 - c
            rows = pl.ds(pl.multiple_of(cc * C, C), C)
            ld_ref, ldt_ref = ld_refs[d]
            ld_t3 = split3(ldt_ref[cc])
            b_all = sums(tri[d], split3(ld_ref[rows, :]))
            bt_all = sums(ld_t3, tri_t[d])
            total_all = sums(ld_t3, ones)
            for h in range(GLA_HEADS):
                ks = slice(h * GLA_DK, (h + 1) * GLA_DK)
                vs = slice(h * GLA_DV, (h + 1) * GLA_DV)
                q = q_ref[rows, ks] * (GLA_DK ** -0.5)
                v = v_ref[rows, vs].astype(bf16)
                k_t = kt_ref[cc, ks, :]
                b, b_t, total = b_all[:, ks], bt_all[ks], total_all[ks]
                q_e = (q * jnp.exp(b)).astype(bf16)
                k_e = (k_t * jnp.exp(-b_t)).astype(bf16)
                k_s = (k_t * jnp.exp(total[:, :C] - b_t)).astype(bf16)
                att = jnp.where(keep[d], jnp.dot(q_e, k_e, preferred_element_type=f32), 0.0).astype(bf16)
                s = s_ref[d, h]
                o = (jnp.dot(q_e, s.astype(bf16), preferred_element_type=f32)
                     + jnp.dot(att, v, preferred_element_type=f32))
                s_new = s * jnp.exp(total) + jnp.dot(k_s, v, preferred_element_type=f32)
                pending.append((d, h, rows, vs, o_ref[rows, vs] + o, s_new))
        for d, h, rows, vs, o, s_new in pending:
            o_ref[rows, vs] = o
            s_ref[d, h] = s_new
        return carry

    lax.fori_loop(0, n, chunk, 0)
    for h in range(GLA_HEADS):
        vs = slice(h * GLA_DV, (h + 1) * GLA_DV)
        x = o_ref[:, vs]
        y = x * lax.rsqrt(jnp.mean(x * x, axis=-1, keepdims=True) + EPS) * norm_ref[...]
        g = gg_ref[:, vs]
        o_ref[:, vs] = y * (g * jax.nn.sigmoid(g))


def gla_bidir_gated(z, ld_f, ld_b, s0, gla_norm):
    B, L, _ = z.shape
    n = L // GLA_CHUNK
    assert n % 2 == 0

    def per_chunk_t(a):
        return a.reshape(B, n, GLA_CHUNK, GLA_QK).transpose(0, 1, 3, 2)

    state = pl.BlockSpec((None, 2, GLA_HEADS, GLA_DK, GLA_DV), lambda b: (b, 0, 0, 0, 0))
    ld = pl.BlockSpec((None, L, GLA_QK), lambda b: (b, 0, 0))
    ld_t = pl.BlockSpec((None, n, GLA_QK, GLA_CHUNK), lambda b: (b, 0, 0, 0))
    return pl.pallas_call(
        functools.partial(_gla_kernel, seq_len=L),
        grid=(B,),
        in_specs=[pl.BlockSpec((None, L, GLA_QK), lambda b: (b, 0, 0)),
                  pl.BlockSpec((None, L, GLA_V), lambda b: (b, 0, 1)),
                  pl.BlockSpec((None, L, GLA_V), lambda b: (b, 0, 2)),
                  ld, ld, ld_t, ld_t, ld_t, state,
                  pl.BlockSpec((1, GLA_DV), lambda b: (0, 0))],
        out_specs=[pl.BlockSpec((None, L, GLA_V), lambda b: (b, 0, 0)), state],
        out_shape=[jax.ShapeDtypeStruct((B, L, GLA_V), f32),
                   jax.ShapeDtypeStruct((B, 2, GLA_HEADS, GLA_DK, GLA_DV), f32)],
        compiler_params=pltpu.CompilerParams(
            dimension_semantics=("arbitrary",),
            vmem_limit_bytes=min(V7X_VMEM_BYTES, 2 * L * (9 * GLA_QK + 3 * GLA_V) * 4 + (16 << 20)),
        ),
        name="gla_bidir",
    )(z, z, z, ld_f, ld_b, per_chunk_t(z[..., GLA_QK:2 * GLA_QK]), per_chunk_t(ld_f), per_chunk_t(ld_b),
      s0, gla_norm.reshape(1, GLA_DV))


ATT_Q = ATT_HEADS * ATT_HD
ATT_KV = ATT_KV_HEADS * ATT_HD


def _window_attn_kernel(q_ref, k_ref, v_ref, kc_ref, vc_ref, sink_ref, o_ref, *, tq, seq_len):
    scale = ATT_HD ** -0.5
    nt = (((1,), (1,)), ((), ()))
    start = pl.multiple_of(pl.program_id(1) * tq, tq)
    k_own = k_ref[pl.ds(start, 3 * tq), :].astype(bf16)
    v_own = v_ref[pl.ds(start, 3 * tq), :].astype(bf16)
    qpos = start + lax.broadcasted_iota(jnp.int32, (tq, 3 * tq), 0)
    kpos = start - tq + lax.broadcasted_iota(jnp.int32, (tq, 3 * tq), 1)
    valid = (jnp.abs(qpos - kpos) <= WINDOW) & (kpos >= 0) & (kpos < seq_len)
    for h in range(ATT_HEADS):
        g = h // ATT_GROUP
        kv = slice(g * ATT_HD, (g + 1) * ATT_HD)
        q = q_ref[:, h * ATT_HD:(h + 1) * ATT_HD].astype(bf16)
        s_own = lax.dot_general(q, k_own[:, kv], nt, preferred_element_type=f32) * scale
        s_own = jnp.where(valid, s_own, NEG)
        s_ctx = lax.dot_general(q, kc_ref[:, kv].astype(bf16), nt, preferred_element_type=f32) * scale
        sink = sink_ref[0:1, h:h + 1]
        m = jnp.maximum(jnp.maximum(jnp.max(s_own, axis=-1, keepdims=True), jnp.max(s_ctx, axis=-1, keepdims=True)),
                        sink)
        p_own = jnp.exp(s_own - m)
        p_ctx = jnp.exp(s_ctx - m)
        denom = (jnp.sum(p_own, axis=-1, keepdims=True) + jnp.sum(p_ctx, axis=-1, keepdims=True)
                 + jnp.exp(sink - m))
        o = (jnp.dot(p_own.astype(bf16), v_own[:, kv], preferred_element_type=f32)
             + jnp.dot(p_ctx.astype(bf16), vc_ref[:, kv].astype(bf16), preferred_element_type=f32))
        o_ref[:, h * ATT_HD:(h + 1) * ATT_HD] = o / denom


def ctx_attn(q, k, v, sink):
    B, L = q.shape[:2]
    scale = ATT_HD ** -0.5
    qg = q.reshape(B, L, ATT_KV_HEADS, ATT_GROUP, ATT_HD)
    sink_b = sink.reshape(ATT_KV_HEADS, ATT_GROUP)[None, :, :, None, None]
    s = jnp.einsum('bqhgd,bkhd->bhgqk', qg, k) * scale
    s_sink = jnp.broadcast_to(sink_b, s.shape[:-1] + (1,))
    p = jax.nn.softmax(jnp.concatenate([s_sink, s], axis=-1), axis=-1)[..., 1:]
    return jnp.einsum('bhgqk,bkhd->bqhgd', p, v).reshape(B, L, ATT_HEADS * ATT_HD)


def window_attn_latent(q, k, v, k_ctx, v_ctx, sink):
    B, S, _ = q.shape
    Lc = k_ctx.shape[1]
    nb = S // ATT_BLOCK
    pad = ((0, 0), (ATT_BLOCK, ATT_BLOCK), (0, 0))
    whole = lambda n: pl.BlockSpec((None, n, ATT_KV), lambda b, i: (b, 0, 0))
    return pl.pallas_call(
        functools.partial(_window_attn_kernel, tq=ATT_BLOCK, seq_len=S),
        grid=(B, nb),
        in_specs=[pl.BlockSpec((None, ATT_BLOCK, ATT_Q), lambda b, i: (b, i, 0)),
                  whole(S + 2 * ATT_BLOCK), whole(S + 2 * ATT_BLOCK), whole(Lc), whole(Lc),
                  pl.BlockSpec((1, ATT_HEADS), lambda b, i: (0, 0))],
        out_specs=pl.BlockSpec((None, ATT_BLOCK, ATT_Q), lambda b, i: (b, i, 0)),
        out_shape=jax.ShapeDtypeStruct((B, S, ATT_Q), f32),
        compiler_params=pltpu.CompilerParams(dimension_semantics=("arbitrary", "arbitrary")),
        name="window_attn",
    )(q, jnp.pad(k, pad), jnp.pad(v, pad), k_ctx, v_ctx, sink.reshape(1, ATT_HEADS))


def even_projections(h, w_in, dec_w, dec_b):
    B, L, _ = h.shape
    z = h @ w_in
    aq, ak, av, lr = jnp.split(z[..., IN_SPLITS[3]:], [s - IN_SPLITS[3] for s in IN_SPLITS[4:]], axis=-1)
    lr_f, lr_b = jnp.split(lr, 2, axis=-1)

    def logdecay(lr_d, w, b):
        return jax.nn.log_sigmoid(lr_d @ w + b) / GLA_GATE_TEMP

    ld_f = logdecay(lr_f, dec_w[0], dec_b[0])
    ld_b = logdecay(lr_b, dec_w[1], dec_b[1])
    aq = aq.reshape(B, L, ATT_HEADS, ATT_HD)
    ak = ak.reshape(B, L, ATT_KV_HEADS, ATT_HD)
    av = av.reshape(B, L, ATT_KV_HEADS, ATT_HD)
    return z, ld_f, ld_b, aq, ak, av


def even_mixer_ctx(h, w_in, w_out, dec_w, dec_b, gla_norm, sink):
    z, ld_f, ld_b, aq, ak, av = even_projections(h, w_in, dec_w, dec_b)
    zero = jnp.zeros((h.shape[0], 2, GLA_HEADS, GLA_DK, GLA_DV), f32)
    o_gla, s_fin = gla_bidir_gated(z, ld_f, ld_b, zero, gla_norm)
    o_att = ctx_attn(aq, ak, av, sink)
    return jnp.concatenate([o_gla, o_att], axis=-1) @ w_out, ak, av, s_fin


def even_mixer_lat(h, w_in, w_out, dec_w, dec_b, gla_norm, sink, rope, k_ctx, v_ctx, s0):
    z, ld_f, ld_b, aq, ak, av = even_projections(h, w_in, dec_w, dec_b)
    o_gla, _ = gla_bidir_gated(z, ld_f, ld_b, s0, gla_norm)
    B, S = h.shape[:2]
    o_att = window_attn_latent(apply_axial_rope(aq, rope).reshape(B, S, ATT_Q),
                               apply_axial_rope(ak, rope).reshape(B, S, ATT_KV), av.reshape(B, S, ATT_KV),
                               k_ctx.reshape(B, -1, ATT_KV), v_ctx.reshape(B, -1, ATT_KV), sink)
    return jnp.concatenate([o_gla, o_att], axis=-1) @ w_out


def centred_shift(x):
    xp = jnp.pad(x, ((0, 0), (1, 1), (0, 0)))
    return 0.5 * (xp[:, :-2] + xp[:, 2:])


LANES = 128
SUBLANES = 8
RWKV_TC = 64


def _rwkv_scan_kernel(r_ref, v_ref, k_ref, zw_ref, za_ref, kk_ref, ka_ref, rk_ref, s0_ref, o_ref, bonus_ref, s_ref,
                      *, tc, nv, n_dir_groups):
    backward = pl.program_id(0) >= n_dir_groups

    @pl.when(pl.program_id(1) == 0)
    def _():
        s_ref[...] = s0_ref[...]

    def step(i, carry):
        t = jnp.where(backward, tc - 1 - i, i)
        k_t, r = k_ref[t], r_ref[t]
        kk = k_t * kk_ref[0]
        kk = kk * lax.rsqrt(jnp.sum(kk * kk, axis=0, keepdims=True) + 1e-12)
        a = jax.nn.sigmoid(za_ref[t])
        z = -zw_ref[t]
        softplus = jnp.maximum(z, 0.0) + jnp.log1p(jnp.exp(-jnp.abs(z)))
        w = jnp.exp(-jnp.exp(-softplus - 0.5))
        alpha, beta = -kk, kk * a
        kd = k_t * (1.0 + (a - 1.0) * ka_ref[0])
        bonus_ref[pl.ds(t, 1), :] = jnp.sum(r * 0.5 * kd * rk_ref[0], axis=0, keepdims=True)
        for vb in range(nv // SUBLANES):
            outs = []
            for v in range(vb * SUBLANES, (vb + 1) * SUBLANES):
                s = s_ref[v]
                sa = jnp.sum(s * alpha, axis=0, keepdims=True)
                s_new = s * w + sa * beta + v_ref[t, pl.ds(v, 1), :] * kd
                s_ref[v] = s_new
                outs.append(jnp.sum(s_new * r, axis=0, keepdims=True))
            o_ref[t, vb * SUBLANES:(vb + 1) * SUBLANES, :] = jnp.concatenate(outs, axis=0)
        return carry

    lax.fori_loop(0, tc, step, 0)


def rwkv_scan_lanes(r, v, k, zw, za, k_k, k_a, r_k, s0):
    gd, L, nv, _ = v.shape
    tc = RWKV_TC
    n = L // tc

    def time_block(g, j):
        return jnp.where(g >= gd, n - 1 - j, j)

    def blk(rows, shared):
        return pl.BlockSpec((None, tc, rows, LANES), lambda g, j: (g % gd if shared else g, time_block(g, j), 0, 0))

    const = pl.BlockSpec((None, 1, RWKV_HEAD, LANES), lambda g, j: (g % gd, 0, 0, 0))
    state = pl.BlockSpec((None, nv, RWKV_HEAD, LANES), lambda g, j: (g, 0, 0, 0))
    block_bytes = tc * RWKV_HEAD * LANES * 4
    return pl.pallas_call(
        functools.partial(_rwkv_scan_kernel, tc=tc, nv=nv, n_dir_groups=gd),
        grid=(2 * gd, n),
        in_specs=[blk(RWKV_HEAD, True), blk(nv, True), blk(RWKV_HEAD, True),
                  blk(RWKV_HEAD, False), blk(RWKV_HEAD, False), const, const, const, state],
        out_specs=[blk(nv, False), pl.BlockSpec((None, tc, LANES), lambda g, j: (g, time_block(g, j), 0)), state],
        out_shape=[jax.ShapeDtypeStruct((2 * gd, L, nv, LANES), f32),
                   jax.ShapeDtypeStruct((2 * gd, L, LANES), f32),
                   jax.ShapeDtypeStruct((2 * gd, nv, RWKV_HEAD, LANES), f32)],
        compiler_params=pltpu.CompilerParams(
            dimension_semantics=("arbitrary", "arbitrary"),
            vmem_limit_bytes=2 * 7 * block_bytes + 4 * nv * RWKV_HEAD * LANES * 4 + (8 << 20),
        ),
        name="rwkv_scan",
    )(r, v, k, zw, za, k_k, k_a, r_k, s0)


def rwkv_scan_bidir(r, v, k, zw, za, k_k, k_a, r_k, s_f0, s_b0):
    B, L, H, N = r.shape
    S = B * H
    gd = S // LANES

    def rows(x):
        T = x.shape[-3]
        lead = x.shape[:-4]
        x = x.reshape(-1, B, T, H, N).transpose(0, 2, 4, 1, 3).reshape(-1, T, N, gd, LANES)
        return x.transpose(0, 3, 1, 2, 4).reshape(-1, T, N, LANES) if lead else x[0].transpose(2, 0, 1, 3)

    def rows_inv(y):
        return y.transpose(1, 2, 0, 3).reshape(y.shape[1], N, B, H)

    def state_in(s):
        return rows(s.transpose(0, 3, 1, 2)).transpose(0, 2, 1, 3)

    def state_out(s):
        return rows_inv(s.transpose(0, 2, 1, 3)).transpose(2, 3, 1, 0)

    def const_rows(c):
        return rows(jnp.broadcast_to(c[None, None], (B, 1, H, N)))

    out, bonus, s_fin = rwkv_scan_lanes(rows(r), rows(v), rows(k), rows(zw), rows(za),
                                        const_rows(k_k), const_rows(k_a), const_rows(r_k),
                                        jnp.concatenate([state_in(s_f0), state_in(s_b0)], axis=0))
    o = rows_inv(out[:gd]) + rows_inv(out[gd:])
    bonus = (bonus[:gd] + bonus[gd:]).transpose(1, 0, 2).reshape(L, B, H).transpose(1, 0, 2)
    return o.transpose(2, 0, 3, 1), bonus, state_out(s_fin[:gd]), state_out(s_fin[gd:])


RWKV_SUB = 64
HEAD_PAIRS = D_MODEL // LANES


def _rwkv_rowsum_kernel(rf, vf, af, wf, kf, bf_, rb, vb, ab, wb, kb, bb, s0, of, ob, s, acc, *, tc):
    j = pl.program_id(1)

    @pl.when(j == 0)
    def _():
        s[...] = s0[...]

    row_head = lax.broadcasted_iota(jnp.int32, (2 * LANES, 2 * LANES), 0) // RWKV_HEAD
    col_head = lax.broadcasted_iota(jnp.int32, (2 * LANES, 2 * LANES), 1) // RWKV_HEAD
    ones_bd = (row_head == col_head).astype(bf16)
    sub = lax.broadcasted_iota(jnp.int32, (RWKV_HEAD, LANES), 0)
    lane_in_head = lax.broadcasted_iota(jnp.int32, (RWKV_HEAD, LANES), 1) % RWKV_HEAD
    eye = (sub == lane_in_head).astype(bf16)
    dirs = ((rf, vf, af, wf, kf, bf_), (rb, vb, ab, wb, kb, bb))

    def row_sums(x):
        return jnp.dot(x, ones_bd, preferred_element_type=f32)

    def row(ref, t, hp, reps):
        return jnp.broadcast_to(ref[t, pl.ds(hp, 1), :], (reps, LANES))

    def sub_chunk(sc, carry):
        def step(tt, carry):
            t_f = sc * RWKV_SUB + tt
            pos = ((t_f, jnp.maximum(t_f - 1, 0), tt - 1),
                   (tc - 1 - t_f, jnp.minimum(tc - t_f, tc - 1), RWKV_SUB - tt))
            for d in range(2):
                r_ref, v_ref, a_ref, w_ref, k_ref, b_ref = dirs[d]
                t_now, t_prev, out_lane = pos[d]
                lhs, tiles = [], []
                for hp2 in range(0, HEAD_PAIRS, 2):
                    v_diag = []
                    for hp in (hp2, hp2 + 1):
                        s_t = s[d, :, hp * LANES:(hp + 1) * LANES]
                        lhs.append(jnp.concatenate([(s_t * row(a_ref, t_now, hp, RWKV_HEAD)).astype(bf16),
                                                    (s_t * row(r_ref, t_prev, hp, RWKV_HEAD)).astype(bf16)], axis=1))
                        v_diag.append(eye * jnp.concatenate([row(v_ref, t_now, hp, 16).astype(bf16)] * 4, axis=0))
                        tiles.append(s_t)
                    lhs.append(jnp.concatenate(v_diag, axis=1))
                res = row_sums(jnp.concatenate(lhs, axis=0))
                for hp, s_t in enumerate(tiles):
                    ls = slice(hp * LANES, (hp + 1) * LANES)
                    base = (hp // 2) * 3 * RWKV_HEAD
                    own = res[base + (hp % 2) * RWKV_HEAD:base + (hp % 2 + 1) * RWKV_HEAD]
                    sa, out_prev = own[:, :LANES], own[:, LANES:]
                    v_col = res[base + 2 * RWKV_HEAD:base + 3 * RWKV_HEAD, (hp % 2) * LANES:(hp % 2 + 1) * LANES]
                    s[d, :, ls] = (s_t * row(w_ref, t_now, hp, RWKV_HEAD) + sa * row(b_ref, t_now, hp, RWKV_HEAD)
                                   + v_col * row(k_ref, t_now, hp, RWKV_HEAD))
                    acc[d, :, ls] = jnp.where(lane_in_head == out_lane, out_prev, acc[d, :, ls])
            return carry

        lax.fori_loop(0, RWKV_SUB, step, 0, unroll=8)
        t_last = sc * RWKV_SUB + RWKV_SUB - 1
        last = ((t_last, RWKV_SUB - 1), (tc - 1 - t_last, 0))
        for d in range(2):
            r_ref = dirs[d][0]
            t_now, out_lane = last[d]
            out_mask = jnp.concatenate([lane_in_head == out_lane] * 2, axis=1)
            for hp in range(0, HEAD_PAIRS, 2):
                ls = slice(hp * LANES, (hp + 2) * LANES)
                r_t = jnp.concatenate([row(r_ref, t_now, hp, RWKV_HEAD), row(r_ref, t_now, hp + 1, RWKV_HEAD)], axis=1)
                out_t = row_sums((s[d, :, ls] * r_t).astype(bf16))
                acc[d, :, ls] = jnp.where(out_mask, out_t, acc[d, :, ls])
        of[sc] = acc[0]
        ob[tc // RWKV_SUB - 1 - sc] = acc[1]
        return carry

    lax.fori_loop(0, tc // RWKV_SUB, sub_chunk, 0)


def rwkv_scan_rowsum(r, v, nkk, fwd, bwd, s_f0, s_b0):
    B, L, H, N = r.shape
    D = H * N
    tc = min(L, 256)
    n = L // tc
    fwd_blk = pl.BlockSpec((None, tc, HEAD_PAIRS, LANES), lambda b, j: (b, j, 0, 0))
    bwd_blk = pl.BlockSpec((None, tc, HEAD_PAIRS, LANES), lambda b, j: (b, n - 1 - j, 0, 0))
    r, v, nkk, w_f, k_f, b_f, w_b, k_b, b_b = (t.reshape(B, L, HEAD_PAIRS, LANES) for t in (r, v, nkk, *fwd, *bwd))
    state = pl.BlockSpec((None, 2, RWKV_HEAD, D), lambda b, j: (b, 0, 0, 0))
    nsub = tc // RWKV_SUB
    block_bytes = tc * D * 4
    s0 = jnp.stack([s.transpose(0, 2, 1, 3).reshape(B, N, D) for s in (s_f0, s_b0)], axis=1)
    o_f, o_b, s_fin = pl.pallas_call(
        functools.partial(_rwkv_rowsum_kernel, tc=tc),
        grid=(B, n),
        in_specs=[fwd_blk] * 6 + [bwd_blk] * 6 + [state],
        out_specs=[pl.BlockSpec((None, nsub, RWKV_HEAD, D), lambda b, j: (b, j, 0, 0)),
                   pl.BlockSpec((None, nsub, RWKV_HEAD, D), lambda b, j: (b, n - 1 - j, 0, 0)),
                   state],
        out_shape=[jax.ShapeDtypeStruct((B, L // RWKV_SUB, RWKV_HEAD, D), f32),
                   jax.ShapeDtypeStruct((B, L // RWKV_SUB, RWKV_HEAD, D), f32),
                   jax.ShapeDtypeStruct((B, 2, RWKV_HEAD, D), f32)],
        scratch_shapes=[pltpu.VMEM((2, RWKV_HEAD, D), f32)],
        compiler_params=pltpu.CompilerParams(
            dimension_semantics=("arbitrary", "arbitrary"),
            vmem_limit_bytes=min(V7X_VMEM_BYTES, 2 * 14 * block_bytes + (8 << 20)),
        ),
        name="rwkv_scan_rowsum",
    )(r, v, nkk, w_f, k_f, b_f, r, v, nkk, w_b, k_b, b_b, s0)
    o = (o_f + o_b).reshape(B, L // RWKV_SUB, RWKV_HEAD, H, RWKV_SUB).transpose(0, 1, 4, 3, 2).reshape(B, L, H, N)
    s_fin = s_fin.reshape(B, 2, N, H, N).transpose(0, 1, 3, 2, 4)
    return o, s_fin[:, 0], s_fin[:, 1]


def rwkv_mix(h, mu, wr, wk, wv, wo, w0, w1, w2, a0, a1, a2, g1, g2, k_k, k_a, r_k, lnx_w, lnx_b, s_f0, s_b0):
    B, L, D = h.shape
    H, N = RWKV_HEADS, RWKV_HEAD
    xx = centred_shift(h) - h
    xr, xw, xk, xv, xa, xg = [h + xx * mu[j] for j in range(6)]
    r = (xr @ wr).reshape(B, L, H, N)
    k = (xk @ wk).reshape(B, L, H, N)
    v = (xv @ wv).reshape(B, L, H, N)
    g = jax.nn.sigmoid(xg @ g1) @ g2
    zw = [(w0[d] + jnp.tanh(xw @ w1[d]) @ w2[d]).reshape(B, L, H, N) for d in range(2)]
    za = [(a0[d] + (xa @ a1[d]) @ a2[d]).reshape(B, L, H, N) for d in range(2)]

    if B * H >= LANES:
        o, bonus, s_f, s_b = rwkv_scan_bidir(r, v, k, jnp.stack(zw), jnp.stack(za), k_k.reshape(H, N),
                                             k_a.reshape(H, N), r_k, s_f0, s_b0)
        bonus = bonus[..., None]
    else:
        kk = k * k_k.reshape(H, N)
        kk = kk * lax.rsqrt(jnp.sum(kk * kk, axis=-1, keepdims=True) + 1e-12)

        def direction(d):
            decay = jnp.exp(-jnp.exp(-jax.nn.softplus(-zw[d]) - 0.5))
            a = jax.nn.sigmoid(za[d])
            return decay, k * (1.0 + (a - 1.0) * k_a.reshape(H, N)), kk * a

        fwd, bwd = direction(0), direction(1)
        o, s_f, s_b = rwkv_scan_rowsum(r, v, -kk, fwd, bwd, s_f0, s_b0)
        bonus = jnp.sum(r * 0.5 * (fwd[1] + bwd[1]) * r_k, axis=-1, keepdims=True)
    m = jnp.mean(o, axis=-1, keepdims=True)
    var = jnp.mean(jnp.square(o - m), axis=-1, keepdims=True)
    o = (o - m) * lax.rsqrt(var + RWKV_LN_EPS) * lnx_w.reshape(H, N) + lnx_b.reshape(H, N)
    o = (o + bonus * v).reshape(B, L, D)
    return (o * g) @ wo, s_f, s_b


GROUP_HID = EXP_PER_GROUP * D_EXPERT


def _moe_group_kernel(tile_group_ref, n_tiles_ref, xs_ref, wg_ref, wu_ref, wd_ref, ys_ref,
                      wg_bf, wu_bf, wd_bf):
    i = pl.program_id(0)
    group = tile_group_ref[i]
    prev_group = tile_group_ref[jnp.maximum(i - 1, 0)]
    group_changed = jnp.logical_or(i == 0, group != prev_group)

    @pl.when(group_changed)
    def _():
        for e in range(EXP_PER_GROUP):
            hs = slice(e * D_EXPERT, (e + 1) * D_EXPERT)
            wg_bf[:, hs] = wg_ref[e].astype(bf16)
            wu_bf[:, hs] = wu_ref[e].astype(bf16)
            wd_bf[hs, :] = wd_ref[e].astype(bf16)

    @pl.when(i < n_tiles_ref[0])
    def _():
        half = wg_bf.shape[0] // 2
        logits = lax.bitcast_convert_type(xs_ref[:, half:], f32)
        lane = lax.broadcasted_iota(jnp.int32, logits.shape, 1)
        grp = jnp.where(lane < N_GROUPS, logits, -jnp.inf)
        grp = jnp.exp(grp - jnp.max(grp, axis=-1, keepdims=True))
        p_group = (jnp.sum(jnp.where(lane == group, grp, 0.0), axis=-1, keepdims=True)
                   / jnp.sum(grp, axis=-1, keepdims=True))
        first = N_GROUPS + group * EXP_PER_GROUP
        le = [jnp.sum(jnp.where(lane == first + e, logits, 0.0), axis=-1, keepdims=True)
              for e in range(EXP_PER_GROUP)]
        chosen = []
        for e in range(EXP_PER_GROUP):
            ahead = sum(((le[k] > le[e]) | ((le[k] == le[e]) & (k < e))).astype(jnp.int32)
                        for k in range(EXP_PER_GROUP) if k != e)
            chosen.append(ahead < TOP_K)
        top = functools.reduce(jnp.maximum, [jnp.where(c, v, -jnp.inf) for c, v in zip(chosen, le)])
        ex = [jnp.where(c, jnp.exp(v - top), 0.0) for c, v in zip(chosen, le)]
        scale = p_group / sum(ex)
        packed = xs_ref[:, :half]
        x = jnp.concatenate([lax.bitcast_convert_type(packed << 16, f32),
                             lax.bitcast_convert_type(packed & jnp.uint32(0xFFFF0000), f32)], axis=1).astype(bf16)
        g = jnp.dot(x, wg_bf[...], preferred_element_type=f32)
        u = jnp.dot(x, wu_bf[...], preferred_element_type=f32)
        gate = jnp.concatenate([jnp.broadcast_to(ex[e] * scale, (MOE_TILE_M, D_EXPERT))
                                for e in range(EXP_PER_GROUP)], axis=1)
        hid = (g * jax.nn.sigmoid(g)) * u * gate
        ys_ref[...] = jnp.dot(hid.astype(bf16), wd_bf[...], preferred_element_type=f32)

    @pl.when(i >= n_tiles_ref[0])
    def _():
        ys_ref[...] = jnp.zeros_like(ys_ref)


def moe_group_experts(xs, tile_group, n_tiles, w_gate, w_up, w_down, layer):
    P = xs.shape[0]
    D = 2 * (xs.shape[1] - LANES)
    max_tiles = P // MOE_TILE_M
    weight_bytes = 3 * EXP_PER_GROUP * D * D_EXPERT * (4 + 2)
    tile_bytes = 2 * MOE_TILE_M * (2 * D * 4 + LANES * 4) + 4 * MOE_TILE_M * GROUP_HID * 4
    once = pl.Buffered(1)
    grid_spec = pltpu.PrefetchScalarGridSpec(
        num_scalar_prefetch=2,
        grid=(max_tiles,),
        in_specs=[
            pl.BlockSpec((MOE_TILE_M, D // 2 + LANES), lambda i, tg, nt: (i, 0)),
            pl.BlockSpec((None, EXP_PER_GROUP, D, D_EXPERT), lambda i, tg, nt: (layer, tg[i], 0, 0),
                         pipeline_mode=once),
            pl.BlockSpec((None, EXP_PER_GROUP, D, D_EXPERT), lambda i, tg, nt: (layer, tg[i], 0, 0),
                         pipeline_mode=once),
            pl.BlockSpec((None, EXP_PER_GROUP, D_EXPERT, D), lambda i, tg, nt: (layer, tg[i], 0, 0),
                         pipeline_mode=once),
        ],
        out_specs=pl.BlockSpec((MOE_TILE_M, D), lambda i, tg, nt: (i, 0)),
        scratch_shapes=[
            pltpu.VMEM((D, GROUP_HID), bf16),
            pltpu.VMEM((D, GROUP_HID), bf16),
            pltpu.VMEM((GROUP_HID, D), bf16),
        ],
    )
    return pl.pallas_call(
        _moe_group_kernel,
        grid_spec=grid_spec,
        out_shape=jax.ShapeDtypeStruct((P, D), f32),
        compiler_params=pltpu.CompilerParams(
            dimension_semantics=("arbitrary",),
            vmem_limit_bytes=min(V7X_VMEM_BYTES - (4 << 20), weight_bytes + tile_bytes + (8 << 20)),
        ),
        name="moe_group_experts",
    )(tile_group, n_tiles, xs, w_gate, w_up, w_down)


def router_params(w_grp, b_grp, w_exp, b_exp):
    w = jnp.concatenate([w_grp, w_exp], axis=1)
    b = jnp.concatenate([b_grp, b_exp], axis=0)[None, :]
    return jnp.pad(w, ((0, 0), (0, LANES - w.shape[1]))), jnp.pad(b, ((0, 0), (0, LANES - b.shape[1])))


def hier_moe(t_aug, w_gate, w_up, w_down, layer):
    T = t_aug.shape[0]
    half = t_aug.shape[1] - LANES
    g_top = jnp.argmax(lax.bitcast_convert_type(t_aug[:, half:half + N_GROUPS], f32), axis=-1).astype(jnp.int32)
    in_group = g_top[:, None] == jnp.arange(N_GROUPS, dtype=jnp.int32)[None, :]

    order = jnp.argsort(g_top, stable=True).astype(jnp.int32)
    grp_i32 = in_group.astype(jnp.int32)
    counts = jnp.sum(grp_i32, axis=0)
    rank = jnp.sum(jnp.where(in_group, jnp.cumsum(grp_i32, axis=0), 0), axis=1) - 1
    tiles_per = (counts + MOE_TILE_M - 1) // MOE_TILE_M
    tile_end = jnp.cumsum(tiles_per)
    start_padded = (tile_end - tiles_per) * MOE_TILE_M
    start_sorted = jnp.cumsum(counts) - counts
    max_tiles = T // MOE_TILE_M + N_GROUPS
    n_tiles = tile_end[-1:].astype(jnp.int32)
    tile_ids = jnp.arange(max_tiles, dtype=jnp.int32)
    tile_group = jnp.minimum(jnp.sum((tile_ids[:, None] >= tile_end[None, :]).astype(jnp.int32), axis=1),
                             N_GROUPS - 1).astype(jnp.int32)
    tile_group = jnp.where(tile_ids < n_tiles[0], tile_group, tile_group[jnp.maximum(n_tiles[0] - 1, 0)])
    row_group = jnp.repeat(tile_group, MOE_TILE_M)
    row_in_group = jnp.arange(max_tiles * MOE_TILE_M, dtype=jnp.int32) - start_padded[row_group]
    row_valid = (row_in_group >= 0) & (row_in_group < counts[row_group])
    row_tok = jnp.where(row_valid, order[jnp.clip(start_sorted[row_group] + row_in_group, 0, T - 1)], 0)
    pos = start_padded[g_top] + rank

    xs = jnp.take(t_aug, row_tok, axis=0)
    ys = moe_group_experts(xs, tile_group, n_tiles, w_gate, w_up, w_down, layer)
    return jnp.take(ys, pos, axis=0)


def kernel(x_prompt, x_sample, c, cache_attn_k, cache_attn_v, state_gla, state_rwkv, c_ctx, ada_w, ada_b, norm_mix, norm_ffn, norm_out, ev_w_in, ev_w_out, gla_dec_w, gla_dec_b, gla_norm, att_sink, rw_mu, rw_wr, rw_wk, rw_wv, rw_wo, rw_w0, rw_w1, rw_w2, rw_a0, rw_a1, rw_a2, rw_g1, rw_g2, rw_kk, rw_ka, rw_rk, rw_lnx_w, rw_lnx_b, moe_w_grp, moe_b_grp, moe_w_exp, moe_b_exp, moe_w_gate, moe_w_up, moe_w_down):
    n_lat = x_sample.shape[1]
    rows = n_lat // GRID_W
    row_pos = jnp.repeat(jnp.arange(rows), GRID_W)
    col_pos = jnp.tile(jnp.arange(GRID_W), rows)
    rope = rope_tables(row_pos, col_pos)
    Bc, Lc, D = x_prompt.shape
    Bl, Ll, _ = x_sample.shape

    n_ctx = Bc * Lc
    x_all = jnp.concatenate([x_prompt.reshape(n_ctx, D), x_sample.reshape(Bl * Ll, D)], axis=0)
    cond_all = jnp.concatenate([c_ctx[None, :], c], axis=0)
    mods = [ada_mod(cond_all, ada_w[l], ada_b[l]) for l in range(DEPTH)]
    no_mod = jnp.zeros((1 + Bl, 1, D), f32)

    hc = modulate(rmsnorm(x_prompt, norm_mix[0]), mods[0][0][:1], mods[0][1][:1])
    hl = modulate(rmsnorm(x_sample, norm_mix[0]), mods[0][0][1:], mods[0][1][1:])
    ks_out, vs_out, gla_out, rwkv_out = [], [], [], []
    for l in range(DEPTH):
        _, _, gate1, shift2, scale2, gate2 = mods[l]
        i = l // 2
        if l % 2 == 0:
            oc, k_c, v_c, s_gla = even_mixer_ctx(hc, ev_w_in[i], ev_w_out[i], gla_dec_w[i], gla_dec_b[i],
                                                 gla_norm[i], att_sink[i])
            ol = even_mixer_lat(hl, ev_w_in[i], ev_w_out[i], gla_dec_w[i], gla_dec_b[i], gla_norm[i], att_sink[i],
                                rope, cache_attn_k[:, i], cache_attn_v[:, i], state_gla[:, i])
            ks_out.append(k_c)
            vs_out.append(v_c)
            gla_out.append(s_gla)
        else:
            rw = (rw_mu[i], rw_wr[i], rw_wk[i], rw_wv[i], rw_wo[i], rw_w0[i], rw_w1[i], rw_w2[i], rw_a0[i],
                  rw_a1[i], rw_a2[i], rw_g1[i], rw_g2[i], rw_kk[i], rw_ka[i], rw_rk[i], rw_lnx_w[i], rw_lnx_b[i])
            zero = jnp.zeros((Bc, RWKV_HEADS, RWKV_HEAD, RWKV_HEAD), f32)
            oc, s_f, s_b = rwkv_mix(hc, *rw, zero, zero)
            ol, _, _ = rwkv_mix(hl, *rw, state_rwkv[:, i, 0], state_rwkv[:, i, 1])
            rwkv_out.append(jnp.stack([s_f, s_b], axis=1))
        x_all, t_aug = resid_norm_mod(x_all, (oc.reshape(n_ctx, D), ol.reshape(Bl * Ll, D)), n_ctx, Ll,
                                      gate1, norm_ffn[l], shift2, scale2,
                                      router_params(moe_w_grp[l], moe_b_grp[l], moe_w_exp[l], moe_b_exp[l]))
        y = hier_moe(t_aug, moe_w_gate, moe_w_up, moe_w_down, l)
        if l + 1 < DEPTH:
            x_all, (hc, hl) = resid_norm_mod(x_all, y, n_ctx, Ll, gate2, norm_mix[l + 1], mods[l + 1][0],
                                             mods[l + 1][1])
        else:
            x_all, (hc, hl) = resid_norm_mod(x_all, y, n_ctx, Ll, gate2, norm_out, no_mod, no_mod)
        hc, hl = hc.reshape(Bc, Lc, D), hl.reshape(Bl, Ll, D)
    y_prompt, y_sample = hc, hl

    new_attn_k = jnp.stack(ks_out, axis=1)
    new_attn_v = jnp.stack(vs_out, axis=1)
    new_gla = jnp.stack(gla_out, axis=1)
    new_rwkv = jnp.stack(rwkv_out, axis=1)
    return (y_prompt, y_sample, new_attn_k, new_attn_v, new_gla, new_rwkv)
```
